```python
import jax, jax.numpy as jnp
from jax import lax
import numpy as np

D_MODEL = 1024
BATCH = 8
SEQ = 4096
DEPTH = 4

CHUNK = 64
Q_BLOCK = 128
PLE_DIM = 256

MLA_HEADS = 16
MLA_Q_LORA = D_MODEL // 2
MLA_KV_LORA = D_MODEL // 4
MLA_NOPE = 128
MLA_ROPE = 64
MLA_V = 128
ROPE_THETA = 10000.0

GLA_HEADS = 4
GLA_QK = D_MODEL // 2
GLA_VD = D_MODEL
GLA_DK = GLA_QK // GLA_HEADS
GLA_DV = GLA_VD // GLA_HEADS
GLA_GATE_RANK = 16
GLA_TAU = 16.0

D_FF = 2816
CONV_W = 3

N_MLA = (DEPTH + 1) // 2
N_GLA = DEPTH // 2
DN_ALPHA = (2 * DEPTH) ** 0.25
DN_BETA = (8 * DEPTH) ** -0.25
EPS = 1e-5
NEG_INF = -1e30

kernel_name = "hybrid_mla_gla_deepnorm_convffn_ple"


def layer_norm(x, g, b):
    xf = x.astype(jnp.float32)
    mu = jnp.mean(xf, -1, keepdims=True)
    var = jnp.mean(jnp.square(xf - mu), -1, keepdims=True)
    return ((xf - mu) * lax.rsqrt(var + EPS) * g + b).astype(x.dtype)


def rms_norm(x, g):
    xf = x.astype(jnp.float32)
    return (xf * lax.rsqrt(jnp.mean(jnp.square(xf), -1, keepdims=True) + EPS) * g).astype(x.dtype)


def rope_tables(positions):
    inv = 1.0 / (ROPE_THETA ** (jnp.arange(0, MLA_ROPE, 2, dtype=jnp.float32) / MLA_ROPE))
    ang = positions.astype(jnp.float32)[..., None] * inv
    return jnp.cos(ang), jnp.sin(ang)


def apply_rope(x, cos, sin):
    x1, x2 = jnp.split(x, 2, axis=-1)
    c = cos.astype(x.dtype)
    s = sin.astype(x.dtype)
    return jnp.concatenate([x1 * c - x2 * s, x1 * s + x2 * c], axis=-1)


def mla_mixer(x, positions, cos, sin, w_in, q_norm, kv_norm, w_uq, w_uk, w_uv, w_o):
    B, S, _ = x.shape
    h = x @ w_in
    c_q, c_kv, k_r = jnp.split(h, [MLA_Q_LORA, MLA_Q_LORA + MLA_KV_LORA], axis=-1)
    c_q = rms_norm(c_q, q_norm)
    c_kv = rms_norm(c_kv, kv_norm)
    q = (c_q @ w_uq).reshape(B, S, MLA_HEADS, MLA_NOPE + MLA_ROPE)
    q_nope = q[..., :MLA_NOPE]
    q_rope = apply_rope(q[..., MLA_NOPE:], cos[:, :, None], sin[:, :, None])
    k_rope = apply_rope(k_r, cos, sin)
    k_nope = (c_kv @ w_uk).reshape(B, S, MLA_HEADS, MLA_NOPE)
    v = (c_kv @ w_uv).reshape(B, S, MLA_HEADS, MLA_V)
    scale = (MLA_NOPE + MLA_ROPE) ** -0.5
    chunk_id = positions // CHUNK
    outs = []
    for blk in range(S // Q_BLOCK):
        q0 = blk * Q_BLOCK
        q1 = q0 + Q_BLOCK
        s = (jnp.einsum('bqhd,bkhd->bhqk', q_nope[:, q0:q1], k_nope[:, :q1])
             + jnp.einsum('bqhr,bkr->bhqk', q_rope[:, q0:q1], k_rope[:, :q1])).astype(jnp.float32) * scale
        mask = chunk_id[:, None, :q1] <= chunk_id[:, q0:q1, None]
        s = jnp.where(mask[:, None], s, NEG_INF)
        pr = jax.nn.softmax(s, axis=-1).astype(v.dtype)
        outs.append(jnp.einsum('bhqk,bkhd->bqhd', pr, v[:, :q1]))
    o = jnp.concatenate(outs, axis=1).reshape(B, S, MLA_HEADS * MLA_V)
    return (o @ w_o).astype(x.dtype)


def gla_mixer(x, w_in, w_a2, b_a, o_norm, w_o):
    B, S, _ = x.shape
    N = S // CHUNK
    h = x @ w_in
    q, k, v, r, a = jnp.split(h, [GLA_QK, 2 * GLA_QK, 2 * GLA_QK + GLA_VD, 2 * GLA_QK + 2 * GLA_VD], axis=-1)
    log_a = jax.nn.log_sigmoid((a @ w_a2 + b_a).astype(jnp.float32)) / GLA_TAU
    q = q.reshape(B, N, CHUNK, GLA_HEADS, GLA_DK) * (GLA_DK ** -0.5)
    k = k.reshape(B, N, CHUNK, GLA_HEADS, GLA_DK)
    v = v.reshape(B, N, CHUNK, GLA_HEADS, GLA_DV)
    log_a = log_a.reshape(B, N, CHUNK, GLA_HEADS, GLA_DK)
    cum = jnp.cumsum(log_a, axis=2)
    tot = cum[:, :, -1]
    k_dec = k * jnp.exp(tot[:, :, None] - cum).astype(k.dtype)
    upd = jnp.einsum('bnchk,bnchv->nbhkv', k_dec, v).astype(jnp.float32)
    decay = jnp.exp(jnp.moveaxis(tot, 1, 0))
    q_n = jnp.moveaxis(q, 1, 0)

    def step(state, inp):
        g, u, qc = inp
        state = state * g[..., None] + u
        return state, jnp.einsum('bchk,bhkv->bchv', qc, state)

    state0 = jnp.zeros((B, GLA_HEADS, GLA_DK, GLA_DV), jnp.float32)
    _, o = lax.scan(step, state0, (decay, upd, q_n))
    o = jnp.moveaxis(o, 0, 1).reshape(B, S, GLA_HEADS, GLA_DV)
    mu = jnp.mean(o, -1, keepdims=True)
    var = jnp.mean(jnp.square(o - mu), -1, keepdims=True)
    o = (o - mu) * lax.rsqrt(var + EPS) * o_norm.reshape(GLA_HEADS, GLA_DV)
    o = o.reshape(B, S, GLA_VD).astype(x.dtype) * jax.nn.silu(r)
    return (o @ w_o).astype(x.dtype)


def conv_ffn(x, w_up, conv_w, conv_b, w_down):
    S = x.shape[1]
    h = x @ w_up
    hp = jnp.pad(h, ((0, 0), (CONV_W - 1, 0), (0, 0)))
    h = hp[:, 0:S] * conv_w[0] + hp[:, 1:S + 1] * conv_w[1] + hp[:, 2:S + 2] * conv_w[2] + conv_b
    u, g = jnp.split(h, 2, axis=-1)
    return ((u * jax.nn.gelu(g)) @ w_down).astype(x.dtype)


def _fwd_setup_inputs(seed: int = 0) -> dict:
    key = jax.random.key(seed)
    ks = iter(jax.random.split(key, 40))
    f32 = jnp.float32

    def nrm(shape, scale):
        return jax.random.normal(next(ks), shape, f32) * scale

    def gain(shape):
        return 1.0 + nrm(shape, 0.01)

    x = jax.random.normal(next(ks), (BATCH, SEQ, D_MODEL), f32)
    p = jax.random.normal(next(ks), (DEPTH, BATCH, SEQ, PLE_DIM), f32)
    offsets = jax.random.randint(next(ks), (BATCH, 1), 0, 16, dtype=jnp.int32) * CHUNK
    positions = (jnp.arange(SEQ, dtype=jnp.int32)[None, :] + offsets).astype(jnp.int32)

    mla_in = MLA_Q_LORA + MLA_KV_LORA + MLA_ROPE
    gla_in = 2 * GLA_QK + 2 * GLA_VD + GLA_GATE_RANK
    return {
        "x": x,
        "p": p,
        "positions": positions,
        "mla_w_in": nrm((N_MLA, D_MODEL, mla_in), D_MODEL ** -0.5),
        "mla_q_norm": gain((N_MLA, MLA_Q_LORA)),
        "mla_kv_norm": gain((N_MLA, MLA_KV_LORA)),
        "mla_w_uq": nrm((N_MLA, MLA_Q_LORA, MLA_HEADS * (MLA_NOPE + MLA_ROPE)), MLA_Q_LORA ** -0.5),
        "mla_w_uk": nrm((N_MLA, MLA_KV_LORA, MLA_HEADS * MLA_NOPE), MLA_KV_LORA ** -0.5),
        "mla_w_uv": nrm((N_MLA, MLA_KV_LORA, MLA_HEADS * MLA_V), DN_BETA * MLA_KV_LORA ** -0.5),
        "mla_w_o": nrm((N_MLA, MLA_HEADS * MLA_V, D_MODEL), DN_BETA * (MLA_HEADS * MLA_V) ** -0.5),
        "gla_w_in": nrm((N_GLA, D_MODEL, gla_in), D_MODEL ** -0.5),
        "gla_w_a2": nrm((N_GLA, GLA_GATE_RANK, GLA_QK), GLA_GATE_RANK ** -0.5),
        "gla_b_a": nrm((N_GLA, GLA_QK), 0.1),
        "gla_o_norm": gain((N_GLA, GLA_VD)),
        "gla_w_o": nrm((N_GLA, GLA_VD, D_MODEL), DN_BETA * GLA_VD ** -0.5),
        "ln1_g": gain((DEPTH, D_MODEL)),
        "ln1_b": nrm((DEPTH, D_MODEL), 0.01),
        "ln2_g": gain((DEPTH, D_MODEL)),
        "ln2_b": nrm((DEPTH, D_MODEL), 0.01),
        "ffn_w_up": nrm((DEPTH, D_MODEL, 2 * D_FF), D_MODEL ** -0.5),
        "ffn_conv_w": nrm((DEPTH, CONV_W, 2 * D_FF), CONV_W ** -0.5),
        "ffn_conv_b": nrm((DEPTH, 2 * D_FF), 0.01),
        "ffn_w_down": nrm((DEPTH, D_FF, D_MODEL), DN_BETA * D_FF ** -0.5),
        "ple_w_proj": nrm((DEPTH, PLE_DIM, D_MODEL), PLE_DIM ** -0.5),
        "ple_w_gate": nrm((DEPTH, D_MODEL, D_MODEL), D_MODEL ** -0.5),
        "ple_b_gate": nrm((DEPTH, D_MODEL), 0.01),
    }


def _fwd_reference(x, p, positions, mla_w_in, mla_q_norm, mla_kv_norm, mla_w_uq, mla_w_uk, mla_w_uv, mla_w_o,
              gla_w_in, gla_w_a2, gla_b_a, gla_o_norm, gla_w_o, ln1_g, ln1_b, ln2_g, ln2_b,
              ffn_w_up, ffn_conv_w, ffn_conv_b, ffn_w_down, ple_w_proj, ple_w_gate, ple_b_gate):
    cos, sin = rope_tables(positions)
    for i in range(DEPTH):
        j = i // 2
        if i % 2 == 0:
            m = mla_mixer(x, positions, cos, sin, mla_w_in[j], mla_q_norm[j], mla_kv_norm[j],
                          mla_w_uq[j], mla_w_uk[j], mla_w_uv[j], mla_w_o[j])
        else:
            m = gla_mixer(x, gla_w_in[j], gla_w_a2[j], gla_b_a[j], gla_o_norm[j], gla_w_o[j])
        x = layer_norm(DN_ALPHA * x + m, ln1_g[i], ln1_b[i])
        x = layer_norm(DN_ALPHA * x + conv_ffn(x, ffn_w_up[i], ffn_conv_w[i], ffn_conv_b[i], ffn_w_down[i]),
                       ln2_g[i], ln2_b[i])
        gate = jax.nn.sigmoid(x @ ple_w_gate[i] + ple_b_gate[i])
        x = x + gate * (p[i] @ ple_w_proj[i])
    return x


import jax as _jax
import jax.numpy as _jnp

TWIN_FORMAT = 'train_step'
FWD_PARAMS = ['x', 'p', 'positions', 'mla_w_in', 'mla_q_norm', 'mla_kv_norm', 'mla_w_uq', 'mla_w_uk', 'mla_w_uv', 'mla_w_o', 'gla_w_in', 'gla_w_a2', 'gla_b_a', 'gla_o_norm', 'gla_w_o', 'ln1_g', 'ln1_b', 'ln2_g', 'ln2_b', 'ffn_w_up', 'ffn_conv_w', 'ffn_conv_b', 'ffn_w_down', 'ple_w_proj', 'ple_w_gate', 'ple_b_gate']
TWIN_WEIGHTS = ['mla_w_in', 'mla_q_norm', 'mla_kv_norm', 'mla_w_uq', 'mla_w_uk', 'mla_w_uv', 'mla_w_o', 'gla_w_in', 'gla_w_a2', 'gla_b_a', 'gla_o_norm', 'gla_w_o', 'ln1_g', 'ln1_b', 'ln2_g', 'ln2_b', 'ffn_w_up', 'ffn_conv_w', 'ffn_conv_b', 'ffn_w_down', 'ple_w_proj', 'ple_w_gate', 'ple_b_gate']
TWIN_DIFF_INPUT = 'x'
TWIN_INPUTS = ['x', 'p', 'positions', 'mla_w_in', 'mla_q_norm', 'mla_kv_norm', 'mla_w_uq', 'mla_w_uk', 'mla_w_uv', 'mla_w_o', 'gla_w_in', 'gla_w_a2', 'gla_b_a', 'gla_o_norm', 'gla_w_o', 'ln1_g', 'ln1_b', 'ln2_g', 'ln2_b', 'ffn_w_up', 'ffn_conv_w', 'ffn_conv_b', 'ffn_w_down', 'ple_w_proj', 'ple_w_gate', 'ple_b_gate', 'loss_target', 'm_mla_w_in', 'm_mla_q_norm', 'm_mla_kv_norm', 'm_mla_w_uq', 'm_mla_w_uk', 'm_mla_w_uv', 'm_mla_w_o', 'm_gla_w_in', 'm_gla_w_a2', 'm_gla_b_a', 'm_gla_o_norm', 'm_gla_w_o', 'm_ln1_g', 'm_ln1_b', 'm_ln2_g', 'm_ln2_b', 'm_ffn_w_up', 'm_ffn_conv_w', 'm_ffn_conv_b', 'm_ffn_w_down', 'm_ple_w_proj', 'm_ple_w_gate', 'm_ple_b_gate', 'v_mla_w_in', 'v_mla_q_norm', 'v_mla_kv_norm', 'v_mla_w_uq', 'v_mla_w_uk', 'v_mla_w_uv', 'v_mla_w_o', 'v_gla_w_in', 'v_gla_w_a2', 'v_gla_b_a', 'v_gla_o_norm', 'v_gla_w_o', 'v_ln1_g', 'v_ln1_b', 'v_ln2_g', 'v_ln2_b', 'v_ffn_w_up', 'v_ffn_conv_w', 'v_ffn_conv_b', 'v_ffn_w_down', 'v_ple_w_proj', 'v_ple_w_gate', 'v_ple_b_gate']
TWIN_OUTPUTS = ['loss', 'grad_x', 'grad_mla_w_in', 'grad_mla_q_norm', 'grad_mla_kv_norm', 'grad_mla_w_uq', 'grad_mla_w_uk', 'grad_mla_w_uv', 'grad_mla_w_o', 'grad_gla_w_in', 'grad_gla_w_a2', 'grad_gla_b_a', 'grad_gla_o_norm', 'grad_gla_w_o', 'grad_ln1_g', 'grad_ln1_b', 'grad_ln2_g', 'grad_ln2_b', 'grad_ffn_w_up', 'grad_ffn_conv_w', 'grad_ffn_conv_b', 'grad_ffn_w_down', 'grad_ple_w_proj', 'grad_ple_w_gate', 'grad_ple_b_gate', 'delta_mla_w_in', 'delta_mla_q_norm', 'delta_mla_kv_norm', 'delta_mla_w_uq', 'delta_mla_w_uk', 'delta_mla_w_uv', 'delta_mla_w_o', 'delta_gla_w_in', 'delta_gla_w_a2', 'delta_gla_b_a', 'delta_gla_o_norm', 'delta_gla_w_o', 'delta_ln1_g', 'delta_ln1_b', 'delta_ln2_g', 'delta_ln2_b', 'delta_ffn_w_up', 'delta_ffn_conv_w', 'delta_ffn_conv_b', 'delta_ffn_w_down', 'delta_ple_w_proj', 'delta_ple_w_gate', 'delta_ple_b_gate', 'new_m_mla_w_in', 'new_m_mla_q_norm', 'new_m_mla_kv_norm', 'new_m_mla_w_uq', 'new_m_mla_w_uk', 'new_m_mla_w_uv', 'new_m_mla_w_o', 'new_m_gla_w_in', 'new_m_gla_w_a2', 'new_m_gla_b_a', 'new_m_gla_o_norm', 'new_m_gla_w_o', 'new_m_ln1_g', 'new_m_ln1_b', 'new_m_ln2_g', 'new_m_ln2_b', 'new_m_ffn_w_up', 'new_m_ffn_conv_w', 'new_m_ffn_conv_b', 'new_m_ffn_w_down', 'new_m_ple_w_proj', 'new_m_ple_w_gate', 'new_m_ple_b_gate', 'new_v_mla_w_in', 'new_v_mla_q_norm', 'new_v_mla_kv_norm', 'new_v_mla_w_uq', 'new_v_mla_w_uk', 'new_v_mla_w_uv', 'new_v_mla_w_o', 'new_v_gla_w_in', 'new_v_gla_w_a2', 'new_v_gla_b_a', 'new_v_gla_o_norm', 'new_v_gla_w_o', 'new_v_ln1_g', 'new_v_ln1_b', 'new_v_ln2_g', 'new_v_ln2_b', 'new_v_ffn_w_up', 'new_v_ffn_conv_w', 'new_v_ffn_conv_b', 'new_v_ffn_w_down', 'new_v_ple_w_proj', 'new_v_ple_w_gate', 'new_v_ple_b_gate']
TWIN_LEAF_KINDS = {'loss': 'loss', 'grad_x': 'grad_x', 'grad_mla_w_in': 'grad_w', 'grad_mla_q_norm': 'grad_w', 'grad_mla_kv_norm': 'grad_w', 'grad_mla_w_uq': 'grad_w', 'grad_mla_w_uk': 'grad_w', 'grad_mla_w_uv': 'grad_w', 'grad_mla_w_o': 'grad_w', 'grad_gla_w_in': 'grad_w', 'grad_gla_w_a2': 'grad_w', 'grad_gla_b_a': 'grad_w', 'grad_gla_o_norm': 'grad_w', 'grad_gla_w_o': 'grad_w', 'grad_ln1_g': 'grad_w', 'grad_ln1_b': 'grad_w', 'grad_ln2_g': 'grad_w', 'grad_ln2_b': 'grad_w', 'grad_ffn_w_up': 'grad_w', 'grad_ffn_conv_w': 'grad_w', 'grad_ffn_conv_b': 'grad_w', 'grad_ffn_w_down': 'grad_w', 'grad_ple_w_proj': 'grad_w', 'grad_ple_w_gate': 'grad_w', 'grad_ple_b_gate': 'grad_w', 'delta_mla_w_in': 'delta_w', 'delta_mla_q_norm': 'delta_w', 'delta_mla_kv_norm': 'delta_w', 'delta_mla_w_uq': 'delta_w', 'delta_mla_w_uk': 'delta_w', 'delta_mla_w_uv': 'delta_w', 'delta_mla_w_o': 'delta_w', 'delta_gla_w_in': 'delta_w', 'delta_gla_w_a2': 'delta_w', 'delta_gla_b_a': 'delta_w', 'delta_gla_o_norm': 'delta_w', 'delta_gla_w_o': 'delta_w', 'delta_ln1_g': 'delta_w', 'delta_ln1_b': 'delta_w', 'delta_ln2_g': 'delta_w', 'delta_ln2_b': 'delta_w', 'delta_ffn_w_up': 'delta_w', 'delta_ffn_conv_w': 'delta_w', 'delta_ffn_conv_b': 'delta_w', 'delta_ffn_w_down': 'delta_w', 'delta_ple_w_proj': 'delta_w', 'delta_ple_w_gate': 'delta_w', 'delta_ple_b_gate': 'delta_w', 'new_m_mla_w_in': 'new_m', 'new_m_mla_q_norm': 'new_m', 'new_m_mla_kv_norm': 'new_m', 'new_m_mla_w_uq': 'new_m', 'new_m_mla_w_uk': 'new_m', 'new_m_mla_w_uv': 'new_m', 'new_m_mla_w_o': 'new_m', 'new_m_gla_w_in': 'new_m', 'new_m_gla_w_a2': 'new_m', 'new_m_gla_b_a': 'new_m', 'new_m_gla_o_norm': 'new_m', 'new_m_gla_w_o': 'new_m', 'new_m_ln1_g': 'new_m', 'new_m_ln1_b': 'new_m', 'new_m_ln2_g': 'new_m', 'new_m_ln2_b': 'new_m', 'new_m_ffn_w_up': 'new_m', 'new_m_ffn_conv_w': 'new_m', 'new_m_ffn_conv_b': 'new_m', 'new_m_ffn_w_down': 'new_m', 'new_m_ple_w_proj': 'new_m', 'new_m_ple_w_gate': 'new_m', 'new_m_ple_b_gate': 'new_m', 'new_v_mla_w_in': 'new_v', 'new_v_mla_q_norm': 'new_v', 'new_v_mla_kv_norm': 'new_v', 'new_v_mla_w_uq': 'new_v', 'new_v_mla_w_uk': 'new_v', 'new_v_mla_w_uv': 'new_v', 'new_v_mla_w_o': 'new_v', 'new_v_gla_w_in': 'new_v', 'new_v_gla_w_a2': 'new_v', 'new_v_gla_b_a': 'new_v', 'new_v_gla_o_norm': 'new_v', 'new_v_gla_w_o': 'new_v', 'new_v_ln1_g': 'new_v', 'new_v_ln1_b': 'new_v', 'new_v_ln2_g': 'new_v', 'new_v_ln2_b': 'new_v', 'new_v_ffn_w_up': 'new_v', 'new_v_ffn_conv_w': 'new_v', 'new_v_ffn_conv_b': 'new_v', 'new_v_ffn_w_down': 'new_v', 'new_v_ple_w_proj': 'new_v', 'new_v_ple_w_gate': 'new_v', 'new_v_ple_b_gate': 'new_v'}


def _forward(args):
    return _fwd_reference(*[args[k] for k in FWD_PARAMS])


def _output_shape():
    out = _jax.eval_shape(lambda: _forward(_fwd_setup_inputs(0)))
    return out.shape, out.dtype

N_MICROBATCH = 1
ADAM_LR = 0.001
ADAM_B1 = 0.9
ADAM_B2 = 0.999
ADAM_EPS = 1e-08
ADAM_WD = 0.01
ADAM_STEP = 10
PER_EXAMPLE_BATCH_AXIS = {'x': 0, 'p': 1, 'positions': 0, 'loss_target': 0}
SHARED_INPUTS = []
_WEIGHT_DTYPES = {'mla_w_in': _jnp.float32, 'mla_q_norm': _jnp.float32, 'mla_kv_norm': _jnp.float32, 'mla_w_uq': _jnp.float32, 'mla_w_uk': _jnp.float32, 'mla_w_uv': _jnp.float32, 'mla_w_o': _jnp.float32, 'gla_w_in': _jnp.float32, 'gla_w_a2': _jnp.float32, 'gla_b_a': _jnp.float32, 'gla_o_norm': _jnp.float32, 'gla_w_o': _jnp.float32, 'ln1_g': _jnp.float32, 'ln1_b': _jnp.float32, 'ln2_g': _jnp.float32, 'ln2_b': _jnp.float32, 'ffn_w_up': _jnp.float32, 'ffn_conv_w': _jnp.float32, 'ffn_conv_b': _jnp.float32, 'ffn_w_down': _jnp.float32, 'ple_w_proj': _jnp.float32, 'ple_w_gate': _jnp.float32, 'ple_b_gate': _jnp.float32}
MOMENT_SCALE = {'mla_w_in': 5.668225e-03, 'mla_q_norm': 4.062321e-03, 'mla_kv_norm': 8.767466e-03, 'mla_w_uq': 1.636895e-03, 'mla_w_uk': 1.650050e-03, 'mla_w_uv': 5.804049e-03, 'mla_w_o': 8.290558e-03, 'gla_w_in': 3.563553e-02, 'gla_w_a2': 5.486947e-03, 'gla_b_a': 1.875869e-02, 'gla_o_norm': 3.335195e-02, 'gla_w_o': 7.149119e-02, 'ln1_g': 3.170993e-01, 'ln1_b': 2.667177e+00, 'ln2_g': 1.649513e+01, 'ln2_b': 2.733467e+00, 'ffn_w_up': 1.944314e-02, 'ffn_conv_w': 1.944577e-02, 'ffn_conv_b': 7.126695e-02, 'ffn_w_down': 7.552458e-02, 'ple_w_proj': 2.107943e-01, 'ple_w_gate': 5.071354e-02, 'ple_b_gate': 1.630048e+00}


def _to_microbatches(a, axis):
    t = _jnp.moveaxis(a, axis, 0)
    t = t.reshape((N_MICROBATCH, t.shape[0] // N_MICROBATCH) + t.shape[1:])
    return _jnp.moveaxis(t, 1, axis + 1)


def setup_inputs(seed: int = 0) -> dict:
    inp = _fwd_setup_inputs(seed)
    key = _jax.random.fold_in(_jax.random.key(seed), 7919)
    shape, _ = _output_shape()
    out = dict(inp)
    out["loss_target"] = _jax.random.normal(_jax.random.fold_in(key, 0), shape, _jnp.float32)
    for i, name in enumerate(TWIN_WEIGHTS):
        w = inp[name].astype(_jnp.float32)
        if MOMENT_SCALE is None:
            s = _jnp.sqrt(_jnp.mean(_jnp.square(w)) + 1e-30)
        else:
            s = MOMENT_SCALE[name]
        km, kv = _jax.random.split(_jax.random.fold_in(key, i + 1))
        out[name] = w
        out["m_" + name] = s * _jax.random.normal(km, w.shape, _jnp.float32)
        out["v_" + name] = (s * s) * _jax.random.uniform(kv, w.shape, _jnp.float32, 0.5, 1.5)
    if N_MICROBATCH > 1:
        for name, axis in PER_EXAMPLE_BATCH_AXIS.items():
            out[name] = _to_microbatches(out[name], axis)
    return {'x': out['x'], 'p': out['p'], 'positions': out['positions'], 'mla_w_in': out['mla_w_in'], 'mla_q_norm': out['mla_q_norm'], 'mla_kv_norm': out['mla_kv_norm'], 'mla_w_uq': out['mla_w_uq'], 'mla_w_uk': out['mla_w_uk'], 'mla_w_uv': out['mla_w_uv'], 'mla_w_o': out['mla_w_o'], 'gla_w_in': out['gla_w_in'], 'gla_w_a2': out['gla_w_a2'], 'gla_b_a': out['gla_b_a'], 'gla_o_norm': out['gla_o_norm'], 'gla_w_o': out['gla_w_o'], 'ln1_g': out['ln1_g'], 'ln1_b': out['ln1_b'], 'ln2_g': out['ln2_g'], 'ln2_b': out['ln2_b'], 'ffn_w_up': out['ffn_w_up'], 'ffn_conv_w': out['ffn_conv_w'], 'ffn_conv_b': out['ffn_conv_b'], 'ffn_w_down': out['ffn_w_down'], 'ple_w_proj': out['ple_w_proj'], 'ple_w_gate': out['ple_w_gate'], 'ple_b_gate': out['ple_b_gate'], 'loss_target': out['loss_target'], 'm_mla_w_in': out['m_mla_w_in'], 'm_mla_q_norm': out['m_mla_q_norm'], 'm_mla_kv_norm': out['m_mla_kv_norm'], 'm_mla_w_uq': out['m_mla_w_uq'], 'm_mla_w_uk': out['m_mla_w_uk'], 'm_mla_w_uv': out['m_mla_w_uv'], 'm_mla_w_o': out['m_mla_w_o'], 'm_gla_w_in': out['m_gla_w_in'], 'm_gla_w_a2': out['m_gla_w_a2'], 'm_gla_b_a': out['m_gla_b_a'], 'm_gla_o_norm': out['m_gla_o_norm'], 'm_gla_w_o': out['m_gla_w_o'], 'm_ln1_g': out['m_ln1_g'], 'm_ln1_b': out['m_ln1_b'], 'm_ln2_g': out['m_ln2_g'], 'm_ln2_b': out['m_ln2_b'], 'm_ffn_w_up': out['m_ffn_w_up'], 'm_ffn_conv_w': out['m_ffn_conv_w'], 'm_ffn_conv_b': out['m_ffn_conv_b'], 'm_ffn_w_down': out['m_ffn_w_down'], 'm_ple_w_proj': out['m_ple_w_proj'], 'm_ple_w_gate': out['m_ple_w_gate'], 'm_ple_b_gate': out['m_ple_b_gate'], 'v_mla_w_in': out['v_mla_w_in'], 'v_mla_q_norm': out['v_mla_q_norm'], 'v_mla_kv_norm': out['v_mla_kv_norm'], 'v_mla_w_uq': out['v_mla_w_uq'], 'v_mla_w_uk': out['v_mla_w_uk'], 'v_mla_w_uv': out['v_mla_w_uv'], 'v_mla_w_o': out['v_mla_w_o'], 'v_gla_w_in': out['v_gla_w_in'], 'v_gla_w_a2': out['v_gla_w_a2'], 'v_gla_b_a': out['v_gla_b_a'], 'v_gla_o_norm': out['v_gla_o_norm'], 'v_gla_w_o': out['v_gla_w_o'], 'v_ln1_g': out['v_ln1_g'], 'v_ln1_b': out['v_ln1_b'], 'v_ln2_g': out['v_ln2_g'], 'v_ln2_b': out['v_ln2_b'], 'v_ffn_w_up': out['v_ffn_w_up'], 'v_ffn_conv_w': out['v_ffn_conv_w'], 'v_ffn_conv_b': out['v_ffn_conv_b'], 'v_ffn_w_down': out['v_ffn_w_down'], 'v_ple_w_proj': out['v_ple_w_proj'], 'v_ple_w_gate': out['v_ple_w_gate'], 'v_ple_b_gate': out['v_ple_b_gate']}


def _loss(weights, diff, rest, loss_target):
    with _jax.named_scope("forward"):
        args = {**rest, TWIN_DIFF_INPUT: diff, **{k: w.astype(_WEIGHT_DTYPES[k]) for k, w in weights.items()}}
        y = _forward(args)
    with _jax.named_scope("loss_head"):
        err = _jnp.square(y.astype(_jnp.float32) - loss_target)
        return 0.5 * _jnp.sum(_jnp.mean(err, axis=-1)) if err.ndim else 0.5 * err


def _adamw(w, g, m, v):
    m = ADAM_B1 * m + (1.0 - ADAM_B1) * g
    v = ADAM_B2 * v + (1.0 - ADAM_B2) * _jnp.square(g)
    m_hat = m / (1.0 - ADAM_B1 ** ADAM_STEP)
    v_hat = v / (1.0 - ADAM_B2 ** ADAM_STEP)
    delta = -ADAM_LR * (m_hat / (_jnp.sqrt(v_hat) + ADAM_EPS) + ADAM_WD * w)
    return delta, m, v


def reference(x, p, positions, mla_w_in, mla_q_norm, mla_kv_norm, mla_w_uq, mla_w_uk, mla_w_uv, mla_w_o, gla_w_in, gla_w_a2, gla_b_a, gla_o_norm, gla_w_o, ln1_g, ln1_b, ln2_g, ln2_b, ffn_w_up, ffn_conv_w, ffn_conv_b, ffn_w_down, ple_w_proj, ple_w_gate, ple_b_gate, loss_target, m_mla_w_in, m_mla_q_norm, m_mla_kv_norm, m_mla_w_uq, m_mla_w_uk, m_mla_w_uv, m_mla_w_o, m_gla_w_in, m_gla_w_a2, m_gla_b_a, m_gla_o_norm, m_gla_w_o, m_ln1_g, m_ln1_b, m_ln2_g, m_ln2_b, m_ffn_w_up, m_ffn_conv_w, m_ffn_conv_b, m_ffn_w_down, m_ple_w_proj, m_ple_w_gate, m_ple_b_gate, v_mla_w_in, v_mla_q_norm, v_mla_kv_norm, v_mla_w_uq, v_mla_w_uk, v_mla_w_uv, v_mla_w_o, v_gla_w_in, v_gla_w_a2, v_gla_b_a, v_gla_o_norm, v_gla_w_o, v_ln1_g, v_ln1_b, v_ln2_g, v_ln2_b, v_ffn_w_up, v_ffn_conv_w, v_ffn_conv_b, v_ffn_w_down, v_ple_w_proj, v_ple_w_gate, v_ple_b_gate):
    given = dict(x=x, p=p, positions=positions, mla_w_in=mla_w_in, mla_q_norm=mla_q_norm, mla_kv_norm=mla_kv_norm, mla_w_uq=mla_w_uq, mla_w_uk=mla_w_uk, mla_w_uv=mla_w_uv, mla_w_o=mla_w_o, gla_w_in=gla_w_in, gla_w_a2=gla_w_a2, gla_b_a=gla_b_a, gla_o_norm=gla_o_norm, gla_w_o=gla_w_o, ln1_g=ln1_g, ln1_b=ln1_b, ln2_g=ln2_g, ln2_b=ln2_b, ffn_w_up=ffn_w_up, ffn_conv_w=ffn_conv_w, ffn_conv_b=ffn_conv_b, ffn_w_down=ffn_w_down, ple_w_proj=ple_w_proj, ple_w_gate=ple_w_gate, ple_b_gate=ple_b_gate, loss_target=loss_target, m_mla_w_in=m_mla_w_in, m_mla_q_norm=m_mla_q_norm, m_mla_kv_norm=m_mla_kv_norm, m_mla_w_uq=m_mla_w_uq, m_mla_w_uk=m_mla_w_uk, m_mla_w_uv=m_mla_w_uv, m_mla_w_o=m_mla_w_o, m_gla_w_in=m_gla_w_in, m_gla_w_a2=m_gla_w_a2, m_gla_b_a=m_gla_b_a, m_gla_o_norm=m_gla_o_norm, m_gla_w_o=m_gla_w_o, m_ln1_g=m_ln1_g, m_ln1_b=m_ln1_b, m_ln2_g=m_ln2_g, m_ln2_b=m_ln2_b, m_ffn_w_up=m_ffn_w_up, m_ffn_conv_w=m_ffn_conv_w, m_ffn_conv_b=m_ffn_conv_b, m_ffn_w_down=m_ffn_w_down, m_ple_w_proj=m_ple_w_proj, m_ple_w_gate=m_ple_w_gate, m_ple_b_gate=m_ple_b_gate, v_mla_w_in=v_mla_w_in, v_mla_q_norm=v_mla_q_norm, v_mla_kv_norm=v_mla_kv_norm, v_mla_w_uq=v_mla_w_uq, v_mla_w_uk=v_mla_w_uk, v_mla_w_uv=v_mla_w_uv, v_mla_w_o=v_mla_w_o, v_gla_w_in=v_gla_w_in, v_gla_w_a2=v_gla_w_a2, v_gla_b_a=v_gla_b_a, v_gla_o_norm=v_gla_o_norm, v_gla_w_o=v_gla_w_o, v_ln1_g=v_ln1_g, v_ln1_b=v_ln1_b, v_ln2_g=v_ln2_g, v_ln2_b=v_ln2_b, v_ffn_w_up=v_ffn_w_up, v_ffn_conv_w=v_ffn_conv_w, v_ffn_conv_b=v_ffn_conv_b, v_ffn_w_down=v_ffn_w_down, v_ple_w_proj=v_ple_w_proj, v_ple_w_gate=v_ple_w_gate, v_ple_b_gate=v_ple_b_gate)
    weights = {n: given[n] for n in TWIN_WEIGHTS}
    shared = {n: given[n] for n in SHARED_INPUTS}
    per_example = {n: given[n] for n in ['x', 'p', 'positions']}
    grad_fn = _jax.value_and_grad(_loss, argnums=(0, 1))

    def one_microbatch(ex, loss_target):
        ex = dict(ex)
        diff = ex.pop(TWIN_DIFF_INPUT)
        return grad_fn(weights, diff, {**shared, **ex}, loss_target)

    if N_MICROBATCH == 1:
        loss, (grad_w, grad_x) = one_microbatch(per_example, given["loss_target"])
    else:
        def body(carry, xs):
            loss_sum, grad_sum = carry
            l_k, (gw_k, gx_k) = one_microbatch(xs[0], xs[1])
            with _jax.named_scope("update"):
                return (loss_sum + l_k, _jax.tree.map(_jnp.add, grad_sum, gw_k)), gx_k

        init = (_jnp.zeros((), _jnp.float32), _jax.tree.map(_jnp.zeros_like, weights))
        (loss, grad_w), grad_x = _jax.lax.scan(body, init, (per_example, given["loss_target"]))
    with _jax.named_scope("update"):
        delta_w, new_m, new_v = {}, {}, {}
        for n in TWIN_WEIGHTS:
            delta_w[n], new_m[n], new_v[n] = _adamw(weights[n], grad_w[n], given["m_" + n], given["v_" + n])
    return (loss, grad_x, *[grad_w[n] for n in TWIN_WEIGHTS], *[delta_w[n] for n in TWIN_WEIGHTS],
            *[new_m[n] for n in TWIN_WEIGHTS], *[new_v[n] for n in TWIN_WEIGHTS])
```

```python
import functools
import math

import jax
import jax.numpy as jnp
from jax import lax
from jax.experimental import pallas as pl
from jax.experimental.pallas import tpu as pltpu

F32 = jnp.float32
MXU_DTYPE = jnp.bfloat16

DEPTH = 4
CHUNK = 64
Q_BLOCK = 128
MLA_NOPE = 128
MLA_ROPE = 64
MLA_V = 128
ROPE_THETA = 10000.0
GLA_HEADS = 4
GLA_GATE_RANK = 16
GLA_TAU = 16.0
DN_ALPHA = (2 * DEPTH) ** 0.25
EPS = 1e-5
NEG_INF = -1e30
ADAM_LR = 0.001
ADAM_B1 = 0.9
ADAM_B2 = 0.999
ADAM_EPS = 1e-08
ADAM_WD = 0.01
ADAM_STEP = 10
GELU_C = math.sqrt(2.0 / math.pi)
GELU_A = 0.044715

LANE = 128
HEAD_PAD = 2 * LANE
VMEM_LIMIT_BYTES = 48 * 1024 * 1024

MESH = pl.DeviceIdType.MESH
ANY_SPEC = pl.BlockSpec(memory_space=pl.ANY)

WEIGHT_NAMES = ('mla_w_in', 'mla_q_norm', 'mla_kv_norm', 'mla_w_uq', 'mla_w_uk', 'mla_w_uv', 'mla_w_o',
                'gla_w_in', 'gla_w_a2', 'gla_b_a', 'gla_o_norm', 'gla_w_o', 'ln1_g', 'ln1_b', 'ln2_g', 'ln2_b',
                'ffn_w_up', 'ffn_conv_w', 'ffn_conv_b', 'ffn_w_down', 'ple_w_proj', 'ple_w_gate', 'ple_b_gate')
BIG = (('mla_w_in', 1), ('mla_w_uq', 2), ('mla_w_uk', 2), ('mla_w_uv', 2), ('mla_w_o', 1), ('gla_w_in', 2),
       ('gla_w_o', 1), ('ffn_w_up', 2), ('ffn_w_down', 1), ('ple_w_proj', 2), ('ple_w_gate', 1))
SMALL = (('mla_q_norm', None), ('mla_kv_norm', None), ('gla_w_a2', 2), ('gla_b_a', 1), ('gla_o_norm', 1),
         ('ln1_g', None), ('ln1_b', None), ('ln2_g', None), ('ln2_b', None), ('ffn_conv_w', 2),
         ('ffn_conv_b', None), ('ple_b_gate', None))
N_CHIPS = 4
N_DEVICES = 8


def _params():
    return pltpu.CompilerParams(vmem_limit_bytes=VMEM_LIMIT_BYTES)


def _tile(n, cap, *offsets, unit=LANE):
    g = n
    for o in offsets:
        if o:
            g = math.gcd(g, o)
    best = 0
    for d in range(unit, min(g, cap) + 1, unit):
        if g % d == 0:
            best = d
    if best:
        return best
    assert not any(offsets), (n, offsets)
    return n


def _dot(a, b, dims):
    return lax.dot_general(a.astype(MXU_DTYPE), b.astype(MXU_DTYPE), (dims, ((), ())),
                           preferred_element_type=F32)


NN = ((1,), (0,))
NT = ((1,), (1,))
TN = ((0,), (0,))


def mm(name, a, b, mode, out_dtype, *, a_pre=(), b_pre=(), b_win=None, acc_in=None, out_stack=None):
    a2 = a.shape[len(a_pre):]
    b2 = b.shape[len(b_pre):]
    br0, brn, bc0, bcn = b_win or (0, b2[0], 0, b2[1])
    if mode == 'nn':
        (M, K), N, dims = a2, bcn, NN
        assert brn == K
    elif mode == 'nt':
        (M, K), N, dims = a2, brn, NT
        assert bcn == K
    else:
        (K, M), N, dims = a2, bcn, TN
        assert brn == K
    oc0 = out_stack[3] if out_stack else 0
    tm = _tile(M, 512)
    if mode == 'nn':
        tn, tk = _tile(N, 1408, bc0, oc0), _tile(K, 1408, br0)
    elif mode == 'nt':
        tn, tk = _tile(N, 1408, br0, oc0), _tile(K, 1408, bc0)
    else:
        tn, tk = _tile(N, 1408, bc0, oc0), _tile(K, 1024, br0)
    nk = K // tk
    grid = (M // tm, N // tn, nk)

    na, nb = len(a_pre), len(b_pre)
    if mode == 'tn':
        a_spec = pl.BlockSpec((None,) * na + (tk, tm), lambda i, j, k: a_pre + (k, i))
    else:
        a_spec = pl.BlockSpec((None,) * na + (tm, tk), lambda i, j, k: a_pre + (i, k))
    if mode == 'nt':
        b_spec = pl.BlockSpec((None,) * nb + (tn, tk), lambda i, j, k: b_pre + (j + br0 // tn, k + bc0 // tk))
    else:
        b_spec = pl.BlockSpec((None,) * nb + (tk, tn), lambda i, j, k: b_pre + (k + br0 // tk, j + bc0 // tn))
    in_specs, args = [a_spec, b_spec], [a, b]
    if acc_in is not None:
        in_specs.append(pl.BlockSpec((tm, tn), lambda i, j, k: (i, j)))
        args.append(acc_in)
    aliases = {}
    if out_stack is None:
        out_shape = jax.ShapeDtypeStruct((M, N), out_dtype)
        out_spec = pl.BlockSpec((tm, tn), lambda i, j, k: (i, j))
    else:
        buf, full_shape, lead, _ = out_stack
        out_shape = jax.ShapeDtypeStruct(full_shape, out_dtype)
        out_spec = pl.BlockSpec((None, tm, tn), lambda i, j, k: (lead, i, j + oc0 // tn))
        if buf is not None:
            aliases = {len(args): 0}
            in_specs.append(ANY_SPEC)
            args.append(buf)
    has_c, has_alias = acc_in is not None, bool(aliases)

    def body(*refs):
        a_ref, b_ref = refs[0], refs[1]
        c_ref = refs[2] if has_c else None
        o_ref = refs[2 + has_c + has_alias]
        prod = _dot(a_ref[...], b_ref[...], dims)
        if nk == 1:
            if has_c:
                prod = prod + c_ref[...]
            o_ref[...] = prod.astype(out_dtype)
            return
        acc_ref = refs[3 + has_c + has_alias]
        k = pl.program_id(2)

        @pl.when(k == 0)
        def _():
            acc_ref[...] = prod + c_ref[...] if has_c else prod

        @pl.when(k > 0)
        def _():
            acc_ref[...] += prod

        @pl.when(k == nk - 1)
        def _():
            o_ref[...] = acc_ref[...].astype(out_dtype)

    scratch = [pltpu.VMEM((tm, tn), F32)] if nk > 1 else []
    return pl.pallas_call(body, out_shape=out_shape, grid=grid, in_specs=in_specs, out_specs=out_spec,
                          scratch_shapes=scratch, input_output_aliases=aliases, name=name,
                          compiler_params=_params())(*args)


def rowcall(name, body, n_rows, ts, row_ins, full_ins, row_outs, acc_outs=()):
    in_specs, args = [], []
    for arr, width, colblk, pre in row_ins:
        in_specs.append(pl.BlockSpec((None,) * len(pre) + (ts, width),
                                     lambda i, pre=pre, cb=colblk: pre + (i, cb)))
        args.append(arr)
    for arr in full_ins:
        in_specs.append(pl.BlockSpec(arr.shape, lambda i, nd=arr.ndim: (0,) * nd))
        args.append(arr)
    out_shape, out_specs = [], []
    for width, dtype in row_outs:
        out_shape.append(jax.ShapeDtypeStruct((n_rows, width), dtype))
        out_specs.append(pl.BlockSpec((ts, width), lambda i: (i, 0)))
    for shape in acc_outs:
        out_shape.append(jax.ShapeDtypeStruct(shape, F32))
        out_specs.append(pl.BlockSpec(shape, lambda i, nd=len(shape): (0,) * nd))
    n_in = len(args)

    def kern(*refs):
        body(pl.program_id(0), refs[:n_in], refs[n_in:])

    return pl.pallas_call(kern, out_shape=tuple(out_shape), grid=(n_rows // ts,), in_specs=in_specs,
                          out_specs=tuple(out_specs), name=name, compiler_params=_params())(*args)


def _row(arr, width=None, colblk=0, pre=()):
    return (arr, arr.shape[-1] if width is None else width, colblk, pre)


def _init_acc(i, refs):
    @pl.when(i == 0)
    def _():
        for r in refs:
            r[...] = jnp.zeros(r.shape, r.dtype)


def _colsum(v):
    return jnp.sum(v, axis=0, keepdims=True)


def _sigmoid(z):
    return 1.0 / (1.0 + jnp.exp(-z))


def _row_tile(S):
    return _tile(S, 256, unit=16)


def _ln_stats(x_ref, m_ref):
    z = DN_ALPHA * x_ref[...] + m_ref[...]
    mu = jnp.mean(z, -1, keepdims=True)
    zc = z - mu
    var = jnp.mean(zc * zc, -1, keepdims=True)
    r = lax.rsqrt(var + EPS)
    return zc * r, r


def ln_fwd(name, x, m, g, b):
    S, D = x.shape

    def body(i, ins, outs):
        x_ref, m_ref, g_ref, b_ref = ins
        xh, _ = _ln_stats(x_ref, m_ref)
        y = xh * g_ref[...] + b_ref[...]
        outs[0][...] = y
        outs[1][...] = y.astype(MXU_DTYPE)

    return rowcall(name, body, S, _row_tile(S), [_row(x), _row(m)], [g, b], [(D, F32), (D, MXU_DTYPE)])


def ln_bwd(name, x, m, dA, ca, dB, g):
    S, D = x.shape
    has_b = dB is not None

    def body(i, ins, outs):
        x_ref, m_ref, a_ref = ins[:3]
        g_ref = ins[-1]
        dz_ref, dzb_ref, dg_ref, db_ref = outs
        _init_acc(i, (dg_ref, db_ref))
        dy = ca * a_ref[...]
        if has_b:
            dy = dy + ins[3][...]
        xh, r = _ln_stats(x_ref, m_ref)
        dg_ref[...] += _colsum(dy * xh)
        db_ref[...] += _colsum(dy)
        dxh = dy * g_ref[...]
        dz = r * (dxh - jnp.mean(dxh, -1, keepdims=True) - xh * jnp.mean(dxh * xh, -1, keepdims=True))
        dz_ref[...] = dz
        dzb_ref[...] = dz.astype(MXU_DTYPE)

    rows = [_row(x), _row(m), _row(dA)] + ([_row(dB)] if has_b else [])
    return rowcall(name, body, S, _row_tile(S), rows, [g], [(D, F32), (D, MXU_DTYPE)], [(1, D), (1, D)])


def ple_fwd(name, x2, gp, pp, bias):
    S, D = x2.shape

    def body(i, ins, outs):
        x_ref, gp_ref, pp_ref, b_ref = ins
        y = x_ref[...] + _sigmoid(gp_ref[...] + b_ref[...]) * pp_ref[...]
        outs[0][...] = y
        outs[1][...] = y.astype(MXU_DTYPE)

    return rowcall(name, body, S, _row_tile(S), [_row(x2), _row(gp), _row(pp)], [bias],
                   [(D, F32), (D, MXU_DTYPE)])


def ple_bwd(name, dA, ca, dB, gp, pp, bias):
    S, D = gp.shape
    has_b = dB is not None

    def body(i, ins, outs):
        a_ref = ins[0]
        gp_ref, pp_ref, b_ref = ins[-3:]
        dx_ref, dpp_ref, dgp_ref, db_ref = outs
        _init_acc(i, (db_ref,))
        dx = ca * a_ref[...]
        if has_b:
            dx = dx + ins[1][...]
        gate = _sigmoid(gp_ref[...] + b_ref[...])
        dgp = dx * pp_ref[...] * gate * (1.0 - gate)
        dx_ref[...] = dx
        dpp_ref[...] = (dx * gate).astype(MXU_DTYPE)
        dgp_ref[...] = dgp.astype(MXU_DTYPE)
        db_ref[...] += _colsum(dgp)

    rows = [_row(dA)] + ([_row(dB)] if has_b else []) + [_row(gp), _row(pp)]
    return rowcall(name, body, S, _row_tile(S), rows, [bias],
                   [(D, F32), (D, MXU_DTYPE), (D, MXU_DTYPE)], [(1, D)])


def loss_head(name, y, target):
    S, D = y.shape

    def body(i, ins, outs):
        _init_acc(i, (outs[1],))
        e = ins[0][...] - ins[1][...]
        outs[0][...] = e * (1.0 / D)
        outs[1][...] += _colsum(e * e)

    return rowcall(name, body, S, _row_tile(S), [_row(y), _row(target)], [], [(D, F32)], [(1, D)])


def axpy(name, a, ca, b):
    S, D = a.shape

    def body(i, ins, outs):
        outs[0][...] = ca * ins[0][...] + ins[1][...]

    return rowcall(name, body, S, _row_tile(S), [_row(a), _row(b)], [], [(D, F32)])[0]


HALF_ROPE = MLA_ROPE // 2


def _rope(x, c, sa, sb):
    n = x.shape[-1]
    return x * c + pltpu.roll(x, n - HALF_ROPE, 1) * sa + pltpu.roll(x, HALF_ROPE, 1) * sb


def _rope_t(d, c, sa, sb):
    n = d.shape[-1]
    return d * c + pltpu.roll(d * sa, HALF_ROPE, 1) + pltpu.roll(d * sb, n - HALF_ROPE, 1)


def rope_tables(positions_row, n_lead):
    inv = 1.0 / (ROPE_THETA ** (jnp.arange(0, MLA_ROPE, 2, dtype=F32) / MLA_ROPE))
    ang = positions_row.astype(F32)[:, None] * inv
    cos, sin = jnp.cos(ang), jnp.sin(ang)
    S = cos.shape[0]
    z = jnp.zeros((S, HALF_ROPE), F32)
    tail = jnp.zeros((S, LANE - MLA_ROPE), F32)
    c = jnp.concatenate([jnp.ones((S, n_lead), F32), cos, cos, tail], -1)
    sa = jnp.concatenate([jnp.zeros((S, n_lead), F32), -sin, z, tail], -1)
    sb = jnp.concatenate([jnp.zeros((S, n_lead), F32), z, sin, tail], -1)
    return c, sa, sb


def mla_pre_fwd(name, h, qn, kvn, tab_k, QL, KL):
    S = h.shape[0]

    def body(i, ins, outs):
        h_ref, c_ref, sa_ref, sb_ref, qn_ref, kvn_ref = ins
        cq = h_ref[:, 0:QL]
        ckv = h_ref[:, QL:QL + KL]
        kr = h_ref[:, QL + KL:QL + KL + LANE]
        outs[0][...] = (cq * lax.rsqrt(jnp.mean(cq * cq, -1, keepdims=True) + EPS) * qn_ref[...]).astype(MXU_DTYPE)
        outs[1][...] = (ckv * lax.rsqrt(jnp.mean(ckv * ckv, -1, keepdims=True) + EPS) * kvn_ref[...]).astype(MXU_DTYPE)
        outs[2][...] = _rope(kr, c_ref[...], sa_ref[...], sb_ref[...]).astype(MXU_DTYPE)

    rows = [_row(h)] + [_row(t) for t in tab_k]
    return rowcall(name, body, S, _row_tile(S), rows, [qn, kvn],
                   [(QL, MXU_DTYPE), (KL, MXU_DTYPE), (LANE, MXU_DTYPE)])


def _rms_bwd(x, g, dy):
    r = lax.rsqrt(jnp.mean(x * x, -1, keepdims=True) + EPS)
    dg = _colsum(dy * x * r)
    dxg = dy * g
    dx = r * dxg - x * (r * r * r) * jnp.mean(dxg * x, -1, keepdims=True)
    return dx, dg


def mla_pre_bwd(name, h, dcq, dckv_a, dckv_b, dkr_heads, qn, kvn, tab_k, QL, KL, H):
    S, HW = h.shape

    def body(i, ins, outs):
        h_ref, dcq_ref, da_ref, db_ref, dkr_ref, c_ref, sa_ref, sb_ref, qn_ref, kvn_ref = ins
        dh_ref, dqn_ref, dkvn_ref = outs
        _init_acc(i, (dqn_ref, dkvn_ref))
        dx, dg = _rms_bwd(h_ref[:, 0:QL], qn_ref[...], dcq_ref[...])
        dh_ref[:, 0:QL] = dx.astype(MXU_DTYPE)
        dqn_ref[...] += dg
        dx, dg = _rms_bwd(h_ref[:, QL:QL + KL], kvn_ref[...], da_ref[...] + db_ref[...])
        dh_ref[:, QL:QL + KL] = dx.astype(MXU_DTYPE)
        dkvn_ref[...] += dg
        d = dkr_ref[:, 0:LANE]
        for hh in range(1, H):
            d = d + dkr_ref[:, hh * LANE:(hh + 1) * LANE]
        dh_ref[:, QL + KL:QL + KL + LANE] = _rope_t(d, c_ref[...], sa_ref[...], sb_ref[...]).astype(MXU_DTYPE)

    rows = [_row(h), _row(dcq), _row(dckv_a), _row(dckv_b), _row(dkr_heads)] + [_row(t) for t in tab_k]
    return rowcall(name, body, S, _row_tile(S), rows, [qn, kvn], [(HW, MXU_DTYPE)], [(1, QL), (1, KL)])


def rope_q(name, qpre, tab_q, H, out_dtype, transpose=False):
    S = qpre.shape[0]
    fn = _rope_t if transpose else _rope

    def body(i, ins, outs):
        c, sa, sb = ins[1][...], ins[2][...], ins[3][...]
        for hh in range(H):
            sl = slice(hh * HEAD_PAD, (hh + 1) * HEAD_PAD)
            outs[0][:, sl] = fn(ins[0][:, sl].astype(F32), c, sa, sb).astype(out_dtype)

    rows = [_row(qpre)] + [_row(t) for t in tab_q]
    return rowcall(name, body, S, _row_tile(S), rows, [], [(H * HEAD_PAD, out_dtype)])[0]


ATT_T = 256


def _att_mask(cq_col, ck_row, q0, k0, tq, tk):
    qpos = q0 + lax.broadcasted_iota(jnp.int32, (tq, 1), 0)
    kpos = k0 + lax.broadcasted_iota(jnp.int32, (1, tk), 1)
    return (ck_row <= cq_col) & (kpos <= (qpos | (Q_BLOCK - 1)))


def attn_fwd(name, q, kn, kr, v, cid_col, cid_blk, H):
    S = q.shape[0]
    t = _tile(S, ATT_T)
    nq = S // t
    scale = (MLA_NOPE + MLA_ROPE) ** -0.5

    def body(q_ref, kn_ref, kr_ref, v_ref, cc_ref, cr_ref, o_ref, lse_ref, k_scr):
        i = pl.program_id(1)

        @pl.when(i == 0)
        def _():
            k_scr[:, 0:LANE] = kn_ref[...]
            k_scr[:, LANE:HEAD_PAD] = kr_ref[...]

        qv = q_ref[...]
        cq = cc_ref[...]

        def step(j, carry):
            m, l, acc = carry
            k0 = pl.multiple_of(j * t, t)
            s = _dot(qv, k_scr[pl.ds(k0, t), :], NT) * scale
            s = jnp.where(_att_mask(cq, cr_ref[j], i * t, k0, t, t), s, NEG_INF)
            m_new = jnp.maximum(m, jnp.max(s, -1, keepdims=True))
            p = jnp.exp(s - m_new)
            alpha = jnp.exp(m - m_new)
            l = alpha * l + jnp.sum(p, -1, keepdims=True)
            acc = alpha * acc + _dot(p, v_ref[pl.ds(k0, t), :], NN)
            return m_new, l, acc

        init = (jnp.full((t, 1), NEG_INF, F32), jnp.zeros((t, 1), F32), jnp.zeros((t, MLA_V), F32))
        m, l, acc = lax.fori_loop(0, i + 1, step, init)
        o_ref[...] = (acc / l).astype(o_ref.dtype)
        lse_ref[...] = jnp.broadcast_to(m + jnp.log(l), (t, LANE))

    return pl.pallas_call(
        body, name=name, grid=(H, nq),
        out_shape=(jax.ShapeDtypeStruct((S, H * MLA_V), MXU_DTYPE), jax.ShapeDtypeStruct((S, H * LANE), F32)),
        in_specs=[pl.BlockSpec((t, HEAD_PAD), lambda h, i: (i, h)),
                  pl.BlockSpec((S, MLA_NOPE), lambda h, i: (0, h)),
                  pl.BlockSpec((S, LANE), lambda h, i: (0, 0)),
                  pl.BlockSpec((S, MLA_V), lambda h, i: (0, h)),
                  pl.BlockSpec((t, 1), lambda h, i: (i, 0)),
                  pl.BlockSpec(cid_blk.shape, lambda h, i: (0, 0, 0))],
        out_specs=(pl.BlockSpec((t, MLA_V), lambda h, i: (i, h)), pl.BlockSpec((t, LANE), lambda h, i: (i, h))),
        scratch_shapes=[pltpu.VMEM((S, HEAD_PAD), MXU_DTYPE)],
        compiler_params=_params())(q, kn, kr, v, cid_col, cid_blk)


def attn_bwd(name, q, kn, kr, v, o, lse, do, cid_col, cid_blk, H):
    S = q.shape[0]
    t = _tile(S, ATT_T)
    nq = S // t
    scale = (MLA_NOPE + MLA_ROPE) ** -0.5

    def body(q_ref, kn_ref, kr_ref, v_ref, o_ref, lse_ref, do_ref, cc_ref, cr_ref,
             dq_ref, dkn_ref, dv_ref, dkr_ref, delta_scr):
        j = pl.program_id(1)

        @pl.when(j == 0)
        def _():
            dq_ref[...] = jnp.zeros(dq_ref.shape, F32)
            delta_scr[...] = jnp.sum(do_ref[...].astype(F32) * o_ref[...].astype(F32), -1, keepdims=True)

        kv = jnp.concatenate([kn_ref[...], kr_ref[...]], axis=-1)
        vv = v_ref[...]
        ck = cr_ref[j]
        k0 = j * t

        def step(i, carry):
            dk, dv = carry
            q0 = pl.multiple_of(i * t, t)
            qv = q_ref[pl.ds(q0, t), :]
            dov = do_ref[pl.ds(q0, t), :]
            s = _dot(qv, kv, NT) * scale
            s = jnp.where(_att_mask(cc_ref[pl.ds(q0, t), :], ck, q0, k0, t, t), s, NEG_INF)
            p = jnp.exp(s - lse_ref[pl.ds(q0, t), 0:1])
            dv = dv + _dot(p, dov, TN)
            dp = _dot(dov, vv, NT)
            ds = p * (dp - delta_scr[pl.ds(q0, t), :]) * scale
            dk = dk + _dot(ds, qv, TN)
            dq_ref[pl.ds(q0, t), :] += _dot(ds, kv, NN)
            return dk, dv

        dk, dv = lax.fori_loop(j, nq, step, (jnp.zeros((t, HEAD_PAD), F32), jnp.zeros((t, MLA_V), F32)))
        dkn_ref[...] = dk[:, 0:LANE].astype(dkn_ref.dtype)
        dkr_ref[...] = dk[:, LANE:HEAD_PAD]
        dv_ref[...] = dv.astype(dv_ref.dtype)

    head_rows = lambda w: pl.BlockSpec((S, w), lambda h, j: (0, h))
    tile_rows = lambda w: pl.BlockSpec((t, w), lambda h, j: (j, h))
    return pl.pallas_call(
        body, name=name, grid=(H, nq),
        out_shape=(jax.ShapeDtypeStruct((S, H * HEAD_PAD), F32), jax.ShapeDtypeStruct((S, H * MLA_NOPE), MXU_DTYPE),
                   jax.ShapeDtypeStruct((S, H * MLA_V), MXU_DTYPE), jax.ShapeDtypeStruct((S, H * LANE), F32)),
        in_specs=[head_rows(HEAD_PAD), tile_rows(MLA_NOPE), pl.BlockSpec((t, LANE), lambda h, j: (j, 0)),
                  tile_rows(MLA_V), head_rows(MLA_V), head_rows(LANE), head_rows(MLA_V),
                  pl.BlockSpec((S, 1), lambda h, j: (0, 0)),
                  pl.BlockSpec(cid_blk.shape, lambda h, j: (0, 0, 0))],
        out_specs=(head_rows(HEAD_PAD), tile_rows(MLA_NOPE), tile_rows(MLA_V), tile_rows(LANE)),
        scratch_shapes=[pltpu.VMEM((S, 1), F32)],
        compiler_params=_params())(q, kn, kr, v, o, lse, do, cid_col, cid_blk)


GLA_GROUP = 8


def _prefix_rows(x):
    n = x.shape[0]
    row = lax.broadcasted_iota(jnp.int32, x.shape, 0)
    d = 1
    while d < n:
        x = x + jnp.where(row >= d, pltpu.roll(x, d, 0), 0.0)
        d *= 2
    return x


def _suffix_rows(x):
    n = x.shape[0]
    row = lax.broadcasted_iota(jnp.int32, x.shape, 0)
    d = 1
    while d < n:
        x = x + jnp.where(row < n - d, pltpu.roll(x, n - d, 0), 0.0)
        d *= 2
    return x


def _log_sigmoid(z):
    return jnp.minimum(z, 0.0) - jnp.log(1.0 + jnp.exp(-jnp.abs(z)))


def gla_pre_fwd(name, h, w2p, b_a, QK, a_blk):
    S = h.shape[0]
    dk = QK // GLA_HEADS

    def body(i, ins, outs):
        q_ref, a_ref, w_ref, b_ref = ins
        outs[0][...] = (q_ref[...] * (dk ** -0.5)).astype(MXU_DTYPE)
        z = _dot(a_ref[...], w_ref[...], NN) + b_ref[...]
        outs[1][...] = _log_sigmoid(z) / GLA_TAU

    return rowcall(name, body, S, _row_tile(S), [_row(h, QK, 0), _row(h, LANE, a_blk)], [w2p, b_a],
                   [(QK, MXU_DTYPE), (QK, F32)])


def _gla_specs(S, QK, VD, rows, gmap):
    dk, dv = QK // GLA_HEADS, VD // GLA_HEADS
    return dict(
        qs=pl.BlockSpec((rows, dk), lambda h, g: (gmap(g), h)),
        k=pl.BlockSpec((rows, dk), lambda h, g: (gmap(g), QK // dk + h)),
        v=pl.BlockSpec((rows, dv), lambda h, g: (gmap(g), 2 * QK // dv + h)),
        la=pl.BlockSpec((rows, dk), lambda h, g: (gmap(g), h)),
        o=pl.BlockSpec((rows, dv), lambda h, g: (gmap(g), h)))


def gla_fwd(name, qs, h, la, QK, VD):
    S = qs.shape[0]
    dk, dv = QK // GLA_HEADS, VD // GLA_HEADS
    n_chunks = S // CHUNK
    cg = min(GLA_GROUP, n_chunks)
    rows = cg * CHUNK
    sp = _gla_specs(S, QK, VD, rows, lambda g: g)

    def body(q_ref, k_ref, v_ref, la_ref, o_ref, st_ref, state):
        @pl.when(pl.program_id(1) == 0)
        def _():
            state[...] = jnp.zeros(state.shape, F32)

        for c in range(cg):
            sl = slice(c * CHUNK, (c + 1) * CHUNK)
            cum = _prefix_rows(la_ref[sl, :])
            tot = cum[CHUNK - 1:CHUNK, :]
            kdec = k_ref[sl, :] * jnp.exp(tot - cum)
            st = state[...] * jnp.exp(tot) + _dot(v_ref[sl, :], kdec, TN)
            state[...] = st
            st_ref[c] = st
            o_ref[sl, :] = _dot(q_ref[sl, :], st, NT)

    return pl.pallas_call(
        body, name=name, grid=(GLA_HEADS, n_chunks // cg),
        out_shape=(jax.ShapeDtypeStruct((S, VD), F32), jax.ShapeDtypeStruct((GLA_HEADS, n_chunks, dv, dk), F32)),
        in_specs=[sp['qs'], sp['k'], sp['v'], sp['la']],
        out_specs=(sp['o'], pl.BlockSpec((None, cg, dv, dk), lambda h, g: (h, g, 0, 0))),
        scratch_shapes=[pltpu.VMEM((dv, dk), F32)],
        compiler_params=_params())(qs, h, h, la)


def gla_bwd(name, qs, h, la, states, do, QK, VD):
    S = qs.shape[0]
    dk, dv = QK // GLA_HEADS, VD // GLA_HEADS
    n_chunks = S // CHUNK
    cg = min(GLA_GROUP, n_chunks)
    ng = n_chunks // cg
    rows = cg * CHUNK
    rev = lambda g: ng - 1 - g
    sp = _gla_specs(S, QK, VD, rows, rev)

    def body(q_ref, k_ref, v_ref, la_ref, st_ref, prev_ref, do_ref, dq_ref, dk_ref, dv_ref, dla_ref, dst):
        g = pl.program_id(1)

        @pl.when(g == 0)
        def _():
            dst[...] = jnp.zeros(dst.shape, F32)

        first_group = (g == ng - 1).astype(F32)
        for c in reversed(range(cg)):
            sl = slice(c * CHUNK, (c + 1) * CHUNK)
            cum = _prefix_rows(la_ref[sl, :])
            tot = cum[CHUNK - 1:CHUNK, :]
            e = jnp.exp(tot - cum)
            kdec = k_ref[sl, :] * e
            decay = jnp.exp(tot)
            st = st_ref[c]
            st_prev = st_ref[c - 1] if c > 0 else prev_ref[0] * (1.0 - first_group)
            dov = do_ref[sl, :]
            qv = q_ref[sl, :]
            dq_ref[sl, :] = (_dot(dov, st, NN) * (dk ** -0.5)).astype(dq_ref.dtype)
            d = dst[...] + _dot(dov, qv, TN)
            ddecay = _colsum(d * st_prev)
            dkdec = _dot(v_ref[sl, :], d, NN)
            dv_ref[sl, :] = _dot(kdec, d, NT).astype(dv_ref.dtype)
            dk_ref[sl, :] = (dkdec * e).astype(dk_ref.dtype)
            darg = dkdec * kdec
            dtot = _colsum(darg) + ddecay * decay
            dla_ref[sl, :] = dtot - _suffix_rows(darg)
            dst[...] = d * decay

    prev_spec = pl.BlockSpec((None, 1, dv, dk), lambda h, g: (h, jnp.maximum(rev(g) * cg - 1, 0), 0, 0))
    return pl.pallas_call(
        body, name=name, grid=(GLA_HEADS, ng),
        out_shape=(jax.ShapeDtypeStruct((S, QK), MXU_DTYPE), jax.ShapeDtypeStruct((S, QK), MXU_DTYPE),
                   jax.ShapeDtypeStruct((S, VD), MXU_DTYPE), jax.ShapeDtypeStruct((S, QK), F32)),
        in_specs=[sp['qs'], sp['k'], sp['v'], sp['la'],
                  pl.BlockSpec((None, cg, dv, dk), lambda h, g: (h, rev(g), 0, 0)), prev_spec, sp['o']],
        out_specs=(sp['qs'], sp['qs'], sp['o'], sp['la']),
        scratch_shapes=[pltpu.VMEM((dv, dk), F32)],
        compiler_params=_params())(qs, h, h, la, states, states, do)


def _head_norm(o):
    mu = jnp.mean(o, -1, keepdims=True)
    oc = o - mu
    r = lax.rsqrt(jnp.mean(oc * oc, -1, keepdims=True) + EPS)
    return oc * r, r


def gla_post_fwd(name, o, h, o_norm, VD, r_blk):
    S = o.shape[0]
    dv = VD // GLA_HEADS

    def body(i, ins, outs):
        o_ref, r_ref, w_ref = ins
        for hh in range(GLA_HEADS):
            sl = slice(hh * dv, (hh + 1) * dv)
            xh, _ = _head_norm(o_ref[:, sl])
            r = r_ref[:, sl]
            outs[0][:, sl] = (xh * w_ref[:, sl] * (r * _sigmoid(r))).astype(MXU_DTYPE)

    return rowcall(name, body, S, _row_tile(S), [_row(o), _row(h, VD, r_blk)], [o_norm], [(VD, MXU_DTYPE)])[0]


def gla_post_bwd(name, o, h, dog, o_norm, VD, r_blk):
    S = o.shape[0]
    dv = VD // GLA_HEADS

    def body(i, ins, outs):
        o_ref, r_ref, dog_ref, w_ref = ins
        do_ref, dr_ref, dw_ref = outs
        _init_acc(i, (dw_ref,))
        for hh in range(GLA_HEADS):
            sl = slice(hh * dv, (hh + 1) * dv)
            xh, rs = _head_norm(o_ref[:, sl])
            r = r_ref[:, sl]
            w = w_ref[:, sl]
            dog = dog_ref[:, sl]
            sg = _sigmoid(r)
            dn = dog * (r * sg)
            dr_ref[:, sl] = (dog * (xh * w) * (sg * (1.0 + r * (1.0 - sg)))).astype(MXU_DTYPE)
            dw_ref[:, sl] += _colsum(dn * xh)
            dxh = dn * w
            do_ref[:, sl] = rs * (dxh - jnp.mean(dxh, -1, keepdims=True) - xh * jnp.mean(dxh * xh, -1, keepdims=True))

    return rowcall(name, body, S, _row_tile(S), [_row(o), _row(h, VD, r_blk), _row(dog)], [o_norm],
                   [(VD, F32), (VD, MXU_DTYPE)], [(1, VD)])


def gla_dh(name, dq, dk, dv, dr, dla, h, w2p, b_a, QK, VD, a_blk, HW):
    S = dq.shape[0]

    def body(i, ins, outs):
        dq_ref, dk_ref, dv_ref, dr_ref, dla_ref, a_ref, w_ref, b_ref = ins
        dh_ref, dw_ref, db_ref = outs
        _init_acc(i, (dw_ref, db_ref))
        a = a_ref[...]
        z = _dot(a, w_ref[...], NN) + b_ref[...]
        dz = dla_ref[...] * (1.0 / GLA_TAU) * _sigmoid(-z)
        dw_ref[...] += _dot(a, dz, TN)
        db_ref[...] += _colsum(dz)
        dh_ref[:, 0:QK] = dq_ref[...]
        dh_ref[:, QK:2 * QK] = dk_ref[...]
        dh_ref[:, 2 * QK:2 * QK + VD] = dv_ref[...]
        dh_ref[:, 2 * QK + VD:2 * QK + 2 * VD] = dr_ref[...]
        dh_ref[:, 2 * QK + 2 * VD:HW] = _dot(dz, w_ref[...], NT).astype(MXU_DTYPE)

    rows = [_row(dq), _row(dk), _row(dv), _row(dr), _row(dla), _row(h, LANE, a_blk)]
    return rowcall(name, body, S, _row_tile(S), rows, [w2p, b_a], [(HW, MXU_DTYPE)], [(LANE, QK), (1, QK)])


CONV_ROWS = 512
HALO = 8


def _gelu(x):
    return 0.5 * x * (1.0 + jnp.tanh(GELU_C * (x + GELU_A * x * x * x)))


def _gelu_grad(x):
    t = jnp.tanh(GELU_C * (x + GELU_A * x * x * x))
    return 0.5 * (1.0 + t) + 0.5 * x * (1.0 - t * t) * GELU_C * (1.0 + 3.0 * GELU_A * x * x)


def _shift_down(x, prev, d):
    row = lax.broadcasted_iota(jnp.int32, x.shape, 0)
    out = pltpu.roll(x, d, 0)
    for t in range(d):
        out = jnp.where(row == t, prev[HALO - d + t:HALO - d + t + 1, :], out)
    return out


def _shift_up(x, nxt, d):
    n = x.shape[0]
    row = lax.broadcasted_iota(jnp.int32, x.shape, 0)
    out = pltpu.roll(x, n - d, 0)
    for t in range(d):
        out = jnp.where(row == n - d + t, nxt[t:t + 1, :], out)
    return out


def _conv_taps(ref, r, rc):
    x = ref[r * rc:(r + 1) * rc, :]
    prev = ref[r * rc - HALO:r * rc, :] if r > 0 else jnp.zeros((HALO, x.shape[1]), F32)
    return x, _shift_down(x, prev, 1), _shift_down(x, prev, 2)


def _conv_apply(taps, w_ref, b_ref):
    x0, x1, x2 = taps
    return x2 * w_ref[0:1, :] + x1 * w_ref[1:2, :] + x0 * w_ref[2:3, :] + b_ref[...]


def conv_fwd(name, hu, hg, cw_u, cw_g, cb_u, cb_g):
    S, F = hu.shape
    tc = LANE
    rc = _tile(S, CONV_ROWS, unit=16)

    def body(u_ref, g_ref, wu_ref, wg_ref, bu_ref, bg_ref, a_ref):
        for r in range(S // rc):
            uc = _conv_apply(_conv_taps(u_ref, r, rc), wu_ref, bu_ref)
            gc = _conv_apply(_conv_taps(g_ref, r, rc), wg_ref, bg_ref)
            a_ref[r * rc:(r + 1) * rc, :] = (uc * _gelu(gc)).astype(a_ref.dtype)

    col = lambda rows: pl.BlockSpec((rows, tc), lambda j: (0, j))
    return pl.pallas_call(
        body, name=name, grid=(F // tc,), out_shape=jax.ShapeDtypeStruct((S, F), MXU_DTYPE),
        in_specs=[col(S), col(S), col(3), col(3), col(1), col(1)], out_specs=col(S),
        compiler_params=_params())(hu, hg, cw_u, cw_g, cb_u, cb_g)


def conv_bwd(name, hu, hg, da, cw_u, cw_g, cb_u, cb_g):
    S, F = hu.shape
    tc = LANE
    rc = _tile(S, CONV_ROWS, unit=16)
    nr = S // rc

    def body(u_ref, g_ref, da_ref, wu_ref, wg_ref, bu_ref, bg_ref,
             dhu_ref, dhg_ref, dwu_ref, dwg_ref, dbu_ref, dbg_ref, du_scr, dg_scr):
        dw = [[jnp.zeros((1, tc), F32) for _ in range(3)] for _ in range(2)]
        db = [jnp.zeros((1, tc), F32) for _ in range(2)]
        for r in range(nr):
            sl = slice(r * rc, (r + 1) * rc)
            ut = _conv_taps(u_ref, r, rc)
            gt = _conv_taps(g_ref, r, rc)
            uc = _conv_apply(ut, wu_ref, bu_ref)
            gc = _conv_apply(gt, wg_ref, bg_ref)
            dav = da_ref[sl, :]
            duc = dav * _gelu(gc)
            dgc = dav * uc * _gelu_grad(gc)
            du_scr[sl, :] = duc
            dg_scr[sl, :] = dgc
            for part, (taps, d) in enumerate(((ut, duc), (gt, dgc))):
                db[part] = db[part] + _colsum(d)
                for tap in range(3):
                    dw[part][tap] = dw[part][tap] + _colsum(taps[2 - tap] * d)
        for part, (w_out, b_out) in enumerate(((dwu_ref, dbu_ref), (dwg_ref, dbg_ref))):
            b_out[...] = db[part]
            for tap in range(3):
                w_out[tap:tap + 1, :] = dw[part][tap]
        for scr, w_ref, out in ((du_scr, wu_ref, dhu_ref), (dg_scr, wg_ref, dhg_ref)):
            for r in range(nr):
                d = scr[r * rc:(r + 1) * rc, :]
                nxt = scr[(r + 1) * rc:(r + 1) * rc + HALO, :] if r + 1 < nr else jnp.zeros((HALO, tc), F32)
                dh = d * w_ref[2:3, :] + _shift_up(d, nxt, 1) * w_ref[1:2, :] + _shift_up(d, nxt, 2) * w_ref[0:1, :]
                out[r * rc:(r + 1) * rc, :] = dh.astype(out.dtype)

    col = lambda rows: pl.BlockSpec((rows, tc), lambda j: (0, j))
    sds = jax.ShapeDtypeStruct
    return pl.pallas_call(
        body, name=name, grid=(F // tc,),
        out_shape=(sds((S, F), MXU_DTYPE), sds((S, F), MXU_DTYPE), sds((3, F), F32), sds((3, F), F32),
                   sds((1, F), F32), sds((1, F), F32)),
        in_specs=[col(S), col(S), col(S), col(3), col(3), col(1), col(1)],
        out_specs=(col(S), col(S), col(3), col(3), col(1), col(1)),
        scratch_shapes=[pltpu.VMEM((S, tc), F32), pltpu.VMEM((S, tc), F32)],
        compiler_params=_params())(hu, hg, da, cw_u, cw_g, cb_u, cb_g)


def adamw(name, w, g, m, v):
    R, C = w.shape
    tr = _tile(R, max(8, (1 << 19) // max(C, 1) // 8 * 8), unit=8)

    def body(w_ref, g_ref, m_ref, v_ref, d_ref, nm_ref, nv_ref):
        gv = g_ref[...]
        mn = ADAM_B1 * m_ref[...] + (1.0 - ADAM_B1) * gv
        vn = ADAM_B2 * v_ref[...] + (1.0 - ADAM_B2) * (gv * gv)
        m_hat = mn / (1.0 - ADAM_B1 ** ADAM_STEP)
        v_hat = vn / (1.0 - ADAM_B2 ** ADAM_STEP)
        d_ref[...] = -ADAM_LR * (m_hat / (jnp.sqrt(v_hat) + ADAM_EPS) + ADAM_WD * w_ref[...])
        nm_ref[...] = mn
        nv_ref[...] = vn

    spec = pl.BlockSpec((tr, C), lambda i: (i, 0))
    shp = jax.ShapeDtypeStruct((R, C), F32)
    return pl.pallas_call(body, name=name, grid=(R // tr,), out_shape=(shp, shp, shp), in_specs=[spec] * 4,
                          out_specs=(spec, spec, spec), compiler_params=_params())(w, g, m, v)


def _position():
    return lax.axis_index('x'), lax.axis_index('y'), lax.axis_index('c')


def _other_chips(x, y):
    return ((1 - x, y), (x, 1 - y), (1 - x, 1 - y))


def _window(ref, dim, lo, n_lead, jj, rs, cs):
    if dim == 1:
        return ref.at[pl.ds(lo, n_lead), pl.ds(pl.multiple_of(jj * rs, 16), rs), :]
    return ref.at[pl.ds(lo, n_lead), :, pl.ds(pl.multiple_of(jj * cs, LANE), cs)]


def _hbm_call(name, body, out_shapes, n_sems, args):
    return pl.pallas_call(
        body, name=name, out_shape=tuple(out_shapes), in_specs=[ANY_SPEC] * len(args),
        out_specs=tuple(ANY_SPEC for _ in out_shapes),
        scratch_shapes=[pltpu.SemaphoreType.DMA((n,)) for n in n_sems],
        compiler_params=pltpu.CompilerParams(has_side_effects=True))(*args)


def gather_weights(shards, dims):
    n = len(shards)
    fulls = []
    for s, d in zip(shards, dims):
        L, rs, cs = s.shape
        fulls.append(jax.ShapeDtypeStruct((L, rs * N_CHIPS, cs) if d == 1 else (L, rs, cs * N_CHIPS), s.dtype))

    def body(*refs):
        sh, full = refs[:n], refs[n:2 * n]
        lsem, ssem, rsem, s2sem, r2sem = refs[2 * n:]
        x, y, c = _position()
        mine = 2 * x + y
        chips = _other_chips(x, y)
        sibling = (x, y, 1 - c)

        def win(w, lo, nl, jj):
            return _window(full[w], dims[w], lo, nl, jj, shards[w].shape[1], shards[w].shape[2])

        local, sends = [], []
        for w in range(n):
            L = shards[w].shape[0]
            half = L // 2
            cp = pltpu.make_async_copy(sh[w], win(w, 0, L, mine), lsem.at[w])
            cp.start()
            local.append(cp)
            for k, (cx, cy) in enumerate(chips):
                cp = pltpu.make_async_remote_copy(
                    src_ref=sh[w].at[pl.ds(c * half, half)], dst_ref=win(w, c * half, half, mine),
                    send_sem=ssem.at[3 * w + k], recv_sem=rsem.at[3 * w + k],
                    device_id=(cx, cy, c), device_id_type=MESH)
                cp.start()
                sends.append(cp)
        for w in range(n):
            half = shards[w].shape[0] // 2
            for k, (cx, cy) in enumerate(chips):
                theirs = win(w, c * half, half, 2 * cx + cy)
                pltpu.make_async_remote_copy(
                    src_ref=sh[w].at[pl.ds(c * half, half)], dst_ref=theirs,
                    send_sem=ssem.at[3 * w + k], recv_sem=rsem.at[3 * w + k],
                    device_id=(cx, cy, c), device_id_type=MESH).wait_recv()
                cp = pltpu.make_async_remote_copy(
                    src_ref=theirs, dst_ref=theirs, send_sem=s2sem.at[3 * w + k], recv_sem=r2sem.at[3 * w + k],
                    device_id=sibling, device_id_type=MESH)
                cp.start()
                sends.append(cp)
        for w in range(n):
            half = shards[w].shape[0] // 2
            for k, (cx, cy) in enumerate(chips):
                other = win(w, (1 - c) * half, half, 2 * cx + cy)
                pltpu.make_async_remote_copy(
                    src_ref=other, dst_ref=other, send_sem=s2sem.at[3 * w + k], recv_sem=r2sem.at[3 * w + k],
                    device_id=sibling, device_id_type=MESH).wait_recv()
        for cp in sends:
            cp.wait_send()
        for cp in local:
            cp.wait()

    return _hbm_call('gather_weights', body, fulls, (n, 3 * n, 3 * n, 3 * n, 3 * n), shards)


def reduce_pair_exchange(grads):
    n = len(grads)
    outs = [jax.ShapeDtypeStruct((g.shape[0] // 2,) + g.shape[1:], g.dtype) for g in grads]

    def body(*refs):
        gr, land = refs[:n], refs[n:2 * n]
        ssem, rsem = refs[2 * n:]
        x, y, c = _position()
        copies = []
        for w in range(n):
            half = grads[w].shape[0] // 2
            cp = pltpu.make_async_remote_copy(
                src_ref=gr[w].at[pl.ds((1 - c) * half, half)], dst_ref=land[w],
                send_sem=ssem.at[w], recv_sem=rsem.at[w], device_id=(x, y, 1 - c), device_id_type=MESH)
            cp.start()
            copies.append(cp)
        for cp in copies:
            cp.wait()

    return _hbm_call('reduce_pair_exchange', body, outs, (n, n), grads)


def pair_add(name, g, landed, c_arr):
    half, R, C = landed.shape
    tr = _tile(R, max(16, (1 << 20) // C // 16 * 16), unit=16)

    def body(c_ref, g_ref, l_ref, o_ref):
        o_ref[...] = (g_ref[...].astype(F32) + l_ref[...].astype(F32)).astype(o_ref.dtype)

    grid_spec = pltpu.PrefetchScalarGridSpec(
        num_scalar_prefetch=1, grid=(half, R // tr),
        in_specs=[pl.BlockSpec((None, tr, C), lambda l, i, c: (c[0] * half + l, i, 0)),
                  pl.BlockSpec((None, tr, C), lambda l, i, c: (l, i, 0))],
        out_specs=pl.BlockSpec((None, tr, C), lambda l, i, c: (l, i, 0)))
    return pl.pallas_call(body, name=name, grid_spec=grid_spec,
                          out_shape=jax.ShapeDtypeStruct(landed.shape, landed.dtype),
                          compiler_params=_params())(c_arr, g, landed)


def reduce_chip_exchange(partials, dims, shard_shapes):
    n = len(partials)
    outs = [jax.ShapeDtypeStruct((3, p.shape[0]) + tuple(s), p.dtype) for p, s in zip(partials, shard_shapes)]

    def body(*refs):
        ps, land = refs[:n], refs[n:2 * n]
        ssem, rsem = refs[2 * n:]
        x, y, c = _position()
        copies = []
        for w in range(n):
            half = partials[w].shape[0]
            rs, cs = shard_shapes[w]
            for k, (cx, cy) in enumerate(_other_chips(x, y)):
                cp = pltpu.make_async_remote_copy(
                    src_ref=_window(ps[w], dims[w], 0, half, 2 * cx + cy, rs, cs), dst_ref=land[w].at[k],
                    send_sem=ssem.at[3 * w + k], recv_sem=rsem.at[3 * w + k],
                    device_id=(cx, cy, c), device_id_type=MESH)
                cp.start()
                copies.append(cp)
        for cp in copies:
            cp.wait()

    return _hbm_call('reduce_chip_exchange', body, outs, (3 * n, 3 * n), partials)


def chip_sum(name, partial, landed, dim, j_arr):
    _, half, rs, cs = landed.shape
    tr = _tile(rs, max(16, (1 << 19) // cs // 16 * 16), unit=16)

    def body(j_ref, p_ref, a_ref, b_ref, c_ref, o_ref):
        o_ref[...] = ((p_ref[...].astype(F32) + a_ref[...].astype(F32)) + b_ref[...].astype(F32)) + c_ref[...].astype(F32)

    if dim == 1:
        own = pl.BlockSpec((None, tr, cs), lambda l, i, j: (l, j[0] * (rs // tr) + i, 0))
    else:
        own = pl.BlockSpec((None, tr, cs), lambda l, i, j: (l, i, j[0]))
    arrived = [pl.BlockSpec((None, None, tr, cs), lambda l, i, j, k=k: (k, l, i, 0)) for k in range(3)]
    grid_spec = pltpu.PrefetchScalarGridSpec(
        num_scalar_prefetch=1, grid=(half, rs // tr), in_specs=[own] + arrived,
        out_specs=pl.BlockSpec((None, tr, cs), lambda l, i, j: (l, i, 0)))
    return pl.pallas_call(body, name=name, grid_spec=grid_spec,
                          out_shape=jax.ShapeDtypeStruct((half, rs, cs), F32),
                          compiler_params=_params())(j_arr, partial, landed, landed, landed)


def share_with_sibling(halves):
    n = len(halves)
    outs = [jax.ShapeDtypeStruct((2 * h.shape[0],) + h.shape[1:], h.dtype) for h in halves]

    def body(*refs):
        hs, full = refs[:n], refs[n:2 * n]
        lsem, ssem, rsem = refs[2 * n:]
        x, y, c = _position()
        copies = []
        for w in range(n):
            half = halves[w].shape[0]
            dst = full[w].at[pl.ds(c * half, half)]
            cp = pltpu.make_async_copy(hs[w], dst, lsem.at[w])
            cp.start()
            copies.append(cp)
            cp = pltpu.make_async_remote_copy(src_ref=hs[w], dst_ref=dst, send_sem=ssem.at[w], recv_sem=rsem.at[w],
                                              device_id=(x, y, 1 - c), device_id_type=MESH)
            cp.start()
            copies.append(cp)
        for w in range(n):
            half = halves[w].shape[0]
            copies[2 * w].wait()
            copies[2 * w + 1].wait_send()
            other = full[w].at[pl.ds((1 - c) * half, half)]
            pltpu.make_async_remote_copy(src_ref=hs[w], dst_ref=other, send_sem=ssem.at[w], recv_sem=rsem.at[w],
                                         device_id=(x, y, 1 - c), device_id_type=MESH).wait_recv()

    return _hbm_call('share_with_sibling', body, outs, (n, n, n), halves)


def exchange_small(name, pack, reduce_all):
    R = pack.shape[0]
    n_slots = N_DEVICES if reduce_all else N_CHIPS
    n_peers = n_slots - 1

    def body(p_ref, o_ref, buf, ssem, rsem):
        x, y, c = _position()
        if reduce_all:
            me = 4 * x + 2 * y + c
            peers = [(x ^ (k >> 2 & 1), y ^ (k >> 1 & 1), c ^ (k & 1)) for k in range(1, N_DEVICES)]
        else:
            me = 2 * x + y
            peers = [(cx, cy, c) for cx, cy in _other_chips(x, y)]
        buf[me] = p_ref[...]
        copies = []
        for k, peer in enumerate(peers):
            cp = pltpu.make_async_remote_copy(src_ref=p_ref, dst_ref=buf.at[me], send_sem=ssem.at[k],
                                              recv_sem=rsem.at[k], device_id=peer, device_id_type=MESH)
            cp.start()
            copies.append(cp)
        for k, (px, py, pc) in enumerate(peers):
            slot = 4 * px + 2 * py + pc if reduce_all else 2 * px + py
            pltpu.make_async_remote_copy(src_ref=p_ref, dst_ref=buf.at[slot], send_sem=ssem.at[k],
                                         recv_sem=rsem.at[k], device_id=(px, py, pc), device_id_type=MESH).wait_recv()
        for cp in copies:
            cp.wait_send()
        if reduce_all:
            acc = buf[0]
            for d in range(1, N_DEVICES):
                acc = acc + buf[d]
            o_ref[...] = acc
        else:
            o_ref[...] = buf[...]

    vmem = pl.BlockSpec(memory_space=pltpu.VMEM)
    out_shape = jax.ShapeDtypeStruct((R, LANE) if reduce_all else (N_CHIPS, R, LANE), F32)
    return pl.pallas_call(
        body, name=name, out_shape=out_shape, in_specs=[vmem], out_specs=vmem,
        scratch_shapes=[pltpu.VMEM((n_slots, R, LANE), F32), pltpu.SemaphoreType.DMA((n_peers,)),
                        pltpu.SemaphoreType.DMA((n_peers,))],
        compiler_params=_params())(pack)


def _pack(arrays):
    flat = jnp.concatenate([a.reshape(-1).astype(F32) for a in arrays])
    n = flat.shape[0]
    rows = -(-n // LANE)
    rows = -(-rows // 8) * 8
    return jnp.pad(flat, (0, rows * LANE - n)).reshape(rows, LANE)


def _unpack(pack, shapes, lead=()):
    flat = pack.reshape(lead + (-1,))
    out, off = [], 0
    for s in shapes:
        n = math.prod(s)
        out.append(flat[..., off:off + n].reshape(lead + tuple(s)))
        off += n
    return out


def kernel(x, p, positions, mla_w_in, mla_q_norm, mla_kv_norm, mla_w_uq, mla_w_uk, mla_w_uv, mla_w_o, gla_w_in, gla_w_a2, gla_b_a, gla_o_norm, gla_w_o, ln1_g, ln1_b, ln2_g, ln2_b, ffn_w_up, ffn_conv_w, ffn_conv_b, ffn_w_down, ple_w_proj, ple_w_gate, ple_b_gate, loss_target, m_mla_w_in, m_mla_q_norm, m_mla_kv_norm, m_mla_w_uq, m_mla_w_uk, m_mla_w_uv, m_mla_w_o, m_gla_w_in, m_gla_w_a2, m_gla_b_a, m_gla_o_norm, m_gla_w_o, m_ln1_g, m_ln1_b, m_ln2_g, m_ln2_b, m_ffn_w_up, m_ffn_conv_w, m_ffn_conv_b, m_ffn_w_down, m_ple_w_proj, m_ple_w_gate, m_ple_b_gate, v_mla_w_in, v_mla_q_norm, v_mla_kv_norm, v_mla_w_uq, v_mla_w_uk, v_mla_w_uv, v_mla_w_o, v_gla_w_in, v_gla_w_a2, v_gla_b_a, v_gla_o_norm, v_gla_w_o, v_ln1_g, v_ln1_b, v_ln2_g, v_ln2_b, v_ffn_w_up, v_ffn_conv_w, v_ffn_conv_b, v_ffn_w_down, v_ple_w_proj, v_ple_w_gate, v_ple_b_gate):
    given = dict(locals())
    S, D = x.shape[1], x.shape[2]
    QL, KL = mla_q_norm.shape[1], mla_kv_norm.shape[1]
    H = mla_w_uq.shape[2] * N_CHIPS // (MLA_NOPE + MLA_ROPE)
    QK, VD = gla_b_a.shape[1] * N_CHIPS, gla_o_norm.shape[1] * N_CHIPS
    FF = ffn_w_down.shape[1] * N_CHIPS
    GIN = 2 * QK + 2 * VD + GLA_GATE_RANK
    GIN_PAD = 2 * QK + 2 * VD + LANE
    MIN = QL + KL + MLA_ROPE
    MIN_PAD = QL + KL + LANE
    a_blk = (2 * QK + 2 * VD) // LANE
    r_blk = (2 * QK + VD) // VD
    xi, yi, ci = _position()
    chip = 2 * xi + yi
    c_arr = jnp.reshape(ci, (1,)).astype(jnp.int32)
    j_arr = jnp.reshape(chip, (1,)).astype(jnp.int32)

    big_dims = [1 if n == 'gla_w_in' else d for n, d in BIG]
    shards = [given[n].astype(MXU_DTYPE) for n, _ in BIG]
    full = dict(zip([n for n, _ in BIG], gather_weights(shards, big_dims)))
    small_sharded = [n for n, d in SMALL if d is not None]
    spack = exchange_small('gather_small', _pack([given[n] for n in small_sharded]), False)
    parts = _unpack(spack, [given[n].shape for n in small_sharded], lead=(N_CHIPS,))
    for n, part in zip(small_sharded, parts):
        d = dict(SMALL)[n]
        full[n] = jnp.concatenate([part[k] for k in range(N_CHIPS)], axis=d)

    w_min = jnp.pad(full['mla_w_in'], ((0, 0), (0, 0), (0, MIN_PAD - MIN)))
    w_uq = full['mla_w_uq'].reshape(-1, QL, H, MLA_NOPE + MLA_ROPE)
    w_uq = jnp.pad(w_uq, ((0, 0), (0, 0), (0, 0), (0, HEAD_PAD - MLA_NOPE - MLA_ROPE))).reshape(-1, QL, H * HEAD_PAD)
    w_gin = full['gla_w_in'].reshape(-1, N_CHIPS, D, GIN // N_CHIPS).transpose(0, 2, 1, 3).reshape(-1, D, GIN)
    w_gin = jnp.pad(w_gin, ((0, 0), (0, 0), (0, GIN_PAD - GIN)))
    w_a2 = jnp.pad(full['gla_w_a2'], ((0, 0), (0, LANE - GLA_GATE_RANK), (0, 0))).astype(MXU_DTYPE)
    cw_u, cw_g = full['ffn_conv_w'][:, :, :FF], full['ffn_conv_w'][:, :, FF:]
    cb_u, cb_g = ffn_conv_b[:, None, :FF], ffn_conv_b[:, None, FF:]

    pos_row = positions[0]
    cid = pos_row // CHUNK
    t_att = _tile(S, ATT_T)
    cid_col = cid.reshape(S, 1)
    cid_blk = cid.reshape(S // t_att, 1, t_att)
    tab_q = rope_tables(pos_row, MLA_NOPE)
    tab_k = rope_tables(pos_row, 0)

    def row1(a, i):
        return a[i:i + 1]

    saved = []
    xa = x[0]
    xb = xa
    for i in range(DEPTH):
        j = i // 2
        sv = dict(x=xa, xb=xb)
        if i % 2 == 0:
            h = mm(f'mla_in_{i}', xb, w_min, 'nn', F32, b_pre=(j,))
            cq, ckv, kr = mla_pre_fwd(f'mla_pre_{i}', h, row1(mla_q_norm, j), row1(mla_kv_norm, j), tab_k, QL, KL)
            qpre = mm(f'mla_uq_{i}', cq, w_uq, 'nn', F32, b_pre=(j,))
            q = rope_q(f'mla_rope_{i}', qpre, tab_q, H, MXU_DTYPE)
            kn = mm(f'mla_uk_{i}', ckv, full['mla_w_uk'], 'nn', MXU_DTYPE, b_pre=(j,))
            vv = mm(f'mla_uv_{i}', ckv, full['mla_w_uv'], 'nn', MXU_DTYPE, b_pre=(j,))
            o, lse = attn_fwd(f'mla_attn_{i}', q, kn, kr, vv, cid_col, cid_blk, H)
            mix = mm(f'mla_o_{i}', o, full['mla_w_o'], 'nn', F32, b_pre=(j,))
            sv.update(h=h, cq=cq, ckv=ckv, kr=kr, q=q, kn=kn, v=vv, o=o, lse=lse)
        else:
            h = mm(f'gla_in_{i}', xb, w_gin, 'nn', F32, b_pre=(j,))
            qs, la = gla_pre_fwd(f'gla_pre_{i}', h, w_a2[j], row1(full['gla_b_a'], j), QK, a_blk)
            o, states = gla_fwd(f'gla_scan_{i}', qs, h, la, QK, VD)
            og = gla_post_fwd(f'gla_post_{i}', o, h, row1(full['gla_o_norm'], j), VD, r_blk)
            mix = mm(f'gla_o_{i}', og, full['gla_w_o'], 'nn', F32, b_pre=(j,))
            sv.update(h=h, qs=qs, la=la, o=o, states=states, og=og)
        x1, x1b = ln_fwd(f'ln1_{i}', xa, mix, row1(ln1_g, i), row1(ln1_b, i))
        hu = mm(f'ffn_up_u_{i}', x1b, full['ffn_w_up'], 'nn', F32, b_pre=(i,), b_win=(0, D, 0, FF))
        hg = mm(f'ffn_up_g_{i}', x1b, full['ffn_w_up'], 'nn', F32, b_pre=(i,), b_win=(0, D, FF, FF))
        act = conv_fwd(f'ffn_conv_{i}', hu, hg, cw_u[i], cw_g[i], cb_u[i], cb_g[i])
        f = mm(f'ffn_down_{i}', act, full['ffn_w_down'], 'nn', F32, b_pre=(i,))
        x2, x2b = ln_fwd(f'ln2_{i}', x1, f, row1(ln2_g, i), row1(ln2_b, i))
        gp = mm(f'ple_gate_{i}', x2b, full['ple_w_gate'], 'nn', F32, b_pre=(i,))
        pp = mm(f'ple_proj_{i}', p, full['ple_w_proj'], 'nn', F32, a_pre=(i, 0), b_pre=(i,))
        x3, x3b = ple_fwd(f'ple_{i}', x2, gp, pp, row1(ple_b_gate, i))
        sv.update(mix=mix, x1=x1, x1b=x1b, hu=hu, hg=hg, act=act, f=f, x2=x2, x2b=x2b, gp=gp, pp=pp)
        saved.append(sv)
        xa, xb = x3, x3b

    dy, sq = loss_head('loss_head', xa, loss_target[0])
    loss_part = (0.5 / D) * jnp.sum(sq)

    G = {n: None for n, _ in BIG}
    small_g = {n: [None] * given[n].shape[0] for n, _ in SMALL}
    G_shape = dict(mla_w_in=(DEPTH // 2, D, MIN_PAD), mla_w_uq=(DEPTH // 2, QL, H * HEAD_PAD),
                   gla_w_in=(DEPTH // 2, D, GIN_PAD))

    def wgrad(name, lead, a, b, a_pre=(), col0=0):
        shape = G_shape.get(name) or full[name].shape
        G[name] = mm(f'd_{name}_{lead}_{col0}', a, b, 'tn', MXU_DTYPE, a_pre=a_pre,
                     out_stack=(G[name], shape, lead, col0))

    dA, ca, dB = dy, 1.0, None
    for i in reversed(range(DEPTH)):
        j = i // 2
        sv = saved[i]
        dx3, dpp, dgp, dbg = ple_bwd(f'ple_b_{i}', dA, ca, dB, sv['gp'], sv['pp'], row1(ple_b_gate, i))
        small_g['ple_b_gate'][i] = dbg
        wgrad('ple_w_proj', i, p, dpp, a_pre=(i, 0))
        wgrad('ple_w_gate', i, sv['x2b'], dgp)
        dx2 = mm(f'ple_gate_b_{i}', dgp, full['ple_w_gate'], 'nt', F32, b_pre=(i,))
        dz2, dz2b, dg2, db2 = ln_bwd(f'ln2_b_{i}', sv['x1'], sv['f'], dx3, 1.0, dx2, row1(ln2_g, i))
        small_g['ln2_g'][i], small_g['ln2_b'][i] = dg2, db2
        wgrad('ffn_w_down', i, sv['act'], dz2b)
        dact = mm(f'ffn_down_b_{i}', dz2b, full['ffn_w_down'], 'nt', F32, b_pre=(i,))
        dhu, dhg, dcwu, dcwg, dcbu, dcbg = conv_bwd(f'ffn_conv_b_{i}', sv['hu'], sv['hg'], dact,
                                                   cw_u[i], cw_g[i], cb_u[i], cb_g[i])
        small_g['ffn_conv_w'][i] = jnp.concatenate([dcwu, dcwg], -1)
        small_g['ffn_conv_b'][i] = jnp.concatenate([dcbu, dcbg], -1)
        wgrad('ffn_w_up', i, sv['x1b'], dhu)
        wgrad('ffn_w_up', i, sv['x1b'], dhg, col0=FF)
        dx1 = mm(f'ffn_up_bu_{i}', dhu, full['ffn_w_up'], 'nt', F32, b_pre=(i,), b_win=(0, D, 0, FF))
        dx1 = mm(f'ffn_up_bg_{i}', dhg, full['ffn_w_up'], 'nt', F32, b_pre=(i,), b_win=(0, D, FF, FF), acc_in=dx1)
        dz1, dz1b, dg1, db1 = ln_bwd(f'ln1_b_{i}', sv['x'], sv['mix'], dz2, DN_ALPHA, dx1, row1(ln1_g, i))
        small_g['ln1_g'][i], small_g['ln1_b'][i] = dg1, db1
        if i % 2 == 0:
            wgrad('mla_w_o', j, sv['o'], dz1b)
            do = mm(f'mla_o_b_{i}', dz1b, full['mla_w_o'], 'nt', MXU_DTYPE, b_pre=(j,))
            dq, dkn, dv, dkr = attn_bwd(f'mla_attn_b_{i}', sv['q'], sv['kn'], sv['kr'], sv['v'], sv['o'],
                                        sv['lse'], do, cid_col, cid_blk, H)
            dqpre = rope_q(f'mla_rope_b_{i}', dq, tab_q, H, MXU_DTYPE, transpose=True)
            wgrad('mla_w_uq', j, sv['cq'], dqpre)
            wgrad('mla_w_uk', j, sv['ckv'], dkn)
            wgrad('mla_w_uv', j, sv['ckv'], dv)
            dcq = mm(f'mla_uq_b_{i}', dqpre, w_uq, 'nt', F32, b_pre=(j,))
            dckv_a = mm(f'mla_uk_b_{i}', dkn, full['mla_w_uk'], 'nt', F32, b_pre=(j,))
            dckv_b = mm(f'mla_uv_b_{i}', dv, full['mla_w_uv'], 'nt', F32, b_pre=(j,))
            dh, dqn, dkvn = mla_pre_bwd(f'mla_pre_b_{i}', sv['h'], dcq, dckv_a, dckv_b, dkr,
                                        row1(mla_q_norm, j), row1(mla_kv_norm, j), tab_k, QL, KL, H)
            small_g['mla_q_norm'][j], small_g['mla_kv_norm'][j] = dqn, dkvn
            wgrad('mla_w_in', j, sv['xb'], dh)
            dmix = mm(f'mla_in_b_{i}', dh, w_min, 'nt', F32, b_pre=(j,))
        else:
            wgrad('gla_w_o', j, sv['og'], dz1b)
            dog = mm(f'gla_o_b_{i}', dz1b, full['gla_w_o'], 'nt', F32, b_pre=(j,))
            do, dr, don = gla_post_bwd(f'gla_post_b_{i}', sv['o'], sv['h'], dog, row1(full['gla_o_norm'], j), VD, r_blk)
            dq, dk, dv, dla = gla_bwd(f'gla_scan_b_{i}', sv['qs'], sv['h'], sv['la'], sv['states'], do, QK, VD)
            dh, dw2, dba = gla_dh(f'gla_dh_{i}', dq, dk, dv, dr, dla, sv['h'], w_a2[j], row1(full['gla_b_a'], j),
                                  QK, VD, a_blk, GIN_PAD)
            small_g['gla_o_norm'][j], small_g['gla_b_a'][j] = don, dba
            small_g['gla_w_a2'][j] = dw2[:GLA_GATE_RANK]
            wgrad('gla_w_in', j, sv['xb'], dh)
            dmix = mm(f'gla_in_b_{i}', dh, w_gin, 'nt', F32, b_pre=(j,))
        dA, ca, dB = dz1, DN_ALPHA, dmix
    grad_x = axpy('grad_x', dA, ca, dB)[None]

    G['mla_w_in'] = G['mla_w_in'][:, :, :MIN]
    G['mla_w_uq'] = G['mla_w_uq'].reshape(-1, QL, H, HEAD_PAD)[..., :MLA_NOPE + MLA_ROPE].reshape(-1, QL, H * (MLA_NOPE + MLA_ROPE))
    G['gla_w_in'] = G['gla_w_in'][:, :, :GIN].reshape(-1, D, N_CHIPS, GIN // N_CHIPS).transpose(0, 2, 1, 3).reshape(-1, N_CHIPS * D, GIN // N_CHIPS)
    names = [n for n, _ in BIG]
    g_list = [G[n] for n in names]
    shard_shapes = [given[n].shape[1:] for n in names]
    landed = reduce_pair_exchange(g_list)
    partial = [pair_add(f'pair_add_{n}', g, l, c_arr) for n, g, l in zip(names, g_list, landed)]
    arrived = reduce_chip_exchange(partial, big_dims, shard_shapes)
    halves = [chip_sum(f'chip_sum_{n}', pt, ar, d, j_arr) for n, pt, ar, d in zip(names, partial, arrived, big_dims)]
    grads = dict(zip(names, share_with_sibling(halves)))

    small_names = [n for n, _ in SMALL]
    small_full = [jnp.concatenate([g.reshape((1,) + g.shape[-(given[n].ndim - 1):]) for g in small_g[n]], 0)
                  for n in small_names]
    pack = _pack([jnp.reshape(loss_part, (1,))] + [jnp.zeros((LANE - 1,), F32)] + small_full)
    red = exchange_small('reduce_small', pack, True)
    red_parts = _unpack(red, [(LANE,)] + [g.shape for g in small_full])
    loss = red_parts[0][0]
    for (n, d), g in zip(SMALL, red_parts[1:]):
        if d is not None:
            width = given[n].shape[d]
            g = lax.dynamic_slice_in_dim(g, chip * width, width, axis=d)
        grads[n] = g

    delta, new_m, new_v = {}, {}, {}
    for n in names:
        shape = given[n].shape
        flat = lambda a: a.reshape(-1, shape[-1])
        d_, m_, v_ = adamw(f'adamw_{n}', flat(given[n]), flat(grads[n]), flat(given['m_' + n]), flat(given['v_' + n]))
        delta[n], new_m[n], new_v[n] = d_.reshape(shape), m_.reshape(shape), v_.reshape(shape)
    packs = [_pack([given[pre + n] for n in small_names]) for pre in ('', 'm_', 'v_')]
    outs = adamw('adamw_small', packs[0], _pack([grads[n] for n in small_names]), packs[1], packs[2])
    small_shapes = [given[n].shape for n in small_names]
    for dst, out in zip((delta, new_m, new_v), outs):
        for n, a in zip(small_names, _unpack(out, small_shapes)):
            dst[n] = a

    return (loss, grad_x, *[grads[n] for n in WEIGHT_NAMES], *[delta[n] for n in WEIGHT_NAMES],
            *[new_m[n] for n in WEIGHT_NAMES], *[new_v[n] for n in WEIGHT_NAMES])
```

```python
import functools
import math

import jax
import jax.numpy as jnp
from jax import lax
from jax.experimental import pallas as pl
from jax.experimental.pallas import tpu as pltpu

F32 = jnp.float32
MXU_DTYPE = jnp.bfloat16

DEPTH = 4
CHUNK = 64
Q_BLOCK = 128
MLA_NOPE = 128
MLA_ROPE = 64
MLA_V = 128
ROPE_THETA = 10000.0
GLA_HEADS = 4
GLA_GATE_RANK = 16
GLA_TAU = 16.0
DN_ALPHA = (2 * DEPTH) ** 0.25
EPS = 1e-5
NEG_INF = -1e30
ADAM_LR = 0.001
ADAM_B1 = 0.9
ADAM_B2 = 0.999
ADAM_EPS = 1e-08
ADAM_WD = 0.01
ADAM_STEP = 10
GELU_C = math.sqrt(2.0 / math.pi)
GELU_A = 0.044715

LANE = 128
HEAD_PAD = 2 * LANE
VMEM_LIMIT_BYTES = 48 * 1024 * 1024

MESH = pl.DeviceIdType.MESH
ANY_SPEC = pl.BlockSpec(memory_space=pl.ANY)

WEIGHT_NAMES = ('mla_w_in', 'mla_q_norm', 'mla_kv_norm', 'mla_w_uq', 'mla_w_uk', 'mla_w_uv', 'mla_w_o',
                'gla_w_in', 'gla_w_a2', 'gla_b_a', 'gla_o_norm', 'gla_w_o', 'ln1_g', 'ln1_b', 'ln2_g', 'ln2_b',
                'ffn_w_up', 'ffn_conv_w', 'ffn_conv_b', 'ffn_w_down', 'ple_w_proj', 'ple_w_gate', 'ple_b_gate')
BIG = (('mla_w_in', 1), ('mla_w_uq', 2), ('mla_w_uk', 2), ('mla_w_uv', 2), ('mla_w_o', 1), ('gla_w_in', 2),
       ('gla_w_o', 1), ('ffn_w_up', 2), ('ffn_w_down', 1), ('ple_w_proj', 2), ('ple_w_gate', 1))
SMALL = (('mla_q_norm', None), ('mla_kv_norm', None), ('gla_w_a2', 2), ('gla_b_a', 1), ('gla_o_norm', 1),
         ('ln1_g', None), ('ln1_b', None), ('ln2_g', None), ('ln2_b', None), ('ffn_conv_w', 2),
         ('ffn_conv_b', None), ('ple_b_gate', None))
N_CHIPS = 4
N_DEVICES = 8


def _params():
    return pltpu.CompilerParams(vmem_limit_bytes=VMEM_LIMIT_BYTES)


def _tile(n, cap, *offsets, unit=LANE):
    g = n
    for o in offsets:
        if o:
            g = math.gcd(g, o)
    best = 0
    for d in range(unit, min(g, cap) + 1, unit):
        if g % d == 0:
            best = d
    if best:
        return best
    assert not any(offsets), (n, offsets)
    return n


def _dot(a, b, dims):
    return lax.dot_general(a.astype(MXU_DTYPE), b.astype(MXU_DTYPE), (dims, ((), ())),
                           preferred_element_type=F32)


NN = ((1,), (0,))
NT = ((1,), (1,))
TN = ((0,), (0,))


def mm(name, a, b, mode, out_dtype, *, a_pre=(), b_pre=(), b_win=None, acc_in=None, out_stack=None):
    a2 = a.shape[len(a_pre):]
    b2 = b.shape[len(b_pre):]
    br0, brn, bc0, bcn = b_win or (0, b2[0], 0, b2[1])
    if mode == 'nn':
        (M, K), N, dims = a2, bcn, NN
        assert brn == K
    elif mode == 'nt':
        (M, K), N, dims = a2, brn, NT
        assert bcn == K
    else:
        (K, M), N, dims = a2, bcn, TN
        assert brn == K
    oc0 = out_stack[3] if out_stack else 0
    tm = _tile(M, 512)
    if mode == 'nn':
        tn, tk = _tile(N, 1408, bc0, oc0), _tile(K, 1408, br0)
    elif mode == 'nt':
        tn, tk = _tile(N, 1408, br0, oc0), _tile(K, 1408, bc0)
    else:
        tn, tk = _tile(N, 1408, bc0, oc0), _tile(K, 1024, br0)
    nk = K // tk
    grid = (M // tm, N // tn, nk)

    na, nb = len(a_pre), len(b_pre)
    if mode == 'tn':
        a_spec = pl.BlockSpec((None,) * na + (tk, tm), lambda i, j, k: a_pre + (k, i))
    else:
        a_spec = pl.BlockSpec((None,) * na + (tm, tk), lambda i, j, k: a_pre + (i, k))
    if mode == 'nt':
        b_spec = pl.BlockSpec((None,) * nb + (tn, tk), lambda i, j, k: b_pre + (j + br0 // tn, k + bc0 // tk))
    else:
        b_spec = pl.BlockSpec((None,) * nb + (tk, tn), lambda i, j, k: b_pre + (k + br0 // tk, j + bc0 // tn))
    in_specs, args = [a_spec, b_spec], [a, b]
    if acc_in is not None:
        in_specs.append(pl.BlockSpec((tm, tn), lambda i, j, k: (i, j)))
        args.append(acc_in)
    aliases = {}
    if out_stack is None:
        out_shape = jax.ShapeDtypeStruct((M, N), out_dtype)
        out_spec = pl.BlockSpec((tm, tn), lambda i, j, k: (i, j))
    else:
        buf, full_shape, lead, _ = out_stack
        out_shape = jax.ShapeDtypeStruct(full_shape, out_dtype)
        out_spec = pl.BlockSpec((None, tm, tn), lambda i, j, k: (lead, i, j + oc0 // tn))
        if buf is not None:
            aliases = {len(args): 0}
            in_specs.append(ANY_SPEC)
            args.append(buf)
    has_c, has_alias = acc_in is not None, bool(aliases)

    def body(*refs):
        a_ref, b_ref = refs[0], refs[1]
        c_ref = refs[2] if has_c else None
        o_ref = refs[2 + has_c + has_alias]
        prod = _dot(a_ref[...], b_ref[...], dims)
        if nk == 1:
            if has_c:
                prod = prod + c_ref[...]
            o_ref[...] = prod.astype(out_dtype)
            return
        acc_ref = refs[3 + has_c + has_alias]
        k = pl.program_id(2)

        @pl.when(k == 0)
        def _():
            acc_ref[...] = prod + c_ref[...] if has_c else prod

        @pl.when(k > 0)
        def _():
            acc_ref[...] += prod

        @pl.when(k == nk - 1)
        def _():
            o_ref[...] = acc_ref[...].astype(out_dtype)

    scratch = [pltpu.VMEM((tm, tn), F32)] if nk > 1 else []
    return pl.pallas_call(body, out_shape=out_shape, grid=grid, in_specs=in_specs, out_specs=out_spec,
                          scratch_shapes=scratch, input_output_aliases=aliases, name=name,
                          compiler_params=_params())(*args)


def rowcall(name, body, n_rows, ts, row_ins, full_ins, row_outs, acc_outs=()):
    in_specs, args = [], []
    for arr, width, colblk, pre in row_ins:
        in_specs.append(pl.BlockSpec((None,) * len(pre) + (ts, width),
                                     lambda i, pre=pre, cb=colblk: pre + (i, cb)))
        args.append(arr)
    for arr in full_ins:
        in_specs.append(pl.BlockSpec(arr.shape, lambda i, nd=arr.ndim: (0,) * nd))
        args.append(arr)
    out_shape, out_specs = [], []
    for width, dtype in row_outs:
        out_shape.append(jax.ShapeDtypeStruct((n_rows, width), dtype))
        out_specs.append(pl.BlockSpec((ts, width), lambda i: (i, 0)))
    for shape in acc_outs:
        out_shape.append(jax.ShapeDtypeStruct(shape, F32))
        out_specs.append(pl.BlockSpec(shape, lambda i, nd=len(shape): (0,) * nd))
    n_in = len(args)

    def kern(*refs):
        body(pl.program_id(0), refs[:n_in], refs[n_in:])

    return pl.pallas_call(kern, out_shape=tuple(out_shape), grid=(n_rows // ts,), in_specs=in_specs,
                          out_specs=tuple(out_specs), name=name, compiler_params=_params())(*args)


def _row(arr, width=None, colblk=0, pre=()):
    return (arr, arr.shape[-1] if width is None else width, colblk, pre)


def _init_acc(i, refs):
    @pl.when(i == 0)
    def _():
        for r in refs:
            r[...] = jnp.zeros(r.shape, r.dtype)


def _colsum(v):
    return jnp.sum(v, axis=0, keepdims=True)


def _sigmoid(z):
    return 1.0 / (1.0 + jnp.exp(-z))


def _row_tile(S):
    return _tile(S, 256, unit=16)


def _ln_stats(x_ref, m_ref):
    z = DN_ALPHA * x_ref[...] + m_ref[...]
    mu = jnp.mean(z, -1, keepdims=True)
    zc = z - mu
    var = jnp.mean(zc * zc, -1, keepdims=True)
    r = lax.rsqrt(var + EPS)
    return zc * r, r


def ln_fwd(name, x, m, g, b):
    S, D = x.shape

    def body(i, ins, outs):
        x_ref, m_ref, g_ref, b_ref = ins
        xh, _ = _ln_stats(x_ref, m_ref)
        y = xh * g_ref[...] + b_ref[...]
        outs[0][...] = y
        outs[1][...] = y.astype(MXU_DTYPE)

    return rowcall(name, body, S, _row_tile(S), [_row(x), _row(m)], [g, b], [(D, F32), (D, MXU_DTYPE)])


def ln_bwd(name, x, m, dA, ca, dB, g):
    S, D = x.shape
    has_b = dB is not None

    def body(i, ins, outs):
        x_ref, m_ref, a_ref = ins[:3]
        g_ref = ins[-1]
        dz_ref, dzb_ref, dg_ref, db_ref = outs
        _init_acc(i, (dg_ref, db_ref))
        dy = ca * a_ref[...]
        if has_b:
            dy = dy + ins[3][...]
        xh, r = _ln_stats(x_ref, m_ref)
        dg_ref[...] += _colsum(dy * xh)
        db_ref[...] += _colsum(dy)
        dxh = dy * g_ref[...]
        dz = r * (dxh - jnp.mean(dxh, -1, keepdims=True) - xh * jnp.mean(dxh * xh, -1, keepdims=True))
        dz_ref[...] = dz
        dzb_ref[...] = dz.astype(MXU_DTYPE)

    rows = [_row(x), _row(m), _row(dA)] + ([_row(dB)] if has_b else [])
    return rowcall(name, body, S, _row_tile(S), rows, [g], [(D, F32), (D, MXU_DTYPE)], [(1, D), (1, D)])


def ple_fwd(name, x2, gp, pp, bias):
    S, D = x2.shape

    def body(i, ins, outs):
        x_ref, gp_ref, pp_ref, b_ref = ins
        y = x_ref[...] + _sigmoid(gp_ref[...] + b_ref[...]) * pp_ref[...]
        outs[0][...] = y
        outs[1][...] = y.astype(MXU_DTYPE)

    return rowcall(name, body, S, _row_tile(S), [_row(x2), _row(gp), _row(pp)], [bias],
                   [(D, F32), (D, MXU_DTYPE)])


def ple_bwd(name, dA, ca, dB, gp, pp, bias):
    S, D = gp.shape
    has_b = dB is not None

    def body(i, ins, outs):
        a_ref = ins[0]
        gp_ref, pp_ref, b_ref = ins[-3:]
        dx_ref, dpp_ref, dgp_ref, db_ref = outs
        _init_acc(i, (db_ref,))
        dx = ca * a_ref[...]
        if has_b:
            dx = dx + ins[1][...]
        gate = _sigmoid(gp_ref[...] + b_ref[...])
        dgp = dx * pp_ref[...] * gate * (1.0 - gate)
        dx_ref[...] = dx
        dpp_ref[...] = (dx * gate).astype(MXU_DTYPE)
        dgp_ref[...] = dgp.astype(MXU_DTYPE)
        db_ref[...] += _colsum(dgp)

    rows = [_row(dA)] + ([_row(dB)] if has_b else []) + [_row(gp), _row(pp)]
    return rowcall(name, body, S, _row_tile(S), rows, [bias],
                   [(D, F32), (D, MXU_DTYPE), (D, MXU_DTYPE)], [(1, D)])


def loss_head(name, y, target):
    S, D = y.shape

    def body(i, ins, outs):
        _init_acc(i, (outs[1],))
        e = ins[0][...] - ins[1][...]
        outs[0][...] = e * (1.0 / D)
        outs[1][...] += _colsum(e * e)

    return rowcall(name, body, S, _row_tile(S), [_row(y), _row(target)], [], [(D, F32)], [(1, D)])


def axpy(name, a, ca, b):
    S, D = a.shape

    def body(i, ins, outs):
        outs[0][...] = ca * ins[0][...] + ins[1][...]

    return rowcall(name, body, S, _row_tile(S), [_row(a), _row(b)], [], [(D, F32)])[0]


HALF_ROPE = MLA_ROPE // 2


def _rope(x, c, sa, sb):
    n = x.shape[-1]
    return x * c + pltpu.roll(x, n - HALF_ROPE, 1) * sa + pltpu.roll(x, HALF_ROPE, 1) * sb


def _rope_t(d, c, sa, sb):
    n = d.shape[-1]
    return d * c + pltpu.roll(d * sa, HALF_ROPE, 1) + pltpu.roll(d * sb, n - HALF_ROPE, 1)


def rope_tables(positions_row, n_lead):
    inv = 1.0 / (ROPE_THETA ** (jnp.arange(0, MLA_ROPE, 2, dtype=F32) / MLA_ROPE))
    ang = positions_row.astype(F32)[:, None] * inv
    cos, sin = jnp.cos(ang), jnp.sin(ang)
    S = cos.shape[0]
    z = jnp.zeros((S, HALF_ROPE), F32)
    tail = jnp.zeros((S, LANE - MLA_ROPE), F32)
    c = jnp.concatenate([jnp.ones((S, n_lead), F32), cos, cos, tail], -1)
    sa = jnp.concatenate([jnp.zeros((S, n_lead), F32), -sin, z, tail], -1)
    sb = jnp.concatenate([jnp.zeros((S, n_lead), F32), z, sin, tail], -1)
    return c, sa, sb


def mla_pre_fwd(name, h, qn, kvn, tab_k, QL, KL):
    S = h.shape[0]

    def body(i, ins, outs):
        h_ref, c_ref, sa_ref, sb_ref, qn_ref, kvn_ref = ins
        cq = h_ref[:, 0:QL]
        ckv = h_ref[:, QL:QL + KL]
        kr = h_ref[:, QL + KL:QL + KL + LANE]
        outs[0][...] = (cq * lax.rsqrt(jnp.mean(cq * cq, -1, keepdims=True) + EPS) * qn_ref[...]).astype(MXU_DTYPE)
        outs[1][...] = (ckv * lax.rsqrt(jnp.mean(ckv * ckv, -1, keepdims=True) + EPS) * kvn_ref[...]).astype(MXU_DTYPE)
        outs[2][...] = _rope(kr, c_ref[...], sa_ref[...], sb_ref[...]).astype(MXU_DTYPE)

    rows = [_row(h)] + [_row(t) for t in tab_k]
    return rowcall(name, body, S, _row_tile(S), rows, [qn, kvn],
                   [(QL, MXU_DTYPE), (KL, MXU_DTYPE), (LANE, MXU_DTYPE)])


def _rms_bwd(x, g, dy):
    r = lax.rsqrt(jnp.mean(x * x, -1, keepdims=True) + EPS)
    dg = _colsum(dy * x * r)
    dxg = dy * g
    dx = r * dxg - x * (r * r * r) * jnp.mean(dxg * x, -1, keepdims=True)
    return dx, dg


def mla_pre_bwd(name, h, dcq, dckv_a, dckv_b, dkr_heads, qn, kvn, tab_k, QL, KL, H):
    S, HW = h.shape

    def body(i, ins, outs):
        h_ref, dcq_ref, da_ref, db_ref, dkr_ref, c_ref, sa_ref, sb_ref, qn_ref, kvn_ref = ins
        dh_ref, dqn_ref, dkvn_ref = outs
        _init_acc(i, (dqn_ref, dkvn_ref))
        dx, dg = _rms_bwd(h_ref[:, 0:QL], qn_ref[...], dcq_ref[...])
        dh_ref[:, 0:QL] = dx.astype(MXU_DTYPE)
        dqn_ref[...] += dg
        dx, dg = _rms_bwd(h_ref[:, QL:QL + KL], kvn_ref[...], da_ref[...] + db_ref[...])
        dh_ref[:, QL:QL + KL] = dx.astype(MXU_DTYPE)
        dkvn_ref[...] += dg
        d = dkr_ref[:, 0:LANE]
        for hh in range(1, H):
            d = d + dkr_ref[:, hh * LANE:(hh + 1) * LANE]
        dh_ref[:, QL + KL:QL + KL + LANE] = _rope_t(d, c_ref[...], sa_ref[...], sb_ref[...]).astype(MXU_DTYPE)

    rows = [_row(h), _row(dcq), _row(dckv_a), _row(dckv_b), _row(dkr_heads)] + [_row(t) for t in tab_k]
    return rowcall(name, body, S, _row_tile(S), rows, [qn, kvn], [(HW, MXU_DTYPE)], [(1, QL), (1, KL)])


def rope_q(name, qpre, tab_q, H, out_dtype):
    S = qpre.shape[0]

    def body(i, ins, outs):
        c, sa, sb = ins[1][...], ins[2][...], ins[3][...]
        for hh in range(H):
            sl = slice(hh * HEAD_PAD, (hh + 1) * HEAD_PAD)
            outs[0][:, sl] = _rope(ins[0][:, sl], c, sa, sb).astype(out_dtype)

    rows = [_row(qpre)] + [_row(t) for t in tab_q]
    return rowcall(name, body, S, _row_tile(S), rows, [], [(H * HEAD_PAD, out_dtype)])[0]


ATT_T = 512


def _att_mask_t(ck_col, cq_row, k0, q0, t):
    kpos = k0 + lax.broadcasted_iota(jnp.int32, (t, 1), 0)
    qpos = q0 + lax.broadcasted_iota(jnp.int32, (1, t), 1)
    return (ck_col <= cq_row) & (kpos <= (qpos | (Q_BLOCK - 1)))


def _transpose(a):
    return a.astype(F32).T.astype(a.dtype)


def attn_fwd(name, q, kn, kr, v, cid_col, cid_blk, H):
    S = q.shape[0]
    t = _tile(S, ATT_T)
    n = S // t
    scale = (MLA_NOPE + MLA_ROPE) ** -0.5

    def body(q_ref, kn_ref, kr_ref, v_ref, cc_ref, cr_ref, o_ref, lse_ref, k_scr, vt_scr):
        i = pl.program_id(1)

        @pl.when(i == 0)
        def _():
            k_scr[:, 0:LANE] = kn_ref[...]
            k_scr[:, LANE:HEAD_PAD] = kr_ref[...]
            for jj in range(n):
                vt_scr[jj] = _transpose(v_ref[jj * t:(jj + 1) * t, :])

        qv = q_ref[...]
        cq = cr_ref[i]

        def update(j, carry, masked):
            m, l, acc = carry
            k0 = pl.multiple_of(j * t, t)
            s = _dot(k_scr[pl.ds(k0, t), :], qv, NT) * scale
            if masked:
                s = jnp.where(_att_mask_t(cc_ref[pl.ds(k0, t), :], cq, k0, i * t, t), s, NEG_INF)
            m_new = jnp.maximum(m, jnp.max(s, 0, keepdims=True))
            p = jnp.exp(s - m_new)
            alpha = jnp.exp(m - m_new)
            l = alpha * l + jnp.sum(p, 0, keepdims=True)
            acc = alpha * acc + _dot(vt_scr[j], p, NN)
            return m_new, l, acc

        init = (jnp.full((1, t), NEG_INF, F32), jnp.zeros((1, t), F32), jnp.zeros((MLA_V, t), F32))
        carry = lax.fori_loop(0, i, lambda j, c: update(j, c, False), init)
        m, l, acc = update(i, carry, True)
        o_ref[...] = (acc / l).T.astype(o_ref.dtype)
        lse_ref[...] = m + jnp.log(l)

    return pl.pallas_call(
        body, name=name, grid=(H, n),
        in_specs=[pl.BlockSpec((t, HEAD_PAD), lambda h, i: (i, h)),
                  pl.BlockSpec((S, MLA_NOPE), lambda h, i: (0, h)),
                  pl.BlockSpec((S, LANE), lambda h, i: (0, 0)),
                  pl.BlockSpec((S, MLA_V), lambda h, i: (0, h)),
                  pl.BlockSpec((S, 1), lambda h, i: (0, 0)),
                  pl.BlockSpec(cid_blk.shape, lambda h, i: (0, 0, 0))],
        out_specs=(pl.BlockSpec((t, MLA_V), lambda h, i: (i, h)),
                   pl.BlockSpec((None, None, 1, t), lambda h, i: (h, i, 0, 0))),
        scratch_shapes=[pltpu.VMEM((S, HEAD_PAD), MXU_DTYPE), pltpu.VMEM((n, MLA_V, t), MXU_DTYPE)],
        out_shape=(jax.ShapeDtypeStruct((S, H * MLA_V), MXU_DTYPE), jax.ShapeDtypeStruct((H, n, 1, t), F32)),
        compiler_params=_params())(q, kn, kr, v, cid_col, cid_blk)


def attn_bwd(name, q, kn, kr, v, o, lse, do, cid_col, cid_blk, H):
    S = q.shape[0]
    t = _tile(S, ATT_T)
    n = S // t
    scale = (MLA_NOPE + MLA_ROPE) ** -0.5

    def body(q_ref, kn_ref, kr_ref, v_ref, o_ref, lse_ref, do_ref, cc_ref, cr_ref,
             dq_ref, dkn_ref, dv_ref, dkr_ref, delta_scr, dk_scr, dv_scr):
        j = pl.program_id(1)

        @pl.when(j == 0)
        def _():
            dq_ref[...] = jnp.zeros(dq_ref.shape, F32)
            for ii in range(n):
                sl = slice(ii * t, (ii + 1) * t)
                prod = do_ref[sl, :].astype(F32) * o_ref[sl, :].astype(F32)
                delta_scr[ii] = jnp.sum(prod.T, 0, keepdims=True)

        kv = jnp.concatenate([kn_ref[...], kr_ref[...]], axis=-1)
        kt = _transpose(kv)
        vv = v_ref[...]
        ck = cc_ref[...]
        k0 = j * t
        dk_scr[...] = jnp.zeros(dk_scr.shape, F32)
        dv_scr[...] = jnp.zeros(dv_scr.shape, F32)

        def update(i, masked):
            q0 = pl.multiple_of(i * t, t)
            qv = q_ref[pl.ds(q0, t), :]
            dov = do_ref[pl.ds(q0, t), :]
            s = _dot(kv, qv, NT) * scale
            if masked:
                s = jnp.where(_att_mask_t(ck, cr_ref[i], k0, q0, t), s, NEG_INF)
            p = jnp.exp(s - lse_ref[i])
            dv_scr[...] += _dot(p, dov, NN)
            dp = _dot(vv, dov, NT)
            ds = p * (dp - delta_scr[i]) * scale
            dk_scr[...] += _dot(ds, qv, NN)
            dq_ref[i] += _dot(kt, ds, NN)

        def step(i, carry):
            update(i, False)
            return carry

        update(j, True)
        lax.fori_loop(j + 1, n, step, 0)
        dkn_ref[...] = dk_scr[:, 0:LANE].astype(dkn_ref.dtype)
        dkr_ref[...] = dk_scr[:, LANE:HEAD_PAD]
        dv_ref[...] = dv_scr[...].astype(dv_ref.dtype)

    head_rows = lambda w: pl.BlockSpec((S, w), lambda h, j: (0, h))
    tile_rows = lambda w: pl.BlockSpec((t, w), lambda h, j: (j, h))
    return pl.pallas_call(
        body, name=name, grid=(H, n),
        in_specs=[head_rows(HEAD_PAD), tile_rows(MLA_NOPE), pl.BlockSpec((t, LANE), lambda h, j: (j, 0)),
                  tile_rows(MLA_V), head_rows(MLA_V),
                  pl.BlockSpec((None, n, 1, t), lambda h, j: (h, 0, 0, 0)), head_rows(MLA_V),
                  pl.BlockSpec((t, 1), lambda h, j: (j, 0)),
                  pl.BlockSpec(cid_blk.shape, lambda h, j: (0, 0, 0))],
        out_specs=(pl.BlockSpec((None, n, HEAD_PAD, t), lambda h, j: (h, 0, 0, 0)),
                   tile_rows(MLA_NOPE), tile_rows(MLA_V), tile_rows(LANE)),
        scratch_shapes=[pltpu.VMEM((n, 1, t), F32), pltpu.VMEM((t, HEAD_PAD), F32), pltpu.VMEM((t, MLA_V), F32)],
        out_shape=(jax.ShapeDtypeStruct((H, n, HEAD_PAD, t), F32), jax.ShapeDtypeStruct((S, H * MLA_NOPE), MXU_DTYPE),
                   jax.ShapeDtypeStruct((S, H * MLA_V), MXU_DTYPE), jax.ShapeDtypeStruct((S, H * LANE), F32)),
        compiler_params=_params())(q, kn, kr, v, o, lse, do, cid_col, cid_blk)


def rope_q_bwd(name, dq_t, tab_q, H):
    _, n, _, t = dq_t.shape

    def body(d_ref, c_ref, sa_ref, sb_ref, o_ref):
        o_ref[...] = _rope_t(d_ref[...].T, c_ref[...], sa_ref[...], sb_ref[...]).astype(o_ref.dtype)

    tab = pl.BlockSpec((t, HEAD_PAD), lambda i, h: (i, 0))
    return pl.pallas_call(
        body, name=name, grid=(n, H), out_shape=jax.ShapeDtypeStruct((n * t, H * HEAD_PAD), MXU_DTYPE),
        in_specs=[pl.BlockSpec((None, None, HEAD_PAD, t), lambda i, h: (h, i, 0, 0)), tab, tab, tab],
        out_specs=pl.BlockSpec((t, HEAD_PAD), lambda i, h: (i, h)),
        compiler_params=_params())(dq_t, *tab_q)


GLA_GROUP = 8


def _prefix_rows(x):
    n = x.shape[0]
    row = lax.broadcasted_iota(jnp.int32, x.shape, 0)
    d = 1
    while d < n:
        x = x + jnp.where(row >= d, pltpu.roll(x, d, 0), 0.0)
        d *= 2
    return x


def _suffix_rows(x):
    n = x.shape[0]
    row = lax.broadcasted_iota(jnp.int32, x.shape, 0)
    d = 1
    while d < n:
        x = x + jnp.where(row < n - d, pltpu.roll(x, n - d, 0), 0.0)
        d *= 2
    return x


def _log_sigmoid(z):
    return jnp.minimum(z, 0.0) - jnp.log(1.0 + jnp.exp(-jnp.abs(z)))


def gla_pre_fwd(name, h, w2p, b_a, QK, a_blk):
    S = h.shape[0]
    dk = QK // GLA_HEADS

    def body(i, ins, outs):
        q_ref, a_ref, w_ref, b_ref = ins
        outs[0][...] = (q_ref[...] * (dk ** -0.5)).astype(MXU_DTYPE)
        z = _dot(a_ref[...], w_ref[...], NN) + b_ref[...]
        outs[1][...] = _log_sigmoid(z) / GLA_TAU

    return rowcall(name, body, S, _row_tile(S), [_row(h, QK, 0), _row(h, LANE, a_blk)], [w2p, b_a],
                   [(QK, MXU_DTYPE), (QK, F32)])


def _gla_specs(S, QK, VD, rows, gmap):
    dk, dv = QK // GLA_HEADS, VD // GLA_HEADS
    return dict(
        qs=pl.BlockSpec((rows, dk), lambda h, g: (gmap(g), h)),
        k=pl.BlockSpec((rows, dk), lambda h, g: (gmap(g), QK // dk + h)),
        v=pl.BlockSpec((rows, dv), lambda h, g: (gmap(g), 2 * QK // dv + h)),
        la=pl.BlockSpec((rows, dk), lambda h, g: (gmap(g), h)),
        o=pl.BlockSpec((rows, dv), lambda h, g: (gmap(g), h)))


def gla_fwd(name, qs, h, la, QK, VD):
    S = qs.shape[0]
    dk, dv = QK // GLA_HEADS, VD // GLA_HEADS
    n_chunks = S // CHUNK
    cg = min(GLA_GROUP, n_chunks)
    rows = cg * CHUNK
    sp = _gla_specs(S, QK, VD, rows, lambda g: g)

    def body(q_ref, k_ref, v_ref, la_ref, o_ref, st_ref, state):
        @pl.when(pl.program_id(1) == 0)
        def _():
            state[...] = jnp.zeros(state.shape, F32)

        for c in range(cg):
            sl = slice(c * CHUNK, (c + 1) * CHUNK)
            cum = _prefix_rows(la_ref[sl, :])
            tot = cum[CHUNK - 1:CHUNK, :]
            kdec = k_ref[sl, :] * jnp.exp(tot - cum)
            st = state[...] * jnp.exp(tot) + _dot(v_ref[sl, :], kdec, TN)
            state[...] = st
            st_ref[c] = st
            o_ref[sl, :] = _dot(q_ref[sl, :], st, NT)

    return pl.pallas_call(
        body, name=name, grid=(GLA_HEADS, n_chunks // cg),
        out_shape=(jax.ShapeDtypeStruct((S, VD), F32), jax.ShapeDtypeStruct((GLA_HEADS, n_chunks, dv, dk), F32)),
        in_specs=[sp['qs'], sp['k'], sp['v'], sp['la']],
        out_specs=(sp['o'], pl.BlockSpec((None, cg, dv, dk), lambda h, g: (h, g, 0, 0))),
        scratch_shapes=[pltpu.VMEM((dv, dk), F32)],
        compiler_params=_params())(qs, h, h, la)


def gla_bwd(name, qs, h, la, states, do, QK, VD):
    S = qs.shape[0]
    dk, dv = QK // GLA_HEADS, VD // GLA_HEADS
    n_chunks = S // CHUNK
    cg = min(GLA_GROUP, n_chunks)
    ng = n_chunks // cg
    rows = cg * CHUNK
    rev = lambda g: ng - 1 - g
    sp = _gla_specs(S, QK, VD, rows, rev)

    def body(q_ref, k_ref, v_ref, la_ref, st_ref, prev_ref, do_ref, dq_ref, dk_ref, dv_ref, dla_ref, dst):
        g = pl.program_id(1)

        @pl.when(g == 0)
        def _():
            dst[...] = jnp.zeros(dst.shape, F32)

        first_group = (g == ng - 1).astype(F32)
        for c in reversed(range(cg)):
            sl = slice(c * CHUNK, (c + 1) * CHUNK)
            cum = _prefix_rows(la_ref[sl, :])
            tot = cum[CHUNK - 1:CHUNK, :]
            e = jnp.exp(tot - cum)
            kdec = k_ref[sl, :] * e
            decay = jnp.exp(tot)
            st = st_ref[c]
            st_prev = st_ref[c - 1] if c > 0 else prev_ref[0] * (1.0 - first_group)
            dov = do_ref[sl, :]
            qv = q_ref[sl, :]
            dq_ref[sl, :] = (_dot(dov, st, NN) * (dk ** -0.5)).astype(dq_ref.dtype)
            d = dst[...] + _dot(dov, qv, TN)
            ddecay = _colsum(d * st_prev)
            dkdec = _dot(v_ref[sl, :], d, NN)
            dv_ref[sl, :] = _dot(kdec, d, NT).astype(dv_ref.dtype)
            dk_ref[sl, :] = (dkdec * e).astype(dk_ref.dtype)
            darg = dkdec * kdec
            dtot = _colsum(darg) + ddecay * decay
            dla_ref[sl, :] = dtot - _suffix_rows(darg)
            dst[...] = d * decay

    prev_spec = pl.BlockSpec((None, 1, dv, dk), lambda h, g: (h, jnp.maximum(rev(g) * cg - 1, 0), 0, 0))
    return pl.pallas_call(
        body, name=name, grid=(GLA_HEADS, ng),
        out_shape=(jax.ShapeDtypeStruct((S, QK), MXU_DTYPE), jax.ShapeDtypeStruct((S, QK), MXU_DTYPE),
                   jax.ShapeDtypeStruct((S, VD), MXU_DTYPE), jax.ShapeDtypeStruct((S, QK), F32)),
        in_specs=[sp['qs'], sp['k'], sp['v'], sp['la'],
                  pl.BlockSpec((None, cg, dv, dk), lambda h, g: (h, rev(g), 0, 0)), prev_spec, sp['o']],
        out_specs=(sp['qs'], sp['qs'], sp['o'], sp['la']),
        scratch_shapes=[pltpu.VMEM((dv, dk), F32)],
        compiler_params=_params())(qs, h, h, la, states, states, do)


def _head_norm(o):
    mu = jnp.mean(o, -1, keepdims=True)
    oc = o - mu
    r = lax.rsqrt(jnp.mean(oc * oc, -1, keepdims=True) + EPS)
    return oc * r, r


def gla_post_fwd(name, o, h, o_norm, VD, r_blk):
    S = o.shape[0]
    dv = VD // GLA_HEADS

    def body(i, ins, outs):
        o_ref, r_ref, w_ref = ins
        for hh in range(GLA_HEADS):
            sl = slice(hh * dv, (hh + 1) * dv)
            xh, _ = _head_norm(o_ref[:, sl])
            r = r_ref[:, sl]
            outs[0][:, sl] = (xh * w_ref[:, sl] * (r * _sigmoid(r))).astype(MXU_DTYPE)

    return rowcall(name, body, S, _row_tile(S), [_row(o), _row(h, VD, r_blk)], [o_norm], [(VD, MXU_DTYPE)])[0]


def gla_post_bwd(name, o, h, dog, o_norm, VD, r_blk):
    S = o.shape[0]
    dv = VD // GLA_HEADS

    def body(i, ins, outs):
        o_ref, r_ref, dog_ref, w_ref = ins
        do_ref, dr_ref, dw_ref = outs
        _init_acc(i, (dw_ref,))
        for hh in range(GLA_HEADS):
            sl = slice(hh * dv, (hh + 1) * dv)
            xh, rs = _head_norm(o_ref[:, sl])
            r = r_ref[:, sl]
            w = w_ref[:, sl]
            dog = dog_ref[:, sl]
            sg = _sigmoid(r)
            dn = dog * (r * sg)
            dr_ref[:, sl] = (dog * (xh * w) * (sg * (1.0 + r * (1.0 - sg)))).astype(MXU_DTYPE)
            dw_ref[:, sl] += _colsum(dn * xh)
            dxh = dn * w
            do_ref[:, sl] = rs * (dxh - jnp.mean(dxh, -1, keepdims=True) - xh * jnp.mean(dxh * xh, -1, keepdims=True))

    return rowcall(name, body, S, _row_tile(S), [_row(o), _row(h, VD, r_blk), _row(dog)], [o_norm],
                   [(VD, F32), (VD, MXU_DTYPE)], [(1, VD)])


def gla_dh(name, dq, dk, dv, dr, dla, h, w2p, b_a, QK, VD, a_blk, HW):
    S = dq.shape[0]

    def body(i, ins, outs):
        dq_ref, dk_ref, dv_ref, dr_ref, dla_ref, a_ref, w_ref, b_ref = ins
        dh_ref, dw_ref, db_ref = outs
        _init_acc(i, (dw_ref, db_ref))
        a = a_ref[...]
        z = _dot(a, w_ref[...], NN) + b_ref[...]
        dz = dla_ref[...] * (1.0 / GLA_TAU) * _sigmoid(-z)
        dw_ref[...] += _dot(a, dz, TN)
        db_ref[...] += _colsum(dz)
        dh_ref[:, 0:QK] = dq_ref[...]
        dh_ref[:, QK:2 * QK] = dk_ref[...]
        dh_ref[:, 2 * QK:2 * QK + VD] = dv_ref[...]
        dh_ref[:, 2 * QK + VD:2 * QK + 2 * VD] = dr_ref[...]
        dh_ref[:, 2 * QK + 2 * VD:HW] = _dot(dz, w_ref[...], NT).astype(MXU_DTYPE)

    rows = [_row(dq), _row(dk), _row(dv), _row(dr), _row(dla), _row(h, LANE, a_blk)]
    return rowcall(name, body, S, _row_tile(S), rows, [w2p, b_a], [(HW, MXU_DTYPE)], [(LANE, QK), (1, QK)])


CONV_ROWS = 512
HALO = 8


def _gelu(x):
    return 0.5 * x * (1.0 + jnp.tanh(GELU_C * (x + GELU_A * x * x * x)))


def _gelu_grad(x):
    t = jnp.tanh(GELU_C * (x + GELU_A * x * x * x))
    return 0.5 * (1.0 + t) + 0.5 * x * (1.0 - t * t) * GELU_C * (1.0 + 3.0 * GELU_A * x * x)


def _shift_down(x, prev, d):
    row = lax.broadcasted_iota(jnp.int32, x.shape, 0)
    out = pltpu.roll(x, d, 0)
    for t in range(d):
        out = jnp.where(row == t, prev[HALO - d + t:HALO - d + t + 1, :], out)
    return out


def _shift_up(x, nxt, d):
    n = x.shape[0]
    row = lax.broadcasted_iota(jnp.int32, x.shape, 0)
    out = pltpu.roll(x, n - d, 0)
    for t in range(d):
        out = jnp.where(row == n - d + t, nxt[t:t + 1, :], out)
    return out


def _conv_taps(ref, r, rc):
    x = ref[r * rc:(r + 1) * rc, :]
    prev = ref[r * rc - HALO:r * rc, :] if r > 0 else jnp.zeros((HALO, x.shape[1]), F32)
    return x, _shift_down(x, prev, 1), _shift_down(x, prev, 2)


def _conv_apply(taps, w_ref, b_ref):
    x0, x1, x2 = taps
    return x2 * w_ref[0:1, :] + x1 * w_ref[1:2, :] + x0 * w_ref[2:3, :] + b_ref[...]


def conv_fwd(name, hu, hg, cw_u, cw_g, cb_u, cb_g):
    S, F = hu.shape
    tc = LANE
    rc = _tile(S, CONV_ROWS, unit=16)

    def body(u_ref, g_ref, wu_ref, wg_ref, bu_ref, bg_ref, a_ref):
        for r in range(S // rc):
            uc = _conv_apply(_conv_taps(u_ref, r, rc), wu_ref, bu_ref)
            gc = _conv_apply(_conv_taps(g_ref, r, rc), wg_ref, bg_ref)
            a_ref[r * rc:(r + 1) * rc, :] = (uc * _gelu(gc)).astype(a_ref.dtype)

    col = lambda rows: pl.BlockSpec((rows, tc), lambda j: (0, j))
    return pl.pallas_call(
        body, name=name, grid=(F // tc,), out_shape=jax.ShapeDtypeStruct((S, F), MXU_DTYPE),
        in_specs=[col(S), col(S), col(3), col(3), col(1), col(1)], out_specs=col(S),
        compiler_params=_params())(hu, hg, cw_u, cw_g, cb_u, cb_g)


def conv_bwd(name, hu, hg, da, cw_u, cw_g, cb_u, cb_g):
    S, F = hu.shape
    tc = LANE
    rc = _tile(S, CONV_ROWS, unit=16)
    nr = S // rc

    def body(u_ref, g_ref, da_ref, wu_ref, wg_ref, bu_ref, bg_ref,
             dhu_ref, dhg_ref, dwu_ref, dwg_ref, dbu_ref, dbg_ref, du_scr, dg_scr):
        dw = [[jnp.zeros((1, tc), F32) for _ in range(3)] for _ in range(2)]
        db = [jnp.zeros((1, tc), F32) for _ in range(2)]
        for r in range(nr):
            sl = slice(r * rc, (r + 1) * rc)
            ut = _conv_taps(u_ref, r, rc)
            gt = _conv_taps(g_ref, r, rc)
            uc = _conv_apply(ut, wu_ref, bu_ref)
            gc = _conv_apply(gt, wg_ref, bg_ref)
            dav = da_ref[sl, :]
            duc = dav * _gelu(gc)
            dgc = dav * uc * _gelu_grad(gc)
            du_scr[sl, :] = duc
            dg_scr[sl, :] = dgc
            for part, (taps, d) in enumerate(((ut, duc), (gt, dgc))):
                db[part] = db[part] + _colsum(d)
                for tap in range(3):
                    dw[part][tap] = dw[part][tap] + _colsum(taps[2 - tap] * d)
        for part, (w_out, b_out) in enumerate(((dwu_ref, dbu_ref), (dwg_ref, dbg_ref))):
            b_out[...] = db[part]
            for tap in range(3):
                w_out[tap:tap + 1, :] = dw[part][tap]
        for scr, w_ref, out in ((du_scr, wu_ref, dhu_ref), (dg_scr, wg_ref, dhg_ref)):
            for r in range(nr):
                d = scr[r * rc:(r + 1) * rc, :]
                nxt = scr[(r + 1) * rc:(r + 1) * rc + HALO, :] if r + 1 < nr else jnp.zeros((HALO, tc), F32)
                dh = d * w_ref[2:3, :] + _shift_up(d, nxt, 1) * w_ref[1:2, :] + _shift_up(d, nxt, 2) * w_ref[0:1, :]
                out[r * rc:(r + 1) * rc, :] = dh.astype(out.dtype)

    col = lambda rows: pl.BlockSpec((rows, tc), lambda j: (0, j))
    sds = jax.ShapeDtypeStruct
    return pl.pallas_call(
        body, name=name, grid=(F // tc,),
        out_shape=(sds((S, F), MXU_DTYPE), sds((S, F), MXU_DTYPE), sds((3, F), F32), sds((3, F), F32),
                   sds((1, F), F32), sds((1, F), F32)),
        in_specs=[col(S), col(S), col(S), col(3), col(3), col(1), col(1)],
        out_specs=(col(S), col(S), col(3), col(3), col(1), col(1)),
        scratch_shapes=[pltpu.VMEM((S, tc), F32), pltpu.VMEM((S, tc), F32)],
        compiler_params=_params())(hu, hg, da, cw_u, cw_g, cb_u, cb_g)


def adamw(name, w, g, m, v):
    R, C = w.shape
    tr = _tile(R, max(8, (1 << 19) // max(C, 1) // 8 * 8), unit=8)

    def body(w_ref, g_ref, m_ref, v_ref, d_ref, nm_ref, nv_ref):
        gv = g_ref[...]
        mn = ADAM_B1 * m_ref[...] + (1.0 - ADAM_B1) * gv
        vn = ADAM_B2 * v_ref[...] + (1.0 - ADAM_B2) * (gv * gv)
        m_hat = mn / (1.0 - ADAM_B1 ** ADAM_STEP)
        v_hat = vn / (1.0 - ADAM_B2 ** ADAM_STEP)
        d_ref[...] = -ADAM_LR * (m_hat / (jnp.sqrt(v_hat) + ADAM_EPS) + ADAM_WD * w_ref[...])
        nm_ref[...] = mn
        nv_ref[...] = vn

    spec = pl.BlockSpec((tr, C), lambda i: (i, 0))
    shp = jax.ShapeDtypeStruct((R, C), F32)
    return pl.pallas_call(body, name=name, grid=(R // tr,), out_shape=(shp, shp, shp), in_specs=[spec] * 4,
                          out_specs=(spec, spec, spec), compiler_params=_params())(w, g, m, v)


def _position():
    return lax.axis_index('x'), lax.axis_index('y'), lax.axis_index('c')


def _other_chips(x, y):
    return ((1 - x, y), (x, 1 - y), (1 - x, 1 - y))


def _window(ref, dim, lo, n_lead, jj, rs, cs):
    if dim == 1:
        return ref.at[pl.ds(lo, n_lead), pl.ds(pl.multiple_of(jj * rs, 16), rs), :]
    return ref.at[pl.ds(lo, n_lead), :, pl.ds(pl.multiple_of(jj * cs, LANE), cs)]


def _hbm_call(name, body, out_shapes, n_sems, args, aliases=None):
    return pl.pallas_call(
        body, name=name, out_shape=tuple(out_shapes), in_specs=[ANY_SPEC] * len(args),
        out_specs=tuple(ANY_SPEC for _ in out_shapes),
        scratch_shapes=[pltpu.SemaphoreType.DMA((n,)) for n in n_sems],
        input_output_aliases=aliases or {},
        compiler_params=pltpu.CompilerParams(has_side_effects=True))(*args)


def place_shard(name, shard, dim, j_arr):
    L, rs, cs = shard.shape
    full_shape = (L, rs * N_CHIPS, cs) if dim == 1 else (L, rs, cs * N_CHIPS)
    tr = _tile(rs, max(16, (1 << 19) // cs // 16 * 16), unit=16)

    def body(j_ref, s_ref, o_ref):
        o_ref[...] = s_ref[...].astype(o_ref.dtype)

    if dim == 1:
        out_spec = pl.BlockSpec((None, tr, cs), lambda l, i, j: (l, j[0] * (rs // tr) + i, 0))
    else:
        out_spec = pl.BlockSpec((None, tr, cs), lambda l, i, j: (l, i, j[0]))
    grid_spec = pltpu.PrefetchScalarGridSpec(
        num_scalar_prefetch=1, grid=(L, rs // tr),
        in_specs=[pl.BlockSpec((None, tr, cs), lambda l, i, j: (l, i, 0))], out_specs=out_spec)
    return pl.pallas_call(body, name=name, grid_spec=grid_spec,
                          out_shape=jax.ShapeDtypeStruct(full_shape, MXU_DTYPE),
                          compiler_params=_params())(j_arr, shard)


def gather_weights(fulls, dims, shard_shapes):
    n = len(fulls)

    def body(*refs):
        full = refs[n:2 * n]
        ssem, rsem, s2sem, r2sem = refs[2 * n:]
        x, y, c = _position()
        mine = 2 * x + y
        chips = _other_chips(x, y)
        sibling = (x, y, 1 - c)

        def win(w, lo, nl, jj):
            return _window(full[w], dims[w], lo, nl, jj, shard_shapes[w][1], shard_shapes[w][2])

        sends = []
        for w in range(n):
            half = shard_shapes[w][0] // 2
            own = win(w, c * half, half, mine)
            for k, (cx, cy) in enumerate(chips):
                cp = pltpu.make_async_remote_copy(
                    src_ref=own, dst_ref=own, send_sem=ssem.at[3 * w + k], recv_sem=rsem.at[3 * w + k],
                    device_id=(cx, cy, c), device_id_type=MESH)
                cp.start()
                sends.append(cp)
        for w in range(n):
            half = shard_shapes[w][0] // 2
            for k, (cx, cy) in enumerate(chips):
                theirs = win(w, c * half, half, 2 * cx + cy)
                pltpu.make_async_remote_copy(
                    src_ref=theirs, dst_ref=theirs, send_sem=ssem.at[3 * w + k], recv_sem=rsem.at[3 * w + k],
                    device_id=(cx, cy, c), device_id_type=MESH).wait_recv()
                cp = pltpu.make_async_remote_copy(
                    src_ref=theirs, dst_ref=theirs, send_sem=s2sem.at[3 * w + k], recv_sem=r2sem.at[3 * w + k],
                    device_id=sibling, device_id_type=MESH)
                cp.start()
                sends.append(cp)
        for w in range(n):
            half = shard_shapes[w][0] // 2
            for k, (cx, cy) in enumerate(chips):
                other = win(w, (1 - c) * half, half, 2 * cx + cy)
                pltpu.make_async_remote_copy(
                    src_ref=other, dst_ref=other, send_sem=s2sem.at[3 * w + k], recv_sem=r2sem.at[3 * w + k],
                    device_id=sibling, device_id_type=MESH).wait_recv()
        for cp in sends:
            cp.wait_send()

    outs = [jax.ShapeDtypeStruct(f.shape, f.dtype) for f in fulls]
    return _hbm_call('gather_weights', body, outs, (3 * n, 3 * n, 3 * n, 3 * n), fulls,
                     aliases={w: w for w in range(n)})


def reduce_pair_exchange(grads):
    n = len(grads)
    outs = [jax.ShapeDtypeStruct((g.shape[0] // 2,) + g.shape[1:], g.dtype) for g in grads]

    def body(*refs):
        gr, land = refs[:n], refs[n:2 * n]
        ssem, rsem = refs[2 * n:]
        x, y, c = _position()
        copies = []
        for w in range(n):
            half = grads[w].shape[0] // 2
            cp = pltpu.make_async_remote_copy(
                src_ref=gr[w].at[pl.ds((1 - c) * half, half)], dst_ref=land[w],
                send_sem=ssem.at[w], recv_sem=rsem.at[w], device_id=(x, y, 1 - c), device_id_type=MESH)
            cp.start()
            copies.append(cp)
        for cp in copies:
            cp.wait()

    return _hbm_call('reduce_pair_exchange', body, outs, (n, n), grads)


def pair_add(name, g, landed, c_arr):
    half, R, C = landed.shape
    tr = _tile(R, max(16, (1 << 20) // C // 16 * 16), unit=16)

    def body(c_ref, g_ref, l_ref, o_ref):
        o_ref[...] = (g_ref[...].astype(F32) + l_ref[...].astype(F32)).astype(o_ref.dtype)

    grid_spec = pltpu.PrefetchScalarGridSpec(
        num_scalar_prefetch=1, grid=(half, R // tr),
        in_specs=[pl.BlockSpec((None, tr, C), lambda l, i, c: (c[0] * half + l, i, 0)),
                  pl.BlockSpec((None, tr, C), lambda l, i, c: (l, i, 0))],
        out_specs=pl.BlockSpec((None, tr, C), lambda l, i, c: (l, i, 0)))
    return pl.pallas_call(body, name=name, grid_spec=grid_spec,
                          out_shape=jax.ShapeDtypeStruct(landed.shape, landed.dtype),
                          compiler_params=_params())(c_arr, g, landed)


def reduce_chip_exchange(partials, dims, shard_shapes):
    n = len(partials)
    outs = [jax.ShapeDtypeStruct((3, p.shape[0]) + tuple(s), p.dtype) for p, s in zip(partials, shard_shapes)]

    def body(*refs):
        ps, land = refs[:n], refs[n:2 * n]
        ssem, rsem = refs[2 * n:]
        x, y, c = _position()
        copies = []
        for w in range(n):
            half = partials[w].shape[0]
            rs, cs = shard_shapes[w]
            for k, (cx, cy) in enumerate(_other_chips(x, y)):
                cp = pltpu.make_async_remote_copy(
                    src_ref=_window(ps[w], dims[w], 0, half, 2 * cx + cy, rs, cs), dst_ref=land[w].at[k],
                    send_sem=ssem.at[3 * w + k], recv_sem=rsem.at[3 * w + k],
                    device_id=(cx, cy, c), device_id_type=MESH)
                cp.start()
                copies.append(cp)
        for cp in copies:
            cp.wait()

    return _hbm_call('reduce_chip_exchange', body, outs, (3 * n, 3 * n), partials)


def chip_sum(name, partial, landed, dim, j_arr, c_arr):
    _, half, rs, cs = landed.shape
    tr = _tile(rs, max(16, (1 << 19) // cs // 16 * 16), unit=16)

    def body(j_ref, core_ref, p_ref, a_ref, b_ref, c_ref, o_ref):
        o_ref[...] = ((p_ref[...].astype(F32) + a_ref[...].astype(F32)) + b_ref[...].astype(F32)) + c_ref[...].astype(F32)

    if dim == 1:
        own = pl.BlockSpec((None, tr, cs), lambda l, i, j, c: (l, j[0] * (rs // tr) + i, 0))
    else:
        own = pl.BlockSpec((None, tr, cs), lambda l, i, j, c: (l, i, j[0]))
    arrived = [pl.BlockSpec((None, None, tr, cs), lambda l, i, j, c, k=k: (k, l, i, 0)) for k in range(3)]
    grid_spec = pltpu.PrefetchScalarGridSpec(
        num_scalar_prefetch=2, grid=(half, rs // tr), in_specs=[own] + arrived,
        out_specs=pl.BlockSpec((None, tr, cs), lambda l, i, j, c: (c[0] * half + l, i, 0)))
    return pl.pallas_call(body, name=name, grid_spec=grid_spec,
                          out_shape=jax.ShapeDtypeStruct((2 * half, rs, cs), F32),
                          compiler_params=_params())(j_arr, c_arr, partial, landed, landed, landed)


def share_with_sibling(fulls):
    n = len(fulls)

    def body(*refs):
        full = refs[n:2 * n]
        ssem, rsem = refs[2 * n:]
        x, y, c = _position()
        copies = []
        for w in range(n):
            half = fulls[w].shape[0] // 2
            own = full[w].at[pl.ds(c * half, half)]
            cp = pltpu.make_async_remote_copy(src_ref=own, dst_ref=own, send_sem=ssem.at[w], recv_sem=rsem.at[w],
                                              device_id=(x, y, 1 - c), device_id_type=MESH)
            cp.start()
            copies.append(cp)
        for w in range(n):
            half = fulls[w].shape[0] // 2
            copies[w].wait_send()
            other = full[w].at[pl.ds((1 - c) * half, half)]
            pltpu.make_async_remote_copy(src_ref=other, dst_ref=other, send_sem=ssem.at[w], recv_sem=rsem.at[w],
                                         device_id=(x, y, 1 - c), device_id_type=MESH).wait_recv()

    outs = [jax.ShapeDtypeStruct(f.shape, f.dtype) for f in fulls]
    return _hbm_call('share_with_sibling', body, outs, (n, n), fulls, aliases={w: w for w in range(n)})


def exchange_small(name, pack, reduce_all):
    R = pack.shape[0]
    n_slots = N_DEVICES if reduce_all else N_CHIPS
    n_peers = n_slots - 1

    def body(p_ref, o_ref, buf, ssem, rsem):
        x, y, c = _position()
        if reduce_all:
            me = 4 * x + 2 * y + c
            peers = [(x ^ (k >> 2 & 1), y ^ (k >> 1 & 1), c ^ (k & 1)) for k in range(1, N_DEVICES)]
        else:
            me = 2 * x + y
            peers = [(cx, cy, c) for cx, cy in _other_chips(x, y)]
        buf[me] = p_ref[...]
        copies = []
        for k, peer in enumerate(peers):
            cp = pltpu.make_async_remote_copy(src_ref=p_ref, dst_ref=buf.at[me], send_sem=ssem.at[k],
                                              recv_sem=rsem.at[k], device_id=peer, device_id_type=MESH)
            cp.start()
            copies.append(cp)
        for k, (px, py, pc) in enumerate(peers):
            slot = 4 * px + 2 * py + pc if reduce_all else 2 * px + py
            pltpu.make_async_remote_copy(src_ref=p_ref, dst_ref=buf.at[slot], send_sem=ssem.at[k],
                                         recv_sem=rsem.at[k], device_id=(px, py, pc), device_id_type=MESH).wait_recv()
        for cp in copies:
            cp.wait_send()
        if reduce_all:
            acc = buf[0]
            for d in range(1, N_DEVICES):
                acc = acc + buf[d]
            o_ref[...] = acc
        else:
            o_ref[...] = buf[...]

    vmem = pl.BlockSpec(memory_space=pltpu.VMEM)
    out_shape = jax.ShapeDtypeStruct((R, LANE) if reduce_all else (N_CHIPS, R, LANE), F32)
    return pl.pallas_call(
        body, name=name, out_shape=out_shape, in_specs=[vmem], out_specs=vmem,
        scratch_shapes=[pltpu.VMEM((n_slots, R, LANE), F32), pltpu.SemaphoreType.DMA((n_peers,)),
                        pltpu.SemaphoreType.DMA((n_peers,))],
        compiler_params=_params())(pack)


def _pack(arrays):
    flat = jnp.concatenate([a.reshape(-1).astype(F32) for a in arrays])
    n = flat.shape[0]
    rows = -(-n // LANE)
    rows = -(-rows // 8) * 8
    return jnp.pad(flat, (0, rows * LANE - n)).reshape(rows, LANE)


def _unpack(pack, shapes, lead=()):
    flat = pack.reshape(lead + (-1,))
    out, off = [], 0
    for s in shapes:
        n = math.prod(s)
        out.append(flat[..., off:off + n].reshape(lead + tuple(s)))
        off += n
    return out


def kernel(x, p, positions, mla_w_in, mla_q_norm, mla_kv_norm, mla_w_uq, mla_w_uk, mla_w_uv, mla_w_o, gla_w_in, gla_w_a2, gla_b_a, gla_o_norm, gla_w_o, ln1_g, ln1_b, ln2_g, ln2_b, ffn_w_up, ffn_conv_w, ffn_conv_b, ffn_w_down, ple_w_proj, ple_w_gate, ple_b_gate, loss_target, m_mla_w_in, m_mla_q_norm, m_mla_kv_norm, m_mla_w_uq, m_mla_w_uk, m_mla_w_uv, m_mla_w_o, m_gla_w_in, m_gla_w_a2, m_gla_b_a, m_gla_o_norm, m_gla_w_o, m_ln1_g, m_ln1_b, m_ln2_g, m_ln2_b, m_ffn_w_up, m_ffn_conv_w, m_ffn_conv_b, m_ffn_w_down, m_ple_w_proj, m_ple_w_gate, m_ple_b_gate, v_mla_w_in, v_mla_q_norm, v_mla_kv_norm, v_mla_w_uq, v_mla_w_uk, v_mla_w_uv, v_mla_w_o, v_gla_w_in, v_gla_w_a2, v_gla_b_a, v_gla_o_norm, v_gla_w_o, v_ln1_g, v_ln1_b, v_ln2_g, v_ln2_b, v_ffn_w_up, v_ffn_conv_w, v_ffn_conv_b, v_ffn_w_down, v_ple_w_proj, v_ple_w_gate, v_ple_b_gate):
    given = dict(locals())
    S, D = x.shape[1], x.shape[2]
    QL, KL = mla_q_norm.shape[1], mla_kv_norm.shape[1]
    H = mla_w_uq.shape[2] * N_CHIPS // (MLA_NOPE + MLA_ROPE)
    QK, VD = gla_b_a.shape[1] * N_CHIPS, gla_o_norm.shape[1] * N_CHIPS
    FF = ffn_w_down.shape[1] * N_CHIPS
    GIN = 2 * QK + 2 * VD + GLA_GATE_RANK
    GIN_PAD = 2 * QK + 2 * VD + LANE
    MIN = QL + KL + MLA_ROPE
    MIN_PAD = QL + KL + LANE
    a_blk = (2 * QK + 2 * VD) // LANE
    r_blk = (2 * QK + VD) // VD
    xi, yi, ci = _position()
    chip = 2 * xi + yi
    c_arr = jnp.reshape(ci, (1,)).astype(jnp.int32)
    j_arr = jnp.reshape(chip, (1,)).astype(jnp.int32)

    big_dims = [1 if n == 'gla_w_in' else d for n, d in BIG]
    placed = [place_shard(f'place_{n}', given[n], d, j_arr) for (n, _), d in zip(BIG, big_dims)]
    full = dict(zip([n for n, _ in BIG], gather_weights(placed, big_dims, [given[n].shape for n, _ in BIG])))
    small_sharded = [n for n, d in SMALL if d is not None]
    spack = exchange_small('gather_small', _pack([given[n] for n in small_sharded]), False)
    parts = _unpack(spack, [given[n].shape for n in small_sharded], lead=(N_CHIPS,))
    for n, part in zip(small_sharded, parts):
        d = dict(SMALL)[n]
        full[n] = jnp.concatenate([part[k] for k in range(N_CHIPS)], axis=d)

    w_min = jnp.pad(full['mla_w_in'], ((0, 0), (0, 0), (0, MIN_PAD - MIN)))
    w_uq = full['mla_w_uq'].reshape(-1, QL, H, MLA_NOPE + MLA_ROPE)
    w_uq = jnp.pad(w_uq, ((0, 0), (0, 0), (0, 0), (0, HEAD_PAD - MLA_NOPE - MLA_ROPE))).reshape(-1, QL, H * HEAD_PAD)
    w_gin = full['gla_w_in'].reshape(-1, N_CHIPS, D, GIN // N_CHIPS).transpose(0, 2, 1, 3).reshape(-1, D, GIN)
    w_gin = jnp.pad(w_gin, ((0, 0), (0, 0), (0, GIN_PAD - GIN)))
    w_a2 = jnp.pad(full['gla_w_a2'], ((0, 0), (0, LANE - GLA_GATE_RANK), (0, 0))).astype(MXU_DTYPE)
    cw_u, cw_g = full['ffn_conv_w'][:, :, :FF], full['ffn_conv_w'][:, :, FF:]
    cb_u, cb_g = ffn_conv_b[:, None, :FF], ffn_conv_b[:, None, FF:]

    pos_row = positions[0]
    cid = pos_row // CHUNK
    t_att = _tile(S, ATT_T)
    cid_col = cid.reshape(S, 1)
    cid_blk = cid.reshape(S // t_att, 1, t_att)
    tab_q = rope_tables(pos_row, MLA_NOPE)
    tab_k = rope_tables(pos_row, 0)

    def row1(a, i):
        return a[i:i + 1]

    saved = []
    xa = x[0]
    xb = xa
    for i in range(DEPTH):
        j = i // 2
        sv = dict(x=xa, xb=xb)
        if i % 2 == 0:
            h = mm(f'mla_in_{i}', xb, w_min, 'nn', F32, b_pre=(j,))
            cq, ckv, kr = mla_pre_fwd(f'mla_pre_{i}', h, row1(mla_q_norm, j), row1(mla_kv_norm, j), tab_k, QL, KL)
            qpre = mm(f'mla_uq_{i}', cq, w_uq, 'nn', F32, b_pre=(j,))
            q = rope_q(f'mla_rope_{i}', qpre, tab_q, H, MXU_DTYPE)
            kn = mm(f'mla_uk_{i}', ckv, full['mla_w_uk'], 'nn', MXU_DTYPE, b_pre=(j,))
            vv = mm(f'mla_uv_{i}', ckv, full['mla_w_uv'], 'nn', MXU_DTYPE, b_pre=(j,))
            o, lse = attn_fwd(f'mla_attn_{i}', q, kn, kr, vv, cid_col, cid_blk, H)
            mix = mm(f'mla_o_{i}', o, full['mla_w_o'], 'nn', F32, b_pre=(j,))
            sv.update(h=h, cq=cq, ckv=ckv, kr=kr, q=q, kn=kn, v=vv, o=o, lse=lse)
        else:
            h = mm(f'gla_in_{i}', xb, w_gin, 'nn', F32, b_pre=(j,))
            qs, la = gla_pre_fwd(f'gla_pre_{i}', h, w_a2[j], row1(full['gla_b_a'], j), QK, a_blk)
            o, states = gla_fwd(f'gla_scan_{i}', qs, h, la, QK, VD)
            og = gla_post_fwd(f'gla_post_{i}', o, h, row1(full['gla_o_norm'], j), VD, r_blk)
            mix = mm(f'gla_o_{i}', og, full['gla_w_o'], 'nn', F32, b_pre=(j,))
            sv.update(h=h, qs=qs, la=la, o=o, states=states, og=og)
        x1, x1b = ln_fwd(f'ln1_{i}', xa, mix, row1(ln1_g, i), row1(ln1_b, i))
        hu = mm(f'ffn_up_u_{i}', x1b, full['ffn_w_up'], 'nn', F32, b_pre=(i,), b_win=(0, D, 0, FF))
        hg = mm(f'ffn_up_g_{i}', x1b, full['ffn_w_up'], 'nn', F32, b_pre=(i,), b_win=(0, D, FF, FF))
        act = conv_fwd(f'ffn_conv_{i}', hu, hg, cw_u[i], cw_g[i], cb_u[i], cb_g[i])
        f = mm(f'ffn_down_{i}', act, full['ffn_w_down'], 'nn', F32, b_pre=(i,))
        x2, x2b = ln_fwd(f'ln2_{i}', x1, f, row1(ln2_g, i), row1(ln2_b, i))
        gp = mm(f'ple_gate_{i}', x2b, full['ple_w_gate'], 'nn', F32, b_pre=(i,))
        pp = mm(f'ple_proj_{i}', p, full['ple_w_proj'], 'nn', F32, a_pre=(i, 0), b_pre=(i,))
        x3, x3b = ple_fwd(f'ple_{i}', x2, gp, pp, row1(ple_b_gate, i))
        sv.update(mix=mix, x1=x1, x1b=x1b, hu=hu, hg=hg, act=act, f=f, x2=x2, x2b=x2b, gp=gp, pp=pp)
        saved.append(sv)
        xa, xb = x3, x3b

    dy, sq = loss_head('loss_head', xa, loss_target[0])
    loss_part = (0.5 / D) * jnp.sum(sq)

    G = {n: None for n, _ in BIG}
    small_g = {n: [None] * given[n].shape[0] for n, _ in SMALL}
    G_shape = dict(mla_w_in=(DEPTH // 2, D, MIN_PAD), mla_w_uq=(DEPTH // 2, QL, H * HEAD_PAD),
                   gla_w_in=(DEPTH // 2, D, GIN_PAD))

    def wgrad(name, lead, a, b, a_pre=(), col0=0):
        shape = G_shape.get(name) or full[name].shape
        G[name] = mm(f'd_{name}_{lead}_{col0}', a, b, 'tn', MXU_DTYPE, a_pre=a_pre,
                     out_stack=(G[name], shape, lead, col0))

    dA, ca, dB = dy, 1.0, None
    for i in reversed(range(DEPTH)):
        j = i // 2
        sv = saved[i]
        dx3, dpp, dgp, dbg = ple_bwd(f'ple_b_{i}', dA, ca, dB, sv['gp'], sv['pp'], row1(ple_b_gate, i))
        small_g['ple_b_gate'][i] = dbg
        wgrad('ple_w_proj', i, p, dpp, a_pre=(i, 0))
        wgrad('ple_w_gate', i, sv['x2b'], dgp)
        dx2 = mm(f'ple_gate_b_{i}', dgp, full['ple_w_gate'], 'nt', F32, b_pre=(i,))
        dz2, dz2b, dg2, db2 = ln_bwd(f'ln2_b_{i}', sv['x1'], sv['f'], dx3, 1.0, dx2, row1(ln2_g, i))
        small_g['ln2_g'][i], small_g['ln2_b'][i] = dg2, db2
        wgrad('ffn_w_down', i, sv['act'], dz2b)
        dact = mm(f'ffn_down_b_{i}', dz2b, full['ffn_w_down'], 'nt', F32, b_pre=(i,))
        dhu, dhg, dcwu, dcwg, dcbu, dcbg = conv_bwd(f'ffn_conv_b_{i}', sv['hu'], sv['hg'], dact,
                                                   cw_u[i], cw_g[i], cb_u[i], cb_g[i])
        small_g['ffn_conv_w'][i] = jnp.concatenate([dcwu, dcwg], -1)
        small_g['ffn_conv_b'][i] = jnp.concatenate([dcbu, dcbg], -1)
        wgrad('ffn_w_up', i, sv['x1b'], dhu)
        wgrad('ffn_w_up', i, sv['x1b'], dhg, col0=FF)
        dx1 = mm(f'ffn_up_bu_{i}', dhu, full['ffn_w_up'], 'nt', F32, b_pre=(i,), b_win=(0, D, 0, FF))
        dx1 = mm(f'ffn_up_bg_{i}', dhg, full['ffn_w_up'], 'nt', F32, b_pre=(i,), b_win=(0, D, FF, FF), acc_in=dx1)
        dz1, dz1b, dg1, db1 = ln_bwd(f'ln1_b_{i}', sv['x'], sv['mix'], dz2, DN_ALPHA, dx1, row1(ln1_g, i))
        small_g['ln1_g'][i], small_g['ln1_b'][i] = dg1, db1
        if i % 2 == 0:
            wgrad('mla_w_o', j, sv['o'], dz1b)
            do = mm(f'mla_o_b_{i}', dz1b, full['mla_w_o'], 'nt', MXU_DTYPE, b_pre=(j,))
            dq, dkn, dv, dkr = attn_bwd(f'mla_attn_b_{i}', sv['q'], sv['kn'], sv['kr'], sv['v'], sv['o'],
                                        sv['lse'], do, cid_col, cid_blk, H)
            dqpre = rope_q_bwd(f'mla_rope_b_{i}', dq, tab_q, H)
            wgrad('mla_w_uq', j, sv['cq'], dqpre)
            wgrad('mla_w_uk', j, sv['ckv'], dkn)
            wgrad('mla_w_uv', j, sv['ckv'], dv)
            dcq = mm(f'mla_uq_b_{i}', dqpre, w_uq, 'nt', F32, b_pre=(j,))
            dckv_a = mm(f'mla_uk_b_{i}', dkn, full['mla_w_uk'], 'nt', F32, b_pre=(j,))
            dckv_b = mm(f'mla_uv_b_{i}', dv, full['mla_w_uv'], 'nt', F32, b_pre=(j,))
            dh, dqn, dkvn = mla_pre_bwd(f'mla_pre_b_{i}', sv['h'], dcq, dckv_a, dckv_b, dkr,
                                        row1(mla_q_norm, j), row1(mla_kv_norm, j), tab_k, QL, KL, H)
            small_g['mla_q_norm'][j], small_g['mla_kv_norm'][j] = dqn, dkvn
            wgrad('mla_w_in', j, sv['xb'], dh)
            dmix = mm(f'mla_in_b_{i}', dh, w_min, 'nt', F32, b_pre=(j,))
        else:
            wgrad('gla_w_o', j, sv['og'], dz1b)
            dog = mm(f'gla_o_b_{i}', dz1b, full['gla_w_o'], 'nt', F32, b_pre=(j,))
            do, dr, don = gla_post_bwd(f'gla_post_b_{i}', sv['o'], sv['h'], dog, row1(full['gla_o_norm'], j), VD, r_blk)
            dq, dk, dv, dla = gla_bwd(f'gla_scan_b_{i}', sv['qs'], sv['h'], sv['la'], sv['states'], do, QK, VD)
            dh, dw2, dba = gla_dh(f'gla_dh_{i}', dq, dk, dv, dr, dla, sv['h'], w_a2[j], row1(full['gla_b_a'], j),
                                  QK, VD, a_blk, GIN_PAD)
            small_g['gla_o_norm'][j], small_g['gla_b_a'][j] = don, dba
            small_g['gla_w_a2'][j] = dw2[:GLA_GATE_RANK]
            wgrad('gla_w_in', j, sv['xb'], dh)
            dmix = mm(f'gla_in_b_{i}', dh, w_gin, 'nt', F32, b_pre=(j,))
        dA, ca, dB = dz1, DN_ALPHA, dmix
    grad_x = axpy('grad_x', dA, ca, dB)[None]

    G['mla_w_in'] = G['mla_w_in'][:, :, :MIN]
    G['mla_w_uq'] = G['mla_w_uq'].reshape(-1, QL, H, HEAD_PAD)[..., :MLA_NOPE + MLA_ROPE].reshape(-1, QL, H * (MLA_NOPE + MLA_ROPE))
    G['gla_w_in'] = G['gla_w_in'][:, :, :GIN].reshape(-1, D, N_CHIPS, GIN // N_CHIPS).transpose(0, 2, 1, 3).reshape(-1, N_CHIPS * D, GIN // N_CHIPS)
    names = [n for n, _ in BIG]
    g_list = [G[n] for n in names]
    shard_shapes = [given[n].shape[1:] for n in names]
    landed = reduce_pair_exchange(g_list)
    partial = [pair_add(f'pair_add_{n}', g, l, c_arr) for n, g, l in zip(names, g_list, landed)]
    arrived = reduce_chip_exchange(partial, big_dims, shard_shapes)
    summed = [chip_sum(f'chip_sum_{n}', pt, ar, d, j_arr, c_arr) for n, pt, ar, d in zip(names, partial, arrived, big_dims)]
    grads = dict(zip(names, share_with_sibling(summed)))

    small_names = [n for n, _ in SMALL]
    small_full = [jnp.concatenate([g.reshape((1,) + g.shape[-(given[n].ndim - 1):]) for g in small_g[n]], 0)
                  for n in small_names]
    pack = _pack([jnp.reshape(loss_part, (1,))] + [jnp.zeros((LANE - 1,), F32)] + small_full)
    red = exchange_small('reduce_small', pack, True)
    red_parts = _unpack(red, [(LANE,)] + [g.shape for g in small_full])
    loss = red_parts[0][0]
    for (n, d), g in zip(SMALL, red_parts[1:]):
        if d is not None:
            width = given[n].shape[d]
            g = lax.dynamic_slice_in_dim(g, chip * width, width, axis=d)
        grads[n] = g

    delta, new_m, new_v = {}, {}, {}
    for n in names:
        shape = given[n].shape
        flat = lambda a: a.reshape(-1, shape[-1])
        d_, m_, v_ = adamw(f'adamw_{n}', flat(given[n]), flat(grads[n]), flat(given['m_' + n]), flat(given['v_' + n]))
        delta[n], new_m[n], new_v[n] = d_.reshape(shape), m_.reshape(shape), v_.reshape(shape)
    packs = [_pack([given[pre + n] for n in small_names]) for pre in ('', 'm_', 'v_')]
    outs = adamw('adamw_small', packs[0], _pack([grads[n] for n in small_names]), packs[1], packs[2])
    small_shapes = [given[n].shape for n in small_names]
    for dst, out in zip((delta, new_m, new_v), outs):
        for n, a in zip(small_names, _unpack(out, small_shapes)):
            dst[n] = a

    return (loss, grad_x, *[grads[n] for n in WEIGHT_NAMES], *[delta[n] for n in WEIGHT_NAMES],
            *[new_m[n] for n in WEIGHT_NAMES], *[new_v[n] for n in WEIGHT_NAMES])
```

```python
import functools
import math
from typing import Callable, NamedTuple

import jax
import jax.numpy as jnp
from jax import lax
from jax.experimental import pallas as pl
from jax.experimental.pallas import tpu as pltpu

F32 = jnp.float32
MXU_DTYPE = jnp.bfloat16

DEPTH = 4
CHUNK = 64
Q_BLOCK = 128
MLA_NOPE = 128
MLA_ROPE = 64
MLA_V = 128
ROPE_THETA = 10000.0
GLA_HEADS = 4
GLA_GATE_RANK = 16
GLA_TAU = 16.0
DN_ALPHA = (2 * DEPTH) ** 0.25
EPS = 1e-5
NEG_INF = -1e30
ADAM_LR = 0.001
ADAM_B1 = 0.9
ADAM_B2 = 0.999
ADAM_EPS = 1e-08
ADAM_WD = 0.01
ADAM_STEP = 10
GELU_C = math.sqrt(2.0 / math.pi)
GELU_A = 0.044715

LANE = 128
HEAD_PAD = 2 * LANE
VMEM_LIMIT_BYTES = 48 * 1024 * 1024

MESH = pl.DeviceIdType.MESH
ANY_SPEC = pl.BlockSpec(memory_space=pl.ANY)

WEIGHT_NAMES = ('mla_w_in', 'mla_q_norm', 'mla_kv_norm', 'mla_w_uq', 'mla_w_uk', 'mla_w_uv', 'mla_w_o',
                'gla_w_in', 'gla_w_a2', 'gla_b_a', 'gla_o_norm', 'gla_w_o', 'ln1_g', 'ln1_b', 'ln2_g', 'ln2_b',
                'ffn_w_up', 'ffn_conv_w', 'ffn_conv_b', 'ffn_w_down', 'ple_w_proj', 'ple_w_gate', 'ple_b_gate')
BIG = (('mla_w_in', 1), ('mla_w_uq', 2), ('mla_w_uk', 2), ('mla_w_uv', 2), ('mla_w_o', 1), ('gla_w_in', 2),
       ('gla_w_o', 1), ('ffn_w_up', 2), ('ffn_w_down', 1), ('ple_w_proj', 2), ('ple_w_gate', 1))
SMALL = (('mla_q_norm', None), ('mla_kv_norm', None), ('gla_w_a2', 2), ('gla_b_a', 1), ('gla_o_norm', 1),
         ('ln1_g', None), ('ln1_b', None), ('ln2_g', None), ('ln2_b', None), ('ffn_conv_w', 2),
         ('ffn_conv_b', None), ('ple_b_gate', None))
N_CHIPS = 4
N_DEVICES = 8


def _params():
    return pltpu.CompilerParams(vmem_limit_bytes=VMEM_LIMIT_BYTES)


def _tile(n, cap, *offsets, unit=LANE):
    g = n
    for o in offsets:
        if o:
            g = math.gcd(g, o)
    best = 0
    for d in range(unit, min(g, cap) + 1, unit):
        if g % d == 0:
            best = d
    if best:
        return best
    assert not any(offsets), (n, offsets)
    return n


def _dot(a, b, dims):
    return lax.dot_general(a.astype(MXU_DTYPE), b.astype(MXU_DTYPE), (dims, ((), ())),
                           preferred_element_type=F32)


NN = ((1,), (0,))
NT = ((1,), (1,))
TN = ((0,), (0,))


def mm(name, a, b, mode, out_dtype, *, a_pre=(), b_pre=(), b_win=None, acc_in=None, out_stack=None):
    a2 = a.shape[len(a_pre):]
    b2 = b.shape[len(b_pre):]
    br0, brn, bc0, bcn = b_win or (0, b2[0], 0, b2[1])
    if mode == 'nn':
        (M, K), N, dims = a2, bcn, NN
        assert brn == K
    elif mode == 'nt':
        (M, K), N, dims = a2, brn, NT
        assert bcn == K
    else:
        (K, M), N, dims = a2, bcn, TN
        assert brn == K
    oc0 = out_stack[3] if out_stack else 0
    tm = _tile(M, 512 if mode == 'tn' else 1024)
    if mode == 'nn':
        tn, tk = _tile(N, 1408, bc0, oc0), _tile(K, 1408, br0)
    elif mode == 'nt':
        tn, tk = _tile(N, 1408, br0, oc0), _tile(K, 1408, bc0)
    else:
        tn, tk = _tile(N, 1408, bc0, oc0), _tile(K, 1024, br0)
    nk = K // tk
    grid = (M // tm, N // tn, nk)

    na, nb = len(a_pre), len(b_pre)
    if mode == 'tn':
        a_spec = pl.BlockSpec((None,) * na + (tk, tm), lambda i, j, k: a_pre + (k, i))
    else:
        a_spec = pl.BlockSpec((None,) * na + (tm, tk), lambda i, j, k: a_pre + (i, k))
    if mode == 'nt':
        b_spec = pl.BlockSpec((None,) * nb + (tn, tk), lambda i, j, k: b_pre + (j + br0 // tn, k + bc0 // tk))
    else:
        b_spec = pl.BlockSpec((None,) * nb + (tk, tn), lambda i, j, k: b_pre + (k + br0 // tk, j + bc0 // tn))
    in_specs, args = [a_spec, b_spec], [a, b]
    if acc_in is not None:
        in_specs.append(pl.BlockSpec((tm, tn), lambda i, j, k: (i, j)))
        args.append(acc_in)
    aliases = {}
    if out_stack is None:
        out_shape = jax.ShapeDtypeStruct((M, N), out_dtype)
        out_spec = pl.BlockSpec((tm, tn), lambda i, j, k: (i, j))
    else:
        buf, full_shape, lead, _ = out_stack
        out_shape = jax.ShapeDtypeStruct(full_shape, out_dtype)
        out_spec = pl.BlockSpec((None, tm, tn), lambda i, j, k: (lead, i, j + oc0 // tn))
        if buf is not None:
            aliases = {len(args): 0}
            in_specs.append(ANY_SPEC)
            args.append(buf)
    has_c, has_alias = acc_in is not None, bool(aliases)

    def body(*refs):
        a_ref, b_ref = refs[0], refs[1]
        c_ref = refs[2] if has_c else None
        o_ref = refs[2 + has_c + has_alias]
        prod = _dot(a_ref[...], b_ref[...], dims)
        if nk == 1:
            if has_c:
                prod = prod + c_ref[...]
            o_ref[...] = prod.astype(out_dtype)
            return
        acc_ref = refs[3 + has_c + has_alias]
        k = pl.program_id(2)

        @pl.when(k == 0)
        def _():
            acc_ref[...] = prod + c_ref[...] if has_c else prod

        @pl.when(k > 0)
        def _():
            acc_ref[...] += prod

        @pl.when(k == nk - 1)
        def _():
            o_ref[...] = acc_ref[...].astype(out_dtype)

    scratch = [pltpu.VMEM((tm, tn), F32)] if nk > 1 else []
    return pl.pallas_call(body, out_shape=out_shape, grid=grid, in_specs=in_specs, out_specs=out_spec,
                          scratch_shapes=scratch, input_output_aliases=aliases, name=name,
                          compiler_params=_params())(*args)


def rowcall(name, body, n_rows, ts, row_ins, full_ins, row_outs, acc_outs=()):
    in_specs, args = [], []
    for arr, width, colblk, pre in row_ins:
        in_specs.append(pl.BlockSpec((None,) * len(pre) + (ts, width),
                                     lambda i, pre=pre, cb=colblk: pre + (i, cb)))
        args.append(arr)
    for arr in full_ins:
        in_specs.append(pl.BlockSpec(arr.shape, lambda i, nd=arr.ndim: (0,) * nd))
        args.append(arr)
    out_shape, out_specs = [], []
    for width, dtype in row_outs:
        out_shape.append(jax.ShapeDtypeStruct((n_rows, width), dtype))
        out_specs.append(pl.BlockSpec((ts, width), lambda i: (i, 0)))
    for shape in acc_outs:
        out_shape.append(jax.ShapeDtypeStruct(shape, F32))
        out_specs.append(pl.BlockSpec(shape, lambda i, nd=len(shape): (0,) * nd))
    n_in = len(args)

    def kern(*refs):
        body(pl.program_id(0), refs[:n_in], refs[n_in:])

    return pl.pallas_call(kern, out_shape=tuple(out_shape), grid=(n_rows // ts,), in_specs=in_specs,
                          out_specs=tuple(out_specs), name=name, compiler_params=_params())(*args)


def _row(arr, width=None, colblk=0, pre=()):
    return (arr, arr.shape[-1] if width is None else width, colblk, pre)


def _init_acc(i, refs):
    @pl.when(i == 0)
    def _():
        for r in refs:
            r[...] = jnp.zeros(r.shape, r.dtype)


def _colsum(v):
    return jnp.sum(v, axis=0, keepdims=True)


def _sigmoid(z):
    return 1.0 / (1.0 + jnp.exp(-z))


def _row_tile(S):
    return _tile(S, 256, unit=16)


def _ln_stats(x_ref, m_ref):
    z = DN_ALPHA * x_ref[...] + m_ref[...]
    mu = jnp.mean(z, -1, keepdims=True)
    zc = z - mu
    var = jnp.mean(zc * zc, -1, keepdims=True)
    r = lax.rsqrt(var + EPS)
    return zc * r, r


def ln_fwd(name, x, m, g, b):
    S, D = x.shape

    def body(i, ins, outs):
        x_ref, m_ref, g_ref, b_ref = ins
        xh, _ = _ln_stats(x_ref, m_ref)
        y = xh * g_ref[...] + b_ref[...]
        outs[0][...] = y
        outs[1][...] = y.astype(MXU_DTYPE)

    return rowcall(name, body, S, _row_tile(S), [_row(x), _row(m)], [g, b], [(D, F32), (D, MXU_DTYPE)])


def ln_bwd(name, x, m, dA, ca, dB, g):
    S, D = x.shape
    has_b = dB is not None

    def body(i, ins, outs):
        x_ref, m_ref, a_ref = ins[:3]
        g_ref = ins[-1]
        dz_ref, dzb_ref, dg_ref, db_ref = outs
        _init_acc(i, (dg_ref, db_ref))
        dy = ca * a_ref[...]
        if has_b:
            dy = dy + ins[3][...]
        xh, r = _ln_stats(x_ref, m_ref)
        dg_ref[...] += _colsum(dy * xh)
        db_ref[...] += _colsum(dy)
        dxh = dy * g_ref[...]
        dz = r * (dxh - jnp.mean(dxh, -1, keepdims=True) - xh * jnp.mean(dxh * xh, -1, keepdims=True))
        dz_ref[...] = dz
        dzb_ref[...] = dz.astype(MXU_DTYPE)

    rows = [_row(x), _row(m), _row(dA)] + ([_row(dB)] if has_b else [])
    return rowcall(name, body, S, _row_tile(S), rows, [g], [(D, F32), (D, MXU_DTYPE)], [(1, D), (1, D)])


def ple_fwd(name, x2, gp, pp, bias):
    S, D = x2.shape

    def body(i, ins, outs):
        x_ref, gp_ref, pp_ref, b_ref = ins
        y = x_ref[...] + _sigmoid(gp_ref[...] + b_ref[...]) * pp_ref[...]
        outs[0][...] = y
        outs[1][...] = y.astype(MXU_DTYPE)

    return rowcall(name, body, S, _row_tile(S), [_row(x2), _row(gp), _row(pp)], [bias],
                   [(D, F32), (D, MXU_DTYPE)])


def ple_bwd(name, dA, ca, dB, gp, pp, bias):
    S, D = gp.shape
    has_b = dB is not None

    def body(i, ins, outs):
        a_ref = ins[0]
        gp_ref, pp_ref, b_ref = ins[-3:]
        dx_ref, dpp_ref, dgp_ref, db_ref = outs
        _init_acc(i, (db_ref,))
        dx = ca * a_ref[...]
        if has_b:
            dx = dx + ins[1][...]
        gate = _sigmoid(gp_ref[...] + b_ref[...])
        dgp = dx * pp_ref[...] * gate * (1.0 - gate)
        dx_ref[...] = dx
        dpp_ref[...] = (dx * gate).astype(MXU_DTYPE)
        dgp_ref[...] = dgp.astype(MXU_DTYPE)
        db_ref[...] += _colsum(dgp)

    rows = [_row(dA)] + ([_row(dB)] if has_b else []) + [_row(gp), _row(pp)]
    return rowcall(name, body, S, _row_tile(S), rows, [bias],
                   [(D, F32), (D, MXU_DTYPE), (D, MXU_DTYPE)], [(1, D)])


def loss_head(name, y, target):
    S, D = y.shape

    def body(i, ins, outs):
        _init_acc(i, (outs[1],))
        e = ins[0][...] - ins[1][...]
        outs[0][...] = e * (1.0 / D)
        outs[1][...] += _colsum(e * e)

    return rowcall(name, body, S, _row_tile(S), [_row(y), _row(target)], [], [(D, F32)], [(1, D)])


def axpy(name, a, ca, b):
    S, D = a.shape

    def body(i, ins, outs):
        outs[0][...] = ca * ins[0][...] + ins[1][...]

    return rowcall(name, body, S, _row_tile(S), [_row(a), _row(b)], [], [(D, F32)])[0]


HALF_ROPE = MLA_ROPE // 2


def _rope(x, c, sa, sb):
    n = x.shape[-1]
    return x * c + pltpu.roll(x, n - HALF_ROPE, 1) * sa + pltpu.roll(x, HALF_ROPE, 1) * sb


def _rope_t(d, c, sa, sb):
    n = d.shape[-1]
    return d * c + pltpu.roll(d * sa, HALF_ROPE, 1) + pltpu.roll(d * sb, n - HALF_ROPE, 1)


def rope_tables(positions_row, n_lead):
    inv = 1.0 / (ROPE_THETA ** (jnp.arange(0, MLA_ROPE, 2, dtype=F32) / MLA_ROPE))
    ang = positions_row.astype(F32)[:, None] * inv
    cos, sin = jnp.cos(ang), jnp.sin(ang)
    S = cos.shape[0]
    z = jnp.zeros((S, HALF_ROPE), F32)
    tail = jnp.zeros((S, LANE - MLA_ROPE), F32)
    c = jnp.concatenate([jnp.ones((S, n_lead), F32), cos, cos, tail], -1)
    sa = jnp.concatenate([jnp.zeros((S, n_lead), F32), -sin, z, tail], -1)
    sb = jnp.concatenate([jnp.zeros((S, n_lead), F32), z, sin, tail], -1)
    return c, sa, sb


def mla_pre_fwd(name, h, qn, kvn, tab_k, QL, KL):
    S = h.shape[0]

    def body(i, ins, outs):
        h_ref, c_ref, sa_ref, sb_ref, qn_ref, kvn_ref = ins
        cq = h_ref[:, 0:QL]
        ckv = h_ref[:, QL:QL + KL]
        kr = h_ref[:, QL + KL:QL + KL + LANE]
        outs[0][...] = (cq * lax.rsqrt(jnp.mean(cq * cq, -1, keepdims=True) + EPS) * qn_ref[...]).astype(MXU_DTYPE)
        outs[1][...] = (ckv * lax.rsqrt(jnp.mean(ckv * ckv, -1, keepdims=True) + EPS) * kvn_ref[...]).astype(MXU_DTYPE)
        outs[2][...] = _rope(kr, c_ref[...], sa_ref[...], sb_ref[...]).astype(MXU_DTYPE)

    rows = [_row(h)] + [_row(t) for t in tab_k]
    return rowcall(name, body, S, _row_tile(S), rows, [qn, kvn],
                   [(QL, MXU_DTYPE), (KL, MXU_DTYPE), (LANE, MXU_DTYPE)])


def _rms_bwd(x, g, dy):
    r = lax.rsqrt(jnp.mean(x * x, -1, keepdims=True) + EPS)
    dg = _colsum(dy * x * r)
    dxg = dy * g
    dx = r * dxg - x * (r * r * r) * jnp.mean(dxg * x, -1, keepdims=True)
    return dx, dg


def mla_pre_bwd(name, h, dcq, dckv_a, dckv_b, dkr_heads, qn, kvn, tab_k, QL, KL, H):
    S, HW = h.shape

    def body(i, ins, outs):
        h_ref, dcq_ref, da_ref, db_ref, dkr_ref, c_ref, sa_ref, sb_ref, qn_ref, kvn_ref = ins
        dh_ref, dqn_ref, dkvn_ref = outs
        _init_acc(i, (dqn_ref, dkvn_ref))
        dx, dg = _rms_bwd(h_ref[:, 0:QL], qn_ref[...], dcq_ref[...])
        dh_ref[:, 0:QL] = dx.astype(MXU_DTYPE)
        dqn_ref[...] += dg
        dx, dg = _rms_bwd(h_ref[:, QL:QL + KL], kvn_ref[...], da_ref[...] + db_ref[...])
        dh_ref[:, QL:QL + KL] = dx.astype(MXU_DTYPE)
        dkvn_ref[...] += dg
        d = dkr_ref[:, 0:LANE]
        for hh in range(1, H):
            d = d + dkr_ref[:, hh * LANE:(hh + 1) * LANE]
        dh_ref[:, QL + KL:QL + KL + LANE] = _rope_t(d, c_ref[...], sa_ref[...], sb_ref[...]).astype(MXU_DTYPE)

    rows = [_row(h), _row(dcq), _row(dckv_a), _row(dckv_b), _row(dkr_heads)] + [_row(t) for t in tab_k]
    return rowcall(name, body, S, _row_tile(S), rows, [qn, kvn], [(HW, MXU_DTYPE)], [(1, QL), (1, KL)])


def rope_q(name, qpre, tab_q, H, out_dtype):
    S = qpre.shape[0]

    def body(i, ins, outs):
        c, sa, sb = ins[1][...], ins[2][...], ins[3][...]
        for hh in range(H):
            sl = slice(hh * HEAD_PAD, (hh + 1) * HEAD_PAD)
            outs[0][:, sl] = _rope(ins[0][:, sl], c, sa, sb).astype(out_dtype)

    rows = [_row(qpre)] + [_row(t) for t in tab_q]
    return rowcall(name, body, S, _row_tile(S), rows, [], [(H * HEAD_PAD, out_dtype)])[0]


ATT_T = 512


def _att_mask_t(ck_col, cq_row, k0, q0, t):
    kpos = k0 + lax.broadcasted_iota(jnp.int32, (t, 1), 0)
    qpos = q0 + lax.broadcasted_iota(jnp.int32, (1, t), 1)
    return (ck_col <= cq_row) & (kpos <= (qpos | (Q_BLOCK - 1)))


def _transpose(a):
    return a.astype(F32).T.astype(a.dtype)


def hosted_call(body, *, name, grid, in_specs, out_specs, out_shape, scratch_shapes, args, rider):
    if rider is None:
        return pl.pallas_call(body, name=name, grid=grid, in_specs=in_specs, out_specs=out_specs,
                              out_shape=out_shape, scratch_shapes=scratch_shapes,
                              compiler_params=_params())(*args), ()
    n_in, n_out, n_scr = len(args), len(out_shape), len(scratch_shapes)
    r_in, r_out = len(rider.ins), len(rider.out_shapes)

    def hosted(*refs):
        ins, refs = refs[:n_in], refs[n_in:]
        rin, refs = refs[:r_in], refs[r_in:]
        outs, refs = refs[:n_out], refs[n_out:]
        rout, refs = refs[:r_out], refs[r_out:]
        scr, sems = refs[:n_scr], refs[n_scr:]
        ids = [pl.program_id(a) for a in range(len(grid))]
        first = functools.reduce(jnp.logical_and, [i == 0 for i in ids])
        last = functools.reduce(jnp.logical_and, [i == g - 1 for i, g in zip(ids, grid)])

        @pl.when(first)
        def _():
            rider.start(rin, rout, sems)

        body(*ins, *outs, *scr)

        @pl.when(last)
        def _():
            rider.finish(rin, rout, sems)

    results = pl.pallas_call(
        hosted, name=name, grid=grid, in_specs=list(in_specs) + [ANY_SPEC] * r_in,
        out_specs=tuple(out_specs) + (ANY_SPEC,) * r_out, out_shape=tuple(out_shape) + tuple(rider.out_shapes),
        scratch_shapes=list(scratch_shapes) + [pltpu.SemaphoreType.DMA((k,)) for k in rider.sem_counts],
        input_output_aliases={n_in + k: n_out + k for k in range(r_in)} if rider.in_place else {},
        compiler_params=pltpu.CompilerParams(vmem_limit_bytes=VMEM_LIMIT_BYTES, has_side_effects=True),
    )(*args, *rider.ins)
    return results[:n_out], results[n_out:]


def attn_fwd(name, q, kn, kr, v, cid_col, cid_blk, H, rider=None):
    S = q.shape[0]
    t = _tile(S, ATT_T)
    n = S // t
    scale = (MLA_NOPE + MLA_ROPE) ** -0.5

    def body(q_ref, kn_ref, kr_ref, v_ref, cc_ref, cr_ref, o_ref, lse_ref, k_scr, vt_scr):
        i = pl.program_id(1)

        @pl.when(i == 0)
        def _():
            k_scr[:, 0:LANE] = kn_ref[...]
            k_scr[:, LANE:HEAD_PAD] = kr_ref[...]
            for jj in range(n):
                vt_scr[jj] = _transpose(v_ref[jj * t:(jj + 1) * t, :])

        qv = q_ref[...]
        cq = cr_ref[i]

        def update(j, carry, masked):
            m, l, acc = carry
            k0 = pl.multiple_of(j * t, t)
            s = _dot(k_scr[pl.ds(k0, t), :], qv, NT) * scale
            if masked:
                s = jnp.where(_att_mask_t(cc_ref[pl.ds(k0, t), :], cq, k0, i * t, t), s, NEG_INF)
            m_new = jnp.maximum(m, jnp.max(s, 0, keepdims=True))
            p = jnp.exp(s - m_new)
            alpha = jnp.exp(m - m_new)
            l = alpha * l + jnp.sum(p, 0, keepdims=True)
            acc = alpha * acc + _dot(vt_scr[j], p, NN)
            return m_new, l, acc

        init = (jnp.full((1, t), NEG_INF, F32), jnp.zeros((1, t), F32), jnp.zeros((MLA_V, t), F32))
        carry = lax.fori_loop(0, i, lambda j, c: update(j, c, False), init)
        m, l, acc = update(i, carry, True)
        o_ref[...] = (acc / l).T.astype(o_ref.dtype)
        lse_ref[...] = m + jnp.log(l)

    return hosted_call(
        body, name=name, grid=(H, n),
        in_specs=[pl.BlockSpec((t, HEAD_PAD), lambda h, i: (i, h)),
                  pl.BlockSpec((S, MLA_NOPE), lambda h, i: (0, h)),
                  pl.BlockSpec((S, LANE), lambda h, i: (0, 0)),
                  pl.BlockSpec((S, MLA_V), lambda h, i: (0, h)),
                  pl.BlockSpec((S, 1), lambda h, i: (0, 0)),
                  pl.BlockSpec(cid_blk.shape, lambda h, i: (0, 0, 0))],
        out_specs=(pl.BlockSpec((t, MLA_V), lambda h, i: (i, h)),
                   pl.BlockSpec((None, None, 1, t), lambda h, i: (h, i, 0, 0))),
        scratch_shapes=[pltpu.VMEM((S, HEAD_PAD), MXU_DTYPE), pltpu.VMEM((n, MLA_V, t), MXU_DTYPE)],
        out_shape=(jax.ShapeDtypeStruct((S, H * MLA_V), MXU_DTYPE), jax.ShapeDtypeStruct((H, n, 1, t), F32)),
        args=(q, kn, kr, v, cid_col, cid_blk), rider=rider)


def attn_bwd(name, q, kn, kr, v, o, lse, do, cid_col, cid_blk, H, rider=None):
    S = q.shape[0]
    t = _tile(S, ATT_T)
    n = S // t
    scale = (MLA_NOPE + MLA_ROPE) ** -0.5

    def body(q_ref, kn_ref, kr_ref, v_ref, o_ref, lse_ref, do_ref, cc_ref, cr_ref,
             dq_ref, dkn_ref, dv_ref, dkr_ref, delta_scr, dk_scr, dv_scr):
        j = pl.program_id(1)

        @pl.when(j == 0)
        def _():
            dq_ref[...] = jnp.zeros(dq_ref.shape, F32)
            for ii in range(n):
                sl = slice(ii * t, (ii + 1) * t)
                prod = do_ref[sl, :].astype(F32) * o_ref[sl, :].astype(F32)
                delta_scr[ii] = jnp.sum(prod.T, 0, keepdims=True)

        kv = jnp.concatenate([kn_ref[...], kr_ref[...]], axis=-1)
        kt = _transpose(kv)
        vv = v_ref[...]
        ck = cc_ref[...]
        k0 = j * t
        dk_scr[...] = jnp.zeros(dk_scr.shape, F32)
        dv_scr[...] = jnp.zeros(dv_scr.shape, F32)

        def update(i, masked):
            q0 = pl.multiple_of(i * t, t)
            qv = q_ref[pl.ds(q0, t), :]
            dov = do_ref[pl.ds(q0, t), :]
            s = _dot(kv, qv, NT) * scale
            if masked:
                s = jnp.where(_att_mask_t(ck, cr_ref[i], k0, q0, t), s, NEG_INF)
            p = jnp.exp(s - lse_ref[i])
            dv_scr[...] += _dot(p, dov, NN)
            dp = _dot(vv, dov, NT)
            ds = p * (dp - delta_scr[i]) * scale
            dk_scr[...] += _dot(ds, qv, NN)
            dq_ref[i] += _dot(kt, ds, NN)

        def step(i, carry):
            update(i, False)
            return carry

        update(j, True)
        lax.fori_loop(j + 1, n, step, 0)
        dkn_ref[...] = dk_scr[:, 0:LANE].astype(dkn_ref.dtype)
        dkr_ref[...] = dk_scr[:, LANE:HEAD_PAD]
        dv_ref[...] = dv_scr[...].astype(dv_ref.dtype)

    head_rows = lambda w: pl.BlockSpec((S, w), lambda h, j: (0, h))
    tile_rows = lambda w: pl.BlockSpec((t, w), lambda h, j: (j, h))
    return hosted_call(
        body, name=name, grid=(H, n),
        in_specs=[head_rows(HEAD_PAD), tile_rows(MLA_NOPE), pl.BlockSpec((t, LANE), lambda h, j: (j, 0)),
                  tile_rows(MLA_V), head_rows(MLA_V),
                  pl.BlockSpec((None, n, 1, t), lambda h, j: (h, 0, 0, 0)), head_rows(MLA_V),
                  pl.BlockSpec((t, 1), lambda h, j: (j, 0)),
                  pl.BlockSpec(cid_blk.shape, lambda h, j: (0, 0, 0))],
        out_specs=(pl.BlockSpec((None, n, HEAD_PAD, t), lambda h, j: (h, 0, 0, 0)),
                   tile_rows(MLA_NOPE), tile_rows(MLA_V), tile_rows(LANE)),
        scratch_shapes=[pltpu.VMEM((n, 1, t), F32), pltpu.VMEM((t, HEAD_PAD), F32), pltpu.VMEM((t, MLA_V), F32)],
        out_shape=(jax.ShapeDtypeStruct((H, n, HEAD_PAD, t), F32), jax.ShapeDtypeStruct((S, H * MLA_NOPE), MXU_DTYPE),
                   jax.ShapeDtypeStruct((S, H * MLA_V), MXU_DTYPE), jax.ShapeDtypeStruct((S, H * LANE), F32)),
        args=(q, kn, kr, v, o, lse, do, cid_col, cid_blk), rider=rider)


def rope_q_bwd(name, dq_t, tab_q_t, H):
    _, n, _, t = dq_t.shape

    def body(d_ref, c_ref, sa_ref, sb_ref, o_ref):
        d = d_ref[...]
        out = (d * c_ref[...] + pltpu.roll(d * sa_ref[...], HALF_ROPE, 0)
               + pltpu.roll(d * sb_ref[...], HEAD_PAD - HALF_ROPE, 0))
        o_ref[...] = out.astype(o_ref.dtype)

    tab = pl.BlockSpec((HEAD_PAD, t), lambda i, h: (0, i))
    return pl.pallas_call(
        body, name=name, grid=(n, H), out_shape=jax.ShapeDtypeStruct((H * HEAD_PAD, n * t), MXU_DTYPE),
        in_specs=[pl.BlockSpec((None, None, HEAD_PAD, t), lambda i, h: (h, i, 0, 0)), tab, tab, tab],
        out_specs=pl.BlockSpec((HEAD_PAD, t), lambda i, h: (h, i)),
        compiler_params=_params())(dq_t, *tab_q_t)


GLA_GROUP = 8


def _prefix_rows(x):
    n = x.shape[0]
    row = lax.broadcasted_iota(jnp.int32, x.shape, 0)
    d = 1
    while d < n:
        x = x + jnp.where(row >= d, pltpu.roll(x, d, 0), 0.0)
        d *= 2
    return x


def _suffix_rows(x):
    n = x.shape[0]
    row = lax.broadcasted_iota(jnp.int32, x.shape, 0)
    d = 1
    while d < n:
        x = x + jnp.where(row < n - d, pltpu.roll(x, n - d, 0), 0.0)
        d *= 2
    return x


def _log_sigmoid(z):
    return jnp.minimum(z, 0.0) - jnp.log(1.0 + jnp.exp(-jnp.abs(z)))


def gla_pre_fwd(name, h, w2p, b_a, QK, a_blk):
    S = h.shape[0]
    dk = QK // GLA_HEADS

    def body(i, ins, outs):
        q_ref, a_ref, w_ref, b_ref = ins
        outs[0][...] = (q_ref[...] * (dk ** -0.5)).astype(MXU_DTYPE)
        z = _dot(a_ref[...], w_ref[...], NN) + b_ref[...]
        outs[1][...] = _log_sigmoid(z) / GLA_TAU

    return rowcall(name, body, S, _row_tile(S), [_row(h, QK, 0), _row(h, LANE, a_blk)], [w2p, b_a],
                   [(QK, MXU_DTYPE), (QK, F32)])


def _gla_specs(S, QK, VD, rows, gmap):
    dk, dv = QK // GLA_HEADS, VD // GLA_HEADS
    return dict(
        qs=pl.BlockSpec((rows, dk), lambda h, g: (gmap(g), h)),
        k=pl.BlockSpec((rows, dk), lambda h, g: (gmap(g), QK // dk + h)),
        v=pl.BlockSpec((rows, dv), lambda h, g: (gmap(g), 2 * QK // dv + h)),
        la=pl.BlockSpec((rows, dk), lambda h, g: (gmap(g), h)),
        o=pl.BlockSpec((rows, dv), lambda h, g: (gmap(g), h)))


def gla_fwd(name, qs, h, la, QK, VD):
    S = qs.shape[0]
    dk, dv = QK // GLA_HEADS, VD // GLA_HEADS
    n_chunks = S // CHUNK
    cg = min(GLA_GROUP, n_chunks)
    rows = cg * CHUNK
    sp = _gla_specs(S, QK, VD, rows, lambda g: g)

    def body(q_ref, k_ref, v_ref, la_ref, o_ref, st_ref, state):
        @pl.when(pl.program_id(1) == 0)
        def _():
            state[...] = jnp.zeros(state.shape, F32)

        for c in range(cg):
            sl = slice(c * CHUNK, (c + 1) * CHUNK)
            cum = _prefix_rows(la_ref[sl, :])
            tot = cum[CHUNK - 1:CHUNK, :]
            kdec = k_ref[sl, :] * jnp.exp(tot - cum)
            st = state[...] * jnp.exp(tot) + _dot(v_ref[sl, :], kdec, TN)
            state[...] = st
            st_ref[c] = st
            o_ref[sl, :] = _dot(q_ref[sl, :], st, NT)

    return pl.pallas_call(
        body, name=name, grid=(GLA_HEADS, n_chunks // cg),
        out_shape=(jax.ShapeDtypeStruct((S, VD), F32), jax.ShapeDtypeStruct((GLA_HEADS, n_chunks, dv, dk), F32)),
        in_specs=[sp['qs'], sp['k'], sp['v'], sp['la']],
        out_specs=(sp['o'], pl.BlockSpec((None, cg, dv, dk), lambda h, g: (h, g, 0, 0))),
        scratch_shapes=[pltpu.VMEM((dv, dk), F32)],
        compiler_params=_params())(qs, h, h, la)


def gla_bwd(name, qs, h, la, states, do, QK, VD):
    S = qs.shape[0]
    dk, dv = QK // GLA_HEADS, VD // GLA_HEADS
    n_chunks = S // CHUNK
    cg = min(GLA_GROUP, n_chunks)
    ng = n_chunks // cg
    rows = cg * CHUNK
    rev = lambda g: ng - 1 - g
    sp = _gla_specs(S, QK, VD, rows, rev)

    def body(q_ref, k_ref, v_ref, la_ref, st_ref, prev_ref, do_ref, dq_ref, dk_ref, dv_ref, dla_ref, dst):
        g = pl.program_id(1)

        @pl.when(g == 0)
        def _():
            dst[...] = jnp.zeros(dst.shape, F32)

        first_group = (g == ng - 1).astype(F32)
        for c in reversed(range(cg)):
            sl = slice(c * CHUNK, (c + 1) * CHUNK)
            cum = _prefix_rows(la_ref[sl, :])
            tot = cum[CHUNK - 1:CHUNK, :]
            e = jnp.exp(tot - cum)
            kdec = k_ref[sl, :] * e
            decay = jnp.exp(tot)
            st = st_ref[c]
            st_prev = st_ref[c - 1] if c > 0 else prev_ref[0] * (1.0 - first_group)
            dov = do_ref[sl, :]
            qv = q_ref[sl, :]
            dq_ref[sl, :] = (_dot(dov, st, NN) * (dk ** -0.5)).astype(dq_ref.dtype)
            d = dst[...] + _dot(dov, qv, TN)
            ddecay = _colsum(d * st_prev)
            dkdec = _dot(v_ref[sl, :], d, NN)
            dv_ref[sl, :] = _dot(kdec, d, NT).astype(dv_ref.dtype)
            dk_ref[sl, :] = (dkdec * e).astype(dk_ref.dtype)
            darg = dkdec * kdec
            dtot = _colsum(darg) + ddecay * decay
            dla_ref[sl, :] = dtot - _suffix_rows(darg)
            dst[...] = d * decay

    prev_spec = pl.BlockSpec((None, 1, dv, dk), lambda h, g: (h, jnp.maximum(rev(g) * cg - 1, 0), 0, 0))
    return pl.pallas_call(
        body, name=name, grid=(GLA_HEADS, ng),
        out_shape=(jax.ShapeDtypeStruct((S, QK), MXU_DTYPE), jax.ShapeDtypeStruct((S, QK), MXU_DTYPE),
                   jax.ShapeDtypeStruct((S, VD), MXU_DTYPE), jax.ShapeDtypeStruct((S, QK), F32)),
        in_specs=[sp['qs'], sp['k'], sp['v'], sp['la'],
                  pl.BlockSpec((None, cg, dv, dk), lambda h, g: (h, rev(g), 0, 0)), prev_spec, sp['o']],
        out_specs=(sp['qs'], sp['qs'], sp['o'], sp['la']),
        scratch_shapes=[pltpu.VMEM((dv, dk), F32)],
        compiler_params=_params())(qs, h, h, la, states, states, do)


def _head_norm(o):
    mu = jnp.mean(o, -1, keepdims=True)
    oc = o - mu
    r = lax.rsqrt(jnp.mean(oc * oc, -1, keepdims=True) + EPS)
    return oc * r, r


def gla_post_fwd(name, o, h, o_norm, VD, r_blk):
    S = o.shape[0]
    dv = VD // GLA_HEADS

    def body(i, ins, outs):
        o_ref, r_ref, w_ref = ins
        for hh in range(GLA_HEADS):
            sl = slice(hh * dv, (hh + 1) * dv)
            xh, _ = _head_norm(o_ref[:, sl])
            r = r_ref[:, sl]
            outs[0][:, sl] = (xh * w_ref[:, sl] * (r * _sigmoid(r))).astype(MXU_DTYPE)

    return rowcall(name, body, S, _row_tile(S), [_row(o), _row(h, VD, r_blk)], [o_norm], [(VD, MXU_DTYPE)])[0]


def gla_post_bwd(name, o, h, dog, o_norm, VD, r_blk):
    S = o.shape[0]
    dv = VD // GLA_HEADS

    def body(i, ins, outs):
        o_ref, r_ref, dog_ref, w_ref = ins
        do_ref, dr_ref, dw_ref = outs
        _init_acc(i, (dw_ref,))
        for hh in range(GLA_HEADS):
            sl = slice(hh * dv, (hh + 1) * dv)
            xh, rs = _head_norm(o_ref[:, sl])
            r = r_ref[:, sl]
            w = w_ref[:, sl]
            dog = dog_ref[:, sl]
            sg = _sigmoid(r)
            dn = dog * (r * sg)
            dr_ref[:, sl] = (dog * (xh * w) * (sg * (1.0 + r * (1.0 - sg)))).astype(MXU_DTYPE)
            dw_ref[:, sl] += _colsum(dn * xh)
            dxh = dn * w
            do_ref[:, sl] = rs * (dxh - jnp.mean(dxh, -1, keepdims=True) - xh * jnp.mean(dxh * xh, -1, keepdims=True))

    return rowcall(name, body, S, _row_tile(S), [_row(o), _row(h, VD, r_blk), _row(dog)], [o_norm],
                   [(VD, F32), (VD, MXU_DTYPE)], [(1, VD)])


def gla_dh(name, dq, dk, dv, dr, dla, h, w2p, b_a, QK, VD, a_blk, HW):
    S = dq.shape[0]

    def body(i, ins, outs):
        dq_ref, dk_ref, dv_ref, dr_ref, dla_ref, a_ref, w_ref, b_ref = ins
        dh_ref, dw_ref, db_ref = outs
        _init_acc(i, (dw_ref, db_ref))
        a = a_ref[...]
        z = _dot(a, w_ref[...], NN) + b_ref[...]
        dz = dla_ref[...] * (1.0 / GLA_TAU) * _sigmoid(-z)
        dw_ref[...] += _dot(a, dz, TN)
        db_ref[...] += _colsum(dz)
        dh_ref[:, 0:QK] = dq_ref[...]
        dh_ref[:, QK:2 * QK] = dk_ref[...]
        dh_ref[:, 2 * QK:2 * QK + VD] = dv_ref[...]
        dh_ref[:, 2 * QK + VD:2 * QK + 2 * VD] = dr_ref[...]
        dh_ref[:, 2 * QK + 2 * VD:HW] = _dot(dz, w_ref[...], NT).astype(MXU_DTYPE)

    rows = [_row(dq), _row(dk), _row(dv), _row(dr), _row(dla), _row(h, LANE, a_blk)]
    return rowcall(name, body, S, _row_tile(S), rows, [w2p, b_a], [(HW, MXU_DTYPE)], [(LANE, QK), (1, QK)])


CONV_ROWS = 512
HALO = 8


def _gelu(x):
    return 0.5 * x * (1.0 + jnp.tanh(GELU_C * (x + GELU_A * x * x * x)))


def _gelu_grad(x):
    t = jnp.tanh(GELU_C * (x + GELU_A * x * x * x))
    return 0.5 * (1.0 + t) + 0.5 * x * (1.0 - t * t) * GELU_C * (1.0 + 3.0 * GELU_A * x * x)


def _shift_down(x, prev, d):
    row = lax.broadcasted_iota(jnp.int32, x.shape, 0)
    out = pltpu.roll(x, d, 0)
    for t in range(d):
        out = jnp.where(row == t, prev[HALO - d + t:HALO - d + t + 1, :], out)
    return out


def _shift_up(x, nxt, d):
    n = x.shape[0]
    row = lax.broadcasted_iota(jnp.int32, x.shape, 0)
    out = pltpu.roll(x, n - d, 0)
    for t in range(d):
        out = jnp.where(row == n - d + t, nxt[t:t + 1, :], out)
    return out


def _conv_taps(ref, r, rc):
    x = ref[r * rc:(r + 1) * rc, :]
    prev = ref[r * rc - HALO:r * rc, :] if r > 0 else jnp.zeros((HALO, x.shape[1]), F32)
    return x, _shift_down(x, prev, 1), _shift_down(x, prev, 2)


def _conv_apply(taps, w_ref, b_ref):
    x0, x1, x2 = taps
    return x2 * w_ref[0:1, :] + x1 * w_ref[1:2, :] + x0 * w_ref[2:3, :] + b_ref[...]


def conv_fwd(name, hu, hg, cw_u, cw_g, cb_u, cb_g):
    S, F = hu.shape
    tc = LANE
    rc = _tile(S, CONV_ROWS, unit=16)

    def body(u_ref, g_ref, wu_ref, wg_ref, bu_ref, bg_ref, a_ref):
        for r in range(S // rc):
            uc = _conv_apply(_conv_taps(u_ref, r, rc), wu_ref, bu_ref)
            gc = _conv_apply(_conv_taps(g_ref, r, rc), wg_ref, bg_ref)
            a_ref[r * rc:(r + 1) * rc, :] = (uc * _gelu(gc)).astype(a_ref.dtype)

    col = lambda rows: pl.BlockSpec((rows, tc), lambda j: (0, j))
    return pl.pallas_call(
        body, name=name, grid=(F // tc,), out_shape=jax.ShapeDtypeStruct((S, F), MXU_DTYPE),
        in_specs=[col(S), col(S), col(3), col(3), col(1), col(1)], out_specs=col(S),
        compiler_params=_params())(hu, hg, cw_u, cw_g, cb_u, cb_g)


def conv_bwd(name, hu, hg, da, cw_u, cw_g, cb_u, cb_g):
    S, F = hu.shape
    tc = LANE
    rc = _tile(S, CONV_ROWS, unit=16)
    nr = S // rc

    def body(u_ref, g_ref, da_ref, wu_ref, wg_ref, bu_ref, bg_ref,
             dhu_ref, dhg_ref, dwu_ref, dwg_ref, dbu_ref, dbg_ref, du_scr, dg_scr):
        dw = [[jnp.zeros((1, tc), F32) for _ in range(3)] for _ in range(2)]
        db = [jnp.zeros((1, tc), F32) for _ in range(2)]
        for r in range(nr):
            sl = slice(r * rc, (r + 1) * rc)
            ut = _conv_taps(u_ref, r, rc)
            gt = _conv_taps(g_ref, r, rc)
            uc = _conv_apply(ut, wu_ref, bu_ref)
            gc = _conv_apply(gt, wg_ref, bg_ref)
            dav = da_ref[sl, :]
            duc = dav * _gelu(gc)
            dgc = dav * uc * _gelu_grad(gc)
            du_scr[sl, :] = duc
            dg_scr[sl, :] = dgc
            for part, (taps, d) in enumerate(((ut, duc), (gt, dgc))):
                db[part] = db[part] + _colsum(d)
                for tap in range(3):
                    dw[part][tap] = dw[part][tap] + _colsum(taps[2 - tap] * d)
        for part, (w_out, b_out) in enumerate(((dwu_ref, dbu_ref), (dwg_ref, dbg_ref))):
            b_out[...] = db[part]
            for tap in range(3):
                w_out[tap:tap + 1, :] = dw[part][tap]
        for scr, w_ref, out in ((du_scr, wu_ref, dhu_ref), (dg_scr, wg_ref, dhg_ref)):
            for r in range(nr):
                d = scr[r * rc:(r + 1) * rc, :]
                nxt = scr[(r + 1) * rc:(r + 1) * rc + HALO, :] if r + 1 < nr else jnp.zeros((HALO, tc), F32)
                dh = d * w_ref[2:3, :] + _shift_up(d, nxt, 1) * w_ref[1:2, :] + _shift_up(d, nxt, 2) * w_ref[0:1, :]
                out[r * rc:(r + 1) * rc, :] = dh.astype(out.dtype)

    col = lambda rows: pl.BlockSpec((rows, tc), lambda j: (0, j))
    sds = jax.ShapeDtypeStruct
    return pl.pallas_call(
        body, name=name, grid=(F // tc,),
        out_shape=(sds((S, F), MXU_DTYPE), sds((S, F), MXU_DTYPE), sds((3, F), F32), sds((3, F), F32),
                   sds((1, F), F32), sds((1, F), F32)),
        in_specs=[col(S), col(S), col(S), col(3), col(3), col(1), col(1)],
        out_specs=(col(S), col(S), col(3), col(3), col(1), col(1)),
        scratch_shapes=[pltpu.VMEM((S, tc), F32), pltpu.VMEM((S, tc), F32)],
        compiler_params=_params())(hu, hg, da, cw_u, cw_g, cb_u, cb_g)


def adamw(name, w, g, m, v):
    R, C = w.shape
    tr = _tile(R, max(8, (1 << 19) // max(C, 1) // 8 * 8), unit=8)

    def body(w_ref, g_ref, m_ref, v_ref, d_ref, nm_ref, nv_ref):
        gv = g_ref[...]
        mn = ADAM_B1 * m_ref[...] + (1.0 - ADAM_B1) * gv
        vn = ADAM_B2 * v_ref[...] + (1.0 - ADAM_B2) * (gv * gv)
        m_hat = mn / (1.0 - ADAM_B1 ** ADAM_STEP)
        v_hat = vn / (1.0 - ADAM_B2 ** ADAM_STEP)
        d_ref[...] = -ADAM_LR * (m_hat / (jnp.sqrt(v_hat) + ADAM_EPS) + ADAM_WD * w_ref[...])
        nm_ref[...] = mn
        nv_ref[...] = vn

    spec = pl.BlockSpec((tr, C), lambda i: (i, 0))
    shp = jax.ShapeDtypeStruct((R, C), F32)
    return pl.pallas_call(body, name=name, grid=(R // tr,), out_shape=(shp, shp, shp), in_specs=[spec] * 4,
                          out_specs=(spec, spec, spec), compiler_params=_params())(w, g, m, v)


def _position():
    return lax.axis_index('x'), lax.axis_index('y'), lax.axis_index('c')


def _other_chips(x, y):
    return ((1 - x, y), (x, 1 - y), (1 - x, 1 - y))


def _window(ref, dim, lo, n_lead, jj, rs, cs):
    if dim == 1:
        return ref.at[pl.ds(lo, n_lead), pl.ds(pl.multiple_of(jj * rs, 16), rs), :]
    return ref.at[pl.ds(lo, n_lead), :, pl.ds(pl.multiple_of(jj * cs, LANE), cs)]


def _hbm_call(name, body, out_shapes, n_sems, args, aliases=None):
    return pl.pallas_call(
        body, name=name, out_shape=tuple(out_shapes), in_specs=[ANY_SPEC] * len(args),
        out_specs=tuple(ANY_SPEC for _ in out_shapes),
        scratch_shapes=[pltpu.SemaphoreType.DMA((n,)) for n in n_sems],
        input_output_aliases=aliases or {},
        compiler_params=pltpu.CompilerParams(has_side_effects=True))(*args)


def place_shard(name, shard, dim, j_arr):
    L, rs, cs = shard.shape
    full_shape = (L, rs * N_CHIPS, cs) if dim == 1 else (L, rs, cs * N_CHIPS)
    tr = _tile(rs, max(16, (1 << 19) // cs // 16 * 16), unit=16)

    def body(j_ref, s_ref, o_ref):
        o_ref[...] = s_ref[...].astype(o_ref.dtype)

    if dim == 1:
        out_spec = pl.BlockSpec((None, tr, cs), lambda l, i, j: (l, j[0] * (rs // tr) + i, 0))
    else:
        out_spec = pl.BlockSpec((None, tr, cs), lambda l, i, j: (l, i, j[0]))
    grid_spec = pltpu.PrefetchScalarGridSpec(
        num_scalar_prefetch=1, grid=(L, rs // tr),
        in_specs=[pl.BlockSpec((None, tr, cs), lambda l, i, j: (l, i, 0))], out_specs=out_spec)
    return pl.pallas_call(body, name=name, grid_spec=grid_spec,
                          out_shape=jax.ShapeDtypeStruct(full_shape, MXU_DTYPE),
                          compiler_params=_params())(j_arr, shard)


class Exchange(NamedTuple):
    ins: tuple
    out_shapes: tuple
    in_place: bool
    sem_counts: tuple
    start: Callable
    finish: Callable


def run_exchange(name, ex):
    n_in, n_out = len(ex.ins), len(ex.out_shapes)

    def body(*refs):
        ins, outs, sems = refs[:n_in], refs[n_in:n_in + n_out], refs[n_in + n_out:]
        ex.start(ins, outs, sems)
        ex.finish(ins, outs, sems)

    return _hbm_call(name, body, ex.out_shapes, ex.sem_counts, ex.ins,
                     {k: k for k in range(n_in)} if ex.in_place else None)


def _rcopy(src, dst, ssem, rsem, device):
    return pltpu.make_async_remote_copy(src_ref=src, dst_ref=dst, send_sem=ssem, recv_sem=rsem,
                                        device_id=device, device_id_type=MESH)


def gather_exchange(fulls, dims, shard_shapes, items):
    n = len(items)

    def win(full, w, lo, nl, jj):
        return _window(full[w], dims[w], lo, nl, jj, shard_shapes[w][1], shard_shapes[w][2])

    def start(_, full, sems):
        ssem, rsem = sems[0], sems[1]
        x, y, c = _position()
        for it, (w, lo, nl, owner) in enumerate(items):
            @pl.when(c == owner)
            def _():
                own = win(full, w, lo, nl, 2 * x + y)
                for k, (cx, cy) in enumerate(_other_chips(x, y)):
                    _rcopy(own, own, ssem.at[3 * it + k], rsem.at[3 * it + k], (cx, cy, c)).start()

    def finish(_, full, sems):
        ssem, rsem, s2sem, r2sem = sems
        x, y, c = _position()
        sibling = (x, y, 1 - c)
        for it, (w, lo, nl, owner) in enumerate(items):
            @pl.when(c == owner)
            def _():
                for k, (cx, cy) in enumerate(_other_chips(x, y)):
                    theirs = win(full, w, lo, nl, 2 * cx + cy)
                    _rcopy(theirs, theirs, ssem.at[3 * it + k], rsem.at[3 * it + k], (cx, cy, c)).wait_recv()
                    _rcopy(theirs, theirs, s2sem.at[3 * it + k], r2sem.at[3 * it + k], sibling).start()
        for it, (w, lo, nl, owner) in enumerate(items):
            @pl.when(c == owner)
            def _():
                own = win(full, w, lo, nl, 2 * x + y)
                for k, (cx, cy) in enumerate(_other_chips(x, y)):
                    theirs = win(full, w, lo, nl, 2 * cx + cy)
                    _rcopy(own, own, ssem.at[3 * it + k], rsem.at[3 * it + k], (cx, cy, c)).wait_send()
                    _rcopy(theirs, theirs, s2sem.at[3 * it + k], r2sem.at[3 * it + k], sibling).wait_send()

            @pl.when(c != owner)
            def _():
                for k, (cx, cy) in enumerate(_other_chips(x, y)):
                    theirs = win(full, w, lo, nl, 2 * cx + cy)
                    _rcopy(theirs, theirs, s2sem.at[3 * it + k], r2sem.at[3 * it + k], sibling).wait_recv()

    shapes = tuple(jax.ShapeDtypeStruct(f.shape, f.dtype) for f in fulls)
    return Exchange(tuple(fulls), shapes, True, (3 * n,) * 4, start, finish)


def pair_exchange(grads, owners):
    n = len(grads)

    def start(gr, land, sems):
        x, y, c = _position()
        for it, owner in enumerate(owners):
            @pl.when(c != owner)
            def _():
                _rcopy(gr[it], land[it], sems[0].at[it], sems[1].at[it], (x, y, 1 - c)).start()

    def finish(gr, land, sems):
        x, y, c = _position()
        for it, owner in enumerate(owners):
            cp = _rcopy(gr[it], land[it], sems[0].at[it], sems[1].at[it], (x, y, 1 - c))
            pl.when(c != owner)(cp.wait_send)
            pl.when(c == owner)(cp.wait_recv)

    shapes = tuple(jax.ShapeDtypeStruct(g.shape, g.dtype) for g in grads)
    return Exchange(tuple(grads), shapes, False, (n, n), start, finish)


def pair_add(name, g, landed, owner, c_arr):
    R, C = g.shape
    tr = _tile(R, max(16, (1 << 20) // C // 16 * 16), unit=16)

    def body(c_ref, g_ref, l_ref, o_ref):
        @pl.when(c_ref[0] == owner)
        def _():
            o_ref[...] = (g_ref[...].astype(F32) + l_ref[...].astype(F32)).astype(o_ref.dtype)

    spec = pl.BlockSpec((tr, C), lambda i, c: (jnp.where(c[0] == owner, i, 0), 0))
    grid_spec = pltpu.PrefetchScalarGridSpec(num_scalar_prefetch=1, grid=(R // tr,), in_specs=[spec, spec],
                                             out_specs=spec)
    return pl.pallas_call(body, name=name, grid_spec=grid_spec, out_shape=jax.ShapeDtypeStruct(g.shape, g.dtype),
                          compiler_params=_params())(c_arr, g, landed)


def _window2(ref, dim, jj, rs, cs):
    if dim == 1:
        return ref.at[pl.ds(pl.multiple_of(jj * rs, 16), rs), :]
    return ref.at[:, pl.ds(pl.multiple_of(jj * cs, LANE), cs)]


def chip_exchange(partials, dims, shard_shapes, owners):
    n = len(partials)

    def copies(ps, land, sems, it):
        x, y, c = _position()
        rs, cs = shard_shapes[it]
        return [_rcopy(_window2(ps[it], dims[it], 2 * cx + cy, rs, cs), land[it].at[k],
                       sems[0].at[3 * it + k], sems[1].at[3 * it + k], (cx, cy, c))
                for k, (cx, cy) in enumerate(_other_chips(x, y))]

    def start(ps, land, sems):
        c = lax.axis_index('c')
        for it, owner in enumerate(owners):
            @pl.when(c == owner)
            def _():
                for cp in copies(ps, land, sems, it):
                    cp.start()

    def finish(ps, land, sems):
        c = lax.axis_index('c')
        for it, owner in enumerate(owners):
            @pl.when(c == owner)
            def _():
                for cp in copies(ps, land, sems, it):
                    cp.wait()

    shapes = tuple(jax.ShapeDtypeStruct((3,) + tuple(s), p.dtype) for p, s in zip(partials, shard_shapes))
    return Exchange(tuple(partials), shapes, False, (3 * n, 3 * n), start, finish)


def chip_sum(name, partial, landed, dim, buf, depth, layer, owner, j_arr, c_arr):
    _, rs, cs = landed.shape
    tr = _tile(rs, max(16, (1 << 19) // cs // 16 * 16), unit=16)
    has_buf = buf is not None

    def body(*refs):
        core_ref, p_ref, a_ref, b_ref, c_ref = refs[1:6]
        o_ref = refs[6 + has_buf]

        @pl.when(core_ref[0] == owner)
        def _():
            o_ref[...] = ((p_ref[...].astype(F32) + a_ref[...].astype(F32)) + b_ref[...].astype(F32)) + c_ref[...].astype(F32)

    def on(c, index):
        return jnp.where(c[0] == owner, index, 0)

    if dim == 1:
        own = pl.BlockSpec((tr, cs), lambda i, j, c: (on(c, j[0] * (rs // tr) + i), 0))
    else:
        own = pl.BlockSpec((tr, cs), lambda i, j, c: (on(c, i), on(c, j[0])))
    arrived = [pl.BlockSpec((None, tr, cs), lambda i, j, c, k=k: (k, on(c, i), 0)) for k in range(3)]
    grid_spec = pltpu.PrefetchScalarGridSpec(
        num_scalar_prefetch=2, grid=(rs // tr,), in_specs=[own] + arrived + ([ANY_SPEC] if has_buf else []),
        out_specs=pl.BlockSpec((None, tr, cs), lambda i, j, c: (layer, on(c, i), 0)))
    args = (j_arr, c_arr, partial, landed, landed, landed) + ((buf,) if has_buf else ())
    return pl.pallas_call(body, name=name, grid_spec=grid_spec,
                          out_shape=jax.ShapeDtypeStruct((depth, rs, cs), F32),
                          input_output_aliases={6: 0} if has_buf else {},
                          compiler_params=_params())(*args)


def share_exchange(bufs, items):
    n = len(items)

    def copy(full, sems, it):
        w, lo, nl, _ = items[it]
        x, y, c = _position()
        own = full[w].at[pl.ds(lo, nl)]
        return _rcopy(own, own, sems[0].at[it], sems[1].at[it], (x, y, 1 - c))

    def start(_, full, sems):
        c = lax.axis_index('c')
        for it in range(n):
            pl.when(c == items[it][3])(copy(full, sems, it).start)

    def finish(_, full, sems):
        c = lax.axis_index('c')
        for it in range(n):
            cp = copy(full, sems, it)
            pl.when(c == items[it][3])(cp.wait_send)
            pl.when(c != items[it][3])(cp.wait_recv)

    shapes = tuple(jax.ShapeDtypeStruct(f.shape, f.dtype) for f in bufs)
    return Exchange(tuple(bufs), shapes, True, (n, n), start, finish)


def exchange_small(name, pack, reduce_all):
    R = pack.shape[0]
    n_slots = N_DEVICES if reduce_all else N_CHIPS
    n_peers = n_slots - 1

    def body(p_ref, o_ref, buf, ssem, rsem):
        x, y, c = _position()
        if reduce_all:
            me = 4 * x + 2 * y + c
            peers = [(x ^ (k >> 2 & 1), y ^ (k >> 1 & 1), c ^ (k & 1)) for k in range(1, N_DEVICES)]
        else:
            me = 2 * x + y
            peers = [(cx, cy, c) for cx, cy in _other_chips(x, y)]
        buf[me] = p_ref[...]
        copies = []
        for k, peer in enumerate(peers):
            cp = pltpu.make_async_remote_copy(src_ref=p_ref, dst_ref=buf.at[me], send_sem=ssem.at[k],
                                              recv_sem=rsem.at[k], device_id=peer, device_id_type=MESH)
            cp.start()
            copies.append(cp)
        for k, (px, py, pc) in enumerate(peers):
            slot = 4 * px + 2 * py + pc if reduce_all else 2 * px + py
            pltpu.make_async_remote_copy(src_ref=p_ref, dst_ref=buf.at[slot], send_sem=ssem.at[k],
                                         recv_sem=rsem.at[k], device_id=(px, py, pc), device_id_type=MESH).wait_recv()
        for cp in copies:
            cp.wait_send()
        if reduce_all:
            acc = buf[0]
            for d in range(1, N_DEVICES):
                acc = acc + buf[d]
            o_ref[...] = acc
        else:
            o_ref[...] = buf[...]

    vmem = pl.BlockSpec(memory_space=pltpu.VMEM)
    out_shape = jax.ShapeDtypeStruct((R, LANE) if reduce_all else (N_CHIPS, R, LANE), F32)
    return pl.pallas_call(
        body, name=name, out_shape=out_shape, in_specs=[vmem], out_specs=vmem,
        scratch_shapes=[pltpu.VMEM((n_slots, R, LANE), F32), pltpu.SemaphoreType.DMA((n_peers,)),
                        pltpu.SemaphoreType.DMA((n_peers,))],
        compiler_params=_params())(pack)


def _pack(arrays):
    flat = jnp.concatenate([a.reshape(-1).astype(F32) for a in arrays])
    n = flat.shape[0]
    rows = -(-n // LANE)
    rows = -(-rows // 8) * 8
    return jnp.pad(flat, (0, rows * LANE - n)).reshape(rows, LANE)


def _unpack(pack, shapes, lead=()):
    flat = pack.reshape(lead + (-1,))
    out, off = [], 0
    for s in shapes:
        n = math.prod(s)
        out.append(flat[..., off:off + n].reshape(lead + tuple(s)))
        off += n
    return out


def kernel(x, p, positions, mla_w_in, mla_q_norm, mla_kv_norm, mla_w_uq, mla_w_uk, mla_w_uv, mla_w_o, gla_w_in, gla_w_a2, gla_b_a, gla_o_norm, gla_w_o, ln1_g, ln1_b, ln2_g, ln2_b, ffn_w_up, ffn_conv_w, ffn_conv_b, ffn_w_down, ple_w_proj, ple_w_gate, ple_b_gate, loss_target, m_mla_w_in, m_mla_q_norm, m_mla_kv_norm, m_mla_w_uq, m_mla_w_uk, m_mla_w_uv, m_mla_w_o, m_gla_w_in, m_gla_w_a2, m_gla_b_a, m_gla_o_norm, m_gla_w_o, m_ln1_g, m_ln1_b, m_ln2_g, m_ln2_b, m_ffn_w_up, m_ffn_conv_w, m_ffn_conv_b, m_ffn_w_down, m_ple_w_proj, m_ple_w_gate, m_ple_b_gate, v_mla_w_in, v_mla_q_norm, v_mla_kv_norm, v_mla_w_uq, v_mla_w_uk, v_mla_w_uv, v_mla_w_o, v_gla_w_in, v_gla_w_a2, v_gla_b_a, v_gla_o_norm, v_gla_w_o, v_ln1_g, v_ln1_b, v_ln2_g, v_ln2_b, v_ffn_w_up, v_ffn_conv_w, v_ffn_conv_b, v_ffn_w_down, v_ple_w_proj, v_ple_w_gate, v_ple_b_gate):
    given = dict(locals())
    S, D = x.shape[1], x.shape[2]
    QL, KL = mla_q_norm.shape[1], mla_kv_norm.shape[1]
    H = mla_w_uq.shape[2] * N_CHIPS // (MLA_NOPE + MLA_ROPE)
    QK, VD = gla_b_a.shape[1] * N_CHIPS, gla_o_norm.shape[1] * N_CHIPS
    FF = ffn_w_down.shape[1] * N_CHIPS
    GIN = 2 * QK + 2 * VD + GLA_GATE_RANK
    GIN_PAD = 2 * QK + 2 * VD + LANE
    MIN = QL + KL + MLA_ROPE
    MIN_PAD = QL + KL + LANE
    a_blk = (2 * QK + 2 * VD) // LANE
    r_blk = (2 * QK + VD) // VD
    xi, yi, ci = _position()
    chip = 2 * xi + yi
    c_arr = jnp.reshape(ci, (1,)).astype(jnp.int32)
    j_arr = jnp.reshape(chip, (1,)).astype(jnp.int32)

    names = [n for n, _ in BIG]
    big_dims = [1 if n == 'gla_w_in' else d for n, d in BIG]
    shapes = [given[n].shape for n in names]

    def layer_of(name, idx):
        return 2 * idx if name.startswith('mla') else 2 * idx + 1 if name.startswith('gla') else idx

    def owner_of(layer):
        return layer // (DEPTH // 2)

    def layer_runs(wanted):
        out = []
        for w, n in enumerate(names):
            for owner in (0, 1):
                idxs = [k for k in range(shapes[w][0])
                        if owner_of(layer_of(n, k)) == owner and wanted(layer_of(n, k))]
                if idxs:
                    assert idxs == list(range(idxs[0], idxs[0] + len(idxs)))
                    out.append((w, idxs[0], len(idxs), owner))
        return out

    placed = [place_shard(f'place_{n}', given[n], d, j_arr) for n, d in zip(names, big_dims)]
    first = run_exchange('gather_first', gather_exchange(placed, big_dims, shapes, layer_runs(lambda l: l == 0)))
    gather_rest = gather_exchange(first, big_dims, shapes, layer_runs(lambda l: l > 0))
    full = dict(zip(names, first))
    small_sharded = [n for n, d in SMALL if d is not None]
    spack = exchange_small('gather_small', _pack([given[n] for n in small_sharded]), False)
    parts = _unpack(spack, [given[n].shape for n in small_sharded], lead=(N_CHIPS,))
    for n, part in zip(small_sharded, parts):
        d = dict(SMALL)[n]
        full[n] = jnp.concatenate([part[k] for k in range(N_CHIPS)], axis=d)

    def mla_in_weights(j):
        w_in = jnp.pad(full['mla_w_in'][j], ((0, 0), (0, MIN_PAD - MIN)))
        w_uq = full['mla_w_uq'][j].reshape(QL, H, MLA_NOPE + MLA_ROPE)
        w_uq = jnp.pad(w_uq, ((0, 0), (0, 0), (0, HEAD_PAD - MLA_NOPE - MLA_ROPE))).reshape(QL, H * HEAD_PAD)
        return w_in, w_uq

    def gla_in_weight(j):
        w = full['gla_w_in'][j].reshape(N_CHIPS, D, GIN // N_CHIPS).transpose(1, 0, 2).reshape(D, GIN)
        return jnp.pad(w, ((0, 0), (0, GIN_PAD - GIN)))

    w_a2 = jnp.pad(full['gla_w_a2'], ((0, 0), (0, LANE - GLA_GATE_RANK), (0, 0))).astype(MXU_DTYPE)
    cw_u, cw_g = full['ffn_conv_w'][:, :, :FF], full['ffn_conv_w'][:, :, FF:]
    cb_u, cb_g = ffn_conv_b[:, None, :FF], ffn_conv_b[:, None, FF:]

    pos_row = positions[0]
    cid = pos_row // CHUNK
    t_att = _tile(S, ATT_T)
    cid_col = cid.reshape(S, 1)
    cid_blk = cid.reshape(S // t_att, 1, t_att)
    tab_q = rope_tables(pos_row, MLA_NOPE)
    tab_k = rope_tables(pos_row, 0)

    def row1(a, i):
        return a[i:i + 1]

    saved = []
    xa = x[0]
    xb = xa
    for i in range(DEPTH):
        j = i // 2
        sv = dict(x=xa, xb=xb)
        if i % 2 == 0:
            w_in, w_uq = mla_in_weights(j)
            h = mm(f'mla_in_{i}', xb, w_in, 'nn', F32)
            cq, ckv, kr = mla_pre_fwd(f'mla_pre_{i}', h, row1(mla_q_norm, j), row1(mla_kv_norm, j), tab_k, QL, KL)
            qpre = mm(f'mla_uq_{i}', cq, w_uq, 'nn', F32)
            q = rope_q(f'mla_rope_{i}', qpre, tab_q, H, MXU_DTYPE)
            kn = mm(f'mla_uk_{i}', ckv, full['mla_w_uk'], 'nn', MXU_DTYPE, b_pre=(j,))
            vv = mm(f'mla_uv_{i}', ckv, full['mla_w_uv'], 'nn', MXU_DTYPE, b_pre=(j,))
            (o, lse), gathered = attn_fwd(f'mla_attn_{i}', q, kn, kr, vv, cid_col, cid_blk, H,
                                          rider=gather_rest if i == 0 else None)
            if i == 0:
                full.update(zip(names, gathered))
            mix = mm(f'mla_o_{i}', o, full['mla_w_o'], 'nn', F32, b_pre=(j,))
            sv.update(h=h, cq=cq, ckv=ckv, kr=kr, q=q, kn=kn, v=vv, o=o, lse=lse, w_in=w_in, w_uq=w_uq)
        else:
            w_gin = gla_in_weight(j)
            sv.update(w_gin=w_gin)
            h = mm(f'gla_in_{i}', xb, w_gin, 'nn', F32)
            qs, la = gla_pre_fwd(f'gla_pre_{i}', h, w_a2[j], row1(full['gla_b_a'], j), QK, a_blk)
            o, states = gla_fwd(f'gla_scan_{i}', qs, h, la, QK, VD)
            og = gla_post_fwd(f'gla_post_{i}', o, h, row1(full['gla_o_norm'], j), VD, r_blk)
            mix = mm(f'gla_o_{i}', og, full['gla_w_o'], 'nn', F32, b_pre=(j,))
            sv.update(h=h, qs=qs, la=la, o=o, states=states, og=og)
        x1, x1b = ln_fwd(f'ln1_{i}', xa, mix, row1(ln1_g, i), row1(ln1_b, i))
        hu = mm(f'ffn_up_u_{i}', x1b, full['ffn_w_up'], 'nn', F32, b_pre=(i,), b_win=(0, D, 0, FF))
        hg = mm(f'ffn_up_g_{i}', x1b, full['ffn_w_up'], 'nn', F32, b_pre=(i,), b_win=(0, D, FF, FF))
        act = conv_fwd(f'ffn_conv_{i}', hu, hg, cw_u[i], cw_g[i], cb_u[i], cb_g[i])
        f = mm(f'ffn_down_{i}', act, full['ffn_w_down'], 'nn', F32, b_pre=(i,))
        x2, x2b = ln_fwd(f'ln2_{i}', x1, f, row1(ln2_g, i), row1(ln2_b, i))
        gp = mm(f'ple_gate_{i}', x2b, full['ple_w_gate'], 'nn', F32, b_pre=(i,))
        pp = mm(f'ple_proj_{i}', p, full['ple_w_proj'], 'nn', F32, a_pre=(i, 0), b_pre=(i,))
        x3, x3b = ple_fwd(f'ple_{i}', x2, gp, pp, row1(ple_b_gate, i))
        sv.update(mix=mix, x1=x1, x1b=x1b, hu=hu, hg=hg, act=act, f=f, x2=x2, x2b=x2b, gp=gp, pp=pp)
        saved.append(sv)
        xa, xb = x3, x3b

    dy, sq = loss_head('loss_head', xa, loss_target[0])
    loss_part = (0.5 / D) * jnp.sum(sq)

    small_g = {n: [None] * given[n].shape[0] for n, _ in SMALL}
    tab_q_t = tuple(t.T for t in tab_q)
    shard_buf = {n: None for n in names}

    def reduce_begin(tag, items):
        owners = [owner_of(layer) for _, _, layer, _ in items]
        landed = run_exchange(f'pair_exchange_{tag}', pair_exchange([g for *_, g in items], owners))
        partial = [pair_add(f'pair_add_{n}_{layer}', g, l, o, c_arr)
                   for (n, _, layer, g), l, o in zip(items, landed, owners)]
        dims = [big_dims[names.index(n)] for n, *_ in items]
        exchange = chip_exchange(partial, dims, [given[n].shape[1:] for n, *_ in items], owners)
        return exchange, (items, partial, dims, owners)

    def reduce_end(context, arrived):
        items, partial, dims, owners = context
        for (n, idx, layer, _), pt, ar, d, o in zip(items, partial, arrived, dims, owners):
            shard_buf[n] = chip_sum(f'chip_sum_{n}_{layer}', pt, ar, d, shard_buf[n], given[n].shape[0], idx, o,
                                    j_arr, c_arr)

    def wgrad(name, a, b, a_pre=()):
        return mm(f'd_{name}_{i}', a, b, 'tn', MXU_DTYPE, a_pre=a_pre)

    dA, ca, dB = dy, 1.0, None
    waiting, riding = [], None
    for i in reversed(range(DEPTH)):
        j = i // 2
        sv = saved[i]
        lg = {}
        dx3, dpp, dgp, dbg = ple_bwd(f'ple_b_{i}', dA, ca, dB, sv['gp'], sv['pp'], row1(ple_b_gate, i))
        small_g['ple_b_gate'][i] = dbg
        lg['ple_w_proj'] = wgrad('ple_w_proj', p, dpp, a_pre=(i, 0))
        lg['ple_w_gate'] = wgrad('ple_w_gate', sv['x2b'], dgp)
        dx2 = mm(f'ple_gate_b_{i}', dgp, full['ple_w_gate'], 'nt', F32, b_pre=(i,))
        dz2, dz2b, dg2, db2 = ln_bwd(f'ln2_b_{i}', sv['x1'], sv['f'], dx3, 1.0, dx2, row1(ln2_g, i))
        small_g['ln2_g'][i], small_g['ln2_b'][i] = dg2, db2
        lg['ffn_w_down'] = wgrad('ffn_w_down', sv['act'], dz2b)
        dact = mm(f'ffn_down_b_{i}', dz2b, full['ffn_w_down'], 'nt', F32, b_pre=(i,))
        dhu, dhg, dcwu, dcwg, dcbu, dcbg = conv_bwd(f'ffn_conv_b_{i}', sv['hu'], sv['hg'], dact,
                                                   cw_u[i], cw_g[i], cb_u[i], cb_g[i])
        small_g['ffn_conv_w'][i] = jnp.concatenate([dcwu, dcwg], -1)
        small_g['ffn_conv_b'][i] = jnp.concatenate([dcbu, dcbg], -1)
        up = mm(f'd_ffn_w_up_u_{i}', sv['x1b'], dhu, 'tn', MXU_DTYPE, out_stack=(None, (1, D, 2 * FF), 0, 0))
        up = mm(f'd_ffn_w_up_g_{i}', sv['x1b'], dhg, 'tn', MXU_DTYPE, out_stack=(up, (1, D, 2 * FF), 0, FF))
        lg['ffn_w_up'] = up[0]
        dx1 = mm(f'ffn_up_bu_{i}', dhu, full['ffn_w_up'], 'nt', F32, b_pre=(i,), b_win=(0, D, 0, FF))
        dx1 = mm(f'ffn_up_bg_{i}', dhg, full['ffn_w_up'], 'nt', F32, b_pre=(i,), b_win=(0, D, FF, FF), acc_in=dx1)
        dz1, dz1b, dg1, db1 = ln_bwd(f'ln1_b_{i}', sv['x'], sv['mix'], dz2, DN_ALPHA, dx1, row1(ln1_g, i))
        small_g['ln1_g'][i], small_g['ln1_b'][i] = dg1, db1
        if i % 2 == 0:
            lg['mla_w_o'] = wgrad('mla_w_o', sv['o'], dz1b)
            do = mm(f'mla_o_b_{i}', dz1b, full['mla_w_o'], 'nt', MXU_DTYPE, b_pre=(j,))
            if waiting:
                riding, context = reduce_begin('_'.join(str(l) for l, _ in waiting),
                                               [item for _, items in waiting for item in items])
                waiting = []
            (dq, dkn, dv, dkr), arrived = attn_bwd(f'mla_attn_b_{i}', sv['q'], sv['kn'], sv['kr'], sv['v'], sv['o'],
                                                   sv['lse'], do, cid_col, cid_blk, H, rider=riding)
            if riding is not None:
                reduce_end(context, arrived)
                riding = None
            dqpre_t = rope_q_bwd(f'mla_rope_b_{i}', dq, tab_q_t, H)
            uq_t = mm(f'd_mla_w_uq_{i}', dqpre_t, sv['cq'], 'nn', MXU_DTYPE)
            lg['mla_w_uq'] = uq_t.T.reshape(QL, H, HEAD_PAD)[..., :MLA_NOPE + MLA_ROPE].reshape(QL, -1)
            lg['mla_w_uk'] = wgrad('mla_w_uk', sv['ckv'], dkn)
            lg['mla_w_uv'] = wgrad('mla_w_uv', sv['ckv'], dv)
            dcq = mm(f'mla_uq_b_{i}', dqpre_t, sv['w_uq'].T, 'tn', F32)
            dckv_a = mm(f'mla_uk_b_{i}', dkn, full['mla_w_uk'], 'nt', F32, b_pre=(j,))
            dckv_b = mm(f'mla_uv_b_{i}', dv, full['mla_w_uv'], 'nt', F32, b_pre=(j,))
            dh, dqn, dkvn = mla_pre_bwd(f'mla_pre_b_{i}', sv['h'], dcq, dckv_a, dckv_b, dkr,
                                        row1(mla_q_norm, j), row1(mla_kv_norm, j), tab_k, QL, KL, H)
            small_g['mla_q_norm'][j], small_g['mla_kv_norm'][j] = dqn, dkvn
            lg['mla_w_in'] = wgrad('mla_w_in', sv['xb'], dh)[:, :MIN]
            dmix = mm(f'mla_in_b_{i}', dh, sv['w_in'], 'nt', F32)
        else:
            lg['gla_w_o'] = wgrad('gla_w_o', sv['og'], dz1b)
            dog = mm(f'gla_o_b_{i}', dz1b, full['gla_w_o'], 'nt', F32, b_pre=(j,))
            do, dr, don = gla_post_bwd(f'gla_post_b_{i}', sv['o'], sv['h'], dog, row1(full['gla_o_norm'], j), VD, r_blk)
            dq, dk, dv, dla = gla_bwd(f'gla_scan_b_{i}', sv['qs'], sv['h'], sv['la'], sv['states'], do, QK, VD)
            dh, dw2, dba = gla_dh(f'gla_dh_{i}', dq, dk, dv, dr, dla, sv['h'], w_a2[j], row1(full['gla_b_a'], j),
                                  QK, VD, a_blk, GIN_PAD)
            small_g['gla_o_norm'][j], small_g['gla_b_a'][j] = don, dba
            small_g['gla_w_a2'][j] = dw2[:GLA_GATE_RANK]
            g_in = wgrad('gla_w_in', sv['xb'], dh)[:, :GIN]
            lg['gla_w_in'] = g_in.reshape(D, N_CHIPS, GIN // N_CHIPS).transpose(1, 0, 2).reshape(N_CHIPS * D, -1)
            dmix = mm(f'gla_in_b_{i}', dh, sv['w_gin'], 'nt', F32)
        dA, ca, dB = dz1, DN_ALPHA, dmix
        waiting.append((i, [(n, i if n.startswith(('ffn', 'ple')) else j, i, g) for n, g in lg.items()]))
    grad_x = axpy('grad_x', dA, ca, dB)[None]

    exchange, context = reduce_begin('_'.join(str(l) for l, _ in waiting), [item for _, items in waiting for item in items])
    reduce_end(context, run_exchange('chip_exchange_last', exchange))
    shared = run_exchange('share_with_sibling',
                          share_exchange([shard_buf[n] for n in names], layer_runs(lambda l: True)))
    grads = dict(zip(names, shared))

    small_names = [n for n, _ in SMALL]
    small_full = [jnp.concatenate([g.reshape((1,) + g.shape[-(given[n].ndim - 1):]) for g in small_g[n]], 0)
                  for n in small_names]
    pack = _pack([jnp.reshape(loss_part, (1,))] + [jnp.zeros((LANE - 1,), F32)] + small_full)
    red = exchange_small('reduce_small', pack, True)
    red_parts = _unpack(red, [(LANE,)] + [g.shape for g in small_full])
    loss = red_parts[0][0]
    for (n, d), g in zip(SMALL, red_parts[1:]):
        if d is not None:
            width = given[n].shape[d]
            g = lax.dynamic_slice_in_dim(g, chip * width, width, axis=d)
        grads[n] = g

    delta, new_m, new_v = {}, {}, {}
    for n in names:
        shape = given[n].shape
        flat = lambda a: a.reshape(-1, shape[-1])
        d_, m_, v_ = adamw(f'adamw_{n}', flat(given[n]), flat(grads[n]), flat(given['m_' + n]), flat(given['v_' + n]))
        delta[n], new_m[n], new_v[n] = d_.reshape(shape), m_.reshape(shape), v_.reshape(shape)
    packs = [_pack([given[pre + n] for n in small_names]) for pre in ('', 'm_', 'v_')]
    outs = adamw('adamw_small', packs[0], _pack([grads[n] for n in small_names]), packs[1], packs[2])
    small_shapes = [given[n].shape for n in small_names]
    for dst, out in zip((delta, new_m, new_v), outs):
        for n, a in zip(small_names, _unpack(out, small_shapes)):
            dst[n] = a

    return (loss, grad_x, *[grads[n] for n in WEIGHT_NAMES], *[delta[n] for n in WEIGHT_NAMES],
            *[new_m[n] for n in WEIGHT_NAMES], *[new_v[n] for n in WEIGHT_NAMES])
```

```python
import functools
import math
from typing import Callable, NamedTuple

import jax
import jax.numpy as jnp
from jax import lax
from jax.experimental import pallas as pl
from jax.experimental.pallas import tpu as pltpu

F32 = jnp.float32
MXU_DTYPE = jnp.bfloat16

DEPTH = 4
CHUNK = 64
Q_BLOCK = 128
MLA_NOPE = 128
MLA_ROPE = 64
MLA_V = 128
ROPE_THETA = 10000.0
GLA_HEADS = 4
GLA_GATE_RANK = 16
GLA_TAU = 16.0
DN_ALPHA = (2 * DEPTH) ** 0.25
EPS = 1e-5
NEG_INF = -1e30
ADAM_LR = 0.001
ADAM_B1 = 0.9
ADAM_B2 = 0.999
ADAM_EPS = 1e-08
ADAM_WD = 0.01
ADAM_STEP = 10
GELU_C = math.sqrt(2.0 / math.pi)
GELU_A = 0.044715

LANE = 128
HEAD_PAD = 2 * LANE
VMEM_LIMIT_BYTES = 48 * 1024 * 1024

MESH = pl.DeviceIdType.MESH
ANY_SPEC = pl.BlockSpec(memory_space=pl.ANY)

WEIGHT_NAMES = ('mla_w_in', 'mla_q_norm', 'mla_kv_norm', 'mla_w_uq', 'mla_w_uk', 'mla_w_uv', 'mla_w_o',
                'gla_w_in', 'gla_w_a2', 'gla_b_a', 'gla_o_norm', 'gla_w_o', 'ln1_g', 'ln1_b', 'ln2_g', 'ln2_b',
                'ffn_w_up', 'ffn_conv_w', 'ffn_conv_b', 'ffn_w_down', 'ple_w_proj', 'ple_w_gate', 'ple_b_gate')
BIG = (('mla_w_in', 1), ('mla_w_uq', 2), ('mla_w_uk', 2), ('mla_w_uv', 2), ('mla_w_o', 1), ('gla_w_in', 2),
       ('gla_w_o', 1), ('ffn_w_up', 2), ('ffn_w_down', 1), ('ple_w_proj', 2), ('ple_w_gate', 1))
SMALL = (('mla_q_norm', None), ('mla_kv_norm', None), ('gla_w_a2', 2), ('gla_b_a', 1), ('gla_o_norm', 1),
         ('ln1_g', None), ('ln1_b', None), ('ln2_g', None), ('ln2_b', None), ('ffn_conv_w', 2),
         ('ffn_conv_b', None), ('ple_b_gate', None))
N_CHIPS = 4
N_DEVICES = 8


def _params():
    return pltpu.CompilerParams(vmem_limit_bytes=VMEM_LIMIT_BYTES)


def _tile(n, cap, *offsets, unit=LANE):
    g = n
    for o in offsets:
        if o:
            g = math.gcd(g, o)
    best = 0
    for d in range(unit, min(g, cap) + 1, unit):
        if g % d == 0:
            best = d
    if best:
        return best
    assert not any(offsets), (n, offsets)
    return n


def _dot(a, b, dims):
    return lax.dot_general(a.astype(MXU_DTYPE), b.astype(MXU_DTYPE), (dims, ((), ())),
                           preferred_element_type=F32)


NN = ((1,), (0,))
NT = ((1,), (1,))
TN = ((0,), (0,))


def mm(name, a, b, mode, out_dtype, *, a_pre=(), b_pre=(), b_win=None, acc_in=None, out_stack=None):
    a2 = a.shape[len(a_pre):]
    b2 = b.shape[len(b_pre):]
    br0, brn, bc0, bcn = b_win or (0, b2[0], 0, b2[1])
    if mode == 'nn':
        (M, K), N, dims = a2, bcn, NN
        assert brn == K
    elif mode == 'nt':
        (M, K), N, dims = a2, brn, NT
        assert bcn == K
    else:
        (K, M), N, dims = a2, bcn, TN
        assert brn == K
    oc0 = out_stack[3] if out_stack else 0
    tm = _tile(M, 1408 if mode == 'tn' else 1024)
    if mode == 'nn':
        tn, tk = _tile(N, 1408, bc0, oc0), _tile(K, 1408, br0)
    elif mode == 'nt':
        tn, tk = _tile(N, 1408, br0, oc0), _tile(K, 1408, bc0)
    else:
        tn, tk = _tile(N, 1408, bc0, oc0), _tile(K, 1024, br0)
    nk = K // tk
    grid = (M // tm, N // tn, nk)

    na, nb = len(a_pre), len(b_pre)
    if mode == 'tn':
        a_spec = pl.BlockSpec((None,) * na + (tk, tm), lambda i, j, k: a_pre + (k, i))
    else:
        a_spec = pl.BlockSpec((None,) * na + (tm, tk), lambda i, j, k: a_pre + (i, k))
    if mode == 'nt':
        b_spec = pl.BlockSpec((None,) * nb + (tn, tk), lambda i, j, k: b_pre + (j + br0 // tn, k + bc0 // tk))
    else:
        b_spec = pl.BlockSpec((None,) * nb + (tk, tn), lambda i, j, k: b_pre + (k + br0 // tk, j + bc0 // tn))
    in_specs, args = [a_spec, b_spec], [a, b]
    if acc_in is not None:
        in_specs.append(pl.BlockSpec((tm, tn), lambda i, j, k: (i, j)))
        args.append(acc_in)
    aliases = {}
    if out_stack is None:
        out_shape = jax.ShapeDtypeStruct((M, N), out_dtype)
        out_spec = pl.BlockSpec((tm, tn), lambda i, j, k: (i, j))
    else:
        buf, full_shape, lead, _ = out_stack
        out_shape = jax.ShapeDtypeStruct(full_shape, out_dtype)
        out_spec = pl.BlockSpec((None, tm, tn), lambda i, j, k: (lead, i, j + oc0 // tn))
        if buf is not None:
            aliases = {len(args): 0}
            in_specs.append(ANY_SPEC)
            args.append(buf)
    has_c, has_alias = acc_in is not None, bool(aliases)

    def body(*refs):
        a_ref, b_ref = refs[0], refs[1]
        c_ref = refs[2] if has_c else None
        o_ref = refs[2 + has_c + has_alias]
        prod = _dot(a_ref[...], b_ref[...], dims)
        if nk == 1:
            if has_c:
                prod = prod + c_ref[...]
            o_ref[...] = prod.astype(out_dtype)
            return
        acc_ref = refs[3 + has_c + has_alias]
        k = pl.program_id(2)

        @pl.when(k == 0)
        def _():
            acc_ref[...] = prod + c_ref[...] if has_c else prod

        @pl.when(k > 0)
        def _():
            acc_ref[...] += prod

        @pl.when(k == nk - 1)
        def _():
            o_ref[...] = acc_ref[...].astype(out_dtype)

    scratch = [pltpu.VMEM((tm, tn), F32)] if nk > 1 else []
    return pl.pallas_call(body, out_shape=out_shape, grid=grid, in_specs=in_specs, out_specs=out_spec,
                          scratch_shapes=scratch, input_output_aliases=aliases, name=name,
                          compiler_params=_params())(*args)


def rowcall(name, body, n_rows, ts, row_ins, full_ins, row_outs, acc_outs=()):
    in_specs, args = [], []
    for arr, width, colblk, pre in row_ins:
        in_specs.append(pl.BlockSpec((None,) * len(pre) + (ts, width),
                                     lambda i, pre=pre, cb=colblk: pre + (i, cb)))
        args.append(arr)
    for arr in full_ins:
        in_specs.append(pl.BlockSpec(arr.shape, lambda i, nd=arr.ndim: (0,) * nd))
        args.append(arr)
    out_shape, out_specs = [], []
    for width, dtype in row_outs:
        out_shape.append(jax.ShapeDtypeStruct((n_rows, width), dtype))
        out_specs.append(pl.BlockSpec((ts, width), lambda i: (i, 0)))
    for shape in acc_outs:
        out_shape.append(jax.ShapeDtypeStruct(shape, F32))
        out_specs.append(pl.BlockSpec(shape, lambda i, nd=len(shape): (0,) * nd))
    n_in = len(args)

    def kern(*refs):
        body(pl.program_id(0), refs[:n_in], refs[n_in:])

    return pl.pallas_call(kern, out_shape=tuple(out_shape), grid=(n_rows // ts,), in_specs=in_specs,
                          out_specs=tuple(out_specs), name=name, compiler_params=_params())(*args)


def _row(arr, width=None, colblk=0, pre=()):
    return (arr, arr.shape[-1] if width is None else width, colblk, pre)


def _init_acc(i, refs):
    @pl.when(i == 0)
    def _():
        for r in refs:
            r[...] = jnp.zeros(r.shape, r.dtype)


def _colsum(v):
    return jnp.sum(v, axis=0, keepdims=True)


def _sigmoid(z):
    return 1.0 / (1.0 + jnp.exp(-z))


def _row_tile(S):
    return _tile(S, 256, unit=16)


def _ln_stats(x_ref, m_ref):
    z = DN_ALPHA * x_ref[...] + m_ref[...]
    mu = jnp.mean(z, -1, keepdims=True)
    zc = z - mu
    var = jnp.mean(zc * zc, -1, keepdims=True)
    r = lax.rsqrt(var + EPS)
    return zc * r, r


def ln_fwd(name, x, m, g, b):
    S, D = x.shape

    def body(i, ins, outs):
        x_ref, m_ref, g_ref, b_ref = ins
        xh, _ = _ln_stats(x_ref, m_ref)
        y = xh * g_ref[...] + b_ref[...]
        outs[0][...] = y
        outs[1][...] = y.astype(MXU_DTYPE)

    return rowcall(name, body, S, _row_tile(S), [_row(x), _row(m)], [g, b], [(D, F32), (D, MXU_DTYPE)])


def ln_bwd(name, x, m, dA, ca, dB, g):
    S, D = x.shape
    has_b = dB is not None

    def body(i, ins, outs):
        x_ref, m_ref, a_ref = ins[:3]
        g_ref = ins[-1]
        dz_ref, dzb_ref, dg_ref, db_ref = outs
        _init_acc(i, (dg_ref, db_ref))
        dy = ca * a_ref[...]
        if has_b:
            dy = dy + ins[3][...]
        xh, r = _ln_stats(x_ref, m_ref)
        dg_ref[...] += _colsum(dy * xh)
        db_ref[...] += _colsum(dy)
        dxh = dy * g_ref[...]
        dz = r * (dxh - jnp.mean(dxh, -1, keepdims=True) - xh * jnp.mean(dxh * xh, -1, keepdims=True))
        dz_ref[...] = dz
        dzb_ref[...] = dz.astype(MXU_DTYPE)

    rows = [_row(x), _row(m), _row(dA)] + ([_row(dB)] if has_b else [])
    return rowcall(name, body, S, _row_tile(S), rows, [g], [(D, F32), (D, MXU_DTYPE)], [(1, D), (1, D)])


def ple_fwd(name, x2, gp, pp, bias):
    S, D = x2.shape

    def body(i, ins, outs):
        x_ref, gp_ref, pp_ref, b_ref = ins
        y = x_ref[...] + _sigmoid(gp_ref[...] + b_ref[...]) * pp_ref[...]
        outs[0][...] = y
        outs[1][...] = y.astype(MXU_DTYPE)

    return rowcall(name, body, S, _row_tile(S), [_row(x2), _row(gp), _row(pp)], [bias],
                   [(D, F32), (D, MXU_DTYPE)])


def ple_bwd(name, dA, ca, dB, gp, pp, bias):
    S, D = gp.shape
    has_b = dB is not None

    def body(i, ins, outs):
        a_ref = ins[0]
        gp_ref, pp_ref, b_ref = ins[-3:]
        dx_ref, dpp_ref, dgp_ref, db_ref = outs
        _init_acc(i, (db_ref,))
        dx = ca * a_ref[...]
        if has_b:
            dx = dx + ins[1][...]
        gate = _sigmoid(gp_ref[...] + b_ref[...])
        dgp = dx * pp_ref[...] * gate * (1.0 - gate)
        dx_ref[...] = dx
        dpp_ref[...] = (dx * gate).astype(MXU_DTYPE)
        dgp_ref[...] = dgp.astype(MXU_DTYPE)
        db_ref[...] += _colsum(dgp)

    rows = [_row(dA)] + ([_row(dB)] if has_b else []) + [_row(gp), _row(pp)]
    return rowcall(name, body, S, _row_tile(S), rows, [bias],
                   [(D, F32), (D, MXU_DTYPE), (D, MXU_DTYPE)], [(1, D)])


def loss_head(name, y, target):
    S, D = y.shape

    def body(i, ins, outs):
        _init_acc(i, (outs[1],))
        e = ins[0][...] - ins[1][...]
        outs[0][...] = e * (1.0 / D)
        outs[1][...] += _colsum(e * e)

    return rowcall(name, body, S, _row_tile(S), [_row(y), _row(target)], [], [(D, F32)], [(1, D)])


def axpy(name, a, ca, b):
    S, D = a.shape

    def body(i, ins, outs):
        outs[0][...] = ca * ins[0][...] + ins[1][...]

    return rowcall(name, body, S, _row_tile(S), [_row(a), _row(b)], [], [(D, F32)])[0]


HALF_ROPE = MLA_ROPE // 2


def _rope(x, c, sa, sb):
    n = x.shape[-1]
    return x * c + pltpu.roll(x, n - HALF_ROPE, 1) * sa + pltpu.roll(x, HALF_ROPE, 1) * sb


def _rope_t(d, c, sa, sb):
    n = d.shape[-1]
    return d * c + pltpu.roll(d * sa, HALF_ROPE, 1) + pltpu.roll(d * sb, n - HALF_ROPE, 1)


def rope_tables(positions_row, n_lead):
    inv = 1.0 / (ROPE_THETA ** (jnp.arange(0, MLA_ROPE, 2, dtype=F32) / MLA_ROPE))
    ang = positions_row.astype(F32)[:, None] * inv
    cos, sin = jnp.cos(ang), jnp.sin(ang)
    S = cos.shape[0]
    z = jnp.zeros((S, HALF_ROPE), F32)
    tail = jnp.zeros((S, LANE - MLA_ROPE), F32)
    c = jnp.concatenate([jnp.ones((S, n_lead), F32), cos, cos, tail], -1)
    sa = jnp.concatenate([jnp.zeros((S, n_lead), F32), -sin, z, tail], -1)
    sb = jnp.concatenate([jnp.zeros((S, n_lead), F32), z, sin, tail], -1)
    return c, sa, sb


def mla_pre_fwd(name, h, qn, kvn, tab_k, QL, KL):
    S = h.shape[0]

    def body(i, ins, outs):
        h_ref, c_ref, sa_ref, sb_ref, qn_ref, kvn_ref = ins
        cq = h_ref[:, 0:QL]
        ckv = h_ref[:, QL:QL + KL]
        kr = h_ref[:, QL + KL:QL + KL + LANE]
        outs[0][...] = (cq * lax.rsqrt(jnp.mean(cq * cq, -1, keepdims=True) + EPS) * qn_ref[...]).astype(MXU_DTYPE)
        outs[1][...] = (ckv * lax.rsqrt(jnp.mean(ckv * ckv, -1, keepdims=True) + EPS) * kvn_ref[...]).astype(MXU_DTYPE)
        outs[2][...] = _rope(kr, c_ref[...], sa_ref[...], sb_ref[...]).astype(MXU_DTYPE)

    rows = [_row(h)] + [_row(t) for t in tab_k]
    return rowcall(name, body, S, _row_tile(S), rows, [qn, kvn],
                   [(QL, MXU_DTYPE), (KL, MXU_DTYPE), (LANE, MXU_DTYPE)])


def _rms_bwd(x, g, dy):
    r = lax.rsqrt(jnp.mean(x * x, -1, keepdims=True) + EPS)
    dg = _colsum(dy * x * r)
    dxg = dy * g
    dx = r * dxg - x * (r * r * r) * jnp.mean(dxg * x, -1, keepdims=True)
    return dx, dg


def mla_pre_bwd(name, h, dcq, dckv_a, dckv_b, dkr_heads, qn, kvn, tab_k, QL, KL, H):
    S, HW = h.shape

    def body(i, ins, outs):
        h_ref, dcq_ref, da_ref, db_ref, dkr_ref, c_ref, sa_ref, sb_ref, qn_ref, kvn_ref = ins
        dh_ref, dqn_ref, dkvn_ref = outs
        _init_acc(i, (dqn_ref, dkvn_ref))
        dx, dg = _rms_bwd(h_ref[:, 0:QL], qn_ref[...], dcq_ref[...])
        dh_ref[:, 0:QL] = dx.astype(MXU_DTYPE)
        dqn_ref[...] += dg
        dx, dg = _rms_bwd(h_ref[:, QL:QL + KL], kvn_ref[...], da_ref[...] + db_ref[...])
        dh_ref[:, QL:QL + KL] = dx.astype(MXU_DTYPE)
        dkvn_ref[...] += dg
        d = dkr_ref[:, 0:LANE]
        for hh in range(1, H):
            d = d + dkr_ref[:, hh * LANE:(hh + 1) * LANE]
        dh_ref[:, QL + KL:QL + KL + LANE] = _rope_t(d, c_ref[...], sa_ref[...], sb_ref[...]).astype(MXU_DTYPE)

    rows = [_row(h), _row(dcq), _row(dckv_a), _row(dckv_b), _row(dkr_heads)] + [_row(t) for t in tab_k]
    return rowcall(name, body, S, _row_tile(S), rows, [qn, kvn], [(HW, MXU_DTYPE)], [(1, QL), (1, KL)])


def rope_q(name, qpre, tab_q, H, out_dtype):
    S = qpre.shape[0]

    def body(i, ins, outs):
        c, sa, sb = ins[1][...], ins[2][...], ins[3][...]
        for hh in range(H):
            sl = slice(hh * HEAD_PAD, (hh + 1) * HEAD_PAD)
            outs[0][:, sl] = _rope(ins[0][:, sl], c, sa, sb).astype(out_dtype)

    rows = [_row(qpre)] + [_row(t) for t in tab_q]
    return rowcall(name, body, S, _row_tile(S), rows, [], [(H * HEAD_PAD, out_dtype)])[0]


ATT_T = 512


def _att_mask_t(ck_col, cq_row, k0, q0, t):
    kpos = k0 + lax.broadcasted_iota(jnp.int32, (t, 1), 0)
    qpos = q0 + lax.broadcasted_iota(jnp.int32, (1, t), 1)
    return (ck_col <= cq_row) & (kpos <= (qpos | (Q_BLOCK - 1)))


def _transpose(a):
    return a.astype(F32).T.astype(a.dtype)


def hosted_call(body, *, name, grid, in_specs, out_specs, out_shape, scratch_shapes, args, rider):
    if rider is None:
        return pl.pallas_call(body, name=name, grid=grid, in_specs=in_specs, out_specs=out_specs,
                              out_shape=out_shape, scratch_shapes=scratch_shapes,
                              compiler_params=_params())(*args), ()
    n_in, n_out, n_scr = len(args), len(out_shape), len(scratch_shapes)
    r_in, r_out = len(rider.ins), len(rider.out_shapes)

    def hosted(*refs):
        ins, refs = refs[:n_in], refs[n_in:]
        rin, refs = refs[:r_in], refs[r_in:]
        outs, refs = refs[:n_out], refs[n_out:]
        rout, refs = refs[:r_out], refs[r_out:]
        scr, sems = refs[:n_scr], refs[n_scr:]
        ids = [pl.program_id(a) for a in range(len(grid))]
        first = functools.reduce(jnp.logical_and, [i == 0 for i in ids])
        last = functools.reduce(jnp.logical_and, [i == g - 1 for i, g in zip(ids, grid)])

        @pl.when(first)
        def _():
            rider.start(rin, rout, sems)

        body(*ins, *outs, *scr)

        @pl.when(last)
        def _():
            rider.finish(rin, rout, sems)

    results = pl.pallas_call(
        hosted, name=name, grid=grid, in_specs=list(in_specs) + [ANY_SPEC] * r_in,
        out_specs=tuple(out_specs) + (ANY_SPEC,) * r_out, out_shape=tuple(out_shape) + tuple(rider.out_shapes),
        scratch_shapes=list(scratch_shapes) + [pltpu.SemaphoreType.DMA((k,)) for k in rider.sem_counts],
        input_output_aliases={n_in + k: n_out + k for k in range(r_in)} if rider.in_place else {},
        compiler_params=pltpu.CompilerParams(vmem_limit_bytes=VMEM_LIMIT_BYTES, has_side_effects=True),
    )(*args, *rider.ins)
    return results[:n_out], results[n_out:]


def attn_fwd(name, q, kn, kr, v, cid_col, cid_blk, H, rider=None):
    S = q.shape[0]
    t = _tile(S, ATT_T)
    n = S // t
    scale = (MLA_NOPE + MLA_ROPE) ** -0.5

    def body(q_ref, kn_ref, kr_ref, v_ref, cc_ref, cr_ref, o_ref, lse_ref, k_scr, vt_scr):
        i = pl.program_id(1)

        @pl.when(i == 0)
        def _():
            k_scr[:, 0:LANE] = kn_ref[...]
            k_scr[:, LANE:HEAD_PAD] = kr_ref[...]
            for jj in range(n):
                vt_scr[jj] = _transpose(v_ref[jj * t:(jj + 1) * t, :])

        qv = q_ref[...]
        cq = cr_ref[i]

        def update(j, carry, masked):
            m, l, acc = carry
            k0 = pl.multiple_of(j * t, t)
            s = _dot(k_scr[pl.ds(k0, t), :], qv, NT) * scale
            if masked:
                s = jnp.where(_att_mask_t(cc_ref[pl.ds(k0, t), :], cq, k0, i * t, t), s, NEG_INF)
            m_new = jnp.maximum(m, jnp.max(s, 0, keepdims=True))
            p = jnp.exp(s - m_new)
            alpha = jnp.exp(m - m_new)
            l = alpha * l + jnp.sum(p, 0, keepdims=True)
            acc = alpha * acc + _dot(vt_scr[j], p, NN)
            return m_new, l, acc

        init = (jnp.full((1, t), NEG_INF, F32), jnp.zeros((1, t), F32), jnp.zeros((MLA_V, t), F32))
        carry = lax.fori_loop(0, i, lambda j, c: update(j, c, False), init)
        m, l, acc = update(i, carry, True)
        o_ref[...] = (acc / l).T.astype(o_ref.dtype)
        lse_ref[...] = m + jnp.log(l)

    return hosted_call(
        body, name=name, grid=(H, n),
        in_specs=[pl.BlockSpec((t, HEAD_PAD), lambda h, i: (i, h)),
                  pl.BlockSpec((S, MLA_NOPE), lambda h, i: (0, h)),
                  pl.BlockSpec((S, LANE), lambda h, i: (0, 0)),
                  pl.BlockSpec((S, MLA_V), lambda h, i: (0, h)),
                  pl.BlockSpec((S, 1), lambda h, i: (0, 0)),
                  pl.BlockSpec(cid_blk.shape, lambda h, i: (0, 0, 0))],
        out_specs=(pl.BlockSpec((t, MLA_V), lambda h, i: (i, h)),
                   pl.BlockSpec((None, None, 1, t), lambda h, i: (h, i, 0, 0))),
        scratch_shapes=[pltpu.VMEM((S, HEAD_PAD), MXU_DTYPE), pltpu.VMEM((n, MLA_V, t), MXU_DTYPE)],
        out_shape=(jax.ShapeDtypeStruct((S, H * MLA_V), MXU_DTYPE), jax.ShapeDtypeStruct((H, n, 1, t), F32)),
        args=(q, kn, kr, v, cid_col, cid_blk), rider=rider)


def attn_bwd(name, q, kn, kr, v, o, lse, do, cid_col, cid_blk, H, rider=None):
    S = q.shape[0]
    t = _tile(S, ATT_T)
    n = S // t
    scale = (MLA_NOPE + MLA_ROPE) ** -0.5

    def body(q_ref, kn_ref, kr_ref, v_ref, o_ref, lse_ref, do_ref, cc_ref, cr_ref,
             dq_ref, dkn_ref, dv_ref, dkr_ref, delta_scr, dk_scr, dv_scr):
        j = pl.program_id(1)

        @pl.when(j == 0)
        def _():
            dq_ref[...] = jnp.zeros(dq_ref.shape, F32)
            for ii in range(n):
                sl = slice(ii * t, (ii + 1) * t)
                prod = do_ref[sl, :].astype(F32) * o_ref[sl, :].astype(F32)
                delta_scr[ii] = jnp.sum(prod.T, 0, keepdims=True)

        kv = jnp.concatenate([kn_ref[...], kr_ref[...]], axis=-1)
        kt = _transpose(kv)
        vv = v_ref[...]
        ck = cc_ref[...]
        k0 = j * t
        dk_scr[...] = jnp.zeros(dk_scr.shape, F32)
        dv_scr[...] = jnp.zeros(dv_scr.shape, F32)

        def update(i, masked):
            q0 = pl.multiple_of(i * t, t)
            qv = q_ref[pl.ds(q0, t), :]
            dov = do_ref[pl.ds(q0, t), :]
            s = _dot(kv, qv, NT) * scale
            if masked:
                s = jnp.where(_att_mask_t(ck, cr_ref[i], k0, q0, t), s, NEG_INF)
            p = jnp.exp(s - lse_ref[i])
            dv_scr[...] += _dot(p, dov, NN)
            dp = _dot(vv, dov, NT)
            ds = p * (dp - delta_scr[i]) * scale
            dk_scr[...] += _dot(ds, qv, NN)
            dq_ref[i] += _dot(kt, ds, NN)

        def step(i, carry):
            update(i, False)
            return carry

        update(j, True)
        lax.fori_loop(j + 1, n, step, 0)
        dkn_ref[...] = dk_scr[:, 0:LANE].astype(dkn_ref.dtype)
        dkr_ref[...] = dk_scr[:, LANE:HEAD_PAD]
        dv_ref[...] = dv_scr[...].astype(dv_ref.dtype)

    head_rows = lambda w: pl.BlockSpec((S, w), lambda h, j: (0, h))
    tile_rows = lambda w: pl.BlockSpec((t, w), lambda h, j: (j, h))
    return hosted_call(
        body, name=name, grid=(H, n),
        in_specs=[head_rows(HEAD_PAD), tile_rows(MLA_NOPE), pl.BlockSpec((t, LANE), lambda h, j: (j, 0)),
                  tile_rows(MLA_V), head_rows(MLA_V),
                  pl.BlockSpec((None, n, 1, t), lambda h, j: (h, 0, 0, 0)), head_rows(MLA_V),
                  pl.BlockSpec((t, 1), lambda h, j: (j, 0)),
                  pl.BlockSpec(cid_blk.shape, lambda h, j: (0, 0, 0))],
        out_specs=(pl.BlockSpec((None, n, HEAD_PAD, t), lambda h, j: (h, 0, 0, 0)),
                   tile_rows(MLA_NOPE), tile_rows(MLA_V), tile_rows(LANE)),
        scratch_shapes=[pltpu.VMEM((n, 1, t), F32), pltpu.VMEM((t, HEAD_PAD), F32), pltpu.VMEM((t, MLA_V), F32)],
        out_shape=(jax.ShapeDtypeStruct((H, n, HEAD_PAD, t), F32), jax.ShapeDtypeStruct((S, H * MLA_NOPE), MXU_DTYPE),
                   jax.ShapeDtypeStruct((S, H * MLA_V), MXU_DTYPE), jax.ShapeDtypeStruct((S, H * LANE), F32)),
        args=(q, kn, kr, v, o, lse, do, cid_col, cid_blk), rider=rider)


def rope_q_bwd(name, dq_t, tab_q_t, H):
    _, n, _, t = dq_t.shape

    def body(d_ref, c_ref, sa_ref, sb_ref, o_ref):
        d = d_ref[...]
        out = (d * c_ref[...] + pltpu.roll(d * sa_ref[...], HALF_ROPE, 0)
               + pltpu.roll(d * sb_ref[...], HEAD_PAD - HALF_ROPE, 0))
        o_ref[...] = out.astype(o_ref.dtype)

    tab = pl.BlockSpec((HEAD_PAD, t), lambda i, h: (0, i))
    return pl.pallas_call(
        body, name=name, grid=(n, H), out_shape=jax.ShapeDtypeStruct((H * HEAD_PAD, n * t), MXU_DTYPE),
        in_specs=[pl.BlockSpec((None, None, HEAD_PAD, t), lambda i, h: (h, i, 0, 0)), tab, tab, tab],
        out_specs=pl.BlockSpec((HEAD_PAD, t), lambda i, h: (h, i)),
        compiler_params=_params())(dq_t, *tab_q_t)


GLA_GROUP = 8


def _prefix_rows(x):
    n = x.shape[0]
    row = lax.broadcasted_iota(jnp.int32, x.shape, 0)
    d = 1
    while d < n:
        x = x + jnp.where(row >= d, pltpu.roll(x, d, 0), 0.0)
        d *= 2
    return x


def _suffix_rows(x):
    n = x.shape[0]
    row = lax.broadcasted_iota(jnp.int32, x.shape, 0)
    d = 1
    while d < n:
        x = x + jnp.where(row < n - d, pltpu.roll(x, n - d, 0), 0.0)
        d *= 2
    return x


def _log_sigmoid(z):
    return jnp.minimum(z, 0.0) - jnp.log(1.0 + jnp.exp(-jnp.abs(z)))


def gla_pre_fwd(name, h, w2p, b_a, QK, a_blk):
    S = h.shape[0]
    dk = QK // GLA_HEADS

    def body(i, ins, outs):
        q_ref, a_ref, w_ref, b_ref = ins
        outs[0][...] = (q_ref[...] * (dk ** -0.5)).astype(MXU_DTYPE)
        z = _dot(a_ref[...], w_ref[...], NN) + b_ref[...]
        outs[1][...] = _log_sigmoid(z) / GLA_TAU

    return rowcall(name, body, S, _row_tile(S), [_row(h, QK, 0), _row(h, LANE, a_blk)], [w2p, b_a],
                   [(QK, MXU_DTYPE), (QK, F32)])


def _gla_specs(S, QK, VD, rows, gmap):
    dk, dv = QK // GLA_HEADS, VD // GLA_HEADS
    return dict(
        qs=pl.BlockSpec((rows, dk), lambda h, g: (gmap(g), h)),
        k=pl.BlockSpec((rows, dk), lambda h, g: (gmap(g), QK // dk + h)),
        v=pl.BlockSpec((rows, dv), lambda h, g: (gmap(g), 2 * QK // dv + h)),
        la=pl.BlockSpec((rows, dk), lambda h, g: (gmap(g), h)),
        o=pl.BlockSpec((rows, dv), lambda h, g: (gmap(g), h)))


def gla_fwd(name, qs, h, la, QK, VD):
    S = qs.shape[0]
    dk, dv = QK // GLA_HEADS, VD // GLA_HEADS
    n_chunks = S // CHUNK
    cg = min(GLA_GROUP, n_chunks)
    rows = cg * CHUNK
    sp = _gla_specs(S, QK, VD, rows, lambda g: g)

    def body(q_ref, k_ref, v_ref, la_ref, o_ref, st_ref, state):
        @pl.when(pl.program_id(1) == 0)
        def _():
            state[...] = jnp.zeros(state.shape, F32)

        for c in range(cg):
            sl = slice(c * CHUNK, (c + 1) * CHUNK)
            cum = _prefix_rows(la_ref[sl, :])
            tot = cum[CHUNK - 1:CHUNK, :]
            kdec = k_ref[sl, :] * jnp.exp(tot - cum)
            st = state[...] * jnp.exp(tot) + _dot(v_ref[sl, :], kdec, TN)
            state[...] = st
            st_ref[c] = st
            o_ref[sl, :] = _dot(q_ref[sl, :], st, NT)

    return pl.pallas_call(
        body, name=name, grid=(GLA_HEADS, n_chunks // cg),
        out_shape=(jax.ShapeDtypeStruct((S, VD), F32), jax.ShapeDtypeStruct((GLA_HEADS, n_chunks, dv, dk), F32)),
        in_specs=[sp['qs'], sp['k'], sp['v'], sp['la']],
        out_specs=(sp['o'], pl.BlockSpec((None, cg, dv, dk), lambda h, g: (h, g, 0, 0))),
        scratch_shapes=[pltpu.VMEM((dv, dk), F32)],
        compiler_params=_params())(qs, h, h, la)


def gla_bwd(name, qs, h, la, states, do, QK, VD):
    S = qs.shape[0]
    dk, dv = QK // GLA_HEADS, VD // GLA_HEADS
    n_chunks = S // CHUNK
    cg = min(GLA_GROUP, n_chunks)
    ng = n_chunks // cg
    rows = cg * CHUNK
    rev = lambda g: ng - 1 - g
    sp = _gla_specs(S, QK, VD, rows, rev)

    def body(q_ref, k_ref, v_ref, la_ref, st_ref, prev_ref, do_ref, dq_ref, dk_ref, dv_ref, dla_ref, dst):
        g = pl.program_id(1)

        @pl.when(g == 0)
        def _():
            dst[...] = jnp.zeros(dst.shape, F32)

        first_group = (g == ng - 1).astype(F32)
        for c in reversed(range(cg)):
            sl = slice(c * CHUNK, (c + 1) * CHUNK)
            cum = _prefix_rows(la_ref[sl, :])
            tot = cum[CHUNK - 1:CHUNK, :]
            e = jnp.exp(tot - cum)
            kdec = k_ref[sl, :] * e
            decay = jnp.exp(tot)
            st = st_ref[c]
            st_prev = st_ref[c - 1] if c > 0 else prev_ref[0] * (1.0 - first_group)
            dov = do_ref[sl, :]
            qv = q_ref[sl, :]
            dq_ref[sl, :] = (_dot(dov, st, NN) * (dk ** -0.5)).astype(dq_ref.dtype)
            d = dst[...] + _dot(dov, qv, TN)
            ddecay = _colsum(d * st_prev)
            dkdec = _dot(v_ref[sl, :], d, NN)
            dv_ref[sl, :] = _dot(kdec, d, NT).astype(dv_ref.dtype)
            dk_ref[sl, :] = (dkdec * e).astype(dk_ref.dtype)
            darg = dkdec * kdec
            dtot = _colsum(darg) + ddecay * decay
            dla_ref[sl, :] = dtot - _suffix_rows(darg)
            dst[...] = d * decay

    prev_spec = pl.BlockSpec((None, 1, dv, dk), lambda h, g: (h, jnp.maximum(rev(g) * cg - 1, 0), 0, 0))
    return pl.pallas_call(
        body, name=name, grid=(GLA_HEADS, ng),
        out_shape=(jax.ShapeDtypeStruct((S, QK), MXU_DTYPE), jax.ShapeDtypeStruct((S, QK), MXU_DTYPE),
                   jax.ShapeDtypeStruct((S, VD), MXU_DTYPE), jax.ShapeDtypeStruct((S, QK), F32)),
        in_specs=[sp['qs'], sp['k'], sp['v'], sp['la'],
                  pl.BlockSpec((None, cg, dv, dk), lambda h, g: (h, rev(g), 0, 0)), prev_spec, sp['o']],
        out_specs=(sp['qs'], sp['qs'], sp['o'], sp['la']),
        scratch_shapes=[pltpu.VMEM((dv, dk), F32)],
        compiler_params=_params())(qs, h, h, la, states, states, do)


def _head_norm(o):
    mu = jnp.mean(o, -1, keepdims=True)
    oc = o - mu
    r = lax.rsqrt(jnp.mean(oc * oc, -1, keepdims=True) + EPS)
    return oc * r, r


def gla_post_fwd(name, o, h, o_norm, VD, r_blk):
    S = o.shape[0]
    dv = VD // GLA_HEADS

    def body(i, ins, outs):
        o_ref, r_ref, w_ref = ins
        for hh in range(GLA_HEADS):
            sl = slice(hh * dv, (hh + 1) * dv)
            xh, _ = _head_norm(o_ref[:, sl])
            r = r_ref[:, sl]
            outs[0][:, sl] = (xh * w_ref[:, sl] * (r * _sigmoid(r))).astype(MXU_DTYPE)

    return rowcall(name, body, S, _row_tile(S), [_row(o), _row(h, VD, r_blk)], [o_norm], [(VD, MXU_DTYPE)])[0]


def gla_post_bwd(name, o, h, dog, o_norm, VD, r_blk):
    S = o.shape[0]
    dv = VD // GLA_HEADS

    def body(i, ins, outs):
        o_ref, r_ref, dog_ref, w_ref = ins
        do_ref, dr_ref, dw_ref = outs
        _init_acc(i, (dw_ref,))
        for hh in range(GLA_HEADS):
            sl = slice(hh * dv, (hh + 1) * dv)
            xh, rs = _head_norm(o_ref[:, sl])
            r = r_ref[:, sl]
            w = w_ref[:, sl]
            dog = dog_ref[:, sl]
            sg = _sigmoid(r)
            dn = dog * (r * sg)
            dr_ref[:, sl] = (dog * (xh * w) * (sg * (1.0 + r * (1.0 - sg)))).astype(MXU_DTYPE)
            dw_ref[:, sl] += _colsum(dn * xh)
            dxh = dn * w
            do_ref[:, sl] = rs * (dxh - jnp.mean(dxh, -1, keepdims=True) - xh * jnp.mean(dxh * xh, -1, keepdims=True))

    return rowcall(name, body, S, _row_tile(S), [_row(o), _row(h, VD, r_blk), _row(dog)], [o_norm],
                   [(VD, F32), (VD, MXU_DTYPE)], [(1, VD)])


def gla_dh(name, dq, dk, dv, dr, dla, h, w2p, b_a, QK, VD, a_blk, HW):
    S = dq.shape[0]

    def body(i, ins, outs):
        dq_ref, dk_ref, dv_ref, dr_ref, dla_ref, a_ref, w_ref, b_ref = ins
        dh_ref, dw_ref, db_ref = outs
        _init_acc(i, (dw_ref, db_ref))
        a = a_ref[...]
        z = _dot(a, w_ref[...], NN) + b_ref[...]
        dz = dla_ref[...] * (1.0 / GLA_TAU) * _sigmoid(-z)
        dw_ref[...] += _dot(a, dz, TN)
        db_ref[...] += _colsum(dz)
        dh_ref[:, 0:QK] = dq_ref[...]
        dh_ref[:, QK:2 * QK] = dk_ref[...]
        dh_ref[:, 2 * QK:2 * QK + VD] = dv_ref[...]
        dh_ref[:, 2 * QK + VD:2 * QK + 2 * VD] = dr_ref[...]
        dh_ref[:, 2 * QK + 2 * VD:HW] = _dot(dz, w_ref[...], NT).astype(MXU_DTYPE)

    rows = [_row(dq), _row(dk), _row(dv), _row(dr), _row(dla), _row(h, LANE, a_blk)]
    return rowcall(name, body, S, _row_tile(S), rows, [w2p, b_a], [(HW, MXU_DTYPE)], [(LANE, QK), (1, QK)])


CONV_ROWS = 512
HALO = 8


def _gelu(x):
    return 0.5 * x * (1.0 + jnp.tanh(GELU_C * (x + GELU_A * x * x * x)))


def _gelu_grad(x):
    t = jnp.tanh(GELU_C * (x + GELU_A * x * x * x))
    return 0.5 * (1.0 + t) + 0.5 * x * (1.0 - t * t) * GELU_C * (1.0 + 3.0 * GELU_A * x * x)


def _shift_down(x, prev, d):
    row = lax.broadcasted_iota(jnp.int32, x.shape, 0)
    out = pltpu.roll(x, d, 0)
    for t in range(d):
        out = jnp.where(row == t, prev[HALO - d + t:HALO - d + t + 1, :], out)
    return out


def _shift_up(x, nxt, d):
    n = x.shape[0]
    row = lax.broadcasted_iota(jnp.int32, x.shape, 0)
    out = pltpu.roll(x, n - d, 0)
    for t in range(d):
        out = jnp.where(row == n - d + t, nxt[t:t + 1, :], out)
    return out


def _conv_taps(ref, r, rc):
    x = ref[r * rc:(r + 1) * rc, :]
    prev = ref[r * rc - HALO:r * rc, :] if r > 0 else jnp.zeros((HALO, x.shape[1]), F32)
    return x, _shift_down(x, prev, 1), _shift_down(x, prev, 2)


def _conv_apply(taps, w_ref, b_ref):
    x0, x1, x2 = taps
    return x2 * w_ref[0:1, :] + x1 * w_ref[1:2, :] + x0 * w_ref[2:3, :] + b_ref[...]


def conv_fwd(name, hu, hg, cw_u, cw_g, cb_u, cb_g):
    S, F = hu.shape
    tc = LANE
    rc = _tile(S, CONV_ROWS, unit=16)

    def body(u_ref, g_ref, wu_ref, wg_ref, bu_ref, bg_ref, a_ref):
        for r in range(S // rc):
            uc = _conv_apply(_conv_taps(u_ref, r, rc), wu_ref, bu_ref)
            gc = _conv_apply(_conv_taps(g_ref, r, rc), wg_ref, bg_ref)
            a_ref[r * rc:(r + 1) * rc, :] = (uc * _gelu(gc)).astype(a_ref.dtype)

    col = lambda rows: pl.BlockSpec((rows, tc), lambda j: (0, j))
    return pl.pallas_call(
        body, name=name, grid=(F // tc,), out_shape=jax.ShapeDtypeStruct((S, F), MXU_DTYPE),
        in_specs=[col(S), col(S), col(3), col(3), col(1), col(1)], out_specs=col(S),
        compiler_params=_params())(hu, hg, cw_u, cw_g, cb_u, cb_g)


def conv_bwd(name, hu, hg, da, cw_u, cw_g, cb_u, cb_g):
    S, F = hu.shape
    tc = LANE
    rc = _tile(S, CONV_ROWS, unit=16)
    nr = S // rc

    def body(u_ref, g_ref, da_ref, wu_ref, wg_ref, bu_ref, bg_ref,
             dhu_ref, dhg_ref, dwu_ref, dwg_ref, dbu_ref, dbg_ref, du_scr, dg_scr):
        dw = [[jnp.zeros((1, tc), F32) for _ in range(3)] for _ in range(2)]
        db = [jnp.zeros((1, tc), F32) for _ in range(2)]
        for r in range(nr):
            sl = slice(r * rc, (r + 1) * rc)
            ut = _conv_taps(u_ref, r, rc)
            gt = _conv_taps(g_ref, r, rc)
            uc = _conv_apply(ut, wu_ref, bu_ref)
            gc = _conv_apply(gt, wg_ref, bg_ref)
            dav = da_ref[sl, :]
            duc = dav * _gelu(gc)
            dgc = dav * uc * _gelu_grad(gc)
            du_scr[sl, :] = duc
            dg_scr[sl, :] = dgc
            for part, (taps, d) in enumerate(((ut, duc), (gt, dgc))):
                db[part] = db[part] + _colsum(d)
                for tap in range(3):
                    dw[part][tap] = dw[part][tap] + _colsum(taps[2 - tap] * d)
        for part, (w_out, b_out) in enumerate(((dwu_ref, dbu_ref), (dwg_ref, dbg_ref))):
            b_out[...] = db[part]
            for tap in range(3):
                w_out[tap:tap + 1, :] = dw[part][tap]
        for scr, w_ref, out in ((du_scr, wu_ref, dhu_ref), (dg_scr, wg_ref, dhg_ref)):
            for r in range(nr):
                d = scr[r * rc:(r + 1) * rc, :]
                nxt = scr[(r + 1) * rc:(r + 1) * rc + HALO, :] if r + 1 < nr else jnp.zeros((HALO, tc), F32)
                dh = d * w_ref[2:3, :] + _shift_up(d, nxt, 1) * w_ref[1:2, :] + _shift_up(d, nxt, 2) * w_ref[0:1, :]
                out[r * rc:(r + 1) * rc, :] = dh.astype(out.dtype)

    col = lambda rows: pl.BlockSpec((rows, tc), lambda j: (0, j))
    sds = jax.ShapeDtypeStruct
    return pl.pallas_call(
        body, name=name, grid=(F // tc,),
        out_shape=(sds((S, F), MXU_DTYPE), sds((S, F), MXU_DTYPE), sds((3, F), F32), sds((3, F), F32),
                   sds((1, F), F32), sds((1, F), F32)),
        in_specs=[col(S), col(S), col(S), col(3), col(3), col(1), col(1)],
        out_specs=(col(S), col(S), col(3), col(3), col(1), col(1)),
        scratch_shapes=[pltpu.VMEM((S, tc), F32), pltpu.VMEM((S, tc), F32)],
        compiler_params=_params())(hu, hg, da, cw_u, cw_g, cb_u, cb_g)


def adamw(name, w, g, m, v):
    R, C = w.shape
    tr = _tile(R, max(8, (1 << 19) // max(C, 1) // 8 * 8), unit=8)

    def body(w_ref, g_ref, m_ref, v_ref, d_ref, nm_ref, nv_ref):
        gv = g_ref[...]
        mn = ADAM_B1 * m_ref[...] + (1.0 - ADAM_B1) * gv
        vn = ADAM_B2 * v_ref[...] + (1.0 - ADAM_B2) * (gv * gv)
        m_hat = mn / (1.0 - ADAM_B1 ** ADAM_STEP)
        v_hat = vn / (1.0 - ADAM_B2 ** ADAM_STEP)
        d_ref[...] = -ADAM_LR * (m_hat / (jnp.sqrt(v_hat) + ADAM_EPS) + ADAM_WD * w_ref[...])
        nm_ref[...] = mn
        nv_ref[...] = vn

    spec = pl.BlockSpec((tr, C), lambda i: (i, 0))
    shp = jax.ShapeDtypeStruct((R, C), F32)
    return pl.pallas_call(body, name=name, grid=(R // tr,), out_shape=(shp, shp, shp), in_specs=[spec] * 4,
                          out_specs=(spec, spec, spec), compiler_params=_params())(w, g, m, v)


def _position():
    return lax.axis_index('x'), lax.axis_index('y'), lax.axis_index('c')


def _other_chips(x, y):
    return ((1 - x, y), (x, 1 - y), (1 - x, 1 - y))


def _window(ref, dim, lo, n_lead, jj, rs, cs):
    if dim == 1:
        return ref.at[pl.ds(lo, n_lead), pl.ds(pl.multiple_of(jj * rs, 16), rs), :]
    return ref.at[pl.ds(lo, n_lead), :, pl.ds(pl.multiple_of(jj * cs, LANE), cs)]


def _hbm_call(name, body, out_shapes, n_sems, args, aliases=None):
    return pl.pallas_call(
        body, name=name, out_shape=tuple(out_shapes), in_specs=[ANY_SPEC] * len(args),
        out_specs=tuple(ANY_SPEC for _ in out_shapes),
        scratch_shapes=[pltpu.SemaphoreType.DMA((n,)) for n in n_sems],
        input_output_aliases=aliases or {},
        compiler_params=pltpu.CompilerParams(has_side_effects=True))(*args)


def place_shard(name, shard, dim, j_arr):
    L, rs, cs = shard.shape
    full_shape = (L, rs * N_CHIPS, cs) if dim == 1 else (L, rs, cs * N_CHIPS)
    tr = _tile(rs, max(16, (1 << 19) // cs // 16 * 16), unit=16)

    def body(j_ref, s_ref, o_ref):
        o_ref[...] = s_ref[...].astype(o_ref.dtype)

    if dim == 1:
        out_spec = pl.BlockSpec((None, tr, cs), lambda l, i, j: (l, j[0] * (rs // tr) + i, 0))
    else:
        out_spec = pl.BlockSpec((None, tr, cs), lambda l, i, j: (l, i, j[0]))
    grid_spec = pltpu.PrefetchScalarGridSpec(
        num_scalar_prefetch=1, grid=(L, rs // tr),
        in_specs=[pl.BlockSpec((None, tr, cs), lambda l, i, j: (l, i, 0))], out_specs=out_spec)
    return pl.pallas_call(body, name=name, grid_spec=grid_spec,
                          out_shape=jax.ShapeDtypeStruct(full_shape, MXU_DTYPE),
                          compiler_params=_params())(j_arr, shard)


class Exchange(NamedTuple):
    ins: tuple
    out_shapes: tuple
    in_place: bool
    sem_counts: tuple
    start: Callable
    finish: Callable


def run_exchange(name, ex):
    n_in, n_out = len(ex.ins), len(ex.out_shapes)

    def body(*refs):
        ins, outs, sems = refs[:n_in], refs[n_in:n_in + n_out], refs[n_in + n_out:]
        ex.start(ins, outs, sems)
        ex.finish(ins, outs, sems)

    return _hbm_call(name, body, ex.out_shapes, ex.sem_counts, ex.ins,
                     {k: k for k in range(n_in)} if ex.in_place else None)


def _rcopy(src, dst, ssem, rsem, device):
    return pltpu.make_async_remote_copy(src_ref=src, dst_ref=dst, send_sem=ssem, recv_sem=rsem,
                                        device_id=device, device_id_type=MESH)


def gather_exchange(fulls, dims, shard_shapes, items):
    n = len(items)

    def win(full, w, lo, nl, jj):
        return _window(full[w], dims[w], lo, nl, jj, shard_shapes[w][1], shard_shapes[w][2])

    def start(_, full, sems):
        ssem, rsem = sems[0], sems[1]
        x, y, c = _position()
        for it, (w, lo, nl, owner) in enumerate(items):
            @pl.when(c == owner)
            def _():
                own = win(full, w, lo, nl, 2 * x + y)
                for k, (cx, cy) in enumerate(_other_chips(x, y)):
                    _rcopy(own, own, ssem.at[3 * it + k], rsem.at[3 * it + k], (cx, cy, c)).start()

    def finish(_, full, sems):
        ssem, rsem, s2sem, r2sem = sems
        x, y, c = _position()
        sibling = (x, y, 1 - c)
        for it, (w, lo, nl, owner) in enumerate(items):
            @pl.when(c == owner)
            def _():
                for k, (cx, cy) in enumerate(_other_chips(x, y)):
                    theirs = win(full, w, lo, nl, 2 * cx + cy)
                    _rcopy(theirs, theirs, ssem.at[3 * it + k], rsem.at[3 * it + k], (cx, cy, c)).wait_recv()
                    _rcopy(theirs, theirs, s2sem.at[3 * it + k], r2sem.at[3 * it + k], sibling).start()
        for it, (w, lo, nl, owner) in enumerate(items):
            @pl.when(c == owner)
            def _():
                own = win(full, w, lo, nl, 2 * x + y)
                for k, (cx, cy) in enumerate(_other_chips(x, y)):
                    theirs = win(full, w, lo, nl, 2 * cx + cy)
                    _rcopy(own, own, ssem.at[3 * it + k], rsem.at[3 * it + k], (cx, cy, c)).wait_send()
                    _rcopy(theirs, theirs, s2sem.at[3 * it + k], r2sem.at[3 * it + k], sibling).wait_send()

            @pl.when(c != owner)
            def _():
                for k, (cx, cy) in enumerate(_other_chips(x, y)):
                    theirs = win(full, w, lo, nl, 2 * cx + cy)
                    _rcopy(theirs, theirs, s2sem.at[3 * it + k], r2sem.at[3 * it + k], sibling).wait_recv()

    shapes = tuple(jax.ShapeDtypeStruct(f.shape, f.dtype) for f in fulls)
    return Exchange(tuple(fulls), shapes, True, (3 * n,) * 4, start, finish)


def pair_exchange(grads, owners):
    n = len(grads)

    def start(gr, land, sems):
        x, y, c = _position()
        for it, owner in enumerate(owners):
            @pl.when(c != owner)
            def _():
                _rcopy(gr[it], land[it], sems[0].at[it], sems[1].at[it], (x, y, 1 - c)).start()

    def finish(gr, land, sems):
        x, y, c = _position()
        for it, owner in enumerate(owners):
            cp = _rcopy(gr[it], land[it], sems[0].at[it], sems[1].at[it], (x, y, 1 - c))
            pl.when(c != owner)(cp.wait_send)
            pl.when(c == owner)(cp.wait_recv)

    shapes = tuple(jax.ShapeDtypeStruct(g.shape, g.dtype) for g in grads)
    return Exchange(tuple(grads), shapes, False, (n, n), start, finish)


def pair_add(name, g, landed, owner, c_arr):
    R, C = g.shape
    tr = _tile(R, max(16, (1 << 20) // C // 16 * 16), unit=16)

    def body(c_ref, g_ref, l_ref, o_ref):
        @pl.when(c_ref[0] == owner)
        def _():
            o_ref[...] = (g_ref[...].astype(F32) + l_ref[...].astype(F32)).astype(o_ref.dtype)

    spec = pl.BlockSpec((tr, C), lambda i, c: (jnp.where(c[0] == owner, i, 0), 0))
    grid_spec = pltpu.PrefetchScalarGridSpec(num_scalar_prefetch=1, grid=(R // tr,), in_specs=[spec, spec],
                                             out_specs=spec)
    return pl.pallas_call(body, name=name, grid_spec=grid_spec, out_shape=jax.ShapeDtypeStruct(g.shape, g.dtype),
                          compiler_params=_params())(c_arr, g, landed)


def _window2(ref, dim, jj, rs, cs):
    if dim == 1:
        return ref.at[pl.ds(pl.multiple_of(jj * rs, 16), rs), :]
    return ref.at[:, pl.ds(pl.multiple_of(jj * cs, LANE), cs)]


def chip_exchange(partials, dims, shard_shapes, owners):
    n = len(partials)

    def copies(ps, land, sems, it):
        x, y, c = _position()
        rs, cs = shard_shapes[it]
        return [_rcopy(_window2(ps[it], dims[it], 2 * cx + cy, rs, cs), land[it].at[k],
                       sems[0].at[3 * it + k], sems[1].at[3 * it + k], (cx, cy, c))
                for k, (cx, cy) in enumerate(_other_chips(x, y))]

    def start(ps, land, sems):
        c = lax.axis_index('c')
        for it, owner in enumerate(owners):
            @pl.when(c == owner)
            def _():
                for cp in copies(ps, land, sems, it):
                    cp.start()

    def finish(ps, land, sems):
        c = lax.axis_index('c')
        for it, owner in enumerate(owners):
            @pl.when(c == owner)
            def _():
                for cp in copies(ps, land, sems, it):
                    cp.wait()

    shapes = tuple(jax.ShapeDtypeStruct((3,) + tuple(s), p.dtype) for p, s in zip(partials, shard_shapes))
    return Exchange(tuple(partials), shapes, False, (3 * n, 3 * n), start, finish)


def chip_sum(name, partial, landed, dim, buf, depth, layer, owner, j_arr, c_arr):
    _, rs, cs = landed.shape
    tr = _tile(rs, max(16, (1 << 19) // cs // 16 * 16), unit=16)
    has_buf = buf is not None

    def body(*refs):
        core_ref, p_ref, a_ref, b_ref, c_ref = refs[1:6]
        o_ref = refs[6 + has_buf]

        @pl.when(core_ref[0] == owner)
        def _():
            o_ref[...] = ((p_ref[...].astype(F32) + a_ref[...].astype(F32)) + b_ref[...].astype(F32)) + c_ref[...].astype(F32)

    def on(c, index):
        return jnp.where(c[0] == owner, index, 0)

    if dim == 1:
        own = pl.BlockSpec((tr, cs), lambda i, j, c: (on(c, j[0] * (rs // tr) + i), 0))
    else:
        own = pl.BlockSpec((tr, cs), lambda i, j, c: (on(c, i), on(c, j[0])))
    arrived = [pl.BlockSpec((None, tr, cs), lambda i, j, c, k=k: (k, on(c, i), 0)) for k in range(3)]
    grid_spec = pltpu.PrefetchScalarGridSpec(
        num_scalar_prefetch=2, grid=(rs // tr,), in_specs=[own] + arrived + ([ANY_SPEC] if has_buf else []),
        out_specs=pl.BlockSpec((None, tr, cs), lambda i, j, c: (layer, on(c, i), 0)))
    args = (j_arr, c_arr, partial, landed, landed, landed) + ((buf,) if has_buf else ())
    return pl.pallas_call(body, name=name, grid_spec=grid_spec,
                          out_shape=jax.ShapeDtypeStruct((depth, rs, cs), F32),
                          input_output_aliases={6: 0} if has_buf else {},
                          compiler_params=_params())(*args)


def share_exchange(bufs, items):
    n = len(items)

    def copy(full, sems, it):
        w, lo, nl, _ = items[it]
        x, y, c = _position()
        own = full[w].at[pl.ds(lo, nl)]
        return _rcopy(own, own, sems[0].at[it], sems[1].at[it], (x, y, 1 - c))

    def start(_, full, sems):
        c = lax.axis_index('c')
        for it in range(n):
            pl.when(c == items[it][3])(copy(full, sems, it).start)

    def finish(_, full, sems):
        c = lax.axis_index('c')
        for it in range(n):
            cp = copy(full, sems, it)
            pl.when(c == items[it][3])(cp.wait_send)
            pl.when(c != items[it][3])(cp.wait_recv)

    shapes = tuple(jax.ShapeDtypeStruct(f.shape, f.dtype) for f in bufs)
    return Exchange(tuple(bufs), shapes, True, (n, n), start, finish)


def exchange_small(name, pack, reduce_all):
    R = pack.shape[0]
    n_slots = N_DEVICES if reduce_all else N_CHIPS
    n_peers = n_slots - 1

    def body(p_ref, o_ref, buf, ssem, rsem):
        x, y, c = _position()
        if reduce_all:
            me = 4 * x + 2 * y + c
            peers = [(x ^ (k >> 2 & 1), y ^ (k >> 1 & 1), c ^ (k & 1)) for k in range(1, N_DEVICES)]
        else:
            me = 2 * x + y
            peers = [(cx, cy, c) for cx, cy in _other_chips(x, y)]
        buf[me] = p_ref[...]
        copies = []
        for k, peer in enumerate(peers):
            cp = pltpu.make_async_remote_copy(src_ref=p_ref, dst_ref=buf.at[me], send_sem=ssem.at[k],
                                              recv_sem=rsem.at[k], device_id=peer, device_id_type=MESH)
            cp.start()
            copies.append(cp)
        for k, (px, py, pc) in enumerate(peers):
            slot = 4 * px + 2 * py + pc if reduce_all else 2 * px + py
            pltpu.make_async_remote_copy(src_ref=p_ref, dst_ref=buf.at[slot], send_sem=ssem.at[k],
                                         recv_sem=rsem.at[k], device_id=(px, py, pc), device_id_type=MESH).wait_recv()
        for cp in copies:
            cp.wait_send()
        if reduce_all:
            acc = buf[0]
            for d in range(1, N_DEVICES):
                acc = acc + buf[d]
            o_ref[...] = acc
        else:
            o_ref[...] = buf[...]

    vmem = pl.BlockSpec(memory_space=pltpu.VMEM)
    out_shape = jax.ShapeDtypeStruct((R, LANE) if reduce_all else (N_CHIPS, R, LANE), F32)
    return pl.pallas_call(
        body, name=name, out_shape=out_shape, in_specs=[vmem], out_specs=vmem,
        scratch_shapes=[pltpu.VMEM((n_slots, R, LANE), F32), pltpu.SemaphoreType.DMA((n_peers,)),
                        pltpu.SemaphoreType.DMA((n_peers,))],
        compiler_params=_params())(pack)


def _pack(arrays):
    flat = jnp.concatenate([a.reshape(-1).astype(F32) for a in arrays])
    n = flat.shape[0]
    rows = -(-n // LANE)
    rows = -(-rows // 8) * 8
    return jnp.pad(flat, (0, rows * LANE - n)).reshape(rows, LANE)


def _unpack(pack, shapes, lead=()):
    flat = pack.reshape(lead + (-1,))
    out, off = [], 0
    for s in shapes:
        n = math.prod(s)
        out.append(flat[..., off:off + n].reshape(lead + tuple(s)))
        off += n
    return out


def kernel(x, p, positions, mla_w_in, mla_q_norm, mla_kv_norm, mla_w_uq, mla_w_uk, mla_w_uv, mla_w_o, gla_w_in, gla_w_a2, gla_b_a, gla_o_norm, gla_w_o, ln1_g, ln1_b, ln2_g, ln2_b, ffn_w_up, ffn_conv_w, ffn_conv_b, ffn_w_down, ple_w_proj, ple_w_gate, ple_b_gate, loss_target, m_mla_w_in, m_mla_q_norm, m_mla_kv_norm, m_mla_w_uq, m_mla_w_uk, m_mla_w_uv, m_mla_w_o, m_gla_w_in, m_gla_w_a2, m_gla_b_a, m_gla_o_norm, m_gla_w_o, m_ln1_g, m_ln1_b, m_ln2_g, m_ln2_b, m_ffn_w_up, m_ffn_conv_w, m_ffn_conv_b, m_ffn_w_down, m_ple_w_proj, m_ple_w_gate, m_ple_b_gate, v_mla_w_in, v_mla_q_norm, v_mla_kv_norm, v_mla_w_uq, v_mla_w_uk, v_mla_w_uv, v_mla_w_o, v_gla_w_in, v_gla_w_a2, v_gla_b_a, v_gla_o_norm, v_gla_w_o, v_ln1_g, v_ln1_b, v_ln2_g, v_ln2_b, v_ffn_w_up, v_ffn_conv_w, v_ffn_conv_b, v_ffn_w_down, v_ple_w_proj, v_ple_w_gate, v_ple_b_gate):
    given = dict(locals())
    S, D = x.shape[1], x.shape[2]
    QL, KL = mla_q_norm.shape[1], mla_kv_norm.shape[1]
    H = mla_w_uq.shape[2] * N_CHIPS // (MLA_NOPE + MLA_ROPE)
    QK, VD = gla_b_a.shape[1] * N_CHIPS, gla_o_norm.shape[1] * N_CHIPS
    FF = ffn_w_down.shape[1] * N_CHIPS
    GIN = 2 * QK + 2 * VD + GLA_GATE_RANK
    GIN_PAD = 2 * QK + 2 * VD + LANE
    MIN = QL + KL + MLA_ROPE
    MIN_PAD = QL + KL + LANE
    a_blk = (2 * QK + 2 * VD) // LANE
    r_blk = (2 * QK + VD) // VD
    xi, yi, ci = _position()
    chip = 2 * xi + yi
    c_arr = jnp.reshape(ci, (1,)).astype(jnp.int32)
    j_arr = jnp.reshape(chip, (1,)).astype(jnp.int32)

    names = [n for n, _ in BIG]
    big_dims = [1 if n == 'gla_w_in' else d for n, d in BIG]
    shapes = [given[n].shape for n in names]

    def layer_of(name, idx):
        return 2 * idx if name.startswith('mla') else 2 * idx + 1 if name.startswith('gla') else idx

    def owner_of(layer):
        return layer // (DEPTH // 2)

    def layer_runs(wanted):
        out = []
        for w, n in enumerate(names):
            for owner in (0, 1):
                idxs = [k for k in range(shapes[w][0])
                        if owner_of(layer_of(n, k)) == owner and wanted(n, layer_of(n, k))]
                if idxs:
                    assert idxs == list(range(idxs[0], idxs[0] + len(idxs)))
                    out.append((w, idxs[0], len(idxs), owner))
        return out

    before_attention = ('mla_w_in', 'mla_w_uq', 'mla_w_uk', 'mla_w_uv')
    gather_phase = {0: lambda n, l: l == 0 and n not in before_attention or l in (1, 2),
                    2: lambda n, l: l == 3}
    placed = [place_shard(f'place_{n}', given[n], d, j_arr) for n, d in zip(names, big_dims)]
    first = run_exchange('gather_first', gather_exchange(
        placed, big_dims, shapes, layer_runs(lambda n, l: l == 0 and n in before_attention)))
    full = dict(zip(names, first))
    small_sharded = [n for n, d in SMALL if d is not None]
    spack = exchange_small('gather_small', _pack([given[n] for n in small_sharded]), False)
    parts = _unpack(spack, [given[n].shape for n in small_sharded], lead=(N_CHIPS,))
    for n, part in zip(small_sharded, parts):
        d = dict(SMALL)[n]
        full[n] = jnp.concatenate([part[k] for k in range(N_CHIPS)], axis=d)

    def mla_in_weights(j):
        w_in = jnp.pad(full['mla_w_in'][j], ((0, 0), (0, MIN_PAD - MIN)))
        w_uq = full['mla_w_uq'][j].reshape(QL, H, MLA_NOPE + MLA_ROPE)
        w_uq = jnp.pad(w_uq, ((0, 0), (0, 0), (0, HEAD_PAD - MLA_NOPE - MLA_ROPE))).reshape(QL, H * HEAD_PAD)
        return w_in, w_uq

    def gla_in_weight(j):
        w = full['gla_w_in'][j].reshape(N_CHIPS, D, GIN // N_CHIPS).transpose(1, 0, 2).reshape(D, GIN)
        return jnp.pad(w, ((0, 0), (0, GIN_PAD - GIN)))

    w_a2 = jnp.pad(full['gla_w_a2'], ((0, 0), (0, LANE - GLA_GATE_RANK), (0, 0))).astype(MXU_DTYPE)
    cw_u, cw_g = full['ffn_conv_w'][:, :, :FF], full['ffn_conv_w'][:, :, FF:]
    cb_u, cb_g = ffn_conv_b[:, None, :FF], ffn_conv_b[:, None, FF:]

    pos_row = positions[0]
    cid = pos_row // CHUNK
    t_att = _tile(S, ATT_T)
    cid_col = cid.reshape(S, 1)
    cid_blk = cid.reshape(S // t_att, 1, t_att)
    tab_q = rope_tables(pos_row, MLA_NOPE)
    tab_k = rope_tables(pos_row, 0)

    def row1(a, i):
        return a[i:i + 1]

    saved = []
    xa = x[0]
    xb = xa
    for i in range(DEPTH):
        j = i // 2
        sv = dict(x=xa, xb=xb)
        if i % 2 == 0:
            w_in, w_uq = mla_in_weights(j)
            h = mm(f'mla_in_{i}', xb, w_in, 'nn', F32)
            cq, ckv, kr = mla_pre_fwd(f'mla_pre_{i}', h, row1(mla_q_norm, j), row1(mla_kv_norm, j), tab_k, QL, KL)
            qpre = mm(f'mla_uq_{i}', cq, w_uq, 'nn', F32)
            q = rope_q(f'mla_rope_{i}', qpre, tab_q, H, MXU_DTYPE)
            kn = mm(f'mla_uk_{i}', ckv, full['mla_w_uk'], 'nn', MXU_DTYPE, b_pre=(j,))
            vv = mm(f'mla_uv_{i}', ckv, full['mla_w_uv'], 'nn', MXU_DTYPE, b_pre=(j,))
            rider = gather_exchange([full[n] for n in names], big_dims, shapes, layer_runs(gather_phase[i]))
            (o, lse), gathered = attn_fwd(f'mla_attn_{i}', q, kn, kr, vv, cid_col, cid_blk, H, rider=rider)
            full.update(zip(names, gathered))
            mix = mm(f'mla_o_{i}', o, full['mla_w_o'], 'nn', F32, b_pre=(j,))
            sv.update(h=h, cq=cq, ckv=ckv, kr=kr, q=q, kn=kn, v=vv, o=o, lse=lse, w_in=w_in, w_uq=w_uq)
        else:
            w_gin = gla_in_weight(j)
            sv.update(w_gin=w_gin)
            h = mm(f'gla_in_{i}', xb, w_gin, 'nn', F32)
            qs, la = gla_pre_fwd(f'gla_pre_{i}', h, w_a2[j], row1(full['gla_b_a'], j), QK, a_blk)
            o, states = gla_fwd(f'gla_scan_{i}', qs, h, la, QK, VD)
            og = gla_post_fwd(f'gla_post_{i}', o, h, row1(full['gla_o_norm'], j), VD, r_blk)
            mix = mm(f'gla_o_{i}', og, full['gla_w_o'], 'nn', F32, b_pre=(j,))
            sv.update(h=h, qs=qs, la=la, o=o, states=states, og=og)
        x1, x1b = ln_fwd(f'ln1_{i}', xa, mix, row1(ln1_g, i), row1(ln1_b, i))
        hu = mm(f'ffn_up_u_{i}', x1b, full['ffn_w_up'], 'nn', F32, b_pre=(i,), b_win=(0, D, 0, FF))
        hg = mm(f'ffn_up_g_{i}', x1b, full['ffn_w_up'], 'nn', F32, b_pre=(i,), b_win=(0, D, FF, FF))
        act = conv_fwd(f'ffn_conv_{i}', hu, hg, cw_u[i], cw_g[i], cb_u[i], cb_g[i])
        f = mm(f'ffn_down_{i}', act, full['ffn_w_down'], 'nn', F32, b_pre=(i,))
        x2, x2b = ln_fwd(f'ln2_{i}', x1, f, row1(ln2_g, i), row1(ln2_b, i))
        gp = mm(f'ple_gate_{i}', x2b, full['ple_w_gate'], 'nn', F32, b_pre=(i,))
        pp = mm(f'ple_proj_{i}', p, full['ple_w_proj'], 'nn', F32, a_pre=(i, 0), b_pre=(i,))
        x3, x3b = ple_fwd(f'ple_{i}', x2, gp, pp, row1(ple_b_gate, i))
        sv.update(mix=mix, x1=x1, x1b=x1b, hu=hu, hg=hg, act=act, f=f, x2=x2, x2b=x2b, gp=gp, pp=pp)
        saved.append(sv)
        xa, xb = x3, x3b

    dy, sq = loss_head('loss_head', xa, loss_target[0])
    loss_part = (0.5 / D) * jnp.sum(sq)

    small_g = {n: [None] * given[n].shape[0] for n, _ in SMALL}
    tab_q_t = tuple(t.T for t in tab_q)
    shard_buf = {n: None for n in names}

    def reduce_begin(tag, items):
        owners = [owner_of(layer) for _, _, layer, _ in items]
        landed = run_exchange(f'pair_exchange_{tag}', pair_exchange([g for *_, g in items], owners))
        partial = [pair_add(f'pair_add_{n}_{layer}', g, l, o, c_arr)
                   for (n, _, layer, g), l, o in zip(items, landed, owners)]
        dims = [big_dims[names.index(n)] for n, *_ in items]
        exchange = chip_exchange(partial, dims, [given[n].shape[1:] for n, *_ in items], owners)
        return exchange, (items, partial, dims, owners)

    def reduce_end(context, arrived):
        items, partial, dims, owners = context
        for (n, idx, layer, _), pt, ar, d, o in zip(items, partial, arrived, dims, owners):
            shard_buf[n] = chip_sum(f'chip_sum_{n}_{layer}', pt, ar, d, shard_buf[n], given[n].shape[0], idx, o,
                                    j_arr, c_arr)

    def wgrad(name, a, b, a_pre=()):
        return mm(f'd_{name}_{i}', a, b, 'tn', MXU_DTYPE, a_pre=a_pre)

    def layer_items(layer_grads):
        return [(n, i if n.startswith(('ffn', 'ple')) else i // 2, i, g) for n, g in layer_grads.items()]

    dA, ca, dB = dy, 1.0, None
    waiting = []
    for i in reversed(range(DEPTH)):
        j = i // 2
        sv = saved[i]
        lg = {}
        dx3, dpp, dgp, dbg = ple_bwd(f'ple_b_{i}', dA, ca, dB, sv['gp'], sv['pp'], row1(ple_b_gate, i))
        small_g['ple_b_gate'][i] = dbg
        lg['ple_w_proj'] = wgrad('ple_w_proj', p, dpp, a_pre=(i, 0))
        lg['ple_w_gate'] = wgrad('ple_w_gate', sv['x2b'], dgp)
        dx2 = mm(f'ple_gate_b_{i}', dgp, full['ple_w_gate'], 'nt', F32, b_pre=(i,))
        dz2, dz2b, dg2, db2 = ln_bwd(f'ln2_b_{i}', sv['x1'], sv['f'], dx3, 1.0, dx2, row1(ln2_g, i))
        small_g['ln2_g'][i], small_g['ln2_b'][i] = dg2, db2
        lg['ffn_w_down'] = wgrad('ffn_w_down', sv['act'], dz2b)
        dact = mm(f'ffn_down_b_{i}', dz2b, full['ffn_w_down'], 'nt', F32, b_pre=(i,))
        dhu, dhg, dcwu, dcwg, dcbu, dcbg = conv_bwd(f'ffn_conv_b_{i}', sv['hu'], sv['hg'], dact,
                                                   cw_u[i], cw_g[i], cb_u[i], cb_g[i])
        small_g['ffn_conv_w'][i] = jnp.concatenate([dcwu, dcwg], -1)
        small_g['ffn_conv_b'][i] = jnp.concatenate([dcbu, dcbg], -1)
        up = mm(f'd_ffn_w_up_u_{i}', sv['x1b'], dhu, 'tn', MXU_DTYPE, out_stack=(None, (1, D, 2 * FF), 0, 0))
        up = mm(f'd_ffn_w_up_g_{i}', sv['x1b'], dhg, 'tn', MXU_DTYPE, out_stack=(up, (1, D, 2 * FF), 0, FF))
        lg['ffn_w_up'] = up[0]
        dx1 = mm(f'ffn_up_bu_{i}', dhu, full['ffn_w_up'], 'nt', F32, b_pre=(i,), b_win=(0, D, 0, FF))
        dx1 = mm(f'ffn_up_bg_{i}', dhg, full['ffn_w_up'], 'nt', F32, b_pre=(i,), b_win=(0, D, FF, FF), acc_in=dx1)
        dz1, dz1b, dg1, db1 = ln_bwd(f'ln1_b_{i}', sv['x'], sv['mix'], dz2, DN_ALPHA, dx1, row1(ln1_g, i))
        small_g['ln1_g'][i], small_g['ln1_b'][i] = dg1, db1
        if i % 2 == 0:
            lg['mla_w_o'] = wgrad('mla_w_o', sv['o'], dz1b)
            do = mm(f'mla_o_b_{i}', dz1b, full['mla_w_o'], 'nt', MXU_DTYPE, b_pre=(j,))
            waiting += layer_items(lg)
            lg = {}
            riding, context = reduce_begin(f'before_{i}', waiting)
            waiting = []
            (dq, dkn, dv, dkr), arrived = attn_bwd(f'mla_attn_b_{i}', sv['q'], sv['kn'], sv['kr'], sv['v'], sv['o'],
                                                   sv['lse'], do, cid_col, cid_blk, H, rider=riding)
            reduce_end(context, arrived)
            dqpre_t = rope_q_bwd(f'mla_rope_b_{i}', dq, tab_q_t, H)
            uq_t = mm(f'd_mla_w_uq_{i}', dqpre_t, sv['cq'], 'nn', MXU_DTYPE)
            lg['mla_w_uq'] = uq_t.T.reshape(QL, H, HEAD_PAD)[..., :MLA_NOPE + MLA_ROPE].reshape(QL, -1)
            lg['mla_w_uk'] = wgrad('mla_w_uk', sv['ckv'], dkn)
            lg['mla_w_uv'] = wgrad('mla_w_uv', sv['ckv'], dv)
            dcq = mm(f'mla_uq_b_{i}', dqpre_t, sv['w_uq'].T, 'tn', F32)
            dckv_a = mm(f'mla_uk_b_{i}', dkn, full['mla_w_uk'], 'nt', F32, b_pre=(j,))
            dckv_b = mm(f'mla_uv_b_{i}', dv, full['mla_w_uv'], 'nt', F32, b_pre=(j,))
            dh, dqn, dkvn = mla_pre_bwd(f'mla_pre_b_{i}', sv['h'], dcq, dckv_a, dckv_b, dkr,
                                        row1(mla_q_norm, j), row1(mla_kv_norm, j), tab_k, QL, KL, H)
            small_g['mla_q_norm'][j], small_g['mla_kv_norm'][j] = dqn, dkvn
            lg['mla_w_in'] = wgrad('mla_w_in', sv['xb'], dh)[:, :MIN]
            dmix = mm(f'mla_in_b_{i}', dh, sv['w_in'], 'nt', F32)
        else:
            lg['gla_w_o'] = wgrad('gla_w_o', sv['og'], dz1b)
            dog = mm(f'gla_o_b_{i}', dz1b, full['gla_w_o'], 'nt', F32, b_pre=(j,))
            do, dr, don = gla_post_bwd(f'gla_post_b_{i}', sv['o'], sv['h'], dog, row1(full['gla_o_norm'], j), VD, r_blk)
            dq, dk, dv, dla = gla_bwd(f'gla_scan_b_{i}', sv['qs'], sv['h'], sv['la'], sv['states'], do, QK, VD)
            dh, dw2, dba = gla_dh(f'gla_dh_{i}', dq, dk, dv, dr, dla, sv['h'], w_a2[j], row1(full['gla_b_a'], j),
                                  QK, VD, a_blk, GIN_PAD)
            small_g['gla_o_norm'][j], small_g['gla_b_a'][j] = don, dba
            small_g['gla_w_a2'][j] = dw2[:GLA_GATE_RANK]
            g_in = wgrad('gla_w_in', sv['xb'], dh)[:, :GIN]
            lg['gla_w_in'] = g_in.reshape(D, N_CHIPS, GIN // N_CHIPS).transpose(1, 0, 2).reshape(N_CHIPS * D, -1)
            dmix = mm(f'gla_in_b_{i}', dh, sv['w_gin'], 'nt', F32)
        dA, ca, dB = dz1, DN_ALPHA, dmix
        waiting += layer_items(lg)
    grad_x = axpy('grad_x', dA, ca, dB)[None]

    exchange, context = reduce_begin('last', waiting)
    reduce_end(context, run_exchange('chip_exchange_last', exchange))
    shared = run_exchange('share_with_sibling',
                          share_exchange([shard_buf[n] for n in names], layer_runs(lambda n, l: True)))
    grads = dict(zip(names, shared))

    small_names = [n for n, _ in SMALL]
    small_full = [jnp.concatenate([g.reshape((1,) + g.shape[-(given[n].ndim - 1):]) for g in small_g[n]], 0)
                  for n in small_names]
    pack = _pack([jnp.reshape(loss_part, (1,))] + [jnp.zeros((LANE - 1,), F32)] + small_full)
    red = exchange_small('reduce_small', pack, True)
    red_parts = _unpack(red, [(LANE,)] + [g.shape for g in small_full])
    loss = red_parts[0][0]
    for (n, d), g in zip(SMALL, red_parts[1:]):
        if d is not None:
            width = given[n].shape[d]
            g = lax.dynamic_slice_in_dim(g, chip * width, width, axis=d)
        grads[n] = g

    delta, new_m, new_v = {}, {}, {}
    for n in names:
        shape = given[n].shape
        flat = lambda a: a.reshape(-1, shape[-1])
        d_, m_, v_ = adamw(f'adamw_{n}', flat(given[n]), flat(grads[n]), flat(given['m_' + n]), flat(given['v_' + n]))
        delta[n], new_m[n], new_v[n] = d_.reshape(shape), m_.reshape(shape), v_.reshape(shape)
    packs = [_pack([given[pre + n] for n in small_names]) for pre in ('', 'm_', 'v_')]
    outs = adamw('adamw_small', packs[0], _pack([grads[n] for n in small_names]), packs[1], packs[2])
    small_shapes = [given[n].shape for n in small_names]
    for dst, out in zip((delta, new_m, new_v), outs):
        for n, a in zip(small_names, _unpack(out, small_shapes)):
            dst[n] = a

    return (loss, grad_x, *[grads[n] for n in WEIGHT_NAMES], *[delta[n] for n in WEIGHT_NAMES],
            *[new_m[n] for n in WEIGHT_NAMES], *[new_v[n] for n in WEIGHT_NAMES])
```

```python
import functools
import math
from typing import Callable, NamedTuple

import jax
import jax.numpy as jnp
from jax import lax
from jax.experimental import pallas as pl
from jax.experimental.pallas import tpu as pltpu

F32 = jnp.float32
MXU_DTYPE = jnp.bfloat16

DEPTH = 4
CHUNK = 64
Q_BLOCK = 128
MLA_NOPE = 128
MLA_ROPE = 64
MLA_V = 128
ROPE_THETA = 10000.0
GLA_HEADS = 4
GLA_GATE_RANK = 16
GLA_TAU = 16.0
DN_ALPHA = (2 * DEPTH) ** 0.25
EPS = 1e-5
NEG_INF = -1e30
ADAM_LR = 0.001
ADAM_B1 = 0.9
ADAM_B2 = 0.999
ADAM_EPS = 1e-08
ADAM_WD = 0.01
ADAM_STEP = 10
GELU_C = math.sqrt(2.0 / math.pi)
GELU_A = 0.044715

LANE = 128
HEAD_PAD = 2 * LANE
VMEM_LIMIT_BYTES = 48 * 1024 * 1024

MESH = pl.DeviceIdType.MESH
ANY_SPEC = pl.BlockSpec(memory_space=pl.ANY)

WEIGHT_NAMES = ('mla_w_in', 'mla_q_norm', 'mla_kv_norm', 'mla_w_uq', 'mla_w_uk', 'mla_w_uv', 'mla_w_o',
                'gla_w_in', 'gla_w_a2', 'gla_b_a', 'gla_o_norm', 'gla_w_o', 'ln1_g', 'ln1_b', 'ln2_g', 'ln2_b',
                'ffn_w_up', 'ffn_conv_w', 'ffn_conv_b', 'ffn_w_down', 'ple_w_proj', 'ple_w_gate', 'ple_b_gate')
BIG = (('mla_w_in', 1), ('mla_w_uq', 2), ('mla_w_uk', 2), ('mla_w_uv', 2), ('mla_w_o', 1), ('gla_w_in', 2),
       ('gla_w_o', 1), ('ffn_w_up', 2), ('ffn_w_down', 1), ('ple_w_proj', 2), ('ple_w_gate', 1))
SMALL = (('mla_q_norm', None), ('mla_kv_norm', None), ('gla_w_a2', 2), ('gla_b_a', 1), ('gla_o_norm', 1),
         ('ln1_g', None), ('ln1_b', None), ('ln2_g', None), ('ln2_b', None), ('ffn_conv_w', 2),
         ('ffn_conv_b', None), ('ple_b_gate', None))
N_CHIPS = 4
N_DEVICES = 8


def _params():
    return pltpu.CompilerParams(vmem_limit_bytes=VMEM_LIMIT_BYTES)


def _tile(n, cap, *offsets, unit=LANE):
    g = n
    for o in offsets:
        if o:
            g = math.gcd(g, o)
    best = 0
    for d in range(unit, min(g, cap) + 1, unit):
        if g % d == 0:
            best = d
    if best:
        return best
    assert not any(offsets), (n, offsets)
    return n


def _dot(a, b, dims):
    return lax.dot_general(a.astype(MXU_DTYPE), b.astype(MXU_DTYPE), (dims, ((), ())),
                           preferred_element_type=F32)


NN = ((1,), (0,))
NT = ((1,), (1,))
TN = ((0,), (0,))


def mm(name, a, b, mode, out_dtype, *, a_pre=(), b_pre=(), b_win=None, acc_in=None, out_stack=None):
    a2 = a.shape[len(a_pre):]
    b2 = b.shape[len(b_pre):]
    br0, brn, bc0, bcn = b_win or (0, b2[0], 0, b2[1])
    if mode == 'nn':
        (M, K), N, dims = a2, bcn, NN
        assert brn == K
    elif mode == 'nt':
        (M, K), N, dims = a2, brn, NT
        assert bcn == K
    else:
        (K, M), N, dims = a2, bcn, TN
        assert brn == K
    oc0 = out_stack[3] if out_stack else 0
    tm = _tile(M, 1408 if mode == 'tn' else 1024)
    if mode == 'nn':
        tn, tk = _tile(N, 1408, bc0, oc0), _tile(K, 1408, br0)
    elif mode == 'nt':
        tn, tk = _tile(N, 1408, br0, oc0), _tile(K, 1408, bc0)
    else:
        tn, tk = _tile(N, 1408, bc0, oc0), _tile(K, 1024, br0)
    nk = K // tk
    grid = (M // tm, N // tn, nk)

    na, nb = len(a_pre), len(b_pre)
    if mode == 'tn':
        a_spec = pl.BlockSpec((None,) * na + (tk, tm), lambda i, j, k: a_pre + (k, i))
    else:
        a_spec = pl.BlockSpec((None,) * na + (tm, tk), lambda i, j, k: a_pre + (i, k))
    if mode == 'nt':
        b_spec = pl.BlockSpec((None,) * nb + (tn, tk), lambda i, j, k: b_pre + (j + br0 // tn, k + bc0 // tk))
    else:
        b_spec = pl.BlockSpec((None,) * nb + (tk, tn), lambda i, j, k: b_pre + (k + br0 // tk, j + bc0 // tn))
    in_specs, args = [a_spec, b_spec], [a, b]
    if acc_in is not None:
        in_specs.append(pl.BlockSpec((tm, tn), lambda i, j, k: (i, j)))
        args.append(acc_in)
    aliases = {}
    if out_stack is None:
        out_shape = jax.ShapeDtypeStruct((M, N), out_dtype)
        out_spec = pl.BlockSpec((tm, tn), lambda i, j, k: (i, j))
    else:
        buf, full_shape, lead, _ = out_stack
        out_shape = jax.ShapeDtypeStruct(full_shape, out_dtype)
        out_spec = pl.BlockSpec((None, tm, tn), lambda i, j, k: (lead, i, j + oc0 // tn))
        if buf is not None:
            aliases = {len(args): 0}
            in_specs.append(ANY_SPEC)
            args.append(buf)
    has_c, has_alias = acc_in is not None, bool(aliases)

    def body(*refs):
        a_ref, b_ref = refs[0], refs[1]
        c_ref = refs[2] if has_c else None
        o_ref = refs[2 + has_c + has_alias]
        prod = _dot(a_ref[...], b_ref[...], dims)
        if nk == 1:
            if has_c:
                prod = prod + c_ref[...]
            o_ref[...] = prod.astype(out_dtype)
            return
        acc_ref = refs[3 + has_c + has_alias]
        k = pl.program_id(2)

        @pl.when(k == 0)
        def _():
            acc_ref[...] = prod + c_ref[...] if has_c else prod

        @pl.when(k > 0)
        def _():
            acc_ref[...] += prod

        @pl.when(k == nk - 1)
        def _():
            o_ref[...] = acc_ref[...].astype(out_dtype)

    scratch = [pltpu.VMEM((tm, tn), F32)] if nk > 1 else []
    return pl.pallas_call(body, out_shape=out_shape, grid=grid, in_specs=in_specs, out_specs=out_spec,
                          scratch_shapes=scratch, input_output_aliases=aliases, name=name,
                          compiler_params=_params())(*args)


def rowcall(name, body, n_rows, ts, row_ins, full_ins, row_outs, acc_outs=()):
    in_specs, args = [], []
    for arr, width, colblk, pre in row_ins:
        in_specs.append(pl.BlockSpec((None,) * len(pre) + (ts, width),
                                     lambda i, pre=pre, cb=colblk: pre + (i, cb)))
        args.append(arr)
    for arr in full_ins:
        in_specs.append(pl.BlockSpec(arr.shape, lambda i, nd=arr.ndim: (0,) * nd))
        args.append(arr)
    out_shape, out_specs = [], []
    for width, dtype in row_outs:
        out_shape.append(jax.ShapeDtypeStruct((n_rows, width), dtype))
        out_specs.append(pl.BlockSpec((ts, width), lambda i: (i, 0)))
    for shape in acc_outs:
        out_shape.append(jax.ShapeDtypeStruct(shape, F32))
        out_specs.append(pl.BlockSpec(shape, lambda i, nd=len(shape): (0,) * nd))
    n_in = len(args)

    def kern(*refs):
        body(pl.program_id(0), refs[:n_in], refs[n_in:])

    return pl.pallas_call(kern, out_shape=tuple(out_shape), grid=(n_rows // ts,), in_specs=in_specs,
                          out_specs=tuple(out_specs), name=name, compiler_params=_params())(*args)


def _row(arr, width=None, colblk=0, pre=()):
    return (arr, arr.shape[-1] if width is None else width, colblk, pre)


def _init_acc(i, refs):
    @pl.when(i == 0)
    def _():
        for r in refs:
            r[...] = jnp.zeros(r.shape, r.dtype)


def _colsum(v):
    return jnp.sum(v, axis=0, keepdims=True)


def _sigmoid(z):
    return 1.0 / (1.0 + jnp.exp(-z))


def _row_tile(S):
    return _tile(S, 256, unit=16)


def _ln_stats(x_ref, m_ref):
    z = DN_ALPHA * x_ref[...] + m_ref[...]
    mu = jnp.mean(z, -1, keepdims=True)
    zc = z - mu
    var = jnp.mean(zc * zc, -1, keepdims=True)
    r = lax.rsqrt(var + EPS)
    return zc * r, r


def ln_fwd(name, x, m, g, b):
    S, D = x.shape

    def body(i, ins, outs):
        x_ref, m_ref, g_ref, b_ref = ins
        xh, _ = _ln_stats(x_ref, m_ref)
        y = xh * g_ref[...] + b_ref[...]
        outs[0][...] = y
        outs[1][...] = y.astype(MXU_DTYPE)

    return rowcall(name, body, S, _row_tile(S), [_row(x), _row(m)], [g, b], [(D, F32), (D, MXU_DTYPE)])


def ln_bwd(name, x, m, dA, ca, dB, g):
    S, D = x.shape
    has_b = dB is not None

    def body(i, ins, outs):
        x_ref, m_ref, a_ref = ins[:3]
        g_ref = ins[-1]
        dz_ref, dzb_ref, dg_ref, db_ref = outs
        _init_acc(i, (dg_ref, db_ref))
        dy = ca * a_ref[...]
        if has_b:
            dy = dy + ins[3][...]
        xh, r = _ln_stats(x_ref, m_ref)
        dg_ref[...] += _colsum(dy * xh)
        db_ref[...] += _colsum(dy)
        dxh = dy * g_ref[...]
        dz = r * (dxh - jnp.mean(dxh, -1, keepdims=True) - xh * jnp.mean(dxh * xh, -1, keepdims=True))
        dz_ref[...] = dz
        dzb_ref[...] = dz.astype(MXU_DTYPE)

    rows = [_row(x), _row(m), _row(dA)] + ([_row(dB)] if has_b else [])
    return rowcall(name, body, S, _row_tile(S), rows, [g], [(D, F32), (D, MXU_DTYPE)], [(1, D), (1, D)])


def ple_fwd(name, x2, gp, pp, bias):
    S, D = x2.shape

    def body(i, ins, outs):
        x_ref, gp_ref, pp_ref, b_ref = ins
        y = x_ref[...] + _sigmoid(gp_ref[...] + b_ref[...]) * pp_ref[...]
        outs[0][...] = y
        outs[1][...] = y.astype(MXU_DTYPE)

    return rowcall(name, body, S, _row_tile(S), [_row(x2), _row(gp), _row(pp)], [bias],
                   [(D, F32), (D, MXU_DTYPE)])


def ple_bwd(name, dA, ca, dB, gp, pp, bias):
    S, D = gp.shape
    has_b = dB is not None

    def body(i, ins, outs):
        a_ref = ins[0]
        gp_ref, pp_ref, b_ref = ins[-3:]
        dx_ref, dpp_ref, dgp_ref, db_ref = outs
        _init_acc(i, (db_ref,))
        dx = ca * a_ref[...]
        if has_b:
            dx = dx + ins[1][...]
        gate = _sigmoid(gp_ref[...] + b_ref[...])
        dgp = dx * pp_ref[...] * gate * (1.0 - gate)
        dx_ref[...] = dx
        dpp_ref[...] = (dx * gate).astype(MXU_DTYPE)
        dgp_ref[...] = dgp.astype(MXU_DTYPE)
        db_ref[...] += _colsum(dgp)

    rows = [_row(dA)] + ([_row(dB)] if has_b else []) + [_row(gp), _row(pp)]
    return rowcall(name, body, S, _row_tile(S), rows, [bias],
                   [(D, F32), (D, MXU_DTYPE), (D, MXU_DTYPE)], [(1, D)])


def loss_head(name, y, target):
    S, D = y.shape

    def body(i, ins, outs):
        _init_acc(i, (outs[1],))
        e = ins[0][...] - ins[1][...]
        outs[0][...] = e * (1.0 / D)
        outs[1][...] += _colsum(e * e)

    return rowcall(name, body, S, _row_tile(S), [_row(y), _row(target)], [], [(D, F32)], [(1, D)])


def axpy(name, a, ca, b):
    S, D = a.shape

    def body(i, ins, outs):
        outs[0][...] = ca * ins[0][...] + ins[1][...]

    return rowcall(name, body, S, _row_tile(S), [_row(a), _row(b)], [], [(D, F32)])[0]


HALF_ROPE = MLA_ROPE // 2


def _rope(x, c, sa, sb):
    n = x.shape[-1]
    return x * c + pltpu.roll(x, n - HALF_ROPE, 1) * sa + pltpu.roll(x, HALF_ROPE, 1) * sb


def _rope_t(d, c, sa, sb):
    n = d.shape[-1]
    return d * c + pltpu.roll(d * sa, HALF_ROPE, 1) + pltpu.roll(d * sb, n - HALF_ROPE, 1)


def rope_tables(positions_row, n_lead):
    inv = 1.0 / (ROPE_THETA ** (jnp.arange(0, MLA_ROPE, 2, dtype=F32) / MLA_ROPE))
    ang = positions_row.astype(F32)[:, None] * inv
    cos, sin = jnp.cos(ang), jnp.sin(ang)
    S = cos.shape[0]
    z = jnp.zeros((S, HALF_ROPE), F32)
    tail = jnp.zeros((S, LANE - MLA_ROPE), F32)
    c = jnp.concatenate([jnp.ones((S, n_lead), F32), cos, cos, tail], -1)
    sa = jnp.concatenate([jnp.zeros((S, n_lead), F32), -sin, z, tail], -1)
    sb = jnp.concatenate([jnp.zeros((S, n_lead), F32), z, sin, tail], -1)
    return c, sa, sb


def mla_pre_fwd(name, h, qn, kvn, tab_k, QL, KL):
    S = h.shape[0]

    def body(i, ins, outs):
        h_ref, c_ref, sa_ref, sb_ref, qn_ref, kvn_ref = ins
        cq = h_ref[:, 0:QL]
        ckv = h_ref[:, QL:QL + KL]
        kr = h_ref[:, QL + KL:QL + KL + LANE]
        outs[0][...] = (cq * lax.rsqrt(jnp.mean(cq * cq, -1, keepdims=True) + EPS) * qn_ref[...]).astype(MXU_DTYPE)
        outs[1][...] = (ckv * lax.rsqrt(jnp.mean(ckv * ckv, -1, keepdims=True) + EPS) * kvn_ref[...]).astype(MXU_DTYPE)
        outs[2][...] = _rope(kr, c_ref[...], sa_ref[...], sb_ref[...]).astype(MXU_DTYPE)

    rows = [_row(h)] + [_row(t) for t in tab_k]
    return rowcall(name, body, S, _row_tile(S), rows, [qn, kvn],
                   [(QL, MXU_DTYPE), (KL, MXU_DTYPE), (LANE, MXU_DTYPE)])


def _rms_bwd(x, g, dy):
    r = lax.rsqrt(jnp.mean(x * x, -1, keepdims=True) + EPS)
    dg = _colsum(dy * x * r)
    dxg = dy * g
    dx = r * dxg - x * (r * r * r) * jnp.mean(dxg * x, -1, keepdims=True)
    return dx, dg


def mla_pre_bwd(name, h, dcq, dckv_a, dckv_b, dkr_heads, qn, kvn, tab_k, QL, KL, H):
    S, HW = h.shape

    def body(i, ins, outs):
        h_ref, dcq_ref, da_ref, db_ref, dkr_ref, c_ref, sa_ref, sb_ref, qn_ref, kvn_ref = ins
        dh_ref, dqn_ref, dkvn_ref = outs
        _init_acc(i, (dqn_ref, dkvn_ref))
        dx, dg = _rms_bwd(h_ref[:, 0:QL], qn_ref[...], dcq_ref[...])
        dh_ref[:, 0:QL] = dx.astype(MXU_DTYPE)
        dqn_ref[...] += dg
        dx, dg = _rms_bwd(h_ref[:, QL:QL + KL], kvn_ref[...], da_ref[...] + db_ref[...])
        dh_ref[:, QL:QL + KL] = dx.astype(MXU_DTYPE)
        dkvn_ref[...] += dg
        d = dkr_ref[:, 0:LANE]
        for hh in range(1, H):
            d = d + dkr_ref[:, hh * LANE:(hh + 1) * LANE]
        dh_ref[:, QL + KL:QL + KL + LANE] = _rope_t(d, c_ref[...], sa_ref[...], sb_ref[...]).astype(MXU_DTYPE)

    rows = [_row(h), _row(dcq), _row(dckv_a), _row(dckv_b), _row(dkr_heads)] + [_row(t) for t in tab_k]
    return rowcall(name, body, S, _row_tile(S), rows, [qn, kvn], [(HW, MXU_DTYPE)], [(1, QL), (1, KL)])


def rope_q(name, qpre, tab_q, H, out_dtype):
    S = qpre.shape[0]

    def body(i, ins, outs):
        c, sa, sb = ins[1][...], ins[2][...], ins[3][...]
        for hh in range(H):
            sl = slice(hh * HEAD_PAD, (hh + 1) * HEAD_PAD)
            outs[0][:, sl] = _rope(ins[0][:, sl], c, sa, sb).astype(out_dtype)

    rows = [_row(qpre)] + [_row(t) for t in tab_q]
    return rowcall(name, body, S, _row_tile(S), rows, [], [(H * HEAD_PAD, out_dtype)])[0]


ATT_T_FWD = 1024
ATT_T_BWD = 512


def _att_mask_t(ck_col, cq_row, k0, q0, t):
    kpos = k0 + lax.broadcasted_iota(jnp.int32, (t, 1), 0)
    qpos = q0 + lax.broadcasted_iota(jnp.int32, (1, t), 1)
    return (ck_col <= cq_row) & (kpos <= (qpos | (Q_BLOCK - 1)))


def _transpose(a):
    return a.astype(F32).T.astype(a.dtype)


def hosted_call(body, *, name, grid, in_specs, out_specs, out_shape, scratch_shapes, args, rider):
    if rider is None:
        return pl.pallas_call(body, name=name, grid=grid, in_specs=in_specs, out_specs=out_specs,
                              out_shape=out_shape, scratch_shapes=scratch_shapes,
                              compiler_params=_params())(*args), ()
    n_in, n_out, n_scr = len(args), len(out_shape), len(scratch_shapes)
    r_in, r_out = len(rider.ins), len(rider.out_shapes)

    def hosted(*refs):
        ins, refs = refs[:n_in], refs[n_in:]
        rin, refs = refs[:r_in], refs[r_in:]
        outs, refs = refs[:n_out], refs[n_out:]
        rout, refs = refs[:r_out], refs[r_out:]
        scr, sems = refs[:n_scr], refs[n_scr:]
        ids = [pl.program_id(a) for a in range(len(grid))]
        first = functools.reduce(jnp.logical_and, [i == 0 for i in ids])
        last = functools.reduce(jnp.logical_and, [i == g - 1 for i, g in zip(ids, grid)])

        @pl.when(first)
        def _():
            rider.start(rin, rout, sems)

        body(*ins, *outs, *scr)

        @pl.when(last)
        def _():
            rider.finish(rin, rout, sems)

    results = pl.pallas_call(
        hosted, name=name, grid=grid, in_specs=list(in_specs) + [ANY_SPEC] * r_in,
        out_specs=tuple(out_specs) + (ANY_SPEC,) * r_out, out_shape=tuple(out_shape) + tuple(rider.out_shapes),
        scratch_shapes=list(scratch_shapes) + [pltpu.SemaphoreType.DMA((k,)) for k in rider.sem_counts],
        input_output_aliases={n_in + k: n_out + k for k in range(r_in)} if rider.in_place else {},
        compiler_params=pltpu.CompilerParams(vmem_limit_bytes=VMEM_LIMIT_BYTES, has_side_effects=True),
    )(*args, *rider.ins)
    return results[:n_out], results[n_out:]


def attn_fwd(name, q, kn, kr, v, cid, H, rider=None):
    S = q.shape[0]
    t = _tile(S, ATT_T_FWD)
    n = S // t
    scale = (MLA_NOPE + MLA_ROPE) ** -0.5
    cid_col, cid_blk = cid.reshape(S, 1), cid.reshape(n, 1, t)

    def body(q_ref, kn_ref, kr_ref, v_ref, cc_ref, cr_ref, o_ref, lse_ref, k_scr, vt_scr):
        i = pl.program_id(1)

        @pl.when(i == 0)
        def _():
            k_scr[:, 0:LANE] = kn_ref[...]
            k_scr[:, LANE:HEAD_PAD] = kr_ref[...]
            for jj in range(n):
                vt_scr[jj] = _transpose(v_ref[jj * t:(jj + 1) * t, :])

        qv = q_ref[...]
        cq = cr_ref[i]

        def update(j, carry, masked):
            m, l, acc = carry
            k0 = pl.multiple_of(j * t, t)
            s = _dot(k_scr[pl.ds(k0, t), :], qv, NT) * scale
            if masked:
                s = jnp.where(_att_mask_t(cc_ref[pl.ds(k0, t), :], cq, k0, i * t, t), s, NEG_INF)
            m_new = jnp.maximum(m, jnp.max(s, 0, keepdims=True))
            p = jnp.exp(s - m_new)
            alpha = jnp.exp(m - m_new)
            l = alpha * l + jnp.sum(p, 0, keepdims=True)
            acc = alpha * acc + _dot(vt_scr[j], p, NN)
            return m_new, l, acc

        init = (jnp.full((1, t), NEG_INF, F32), jnp.zeros((1, t), F32), jnp.zeros((MLA_V, t), F32))
        carry = lax.fori_loop(0, i, lambda j, c: update(j, c, False), init)
        m, l, acc = update(i, carry, True)
        o_ref[...] = (acc / l).T.astype(o_ref.dtype)
        lse_ref[...] = m + jnp.log(l)

    return hosted_call(
        body, name=name, grid=(H, n),
        in_specs=[pl.BlockSpec((t, HEAD_PAD), lambda h, i: (i, h)),
                  pl.BlockSpec((S, MLA_NOPE), lambda h, i: (0, h)),
                  pl.BlockSpec((S, LANE), lambda h, i: (0, 0)),
                  pl.BlockSpec((S, MLA_V), lambda h, i: (0, h)),
                  pl.BlockSpec((S, 1), lambda h, i: (0, 0)),
                  pl.BlockSpec(cid_blk.shape, lambda h, i: (0, 0, 0))],
        out_specs=(pl.BlockSpec((t, MLA_V), lambda h, i: (i, h)),
                   pl.BlockSpec((None, None, 1, t), lambda h, i: (h, i, 0, 0))),
        scratch_shapes=[pltpu.VMEM((S, HEAD_PAD), MXU_DTYPE), pltpu.VMEM((n, MLA_V, t), MXU_DTYPE)],
        out_shape=(jax.ShapeDtypeStruct((S, H * MLA_V), MXU_DTYPE), jax.ShapeDtypeStruct((H, n, 1, t), F32)),
        args=(q, kn, kr, v, cid_col, cid_blk), rider=rider)


def attn_bwd(name, q, kn, kr, v, o, lse, do, cid, H, rider=None):
    S = q.shape[0]
    t = _tile(S, ATT_T_BWD)
    n = S // t
    scale = (MLA_NOPE + MLA_ROPE) ** -0.5
    cid_col, cid_blk = cid.reshape(S, 1), cid.reshape(n, 1, t)
    lse = lse.reshape(H, n, 1, t)

    def body(q_ref, kn_ref, kr_ref, v_ref, o_ref, lse_ref, do_ref, cc_ref, cr_ref,
             dq_ref, dkn_ref, dv_ref, dkr_ref, delta_scr, dk_scr, dv_scr):
        j = pl.program_id(1)

        @pl.when(j == 0)
        def _():
            dq_ref[...] = jnp.zeros(dq_ref.shape, F32)
            for ii in range(n):
                sl = slice(ii * t, (ii + 1) * t)
                prod = do_ref[sl, :].astype(F32) * o_ref[sl, :].astype(F32)
                delta_scr[ii] = jnp.sum(prod.T, 0, keepdims=True)

        kv = jnp.concatenate([kn_ref[...], kr_ref[...]], axis=-1)
        kt = _transpose(kv)
        vv = v_ref[...]
        ck = cc_ref[...]
        k0 = j * t
        dk_scr[...] = jnp.zeros(dk_scr.shape, F32)
        dv_scr[...] = jnp.zeros(dv_scr.shape, F32)

        def update(i, masked):
            q0 = pl.multiple_of(i * t, t)
            qv = q_ref[pl.ds(q0, t), :]
            dov = do_ref[pl.ds(q0, t), :]
            s = _dot(kv, qv, NT) * scale
            if masked:
                s = jnp.where(_att_mask_t(ck, cr_ref[i], k0, q0, t), s, NEG_INF)
            p = jnp.exp(s - lse_ref[i])
            dv_scr[...] += _dot(p, dov, NN)
            dp = _dot(vv, dov, NT)
            ds = p * (dp - delta_scr[i]) * scale
            dk_scr[...] += _dot(ds, qv, NN)
            dq_ref[i] += _dot(kt, ds, NN)

        def step(i, carry):
            update(i, False)
            return carry

        update(j, True)
        lax.fori_loop(j + 1, n, step, 0)
        dkn_ref[...] = dk_scr[:, 0:LANE].astype(dkn_ref.dtype)
        dkr_ref[...] = dk_scr[:, LANE:HEAD_PAD]
        dv_ref[...] = dv_scr[...].astype(dv_ref.dtype)

    head_rows = lambda w: pl.BlockSpec((S, w), lambda h, j: (0, h))
    tile_rows = lambda w: pl.BlockSpec((t, w), lambda h, j: (j, h))
    return hosted_call(
        body, name=name, grid=(H, n),
        in_specs=[head_rows(HEAD_PAD), tile_rows(MLA_NOPE), pl.BlockSpec((t, LANE), lambda h, j: (j, 0)),
                  tile_rows(MLA_V), head_rows(MLA_V),
                  pl.BlockSpec((None, n, 1, t), lambda h, j: (h, 0, 0, 0)), head_rows(MLA_V),
                  pl.BlockSpec((t, 1), lambda h, j: (j, 0)),
                  pl.BlockSpec(cid_blk.shape, lambda h, j: (0, 0, 0))],
        out_specs=(pl.BlockSpec((None, n, HEAD_PAD, t), lambda h, j: (h, 0, 0, 0)),
                   tile_rows(MLA_NOPE), tile_rows(MLA_V), tile_rows(LANE)),
        scratch_shapes=[pltpu.VMEM((n, 1, t), F32), pltpu.VMEM((t, HEAD_PAD), F32), pltpu.VMEM((t, MLA_V), F32)],
        out_shape=(jax.ShapeDtypeStruct((H, n, HEAD_PAD, t), F32), jax.ShapeDtypeStruct((S, H * MLA_NOPE), MXU_DTYPE),
                   jax.ShapeDtypeStruct((S, H * MLA_V), MXU_DTYPE), jax.ShapeDtypeStruct((S, H * LANE), F32)),
        args=(q, kn, kr, v, o, lse, do, cid_col, cid_blk), rider=rider)


def rope_q_bwd(name, dq_t, tab_q_t, H):
    _, n, _, t = dq_t.shape

    def body(d_ref, c_ref, sa_ref, sb_ref, o_ref):
        d = d_ref[...]
        out = (d * c_ref[...] + pltpu.roll(d * sa_ref[...], HALF_ROPE, 0)
               + pltpu.roll(d * sb_ref[...], HEAD_PAD - HALF_ROPE, 0))
        o_ref[...] = out.astype(o_ref.dtype)

    tab = pl.BlockSpec((HEAD_PAD, t), lambda i, h: (0, i))
    return pl.pallas_call(
        body, name=name, grid=(n, H), out_shape=jax.ShapeDtypeStruct((H * HEAD_PAD, n * t), MXU_DTYPE),
        in_specs=[pl.BlockSpec((None, None, HEAD_PAD, t), lambda i, h: (h, i, 0, 0)), tab, tab, tab],
        out_specs=pl.BlockSpec((HEAD_PAD, t), lambda i, h: (h, i)),
        compiler_params=_params())(dq_t, *tab_q_t)


GLA_GROUP = 8


def _prefix_rows(x):
    n = x.shape[0]
    row = lax.broadcasted_iota(jnp.int32, x.shape, 0)
    d = 1
    while d < n:
        x = x + jnp.where(row >= d, pltpu.roll(x, d, 0), 0.0)
        d *= 2
    return x


def _suffix_rows(x):
    n = x.shape[0]
    row = lax.broadcasted_iota(jnp.int32, x.shape, 0)
    d = 1
    while d < n:
        x = x + jnp.where(row < n - d, pltpu.roll(x, n - d, 0), 0.0)
        d *= 2
    return x


def _log_sigmoid(z):
    return jnp.minimum(z, 0.0) - jnp.log(1.0 + jnp.exp(-jnp.abs(z)))


def gla_pre_fwd(name, h, w2p, b_a, QK, a_blk):
    S = h.shape[0]
    dk = QK // GLA_HEADS

    def body(i, ins, outs):
        q_ref, a_ref, w_ref, b_ref = ins
        outs[0][...] = (q_ref[...] * (dk ** -0.5)).astype(MXU_DTYPE)
        z = _dot(a_ref[...], w_ref[...], NN) + b_ref[...]
        outs[1][...] = _log_sigmoid(z) / GLA_TAU

    return rowcall(name, body, S, _row_tile(S), [_row(h, QK, 0), _row(h, LANE, a_blk)], [w2p, b_a],
                   [(QK, MXU_DTYPE), (QK, F32)])


def _gla_specs(S, QK, VD, rows, gmap):
    dk, dv = QK // GLA_HEADS, VD // GLA_HEADS
    return dict(
        qs=pl.BlockSpec((rows, dk), lambda h, g: (gmap(g), h)),
        k=pl.BlockSpec((rows, dk), lambda h, g: (gmap(g), QK // dk + h)),
        v=pl.BlockSpec((rows, dv), lambda h, g: (gmap(g), 2 * QK // dv + h)),
        la=pl.BlockSpec((rows, dk), lambda h, g: (gmap(g), h)),
        o=pl.BlockSpec((rows, dv), lambda h, g: (gmap(g), h)))


def gla_fwd(name, qs, h, la, QK, VD):
    S = qs.shape[0]
    dk, dv = QK // GLA_HEADS, VD // GLA_HEADS
    n_chunks = S // CHUNK
    cg = min(GLA_GROUP, n_chunks)
    rows = cg * CHUNK
    sp = _gla_specs(S, QK, VD, rows, lambda g: g)

    def body(q_ref, k_ref, v_ref, la_ref, o_ref, st_ref, state):
        @pl.when(pl.program_id(1) == 0)
        def _():
            state[...] = jnp.zeros(state.shape, F32)

        for c in range(cg):
            sl = slice(c * CHUNK, (c + 1) * CHUNK)
            cum = _prefix_rows(la_ref[sl, :])
            tot = cum[CHUNK - 1:CHUNK, :]
            kdec = k_ref[sl, :] * jnp.exp(tot - cum)
            st = state[...] * jnp.exp(tot) + _dot(v_ref[sl, :], kdec, TN)
            state[...] = st
            st_ref[c] = st
            o_ref[sl, :] = _dot(q_ref[sl, :], st, NT)

    return pl.pallas_call(
        body, name=name, grid=(GLA_HEADS, n_chunks // cg),
        out_shape=(jax.ShapeDtypeStruct((S, VD), F32), jax.ShapeDtypeStruct((GLA_HEADS, n_chunks, dv, dk), F32)),
        in_specs=[sp['qs'], sp['k'], sp['v'], sp['la']],
        out_specs=(sp['o'], pl.BlockSpec((None, cg, dv, dk), lambda h, g: (h, g, 0, 0))),
        scratch_shapes=[pltpu.VMEM((dv, dk), F32)],
        compiler_params=_params())(qs, h, h, la)


def gla_bwd(name, qs, h, la, states, do, QK, VD):
    S = qs.shape[0]
    dk, dv = QK // GLA_HEADS, VD // GLA_HEADS
    n_chunks = S // CHUNK
    cg = min(GLA_GROUP, n_chunks)
    ng = n_chunks // cg
    rows = cg * CHUNK
    rev = lambda g: ng - 1 - g
    sp = _gla_specs(S, QK, VD, rows, rev)

    def body(q_ref, k_ref, v_ref, la_ref, st_ref, prev_ref, do_ref, dq_ref, dk_ref, dv_ref, dla_ref, dst):
        g = pl.program_id(1)

        @pl.when(g == 0)
        def _():
            dst[...] = jnp.zeros(dst.shape, F32)

        first_group = (g == ng - 1).astype(F32)
        for c in reversed(range(cg)):
            sl = slice(c * CHUNK, (c + 1) * CHUNK)
            cum = _prefix_rows(la_ref[sl, :])
            tot = cum[CHUNK - 1:CHUNK, :]
            e = jnp.exp(tot - cum)
            kdec = k_ref[sl, :] * e
            decay = jnp.exp(tot)
            st = st_ref[c]
            st_prev = st_ref[c - 1] if c > 0 else prev_ref[0] * (1.0 - first_group)
            dov = do_ref[sl, :]
            qv = q_ref[sl, :]
            dq_ref[sl, :] = (_dot(dov, st, NN) * (dk ** -0.5)).astype(dq_ref.dtype)
            d = dst[...] + _dot(dov, qv, TN)
            ddecay = _colsum(d * st_prev)
            dkdec = _dot(v_ref[sl, :], d, NN)
            dv_ref[sl, :] = _dot(kdec, d, NT).astype(dv_ref.dtype)
            dk_ref[sl, :] = (dkdec * e).astype(dk_ref.dtype)
            darg = dkdec * kdec
            dtot = _colsum(darg) + ddecay * decay
            dla_ref[sl, :] = dtot - _suffix_rows(darg)
            dst[...] = d * decay

    prev_spec = pl.BlockSpec((None, 1, dv, dk), lambda h, g: (h, jnp.maximum(rev(g) * cg - 1, 0), 0, 0))
    return pl.pallas_call(
        body, name=name, grid=(GLA_HEADS, ng),
        out_shape=(jax.ShapeDtypeStruct((S, QK), MXU_DTYPE), jax.ShapeDtypeStruct((S, QK), MXU_DTYPE),
                   jax.ShapeDtypeStruct((S, VD), MXU_DTYPE), jax.ShapeDtypeStruct((S, QK), F32)),
        in_specs=[sp['qs'], sp['k'], sp['v'], sp['la'],
                  pl.BlockSpec((None, cg, dv, dk), lambda h, g: (h, rev(g), 0, 0)), prev_spec, sp['o']],
        out_specs=(sp['qs'], sp['qs'], sp['o'], sp['la']),
        scratch_shapes=[pltpu.VMEM((dv, dk), F32)],
        compiler_params=_params())(qs, h, h, la, states, states, do)


def _head_norm(o):
    mu = jnp.mean(o, -1, keepdims=True)
    oc = o - mu
    r = lax.rsqrt(jnp.mean(oc * oc, -1, keepdims=True) + EPS)
    return oc * r, r


def gla_post_fwd(name, o, h, o_norm, VD, r_blk):
    S = o.shape[0]
    dv = VD // GLA_HEADS

    def body(i, ins, outs):
        o_ref, r_ref, w_ref = ins
        for hh in range(GLA_HEADS):
            sl = slice(hh * dv, (hh + 1) * dv)
            xh, _ = _head_norm(o_ref[:, sl])
            r = r_ref[:, sl]
            outs[0][:, sl] = (xh * w_ref[:, sl] * (r * _sigmoid(r))).astype(MXU_DTYPE)

    return rowcall(name, body, S, _row_tile(S), [_row(o), _row(h, VD, r_blk)], [o_norm], [(VD, MXU_DTYPE)])[0]


def gla_post_bwd(name, o, h, dog, o_norm, VD, r_blk):
    S = o.shape[0]
    dv = VD // GLA_HEADS

    def body(i, ins, outs):
        o_ref, r_ref, dog_ref, w_ref = ins
        do_ref, dr_ref, dw_ref = outs
        _init_acc(i, (dw_ref,))
        for hh in range(GLA_HEADS):
            sl = slice(hh * dv, (hh + 1) * dv)
            xh, rs = _head_norm(o_ref[:, sl])
            r = r_ref[:, sl]
            w = w_ref[:, sl]
            dog = dog_ref[:, sl]
            sg = _sigmoid(r)
            dn = dog * (r * sg)
            dr_ref[:, sl] = (dog * (xh * w) * (sg * (1.0 + r * (1.0 - sg)))).astype(MXU_DTYPE)
            dw_ref[:, sl] += _colsum(dn * xh)
            dxh = dn * w
            do_ref[:, sl] = rs * (dxh - jnp.mean(dxh, -1, keepdims=True) - xh * jnp.mean(dxh * xh, -1, keepdims=True))

    return rowcall(name, body, S, _row_tile(S), [_row(o), _row(h, VD, r_blk), _row(dog)], [o_norm],
                   [(VD, F32), (VD, MXU_DTYPE)], [(1, VD)])


def gla_dh(name, dq, dk, dv, dr, dla, h, w2p, b_a, QK, VD, a_blk, HW):
    S = dq.shape[0]

    def body(i, ins, outs):
        dq_ref, dk_ref, dv_ref, dr_ref, dla_ref, a_ref, w_ref, b_ref = ins
        dh_ref, dw_ref, db_ref = outs
        _init_acc(i, (dw_ref, db_ref))
        a = a_ref[...]
        z = _dot(a, w_ref[...], NN) + b_ref[...]
        dz = dla_ref[...] * (1.0 / GLA_TAU) * _sigmoid(-z)
        dw_ref[...] += _dot(a, dz, TN)
        db_ref[...] += _colsum(dz)
        dh_ref[:, 0:QK] = dq_ref[...]
        dh_ref[:, QK:2 * QK] = dk_ref[...]
        dh_ref[:, 2 * QK:2 * QK + VD] = dv_ref[...]
        dh_ref[:, 2 * QK + VD:2 * QK + 2 * VD] = dr_ref[...]
        dh_ref[:, 2 * QK + 2 * VD:HW] = _dot(dz, w_ref[...], NT).astype(MXU_DTYPE)

    rows = [_row(dq), _row(dk), _row(dv), _row(dr), _row(dla), _row(h, LANE, a_blk)]
    return rowcall(name, body, S, _row_tile(S), rows, [w2p, b_a], [(HW, MXU_DTYPE)], [(LANE, QK), (1, QK)])


CONV_ROWS = 512
HALO = 8


def _gelu(x):
    return 0.5 * x * (1.0 + jnp.tanh(GELU_C * (x + GELU_A * x * x * x)))


def _gelu_grad(x):
    t = jnp.tanh(GELU_C * (x + GELU_A * x * x * x))
    return 0.5 * (1.0 + t) + 0.5 * x * (1.0 - t * t) * GELU_C * (1.0 + 3.0 * GELU_A * x * x)


def _shift_down(x, prev, d):
    row = lax.broadcasted_iota(jnp.int32, x.shape, 0)
    out = pltpu.roll(x, d, 0)
    for t in range(d):
        out = jnp.where(row == t, prev[HALO - d + t:HALO - d + t + 1, :], out)
    return out


def _shift_up(x, nxt, d):
    n = x.shape[0]
    row = lax.broadcasted_iota(jnp.int32, x.shape, 0)
    out = pltpu.roll(x, n - d, 0)
    for t in range(d):
        out = jnp.where(row == n - d + t, nxt[t:t + 1, :], out)
    return out


def _conv_taps(ref, r, rc):
    x = ref[r * rc:(r + 1) * rc, :]
    prev = ref[r * rc - HALO:r * rc, :] if r > 0 else jnp.zeros((HALO, x.shape[1]), F32)
    return x, _shift_down(x, prev, 1), _shift_down(x, prev, 2)


def _conv_apply(taps, w_ref, b_ref):
    x0, x1, x2 = taps
    return x2 * w_ref[0:1, :] + x1 * w_ref[1:2, :] + x0 * w_ref[2:3, :] + b_ref[...]


def conv_fwd(name, hu, hg, cw_u, cw_g, cb_u, cb_g):
    S, F = hu.shape
    tc = LANE
    rc = _tile(S, CONV_ROWS, unit=16)

    def body(u_ref, g_ref, wu_ref, wg_ref, bu_ref, bg_ref, a_ref):
        for r in range(S // rc):
            uc = _conv_apply(_conv_taps(u_ref, r, rc), wu_ref, bu_ref)
            gc = _conv_apply(_conv_taps(g_ref, r, rc), wg_ref, bg_ref)
            a_ref[r * rc:(r + 1) * rc, :] = (uc * _gelu(gc)).astype(a_ref.dtype)

    col = lambda rows: pl.BlockSpec((rows, tc), lambda j: (0, j))
    return pl.pallas_call(
        body, name=name, grid=(F // tc,), out_shape=jax.ShapeDtypeStruct((S, F), MXU_DTYPE),
        in_specs=[col(S), col(S), col(3), col(3), col(1), col(1)], out_specs=col(S),
        compiler_params=_params())(hu, hg, cw_u, cw_g, cb_u, cb_g)


def conv_bwd(name, hu, hg, da, cw_u, cw_g, cb_u, cb_g):
    S, F = hu.shape
    tc = LANE
    rc = _tile(S, CONV_ROWS, unit=16)
    nr = S // rc

    def body(u_ref, g_ref, da_ref, wu_ref, wg_ref, bu_ref, bg_ref,
             dhu_ref, dhg_ref, dwu_ref, dwg_ref, dbu_ref, dbg_ref, du_scr, dg_scr):
        dw = [[jnp.zeros((1, tc), F32) for _ in range(3)] for _ in range(2)]
        db = [jnp.zeros((1, tc), F32) for _ in range(2)]
        for r in range(nr):
            sl = slice(r * rc, (r + 1) * rc)
            ut = _conv_taps(u_ref, r, rc)
            gt = _conv_taps(g_ref, r, rc)
            uc = _conv_apply(ut, wu_ref, bu_ref)
            gc = _conv_apply(gt, wg_ref, bg_ref)
            dav = da_ref[sl, :]
            duc = dav * _gelu(gc)
            dgc = dav * uc * _gelu_grad(gc)
            du_scr[sl, :] = duc
            dg_scr[sl, :] = dgc
            for part, (taps, d) in enumerate(((ut, duc), (gt, dgc))):
                db[part] = db[part] + _colsum(d)
                for tap in range(3):
                    dw[part][tap] = dw[part][tap] + _colsum(taps[2 - tap] * d)
        for part, (w_out, b_out) in enumerate(((dwu_ref, dbu_ref), (dwg_ref, dbg_ref))):
            b_out[...] = db[part]
            for tap in range(3):
                w_out[tap:tap + 1, :] = dw[part][tap]
        for scr, w_ref, out in ((du_scr, wu_ref, dhu_ref), (dg_scr, wg_ref, dhg_ref)):
            for r in range(nr):
                d = scr[r * rc:(r + 1) * rc, :]
                nxt = scr[(r + 1) * rc:(r + 1) * rc + HALO, :] if r + 1 < nr else jnp.zeros((HALO, tc), F32)
                dh = d * w_ref[2:3, :] + _shift_up(d, nxt, 1) * w_ref[1:2, :] + _shift_up(d, nxt, 2) * w_ref[0:1, :]
                out[r * rc:(r + 1) * rc, :] = dh.astype(out.dtype)

    col = lambda rows: pl.BlockSpec((rows, tc), lambda j: (0, j))
    sds = jax.ShapeDtypeStruct
    return pl.pallas_call(
        body, name=name, grid=(F // tc,),
        out_shape=(sds((S, F), MXU_DTYPE), sds((S, F), MXU_DTYPE), sds((3, F), F32), sds((3, F), F32),
                   sds((1, F), F32), sds((1, F), F32)),
        in_specs=[col(S), col(S), col(S), col(3), col(3), col(1), col(1)],
        out_specs=(col(S), col(S), col(3), col(3), col(1), col(1)),
        scratch_shapes=[pltpu.VMEM((S, tc), F32), pltpu.VMEM((S, tc), F32)],
        compiler_params=_params())(hu, hg, da, cw_u, cw_g, cb_u, cb_g)


def adamw(name, w, g, m, v):
    R, C = w.shape
    tr = _tile(R, max(8, (1 << 19) // max(C, 1) // 8 * 8), unit=8)

    def body(w_ref, g_ref, m_ref, v_ref, d_ref, nm_ref, nv_ref):
        gv = g_ref[...]
        mn = ADAM_B1 * m_ref[...] + (1.0 - ADAM_B1) * gv
        vn = ADAM_B2 * v_ref[...] + (1.0 - ADAM_B2) * (gv * gv)
        m_hat = mn / (1.0 - ADAM_B1 ** ADAM_STEP)
        v_hat = vn / (1.0 - ADAM_B2 ** ADAM_STEP)
        d_ref[...] = -ADAM_LR * (m_hat / (jnp.sqrt(v_hat) + ADAM_EPS) + ADAM_WD * w_ref[...])
        nm_ref[...] = mn
        nv_ref[...] = vn

    spec = pl.BlockSpec((tr, C), lambda i: (i, 0))
    shp = jax.ShapeDtypeStruct((R, C), F32)
    return pl.pallas_call(body, name=name, grid=(R // tr,), out_shape=(shp, shp, shp), in_specs=[spec] * 4,
                          out_specs=(spec, spec, spec), compiler_params=_params())(w, g, m, v)


def _position():
    return lax.axis_index('x'), lax.axis_index('y'), lax.axis_index('c')


def _other_chips(x, y):
    return ((1 - x, y), (x, 1 - y), (1 - x, 1 - y))


def _window(ref, dim, lo, n_lead, jj, rs, cs):
    if dim == 1:
        return ref.at[pl.ds(lo, n_lead), pl.ds(pl.multiple_of(jj * rs, 16), rs), :]
    return ref.at[pl.ds(lo, n_lead), :, pl.ds(pl.multiple_of(jj * cs, LANE), cs)]


def _hbm_call(name, body, out_shapes, n_sems, args, aliases=None):
    return pl.pallas_call(
        body, name=name, out_shape=tuple(out_shapes), in_specs=[ANY_SPEC] * len(args),
        out_specs=tuple(ANY_SPEC for _ in out_shapes),
        scratch_shapes=[pltpu.SemaphoreType.DMA((n,)) for n in n_sems],
        input_output_aliases=aliases or {},
        compiler_params=pltpu.CompilerParams(has_side_effects=True))(*args)


def place_shard(name, shard, dim, j_arr):
    L, rs, cs = shard.shape
    full_shape = (L, rs * N_CHIPS, cs) if dim == 1 else (L, rs, cs * N_CHIPS)
    tr = _tile(rs, max(16, (1 << 19) // cs // 16 * 16), unit=16)

    def body(j_ref, s_ref, o_ref):
        o_ref[...] = s_ref[...].astype(o_ref.dtype)

    if dim == 1:
        out_spec = pl.BlockSpec((None, tr, cs), lambda l, i, j: (l, j[0] * (rs // tr) + i, 0))
    else:
        out_spec = pl.BlockSpec((None, tr, cs), lambda l, i, j: (l, i, j[0]))
    grid_spec = pltpu.PrefetchScalarGridSpec(
        num_scalar_prefetch=1, grid=(L, rs // tr),
        in_specs=[pl.BlockSpec((None, tr, cs), lambda l, i, j: (l, i, 0))], out_specs=out_spec)
    return pl.pallas_call(body, name=name, grid_spec=grid_spec,
                          out_shape=jax.ShapeDtypeStruct(full_shape, MXU_DTYPE),
                          compiler_params=_params())(j_arr, shard)


class Exchange(NamedTuple):
    ins: tuple
    out_shapes: tuple
    in_place: bool
    sem_counts: tuple
    start: Callable
    finish: Callable


def run_exchange(name, ex):
    n_in, n_out = len(ex.ins), len(ex.out_shapes)

    def body(*refs):
        ins, outs, sems = refs[:n_in], refs[n_in:n_in + n_out], refs[n_in + n_out:]
        ex.start(ins, outs, sems)
        ex.finish(ins, outs, sems)

    return _hbm_call(name, body, ex.out_shapes, ex.sem_counts, ex.ins,
                     {k: k for k in range(n_in)} if ex.in_place else None)


def _rcopy(src, dst, ssem, rsem, device):
    return pltpu.make_async_remote_copy(src_ref=src, dst_ref=dst, send_sem=ssem, recv_sem=rsem,
                                        device_id=device, device_id_type=MESH)


def gather_exchange(fulls, dims, shard_shapes, items):
    n = len(items)

    def win(full, w, lo, nl, jj):
        return _window(full[w], dims[w], lo, nl, jj, shard_shapes[w][1], shard_shapes[w][2])

    def start(_, full, sems):
        ssem, rsem = sems[0], sems[1]
        x, y, c = _position()
        for it, (w, lo, nl, owner) in enumerate(items):
            @pl.when(c == owner)
            def _():
                own = win(full, w, lo, nl, 2 * x + y)
                for k, (cx, cy) in enumerate(_other_chips(x, y)):
                    _rcopy(own, own, ssem.at[3 * it + k], rsem.at[3 * it + k], (cx, cy, c)).start()

    def finish(_, full, sems):
        ssem, rsem, s2sem, r2sem = sems
        x, y, c = _position()
        sibling = (x, y, 1 - c)
        for it, (w, lo, nl, owner) in enumerate(items):
            @pl.when(c == owner)
            def _():
                for k, (cx, cy) in enumerate(_other_chips(x, y)):
                    theirs = win(full, w, lo, nl, 2 * cx + cy)
                    _rcopy(theirs, theirs, ssem.at[3 * it + k], rsem.at[3 * it + k], (cx, cy, c)).wait_recv()
                    _rcopy(theirs, theirs, s2sem.at[3 * it + k], r2sem.at[3 * it + k], sibling).start()
        for it, (w, lo, nl, owner) in enumerate(items):
            @pl.when(c == owner)
            def _():
                own = win(full, w, lo, nl, 2 * x + y)
                for k, (cx, cy) in enumerate(_other_chips(x, y)):
                    theirs = win(full, w, lo, nl, 2 * cx + cy)
                    _rcopy(own, own, ssem.at[3 * it + k], rsem.at[3 * it + k], (cx, cy, c)).wait_send()
                    _rcopy(theirs, theirs, s2sem.at[3 * it + k], r2sem.at[3 * it + k], sibling).wait_send()

            @pl.when(c != owner)
            def _():
                for k, (cx, cy) in enumerate(_other_chips(x, y)):
                    theirs = win(full, w, lo, nl, 2 * cx + cy)
                    _rcopy(theirs, theirs, s2sem.at[3 * it + k], r2sem.at[3 * it + k], sibling).wait_recv()

    shapes = tuple(jax.ShapeDtypeStruct(f.shape, f.dtype) for f in fulls)
    return Exchange(tuple(fulls), shapes, True, (3 * n,) * 4, start, finish)


def pair_exchange(grads, owners):
    n = len(grads)

    def start(gr, land, sems):
        x, y, c = _position()
        for it, owner in enumerate(owners):
            @pl.when(c != owner)
            def _():
                _rcopy(gr[it], land[it], sems[0].at[it], sems[1].at[it], (x, y, 1 - c)).start()

    def finish(gr, land, sems):
        x, y, c = _position()
        for it, owner in enumerate(owners):
            cp = _rcopy(gr[it], land[it], sems[0].at[it], sems[1].at[it], (x, y, 1 - c))
            pl.when(c != owner)(cp.wait_send)
            pl.when(c == owner)(cp.wait_recv)

    shapes = tuple(jax.ShapeDtypeStruct(g.shape, g.dtype) for g in grads)
    return Exchange(tuple(grads), shapes, False, (n, n), start, finish)


def pair_add(name, g, landed, owner, c_arr):
    R, C = g.shape
    tr = _tile(R, max(16, (1 << 20) // C // 16 * 16), unit=16)

    def body(c_ref, g_ref, l_ref, o_ref):
        @pl.when(c_ref[0] == owner)
        def _():
            o_ref[...] = (g_ref[...].astype(F32) + l_ref[...].astype(F32)).astype(o_ref.dtype)

    spec = pl.BlockSpec((tr, C), lambda i, c: (jnp.where(c[0] == owner, i, 0), 0))
    grid_spec = pltpu.PrefetchScalarGridSpec(num_scalar_prefetch=1, grid=(R // tr,), in_specs=[spec, spec],
                                             out_specs=spec)
    return pl.pallas_call(body, name=name, grid_spec=grid_spec, out_shape=jax.ShapeDtypeStruct(g.shape, g.dtype),
                          compiler_params=_params())(c_arr, g, landed)


def _window2(ref, dim, jj, rs, cs):
    if dim == 1:
        return ref.at[pl.ds(pl.multiple_of(jj * rs, 16), rs), :]
    return ref.at[:, pl.ds(pl.multiple_of(jj * cs, LANE), cs)]


def chip_exchange(partials, dims, shard_shapes, owners):
    n = len(partials)

    def copies(ps, land, sems, it):
        x, y, c = _position()
        rs, cs = shard_shapes[it]
        return [_rcopy(_window2(ps[it], dims[it], 2 * cx + cy, rs, cs), land[it].at[k],
                       sems[0].at[3 * it + k], sems[1].at[3 * it + k], (cx, cy, c))
                for k, (cx, cy) in enumerate(_other_chips(x, y))]

    def start(ps, land, sems):
        c = lax.axis_index('c')
        for it, owner in enumerate(owners):
            @pl.when(c == owner)
            def _():
                for cp in copies(ps, land, sems, it):
                    cp.start()

    def finish(ps, land, sems):
        c = lax.axis_index('c')
        for it, owner in enumerate(owners):
            @pl.when(c == owner)
            def _():
                for cp in copies(ps, land, sems, it):
                    cp.wait()

    shapes = tuple(jax.ShapeDtypeStruct((3,) + tuple(s), p.dtype) for p, s in zip(partials, shard_shapes))
    return Exchange(tuple(partials), shapes, False, (3 * n, 3 * n), start, finish)


def chip_sum(name, partial, landed, dim, buf, depth, layer, owner, j_arr, c_arr):
    _, rs, cs = landed.shape
    tr = _tile(rs, max(16, (1 << 19) // cs // 16 * 16), unit=16)
    has_buf = buf is not None

    def body(*refs):
        core_ref, p_ref, a_ref, b_ref, c_ref = refs[1:6]
        o_ref = refs[6 + has_buf]

        @pl.when(core_ref[0] == owner)
        def _():
            o_ref[...] = ((p_ref[...].astype(F32) + a_ref[...].astype(F32)) + b_ref[...].astype(F32)) + c_ref[...].astype(F32)

    def on(c, index):
        return jnp.where(c[0] == owner, index, 0)

    if dim == 1:
        own = pl.BlockSpec((tr, cs), lambda i, j, c: (on(c, j[0] * (rs // tr) + i), 0))
    else:
        own = pl.BlockSpec((tr, cs), lambda i, j, c: (on(c, i), on(c, j[0])))
    arrived = [pl.BlockSpec((None, tr, cs), lambda i, j, c, k=k: (k, on(c, i), 0)) for k in range(3)]
    grid_spec = pltpu.PrefetchScalarGridSpec(
        num_scalar_prefetch=2, grid=(rs // tr,), in_specs=[own] + arrived + ([ANY_SPEC] if has_buf else []),
        out_specs=pl.BlockSpec((None, tr, cs), lambda i, j, c: (layer, on(c, i), 0)))
    args = (j_arr, c_arr, partial, landed, landed, landed) + ((buf,) if has_buf else ())
    return pl.pallas_call(body, name=name, grid_spec=grid_spec,
                          out_shape=jax.ShapeDtypeStruct((depth, rs, cs), F32),
                          input_output_aliases={6: 0} if has_buf else {},
                          compiler_params=_params())(*args)


def share_exchange(bufs, items):
    n = len(items)

    def copy(full, sems, it):
        w, lo, nl, _ = items[it]
        x, y, c = _position()
        own = full[w].at[pl.ds(lo, nl)]
        return _rcopy(own, own, sems[0].at[it], sems[1].at[it], (x, y, 1 - c))

    def start(_, full, sems):
        c = lax.axis_index('c')
        for it in range(n):
            pl.when(c == items[it][3])(copy(full, sems, it).start)

    def finish(_, full, sems):
        c = lax.axis_index('c')
        for it in range(n):
            cp = copy(full, sems, it)
            pl.when(c == items[it][3])(cp.wait_send)
            pl.when(c != items[it][3])(cp.wait_recv)

    shapes = tuple(jax.ShapeDtypeStruct(f.shape, f.dtype) for f in bufs)
    return Exchange(tuple(bufs), shapes, True, (n, n), start, finish)


def exchange_small(name, pack, reduce_all):
    R = pack.shape[0]
    n_slots = N_DEVICES if reduce_all else N_CHIPS
    n_peers = n_slots - 1

    def body(p_ref, o_ref, buf, ssem, rsem):
        x, y, c = _position()
        if reduce_all:
            me = 4 * x + 2 * y + c
            peers = [(x ^ (k >> 2 & 1), y ^ (k >> 1 & 1), c ^ (k & 1)) for k in range(1, N_DEVICES)]
        else:
            me = 2 * x + y
            peers = [(cx, cy, c) for cx, cy in _other_chips(x, y)]
        buf[me] = p_ref[...]
        copies = []
        for k, peer in enumerate(peers):
            cp = pltpu.make_async_remote_copy(src_ref=p_ref, dst_ref=buf.at[me], send_sem=ssem.at[k],
                                              recv_sem=rsem.at[k], device_id=peer, device_id_type=MESH)
            cp.start()
            copies.append(cp)
        for k, (px, py, pc) in enumerate(peers):
            slot = 4 * px + 2 * py + pc if reduce_all else 2 * px + py
            pltpu.make_async_remote_copy(src_ref=p_ref, dst_ref=buf.at[slot], send_sem=ssem.at[k],
                                         recv_sem=rsem.at[k], device_id=(px, py, pc), device_id_type=MESH).wait_recv()
        for cp in copies:
            cp.wait_send()
        if reduce_all:
            acc = buf[0]
            for d in range(1, N_DEVICES):
                acc = acc + buf[d]
            o_ref[...] = acc
        else:
            o_ref[...] = buf[...]

    vmem = pl.BlockSpec(memory_space=pltpu.VMEM)
    out_shape = jax.ShapeDtypeStruct((R, LANE) if reduce_all else (N_CHIPS, R, LANE), F32)
    return pl.pallas_call(
        body, name=name, out_shape=out_shape, in_specs=[vmem], out_specs=vmem,
        scratch_shapes=[pltpu.VMEM((n_slots, R, LANE), F32), pltpu.SemaphoreType.DMA((n_peers,)),
                        pltpu.SemaphoreType.DMA((n_peers,))],
        compiler_params=_params())(pack)


def _pack(arrays):
    flat = jnp.concatenate([a.reshape(-1).astype(F32) for a in arrays])
    n = flat.shape[0]
    rows = -(-n // LANE)
    rows = -(-rows // 8) * 8
    return jnp.pad(flat, (0, rows * LANE - n)).reshape(rows, LANE)


def _unpack(pack, shapes, lead=()):
    flat = pack.reshape(lead + (-1,))
    out, off = [], 0
    for s in shapes:
        n = math.prod(s)
        out.append(flat[..., off:off + n].reshape(lead + tuple(s)))
        off += n
    return out


def kernel(x, p, positions, mla_w_in, mla_q_norm, mla_kv_norm, mla_w_uq, mla_w_uk, mla_w_uv, mla_w_o, gla_w_in, gla_w_a2, gla_b_a, gla_o_norm, gla_w_o, ln1_g, ln1_b, ln2_g, ln2_b, ffn_w_up, ffn_conv_w, ffn_conv_b, ffn_w_down, ple_w_proj, ple_w_gate, ple_b_gate, loss_target, m_mla_w_in, m_mla_q_norm, m_mla_kv_norm, m_mla_w_uq, m_mla_w_uk, m_mla_w_uv, m_mla_w_o, m_gla_w_in, m_gla_w_a2, m_gla_b_a, m_gla_o_norm, m_gla_w_o, m_ln1_g, m_ln1_b, m_ln2_g, m_ln2_b, m_ffn_w_up, m_ffn_conv_w, m_ffn_conv_b, m_ffn_w_down, m_ple_w_proj, m_ple_w_gate, m_ple_b_gate, v_mla_w_in, v_mla_q_norm, v_mla_kv_norm, v_mla_w_uq, v_mla_w_uk, v_mla_w_uv, v_mla_w_o, v_gla_w_in, v_gla_w_a2, v_gla_b_a, v_gla_o_norm, v_gla_w_o, v_ln1_g, v_ln1_b, v_ln2_g, v_ln2_b, v_ffn_w_up, v_ffn_conv_w, v_ffn_conv_b, v_ffn_w_down, v_ple_w_proj, v_ple_w_gate, v_ple_b_gate):
    given = dict(locals())
    S, D = x.shape[1], x.shape[2]
    QL, KL = mla_q_norm.shape[1], mla_kv_norm.shape[1]
    H = mla_w_uq.shape[2] * N_CHIPS // (MLA_NOPE + MLA_ROPE)
    QK, VD = gla_b_a.shape[1] * N_CHIPS, gla_o_norm.shape[1] * N_CHIPS
    FF = ffn_w_down.shape[1] * N_CHIPS
    GIN = 2 * QK + 2 * VD + GLA_GATE_RANK
    GIN_PAD = 2 * QK + 2 * VD + LANE
    MIN = QL + KL + MLA_ROPE
    MIN_PAD = QL + KL + LANE
    a_blk = (2 * QK + 2 * VD) // LANE
    r_blk = (2 * QK + VD) // VD
    xi, yi, ci = _position()
    chip = 2 * xi + yi
    c_arr = jnp.reshape(ci, (1,)).astype(jnp.int32)
    j_arr = jnp.reshape(chip, (1,)).astype(jnp.int32)

    names = [n for n, _ in BIG]
    big_dims = [1 if n == 'gla_w_in' else d for n, d in BIG]
    shapes = [given[n].shape for n in names]

    def layer_of(name, idx):
        return 2 * idx if name.startswith('mla') else 2 * idx + 1 if name.startswith('gla') else idx

    def owner_of(layer):
        return layer % 2

    def layer_runs(wanted):
        out = []
        for w, n in enumerate(names):
            for owner in (0, 1):
                idxs = [k for k in range(shapes[w][0])
                        if owner_of(layer_of(n, k)) == owner and wanted(n, layer_of(n, k))]
                for k in idxs:
                    if out and out[-1][0] == w and out[-1][3] == owner and out[-1][1] + out[-1][2] == k:
                        out[-1] = (w, out[-1][1], out[-1][2] + 1, owner)
                    else:
                        out.append((w, k, 1, owner))
        return out

    before_attention = ('mla_w_in', 'mla_w_uq', 'mla_w_uk', 'mla_w_uv')
    gather_phase = {0: lambda n, l: l == 0 and n not in before_attention or l in (1, 2),
                    2: lambda n, l: l == 3}
    placed = [place_shard(f'place_{n}', given[n], d, j_arr) for n, d in zip(names, big_dims)]
    first = run_exchange('gather_first', gather_exchange(
        placed, big_dims, shapes, layer_runs(lambda n, l: l == 0 and n in before_attention)))
    full = dict(zip(names, first))
    small_sharded = [n for n, d in SMALL if d is not None]
    spack = exchange_small('gather_small', _pack([given[n] for n in small_sharded]), False)
    parts = _unpack(spack, [given[n].shape for n in small_sharded], lead=(N_CHIPS,))
    for n, part in zip(small_sharded, parts):
        d = dict(SMALL)[n]
        full[n] = jnp.concatenate([part[k] for k in range(N_CHIPS)], axis=d)

    def mla_in_weights(j):
        w_in = jnp.pad(full['mla_w_in'][j], ((0, 0), (0, MIN_PAD - MIN)))
        w_uq = full['mla_w_uq'][j].reshape(QL, H, MLA_NOPE + MLA_ROPE)
        w_uq = jnp.pad(w_uq, ((0, 0), (0, 0), (0, HEAD_PAD - MLA_NOPE - MLA_ROPE))).reshape(QL, H * HEAD_PAD)
        return w_in, w_uq

    def gla_in_weight(j):
        w = full['gla_w_in'][j].reshape(N_CHIPS, D, GIN // N_CHIPS).transpose(1, 0, 2).reshape(D, GIN)
        return jnp.pad(w, ((0, 0), (0, GIN_PAD - GIN)))

    w_a2 = jnp.pad(full['gla_w_a2'], ((0, 0), (0, LANE - GLA_GATE_RANK), (0, 0))).astype(MXU_DTYPE)
    cw_u, cw_g = full['ffn_conv_w'][:, :, :FF], full['ffn_conv_w'][:, :, FF:]
    cb_u, cb_g = ffn_conv_b[:, None, :FF], ffn_conv_b[:, None, FF:]

    pos_row = positions[0]
    cid = pos_row // CHUNK
    tab_q = rope_tables(pos_row, MLA_NOPE)
    tab_k = rope_tables(pos_row, 0)

    def row1(a, i):
        return a[i:i + 1]

    saved = []
    xa = x[0]
    xb = xa
    for i in range(DEPTH):
        j = i // 2
        sv = dict(x=xa, xb=xb)
        if i % 2 == 0:
            w_in, w_uq = mla_in_weights(j)
            h = mm(f'mla_in_{i}', xb, w_in, 'nn', F32)
            cq, ckv, kr = mla_pre_fwd(f'mla_pre_{i}', h, row1(mla_q_norm, j), row1(mla_kv_norm, j), tab_k, QL, KL)
            qpre = mm(f'mla_uq_{i}', cq, w_uq, 'nn', F32)
            q = rope_q(f'mla_rope_{i}', qpre, tab_q, H, MXU_DTYPE)
            kn = mm(f'mla_uk_{i}', ckv, full['mla_w_uk'], 'nn', MXU_DTYPE, b_pre=(j,))
            vv = mm(f'mla_uv_{i}', ckv, full['mla_w_uv'], 'nn', MXU_DTYPE, b_pre=(j,))
            rider = gather_exchange([full[n] for n in names], big_dims, shapes, layer_runs(gather_phase[i]))
            (o, lse), gathered = attn_fwd(f'mla_attn_{i}', q, kn, kr, vv, cid, H, rider=rider)
            full.update(zip(names, gathered))
            mix = mm(f'mla_o_{i}', o, full['mla_w_o'], 'nn', F32, b_pre=(j,))
            sv.update(h=h, cq=cq, ckv=ckv, kr=kr, q=q, kn=kn, v=vv, o=o, lse=lse, w_in=w_in, w_uq=w_uq)
        else:
            w_gin = gla_in_weight(j)
            sv.update(w_gin=w_gin)
            h = mm(f'gla_in_{i}', xb, w_gin, 'nn', F32)
            qs, la = gla_pre_fwd(f'gla_pre_{i}', h, w_a2[j], row1(full['gla_b_a'], j), QK, a_blk)
            o, states = gla_fwd(f'gla_scan_{i}', qs, h, la, QK, VD)
            og = gla_post_fwd(f'gla_post_{i}', o, h, row1(full['gla_o_norm'], j), VD, r_blk)
            mix = mm(f'gla_o_{i}', og, full['gla_w_o'], 'nn', F32, b_pre=(j,))
            sv.update(h=h, qs=qs, la=la, o=o, states=states, og=og)
        x1, x1b = ln_fwd(f'ln1_{i}', xa, mix, row1(ln1_g, i), row1(ln1_b, i))
        hu = mm(f'ffn_up_u_{i}', x1b, full['ffn_w_up'], 'nn', F32, b_pre=(i,), b_win=(0, D, 0, FF))
        hg = mm(f'ffn_up_g_{i}', x1b, full['ffn_w_up'], 'nn', F32, b_pre=(i,), b_win=(0, D, FF, FF))
        act = conv_fwd(f'ffn_conv_{i}', hu, hg, cw_u[i], cw_g[i], cb_u[i], cb_g[i])
        f = mm(f'ffn_down_{i}', act, full['ffn_w_down'], 'nn', F32, b_pre=(i,))
        x2, x2b = ln_fwd(f'ln2_{i}', x1, f, row1(ln2_g, i), row1(ln2_b, i))
        gp = mm(f'ple_gate_{i}', x2b, full['ple_w_gate'], 'nn', F32, b_pre=(i,))
        pp = mm(f'ple_proj_{i}', p, full['ple_w_proj'], 'nn', F32, a_pre=(i, 0), b_pre=(i,))
        x3, x3b = ple_fwd(f'ple_{i}', x2, gp, pp, row1(ple_b_gate, i))
        sv.update(mix=mix, x1=x1, x1b=x1b, hu=hu, hg=hg, act=act, f=f, x2=x2, x2b=x2b, gp=gp, pp=pp)
        saved.append(sv)
        xa, xb = x3, x3b

    dy, sq = loss_head('loss_head', xa, loss_target[0])
    loss_part = (0.5 / D) * jnp.sum(sq)

    small_g = {n: [None] * given[n].shape[0] for n, _ in SMALL}
    tab_q_t = tuple(t.T for t in tab_q)
    shard_buf = {n: None for n in names}

    def reduce_begin(tag, items):
        owners = [owner_of(layer) for _, _, layer, _ in items]
        landed = run_exchange(f'pair_exchange_{tag}', pair_exchange([g for *_, g in items], owners))
        partial = [pair_add(f'pair_add_{n}_{layer}', g, l, o, c_arr)
                   for (n, _, layer, g), l, o in zip(items, landed, owners)]
        dims = [big_dims[names.index(n)] for n, *_ in items]
        exchange = chip_exchange(partial, dims, [given[n].shape[1:] for n, *_ in items], owners)
        return exchange, (items, partial, dims, owners)

    def reduce_end(context, arrived):
        items, partial, dims, owners = context
        for (n, idx, layer, _), pt, ar, d, o in zip(items, partial, arrived, dims, owners):
            shard_buf[n] = chip_sum(f'chip_sum_{n}_{layer}', pt, ar, d, shard_buf[n], given[n].shape[0], idx, o,
                                    j_arr, c_arr)

    def wgrad(name, a, b, a_pre=()):
        return mm(f'd_{name}_{i}', a, b, 'tn', MXU_DTYPE, a_pre=a_pre)

    def layer_items(layer_grads):
        return [(n, i if n.startswith(('ffn', 'ple')) else i // 2, i, g) for n, g in layer_grads.items()]

    dA, ca, dB = dy, 1.0, None
    waiting = []
    for i in reversed(range(DEPTH)):
        j = i // 2
        sv = saved[i]
        lg = {}
        dx3, dpp, dgp, dbg = ple_bwd(f'ple_b_{i}', dA, ca, dB, sv['gp'], sv['pp'], row1(ple_b_gate, i))
        small_g['ple_b_gate'][i] = dbg
        lg['ple_w_proj'] = wgrad('ple_w_proj', p, dpp, a_pre=(i, 0))
        lg['ple_w_gate'] = wgrad('ple_w_gate', sv['x2b'], dgp)
        dx2 = mm(f'ple_gate_b_{i}', dgp, full['ple_w_gate'], 'nt', F32, b_pre=(i,))
        dz2, dz2b, dg2, db2 = ln_bwd(f'ln2_b_{i}', sv['x1'], sv['f'], dx3, 1.0, dx2, row1(ln2_g, i))
        small_g['ln2_g'][i], small_g['ln2_b'][i] = dg2, db2
        lg['ffn_w_down'] = wgrad('ffn_w_down', sv['act'], dz2b)
        dact = mm(f'ffn_down_b_{i}', dz2b, full['ffn_w_down'], 'nt', F32, b_pre=(i,))
        dhu, dhg, dcwu, dcwg, dcbu, dcbg = conv_bwd(f'ffn_conv_b_{i}', sv['hu'], sv['hg'], dact,
                                                   cw_u[i], cw_g[i], cb_u[i], cb_g[i])
        small_g['ffn_conv_w'][i] = jnp.concatenate([dcwu, dcwg], -1)
        small_g['ffn_conv_b'][i] = jnp.concatenate([dcbu, dcbg], -1)
        up = mm(f'd_ffn_w_up_u_{i}', sv['x1b'], dhu, 'tn', MXU_DTYPE, out_stack=(None, (1, D, 2 * FF), 0, 0))
        up = mm(f'd_ffn_w_up_g_{i}', sv['x1b'], dhg, 'tn', MXU_DTYPE, out_stack=(up, (1, D, 2 * FF), 0, FF))
        lg['ffn_w_up'] = up[0]
        dx1 = mm(f'ffn_up_bu_{i}', dhu, full['ffn_w_up'], 'nt', F32, b_pre=(i,), b_win=(0, D, 0, FF))
        dx1 = mm(f'ffn_up_bg_{i}', dhg, full['ffn_w_up'], 'nt', F32, b_pre=(i,), b_win=(0, D, FF, FF), acc_in=dx1)
        dz1, dz1b, dg1, db1 = ln_bwd(f'ln1_b_{i}', sv['x'], sv['mix'], dz2, DN_ALPHA, dx1, row1(ln1_g, i))
        small_g['ln1_g'][i], small_g['ln1_b'][i] = dg1, db1
        if i % 2 == 0:
            lg['mla_w_o'] = wgrad('mla_w_o', sv['o'], dz1b)
            do = mm(f'mla_o_b_{i}', dz1b, full['mla_w_o'], 'nt', MXU_DTYPE, b_pre=(j,))
            waiting += layer_items(lg)
            lg = {}
            riding, context = reduce_begin(f'before_{i}', waiting)
            waiting = []
            (dq, dkn, dv, dkr), arrived = attn_bwd(f'mla_attn_b_{i}', sv['q'], sv['kn'], sv['kr'], sv['v'], sv['o'],
                                                   sv['lse'], do, cid, H, rider=riding)
            reduce_end(context, arrived)
            dqpre_t = rope_q_bwd(f'mla_rope_b_{i}', dq, tab_q_t, H)
            uq_t = mm(f'd_mla_w_uq_{i}', dqpre_t, sv['cq'], 'nn', MXU_DTYPE)
            lg['mla_w_uq'] = uq_t.T.reshape(QL, H, HEAD_PAD)[..., :MLA_NOPE + MLA_ROPE].reshape(QL, -1)
            lg['mla_w_uk'] = wgrad('mla_w_uk', sv['ckv'], dkn)
            lg['mla_w_uv'] = wgrad('mla_w_uv', sv['ckv'], dv)
            dcq = mm(f'mla_uq_b_{i}', dqpre_t, sv['w_uq'].T, 'tn', F32)
            dckv_a = mm(f'mla_uk_b_{i}', dkn, full['mla_w_uk'], 'nt', F32, b_pre=(j,))
            dckv_b = mm(f'mla_uv_b_{i}', dv, full['mla_w_uv'], 'nt', F32, b_pre=(j,))
            dh, dqn, dkvn = mla_pre_bwd(f'mla_pre_b_{i}', sv['h'], dcq, dckv_a, dckv_b, dkr,
                                        row1(mla_q_norm, j), row1(mla_kv_norm, j), tab_k, QL, KL, H)
            small_g['mla_q_norm'][j], small_g['mla_kv_norm'][j] = dqn, dkvn
            lg['mla_w_in'] = wgrad('mla_w_in', sv['xb'], dh)[:, :MIN]
            dmix = mm(f'mla_in_b_{i}', dh, sv['w_in'], 'nt', F32)
        else:
            lg['gla_w_o'] = wgrad('gla_w_o', sv['og'], dz1b)
            dog = mm(f'gla_o_b_{i}', dz1b, full['gla_w_o'], 'nt', F32, b_pre=(j,))
            do, dr, don = gla_post_bwd(f'gla_post_b_{i}', sv['o'], sv['h'], dog, row1(full['gla_o_norm'], j), VD, r_blk)
            dq, dk, dv, dla = gla_bwd(f'gla_scan_b_{i}', sv['qs'], sv['h'], sv['la'], sv['states'], do, QK, VD)
            dh, dw2, dba = gla_dh(f'gla_dh_{i}', dq, dk, dv, dr, dla, sv['h'], w_a2[j], row1(full['gla_b_a'], j),
                                  QK, VD, a_blk, GIN_PAD)
            small_g['gla_o_norm'][j], small_g['gla_b_a'][j] = don, dba
            small_g['gla_w_a2'][j] = dw2[:GLA_GATE_RANK]
            g_in = wgrad('gla_w_in', sv['xb'], dh)[:, :GIN]
            lg['gla_w_in'] = g_in.reshape(D, N_CHIPS, GIN // N_CHIPS).transpose(1, 0, 2).reshape(N_CHIPS * D, -1)
            dmix = mm(f'gla_in_b_{i}', dh, sv['w_gin'], 'nt', F32)
        dA, ca, dB = dz1, DN_ALPHA, dmix
        waiting += layer_items(lg)
    grad_x = axpy('grad_x', dA, ca, dB)[None]

    exchange, context = reduce_begin('last', waiting)
    reduce_end(context, run_exchange('chip_exchange_last', exchange))
    shared = run_exchange('share_with_sibling',
                          share_exchange([shard_buf[n] for n in names], layer_runs(lambda n, l: True)))
    grads = dict(zip(names, shared))

    small_names = [n for n, _ in SMALL]
    small_full = [jnp.concatenate([g.reshape((1,) + g.shape[-(given[n].ndim - 1):]) for g in small_g[n]], 0)
                  for n in small_names]
    pack = _pack([jnp.reshape(loss_part, (1,))] + [jnp.zeros((LANE - 1,), F32)] + small_full)
    red = exchange_small('reduce_small', pack, True)
    red_parts = _unpack(red, [(LANE,)] + [g.shape for g in small_full])
    loss = red_parts[0][0]
    for (n, d), g in zip(SMALL, red_parts[1:]):
        if d is not None:
            width = given[n].shape[d]
            g = lax.dynamic_slice_in_dim(g, chip * width, width, axis=d)
        grads[n] = g

    delta, new_m, new_v = {}, {}, {}
    for n in names:
        shape = given[n].shape
        flat = lambda a: a.reshape(-1, shape[-1])
        d_, m_, v_ = adamw(f'adamw_{n}', flat(given[n]), flat(grads[n]), flat(given['m_' + n]), flat(given['v_' + n]))
        delta[n], new_m[n], new_v[n] = d_.reshape(shape), m_.reshape(shape), v_.reshape(shape)
    packs = [_pack([given[pre + n] for n in small_names]) for pre in ('', 'm_', 'v_')]
    outs = adamw('adamw_small', packs[0], _pack([grads[n] for n in small_names]), packs[1], packs[2])
    small_shapes = [given[n].shape for n in small_names]
    for dst, out in zip((delta, new_m, new_v), outs):
        for n, a in zip(small_names, _unpack(out, small_shapes)):
            dst[n] = a

    return (loss, grad_x, *[grads[n] for n in WEIGHT_NAMES], *[delta[n] for n in WEIGHT_NAMES],
            *[new_m[n] for n in WEIGHT_NAMES], *[new_v[n] for n in WEIGHT_NAMES])
```

```python
import functools
import math
from typing import Callable, NamedTuple

import jax
import jax.numpy as jnp
from jax import lax
from jax.experimental import pallas as pl
from jax.experimental.pallas import tpu as pltpu

F32 = jnp.float32
MXU_DTYPE = jnp.bfloat16

DEPTH = 4
CHUNK = 64
Q_BLOCK = 128
MLA_NOPE = 128
MLA_ROPE = 64
MLA_V = 128
ROPE_THETA = 10000.0
GLA_HEADS = 4
GLA_GATE_RANK = 16
GLA_TAU = 16.0
DN_ALPHA = (2 * DEPTH) ** 0.25
EPS = 1e-5
NEG_INF = -1e30
ADAM_LR = 0.001
ADAM_B1 = 0.9
ADAM_B2 = 0.999
ADAM_EPS = 1e-08
ADAM_WD = 0.01
ADAM_STEP = 10
GELU_C = math.sqrt(2.0 / math.pi)
GELU_A = 0.044715

LANE = 128
HEAD_PAD = 2 * LANE
VMEM_LIMIT_BYTES = 48 * 1024 * 1024

MESH = pl.DeviceIdType.MESH
ANY_SPEC = pl.BlockSpec(memory_space=pl.ANY)

WEIGHT_NAMES = ('mla_w_in', 'mla_q_norm', 'mla_kv_norm', 'mla_w_uq', 'mla_w_uk', 'mla_w_uv', 'mla_w_o',
                'gla_w_in', 'gla_w_a2', 'gla_b_a', 'gla_o_norm', 'gla_w_o', 'ln1_g', 'ln1_b', 'ln2_g', 'ln2_b',
                'ffn_w_up', 'ffn_conv_w', 'ffn_conv_b', 'ffn_w_down', 'ple_w_proj', 'ple_w_gate', 'ple_b_gate')
BIG = (('mla_w_in', 1), ('mla_w_uq', 2), ('mla_w_uk', 2), ('mla_w_uv', 2), ('mla_w_o', 1), ('gla_w_in', 2),
       ('gla_w_o', 1), ('ffn_w_up', 2), ('ffn_w_down', 1), ('ple_w_proj', 2), ('ple_w_gate', 1))
SMALL = (('mla_q_norm', None), ('mla_kv_norm', None), ('gla_w_a2', 2), ('gla_b_a', 1), ('gla_o_norm', 1),
         ('ln1_g', None), ('ln1_b', None), ('ln2_g', None), ('ln2_b', None), ('ffn_conv_w', 2),
         ('ffn_conv_b', None), ('ple_b_gate', None))
N_CHIPS = 4
N_DEVICES = 8


def _params():
    return pltpu.CompilerParams(vmem_limit_bytes=VMEM_LIMIT_BYTES)


def _tile(n, cap, *offsets, unit=LANE):
    g = n
    for o in offsets:
        if o:
            g = math.gcd(g, o)
    best = 0
    for d in range(unit, min(g, cap) + 1, unit):
        if g % d == 0:
            best = d
    if best:
        return best
    assert not any(offsets), (n, offsets)
    return n


def _dot(a, b, dims):
    return lax.dot_general(a.astype(MXU_DTYPE), b.astype(MXU_DTYPE), (dims, ((), ())),
                           preferred_element_type=F32)


NN = ((1,), (0,))
NT = ((1,), (1,))
TN = ((0,), (0,))


def mm(name, a, b, mode, out_dtype, *, a_pre=(), b_pre=(), b_win=None, acc_in=None, out_stack=None):
    a2 = a.shape[len(a_pre):]
    b2 = b.shape[len(b_pre):]
    br0, brn, bc0, bcn = b_win or (0, b2[0], 0, b2[1])
    if mode == 'nn':
        (M, K), N, dims = a2, bcn, NN
        assert brn == K
    elif mode == 'nt':
        (M, K), N, dims = a2, brn, NT
        assert bcn == K
    else:
        (K, M), N, dims = a2, bcn, TN
        assert brn == K
    oc0 = out_stack[3] if out_stack else 0
    tm = _tile(M, 1408)
    if mode == 'nn':
        tn, tk = _tile(N, 1408, bc0, oc0), _tile(K, 1408, br0)
    elif mode == 'nt':
        tn, tk = _tile(N, 1408, br0, oc0), _tile(K, 1408, bc0)
    else:
        tn, tk = _tile(N, 1408, bc0, oc0), _tile(K, 1024, br0)
    nk = K // tk
    grid = (M // tm, N // tn, nk)

    na, nb = len(a_pre), len(b_pre)
    if mode == 'tn':
        a_spec = pl.BlockSpec((None,) * na + (tk, tm), lambda i, j, k: a_pre + (k, i))
    else:
        a_spec = pl.BlockSpec((None,) * na + (tm, tk), lambda i, j, k: a_pre + (i, k))
    if mode == 'nt':
        b_spec = pl.BlockSpec((None,) * nb + (tn, tk), lambda i, j, k: b_pre + (j + br0 // tn, k + bc0 // tk))
    else:
        b_spec = pl.BlockSpec((None,) * nb + (tk, tn), lambda i, j, k: b_pre + (k + br0 // tk, j + bc0 // tn))
    in_specs, args = [a_spec, b_spec], [a, b]
    if acc_in is not None:
        in_specs.append(pl.BlockSpec((tm, tn), lambda i, j, k: (i, j)))
        args.append(acc_in)
    aliases = {}
    if out_stack is None:
        out_shape = jax.ShapeDtypeStruct((M, N), out_dtype)
        out_spec = pl.BlockSpec((tm, tn), lambda i, j, k: (i, j))
    else:
        buf, full_shape, lead, _ = out_stack
        out_shape = jax.ShapeDtypeStruct(full_shape, out_dtype)
        out_spec = pl.BlockSpec((None, tm, tn), lambda i, j, k: (lead, i, j + oc0 // tn))
        if buf is not None:
            aliases = {len(args): 0}
            in_specs.append(ANY_SPEC)
            args.append(buf)
    has_c, has_alias = acc_in is not None, bool(aliases)

    def body(*refs):
        a_ref, b_ref = refs[0], refs[1]
        c_ref = refs[2] if has_c else None
        o_ref = refs[2 + has_c + has_alias]
        prod = _dot(a_ref[...], b_ref[...], dims)
        if nk == 1:
            if has_c:
                prod = prod + c_ref[...]
            o_ref[...] = prod.astype(out_dtype)
            return
        acc_ref = refs[3 + has_c + has_alias]
        k = pl.program_id(2)

        @pl.when(k == 0)
        def _():
            acc_ref[...] = prod + c_ref[...] if has_c else prod

        @pl.when(k > 0)
        def _():
            acc_ref[...] += prod

        @pl.when(k == nk - 1)
        def _():
            o_ref[...] = acc_ref[...].astype(out_dtype)

    scratch = [pltpu.VMEM((tm, tn), F32)] if nk > 1 else []
    return pl.pallas_call(body, out_shape=out_shape, grid=grid, in_specs=in_specs, out_specs=out_spec,
                          scratch_shapes=scratch, input_output_aliases=aliases, name=name,
                          compiler_params=_params())(*args)


def rowcall(name, body, n_rows, ts, row_ins, full_ins, row_outs, acc_outs=()):
    in_specs, args = [], []
    for arr, width, colblk, pre in row_ins:
        in_specs.append(pl.BlockSpec((None,) * len(pre) + (ts, width),
                                     lambda i, pre=pre, cb=colblk: pre + (i, cb)))
        args.append(arr)
    for arr in full_ins:
        in_specs.append(pl.BlockSpec(arr.shape, lambda i, nd=arr.ndim: (0,) * nd))
        args.append(arr)
    out_shape, out_specs = [], []
    for width, dtype, *transposed in row_outs:
        if transposed:
            out_shape.append(jax.ShapeDtypeStruct((width, n_rows), dtype))
            out_specs.append(pl.BlockSpec((width, ts), lambda i: (0, i)))
        else:
            out_shape.append(jax.ShapeDtypeStruct((n_rows, width), dtype))
            out_specs.append(pl.BlockSpec((ts, width), lambda i: (i, 0)))
    for shape in acc_outs:
        out_shape.append(jax.ShapeDtypeStruct(shape, F32))
        out_specs.append(pl.BlockSpec(shape, lambda i, nd=len(shape): (0,) * nd))
    n_in = len(args)

    def kern(*refs):
        body(pl.program_id(0), refs[:n_in], refs[n_in:])

    return pl.pallas_call(kern, out_shape=tuple(out_shape), grid=(n_rows // ts,), in_specs=in_specs,
                          out_specs=tuple(out_specs), name=name, compiler_params=_params())(*args)


def _row(arr, width=None, colblk=0, pre=()):
    return (arr, arr.shape[-1] if width is None else width, colblk, pre)


def _init_acc(i, refs):
    @pl.when(i == 0)
    def _():
        for r in refs:
            r[...] = jnp.zeros(r.shape, r.dtype)


def _colsum(v):
    return jnp.sum(v, axis=0, keepdims=True)


def _sigmoid(z):
    return 1.0 / (1.0 + jnp.exp(-z))


def _row_tile(S):
    return _tile(S, 256, unit=16)


def _ln_stats(x_ref, m_ref):
    z = DN_ALPHA * x_ref[...] + m_ref[...]
    mu = jnp.mean(z, -1, keepdims=True)
    zc = z - mu
    var = jnp.mean(zc * zc, -1, keepdims=True)
    r = lax.rsqrt(var + EPS)
    return zc * r, r


def ln_fwd(name, x, m, g, b):
    S, D = x.shape

    def body(i, ins, outs):
        x_ref, m_ref, g_ref, b_ref = ins
        xh, _ = _ln_stats(x_ref, m_ref)
        y = xh * g_ref[...] + b_ref[...]
        outs[0][...] = y
        outs[1][...] = y.astype(MXU_DTYPE)
        outs[2][...] = y.T.astype(MXU_DTYPE)

    return rowcall(name, body, S, _row_tile(S), [_row(x), _row(m)], [g, b],
                   [(D, F32), (D, MXU_DTYPE), (D, MXU_DTYPE, 'T')])


def ln_bwd(name, x, m, dA, ca, dB, g):
    S, D = x.shape
    has_b = dB is not None

    def body(i, ins, outs):
        x_ref, m_ref, a_ref = ins[:3]
        g_ref = ins[-1]
        dz_ref, dzb_ref, dg_ref, db_ref = outs
        _init_acc(i, (dg_ref, db_ref))
        dy = ca * a_ref[...]
        if has_b:
            dy = dy + ins[3][...]
        xh, r = _ln_stats(x_ref, m_ref)
        dg_ref[...] += _colsum(dy * xh)
        db_ref[...] += _colsum(dy)
        dxh = dy * g_ref[...]
        dz = r * (dxh - jnp.mean(dxh, -1, keepdims=True) - xh * jnp.mean(dxh * xh, -1, keepdims=True))
        dz_ref[...] = dz
        dzb_ref[...] = dz.astype(MXU_DTYPE)

    rows = [_row(x), _row(m), _row(dA)] + ([_row(dB)] if has_b else [])
    return rowcall(name, body, S, _row_tile(S), rows, [g], [(D, F32), (D, MXU_DTYPE)], [(1, D), (1, D)])


def ple_fwd(name, x2, gp, pp, bias):
    S, D = x2.shape

    def body(i, ins, outs):
        x_ref, gp_ref, pp_ref, b_ref = ins
        y = x_ref[...] + _sigmoid(gp_ref[...] + b_ref[...]) * pp_ref[...]
        outs[0][...] = y
        outs[1][...] = y.astype(MXU_DTYPE)
        outs[2][...] = y.T.astype(MXU_DTYPE)

    return rowcall(name, body, S, _row_tile(S), [_row(x2), _row(gp), _row(pp)], [bias],
                   [(D, F32), (D, MXU_DTYPE), (D, MXU_DTYPE, 'T')])


def ple_bwd(name, dA, ca, dB, gp, pp, bias):
    S, D = gp.shape
    has_b = dB is not None

    def body(i, ins, outs):
        a_ref = ins[0]
        gp_ref, pp_ref, b_ref = ins[-3:]
        dx_ref, dpp_ref, dgp_ref, db_ref = outs
        _init_acc(i, (db_ref,))
        dx = ca * a_ref[...]
        if has_b:
            dx = dx + ins[1][...]
        gate = _sigmoid(gp_ref[...] + b_ref[...])
        dgp = dx * pp_ref[...] * gate * (1.0 - gate)
        dx_ref[...] = dx
        dpp_ref[...] = (dx * gate).astype(MXU_DTYPE)
        dgp_ref[...] = dgp.astype(MXU_DTYPE)
        db_ref[...] += _colsum(dgp)

    rows = [_row(dA)] + ([_row(dB)] if has_b else []) + [_row(gp), _row(pp)]
    return rowcall(name, body, S, _row_tile(S), rows, [bias],
                   [(D, F32), (D, MXU_DTYPE), (D, MXU_DTYPE)], [(1, D)])


def loss_head(name, y, target):
    S, D = y.shape

    def body(i, ins, outs):
        _init_acc(i, (outs[1],))
        e = ins[0][...] - ins[1][...]
        outs[0][...] = e * (1.0 / D)
        outs[1][...] += _colsum(e * e)

    return rowcall(name, body, S, _row_tile(S), [_row(y), _row(target)], [], [(D, F32)], [(1, D)])


def axpy(name, a, ca, b):
    S, D = a.shape

    def body(i, ins, outs):
        outs[0][...] = ca * ins[0][...] + ins[1][...]

    return rowcall(name, body, S, _row_tile(S), [_row(a), _row(b)], [], [(D, F32)])[0]


HALF_ROPE = MLA_ROPE // 2


def _rope(x, c, sa, sb):
    n = x.shape[-1]
    return x * c + pltpu.roll(x, n - HALF_ROPE, 1) * sa + pltpu.roll(x, HALF_ROPE, 1) * sb


def _rope_t(d, c, sa, sb):
    n = d.shape[-1]
    return d * c + pltpu.roll(d * sa, HALF_ROPE, 1) + pltpu.roll(d * sb, n - HALF_ROPE, 1)


def rope_tables(positions_row, n_lead):
    inv = 1.0 / (ROPE_THETA ** (jnp.arange(0, MLA_ROPE, 2, dtype=F32) / MLA_ROPE))
    ang = positions_row.astype(F32)[:, None] * inv
    cos, sin = jnp.cos(ang), jnp.sin(ang)
    S = cos.shape[0]
    z = jnp.zeros((S, HALF_ROPE), F32)
    tail = jnp.zeros((S, LANE - MLA_ROPE), F32)
    c = jnp.concatenate([jnp.ones((S, n_lead), F32), cos, cos, tail], -1)
    sa = jnp.concatenate([jnp.zeros((S, n_lead), F32), -sin, z, tail], -1)
    sb = jnp.concatenate([jnp.zeros((S, n_lead), F32), z, sin, tail], -1)
    return c, sa, sb


def mla_pre_fwd(name, h, qn, kvn, tab_k, QL, KL):
    S = h.shape[0]

    def body(i, ins, outs):
        h_ref, c_ref, sa_ref, sb_ref, qn_ref, kvn_ref = ins
        cq = h_ref[:, 0:QL]
        ckv = h_ref[:, QL:QL + KL]
        kr = h_ref[:, QL + KL:QL + KL + LANE]
        outs[0][...] = (cq * lax.rsqrt(jnp.mean(cq * cq, -1, keepdims=True) + EPS) * qn_ref[...]).astype(MXU_DTYPE)
        outs[1][...] = (ckv * lax.rsqrt(jnp.mean(ckv * ckv, -1, keepdims=True) + EPS) * kvn_ref[...]).astype(MXU_DTYPE)
        outs[2][...] = _rope(kr, c_ref[...], sa_ref[...], sb_ref[...]).astype(MXU_DTYPE)

    rows = [_row(h)] + [_row(t) for t in tab_k]
    return rowcall(name, body, S, _row_tile(S), rows, [qn, kvn],
                   [(QL, MXU_DTYPE), (KL, MXU_DTYPE), (LANE, MXU_DTYPE)])


def _rms_bwd(x, g, dy):
    r = lax.rsqrt(jnp.mean(x * x, -1, keepdims=True) + EPS)
    dg = _colsum(dy * x * r)
    dxg = dy * g
    dx = r * dxg - x * (r * r * r) * jnp.mean(dxg * x, -1, keepdims=True)
    return dx, dg


def mla_pre_bwd(name, h, dcq, dckv_a, dckv_b, dkr_heads, qn, kvn, tab_k, QL, KL, H):
    S, HW = h.shape

    def body(i, ins, outs):
        h_ref, dcq_ref, da_ref, db_ref, dkr_ref, c_ref, sa_ref, sb_ref, qn_ref, kvn_ref = ins
        dh_ref, dqn_ref, dkvn_ref = outs
        _init_acc(i, (dqn_ref, dkvn_ref))
        dx, dg = _rms_bwd(h_ref[:, 0:QL], qn_ref[...], dcq_ref[...])
        dh_ref[:, 0:QL] = dx.astype(MXU_DTYPE)
        dqn_ref[...] += dg
        dx, dg = _rms_bwd(h_ref[:, QL:QL + KL], kvn_ref[...], da_ref[...] + db_ref[...])
        dh_ref[:, QL:QL + KL] = dx.astype(MXU_DTYPE)
        dkvn_ref[...] += dg
        d = dkr_ref[:, 0:LANE]
        for hh in range(1, H):
            d = d + dkr_ref[:, hh * LANE:(hh + 1) * LANE]
        dh_ref[:, QL + KL:QL + KL + LANE] = _rope_t(d, c_ref[...], sa_ref[...], sb_ref[...]).astype(MXU_DTYPE)

    rows = [_row(h), _row(dcq), _row(dckv_a), _row(dckv_b), _row(dkr_heads)] + [_row(t) for t in tab_k]
    return rowcall(name, body, S, _row_tile(S), rows, [qn, kvn], [(HW, MXU_DTYPE)], [(1, QL), (1, KL)])


def rope_q(name, qpre, tab_q, H, out_dtype):
    S = qpre.shape[0]

    def body(i, ins, outs):
        c, sa, sb = ins[1][...], ins[2][...], ins[3][...]
        for hh in range(H):
            sl = slice(hh * HEAD_PAD, (hh + 1) * HEAD_PAD)
            outs[0][:, sl] = _rope(ins[0][:, sl], c, sa, sb).astype(out_dtype)

    rows = [_row(qpre)] + [_row(t) for t in tab_q]
    return rowcall(name, body, S, _row_tile(S), rows, [], [(H * HEAD_PAD, out_dtype)])[0]


ATT_T_FWD = 1024
ATT_T_BWD = 512


def _att_mask_t(ck_col, cq_row, k0, q0, t):
    kpos = k0 + lax.broadcasted_iota(jnp.int32, (t, 1), 0)
    qpos = q0 + lax.broadcasted_iota(jnp.int32, (1, t), 1)
    return (ck_col <= cq_row) & (kpos <= (qpos | (Q_BLOCK - 1)))


def _transpose(a):
    return a.astype(F32).T.astype(a.dtype)


def hosted_call(body, *, name, grid, in_specs, out_specs, out_shape, scratch_shapes, args, rider):
    if rider is None:
        return pl.pallas_call(body, name=name, grid=grid, in_specs=in_specs, out_specs=out_specs,
                              out_shape=out_shape, scratch_shapes=scratch_shapes,
                              compiler_params=_params())(*args), ()
    n_in, n_out, n_scr = len(args), len(out_shape), len(scratch_shapes)
    r_in, r_out = len(rider.ins), len(rider.out_shapes)

    def hosted(*refs):
        ins, refs = refs[:n_in], refs[n_in:]
        rin, refs = refs[:r_in], refs[r_in:]
        outs, refs = refs[:n_out], refs[n_out:]
        rout, refs = refs[:r_out], refs[r_out:]
        scr, sems = refs[:n_scr], refs[n_scr:]
        ids = [pl.program_id(a) for a in range(len(grid))]
        first = functools.reduce(jnp.logical_and, [i == 0 for i in ids])
        last = functools.reduce(jnp.logical_and, [i == g - 1 for i, g in zip(ids, grid)])

        @pl.when(first)
        def _():
            rider.start(rin, rout, sems)

        body(*ins, *outs, *scr)

        @pl.when(last)
        def _():
            rider.finish(rin, rout, sems)

    results = pl.pallas_call(
        hosted, name=name, grid=grid, in_specs=list(in_specs) + [ANY_SPEC] * r_in,
        out_specs=tuple(out_specs) + (ANY_SPEC,) * r_out, out_shape=tuple(out_shape) + tuple(rider.out_shapes),
        scratch_shapes=list(scratch_shapes) + [pltpu.SemaphoreType.DMA((k,)) for k in rider.sem_counts],
        input_output_aliases={n_in + k: n_out + k for k in range(r_in)} if rider.in_place else {},
        compiler_params=pltpu.CompilerParams(vmem_limit_bytes=VMEM_LIMIT_BYTES, has_side_effects=True),
    )(*args, *rider.ins)
    return results[:n_out], results[n_out:]


def attn_fwd(name, q, kn, kr, v, cid, H, rider=None):
    S = q.shape[0]
    t = _tile(S, ATT_T_FWD)
    n = S // t
    scale = (MLA_NOPE + MLA_ROPE) ** -0.5
    cid_col, cid_blk = cid.reshape(S, 1), cid.reshape(n, 1, t)

    def body(q_ref, kn_ref, kr_ref, v_ref, cc_ref, cr_ref, o_ref, lse_ref, ot_ref, k_scr, vt_scr):
        i = pl.program_id(1)

        @pl.when(i == 0)
        def _():
            k_scr[:, 0:LANE] = kn_ref[...]
            k_scr[:, LANE:HEAD_PAD] = kr_ref[...]
            for jj in range(n):
                vt_scr[jj] = _transpose(v_ref[jj * t:(jj + 1) * t, :])

        qv = q_ref[...]
        cq = cr_ref[i]

        def update(j, carry, masked):
            m, l, acc = carry
            k0 = pl.multiple_of(j * t, t)
            s = _dot(k_scr[pl.ds(k0, t), :], qv, NT) * scale
            if masked:
                s = jnp.where(_att_mask_t(cc_ref[pl.ds(k0, t), :], cq, k0, i * t, t), s, NEG_INF)
            m_new = jnp.maximum(m, jnp.max(s, 0, keepdims=True))
            p = jnp.exp(s - m_new)
            alpha = jnp.exp(m - m_new)
            l = alpha * l + jnp.sum(p, 0, keepdims=True)
            acc = alpha * acc + _dot(vt_scr[j], p, NN)
            return m_new, l, acc

        init = (jnp.full((1, t), NEG_INF, F32), jnp.zeros((1, t), F32), jnp.zeros((MLA_V, t), F32))
        carry = lax.fori_loop(0, i, lambda j, c: update(j, c, False), init)
        m, l, acc = update(i, carry, True)
        ot = acc / l
        o_ref[...] = ot.T.astype(o_ref.dtype)
        ot_ref[...] = ot.astype(ot_ref.dtype)
        lse_ref[...] = m + jnp.log(l)

    return hosted_call(
        body, name=name, grid=(H, n),
        in_specs=[pl.BlockSpec((t, HEAD_PAD), lambda h, i: (i, h)),
                  pl.BlockSpec((S, MLA_NOPE), lambda h, i: (0, h)),
                  pl.BlockSpec((S, LANE), lambda h, i: (0, 0)),
                  pl.BlockSpec((S, MLA_V), lambda h, i: (0, h)),
                  pl.BlockSpec((S, 1), lambda h, i: (0, 0)),
                  pl.BlockSpec(cid_blk.shape, lambda h, i: (0, 0, 0))],
        out_specs=(pl.BlockSpec((t, MLA_V), lambda h, i: (i, h)),
                   pl.BlockSpec((None, None, 1, t), lambda h, i: (h, i, 0, 0)),
                   pl.BlockSpec((MLA_V, t), lambda h, i: (h, i))),
        scratch_shapes=[pltpu.VMEM((S, HEAD_PAD), MXU_DTYPE), pltpu.VMEM((n, MLA_V, t), MXU_DTYPE)],
        out_shape=(jax.ShapeDtypeStruct((S, H * MLA_V), MXU_DTYPE), jax.ShapeDtypeStruct((H, n, 1, t), F32),
                   jax.ShapeDtypeStruct((H * MLA_V, S), MXU_DTYPE)),
        args=(q, kn, kr, v, cid_col, cid_blk), rider=rider)


def attn_bwd(name, q, kn, kr, v, o, lse, do, cid, H, rider=None):
    S = q.shape[0]
    t = _tile(S, ATT_T_BWD)
    n = S // t
    scale = (MLA_NOPE + MLA_ROPE) ** -0.5
    cid_col, cid_blk = cid.reshape(S, 1), cid.reshape(n, 1, t)
    lse = lse.reshape(H, n, 1, t)

    def body(q_ref, kn_ref, kr_ref, v_ref, o_ref, lse_ref, do_ref, cc_ref, cr_ref,
             dq_ref, dkn_ref, dv_ref, dkr_ref, delta_scr, dk_scr, dv_scr):
        j = pl.program_id(1)

        @pl.when(j == 0)
        def _():
            dq_ref[...] = jnp.zeros(dq_ref.shape, F32)
            for ii in range(n):
                sl = slice(ii * t, (ii + 1) * t)
                prod = do_ref[sl, :].astype(F32) * o_ref[sl, :].astype(F32)
                delta_scr[ii] = jnp.sum(prod.T, 0, keepdims=True)

        kv = jnp.concatenate([kn_ref[...], kr_ref[...]], axis=-1)
        kt = _transpose(kv)
        vv = v_ref[...]
        ck = cc_ref[...]
        k0 = j * t
        dk_scr[...] = jnp.zeros(dk_scr.shape, F32)
        dv_scr[...] = jnp.zeros(dv_scr.shape, F32)

        def update(i, masked):
            q0 = pl.multiple_of(i * t, t)
            qv = q_ref[pl.ds(q0, t), :]
            dov = do_ref[pl.ds(q0, t), :]
            s = _dot(kv, qv, NT) * scale
            if masked:
                s = jnp.where(_att_mask_t(ck, cr_ref[i], k0, q0, t), s, NEG_INF)
            p = jnp.exp(s - lse_ref[i])
            dv_scr[...] += _dot(p, dov, NN)
            dp = _dot(vv, dov, NT)
            ds = p * (dp - delta_scr[i]) * scale
            dk_scr[...] += _dot(ds, qv, NN)
            dq_ref[i] += _dot(kt, ds, NN)

        def step(i, carry):
            update(i, False)
            return carry

        update(j, True)
        lax.fori_loop(j + 1, n, step, 0)
        dkn_ref[...] = dk_scr[:, 0:LANE].astype(dkn_ref.dtype)
        dkr_ref[...] = dk_scr[:, LANE:HEAD_PAD]
        dv_ref[...] = dv_scr[...].astype(dv_ref.dtype)

    head_rows = lambda w: pl.BlockSpec((S, w), lambda h, j: (0, h))
    tile_rows = lambda w: pl.BlockSpec((t, w), lambda h, j: (j, h))
    return hosted_call(
        body, name=name, grid=(H, n),
        in_specs=[head_rows(HEAD_PAD), tile_rows(MLA_NOPE), pl.BlockSpec((t, LANE), lambda h, j: (j, 0)),
                  tile_rows(MLA_V), head_rows(MLA_V),
                  pl.BlockSpec((None, n, 1, t), lambda h, j: (h, 0, 0, 0)), head_rows(MLA_V),
                  pl.BlockSpec((t, 1), lambda h, j: (j, 0)),
                  pl.BlockSpec(cid_blk.shape, lambda h, j: (0, 0, 0))],
        out_specs=(pl.BlockSpec((None, n, HEAD_PAD, t), lambda h, j: (h, 0, 0, 0)),
                   tile_rows(MLA_NOPE), tile_rows(MLA_V), tile_rows(LANE)),
        scratch_shapes=[pltpu.VMEM((n, 1, t), F32), pltpu.VMEM((t, HEAD_PAD), F32), pltpu.VMEM((t, MLA_V), F32)],
        out_shape=(jax.ShapeDtypeStruct((H, n, HEAD_PAD, t), F32), jax.ShapeDtypeStruct((S, H * MLA_NOPE), MXU_DTYPE),
                   jax.ShapeDtypeStruct((S, H * MLA_V), MXU_DTYPE), jax.ShapeDtypeStruct((S, H * LANE), F32)),
        args=(q, kn, kr, v, o, lse, do, cid_col, cid_blk), rider=rider)


def rope_q_bwd(name, dq_t, tab_q_t, H):
    _, n, _, t = dq_t.shape

    def body(d_ref, c_ref, sa_ref, sb_ref, o_ref):
        c, sa, sb = c_ref[...], sa_ref[...], sb_ref[...]
        for hh in range(H):
            d = d_ref[hh]
            out = d * c + pltpu.roll(d * sa, HALF_ROPE, 0) + pltpu.roll(d * sb, HEAD_PAD - HALF_ROPE, 0)
            o_ref[hh * HEAD_PAD:(hh + 1) * HEAD_PAD, :] = out.astype(o_ref.dtype)

    tab = pl.BlockSpec((HEAD_PAD, t), lambda i: (0, i))
    return pl.pallas_call(
        body, name=name, grid=(n,), out_shape=jax.ShapeDtypeStruct((H * HEAD_PAD, n * t), MXU_DTYPE),
        in_specs=[pl.BlockSpec((H, None, HEAD_PAD, t), lambda i: (0, i, 0, 0)), tab, tab, tab],
        out_specs=pl.BlockSpec((H * HEAD_PAD, t), lambda i: (0, i)),
        compiler_params=_params())(dq_t, *tab_q_t)


GLA_GROUP = 8


def _prefix_rows(x):
    n = x.shape[0]
    row = lax.broadcasted_iota(jnp.int32, x.shape, 0)
    d = 1
    while d < n:
        x = x + jnp.where(row >= d, pltpu.roll(x, d, 0), 0.0)
        d *= 2
    return x


def _suffix_rows(x):
    n = x.shape[0]
    row = lax.broadcasted_iota(jnp.int32, x.shape, 0)
    d = 1
    while d < n:
        x = x + jnp.where(row < n - d, pltpu.roll(x, n - d, 0), 0.0)
        d *= 2
    return x


def _log_sigmoid(z):
    return jnp.minimum(z, 0.0) - jnp.log(1.0 + jnp.exp(-jnp.abs(z)))


def gla_pre_fwd(name, h, w2p, b_a, QK, a_blk):
    S = h.shape[0]
    dk = QK // GLA_HEADS

    def body(i, ins, outs):
        q_ref, a_ref, w_ref, b_ref = ins
        outs[0][...] = (q_ref[...] * (dk ** -0.5)).astype(MXU_DTYPE)
        z = _dot(a_ref[...], w_ref[...], NN) + b_ref[...]
        outs[1][...] = _log_sigmoid(z) / GLA_TAU

    return rowcall(name, body, S, _row_tile(S), [_row(h, QK, 0), _row(h, LANE, a_blk)], [w2p, b_a],
                   [(QK, MXU_DTYPE), (QK, F32)])


def _gla_specs(S, QK, VD, rows, gmap):
    dk, dv = QK // GLA_HEADS, VD // GLA_HEADS
    return dict(
        qs=pl.BlockSpec((rows, dk), lambda h, g: (gmap(g), h)),
        k=pl.BlockSpec((rows, dk), lambda h, g: (gmap(g), QK // dk + h)),
        v=pl.BlockSpec((rows, dv), lambda h, g: (gmap(g), 2 * QK // dv + h)),
        la=pl.BlockSpec((rows, dk), lambda h, g: (gmap(g), h)),
        o=pl.BlockSpec((rows, dv), lambda h, g: (gmap(g), h)))


def gla_fwd(name, qs, h, la, QK, VD):
    S = qs.shape[0]
    dk, dv = QK // GLA_HEADS, VD // GLA_HEADS
    n_chunks = S // CHUNK
    cg = min(GLA_GROUP, n_chunks)
    rows = cg * CHUNK
    sp = _gla_specs(S, QK, VD, rows, lambda g: g)

    def body(q_ref, k_ref, v_ref, la_ref, o_ref, st_ref, state):
        @pl.when(pl.program_id(1) == 0)
        def _():
            state[...] = jnp.zeros(state.shape, F32)

        for c in range(cg):
            sl = slice(c * CHUNK, (c + 1) * CHUNK)
            cum = _prefix_rows(la_ref[sl, :])
            tot = cum[CHUNK - 1:CHUNK, :]
            kdec = k_ref[sl, :] * jnp.exp(tot - cum)
            st = state[...] * jnp.exp(tot) + _dot(v_ref[sl, :], kdec, TN)
            state[...] = st
            st_ref[c] = st
            o_ref[sl, :] = _dot(q_ref[sl, :], st, NT)

    return pl.pallas_call(
        body, name=name, grid=(GLA_HEADS, n_chunks // cg),
        out_shape=(jax.ShapeDtypeStruct((S, VD), F32), jax.ShapeDtypeStruct((GLA_HEADS, n_chunks, dv, dk), F32)),
        in_specs=[sp['qs'], sp['k'], sp['v'], sp['la']],
        out_specs=(sp['o'], pl.BlockSpec((None, cg, dv, dk), lambda h, g: (h, g, 0, 0))),
        scratch_shapes=[pltpu.VMEM((dv, dk), F32)],
        compiler_params=_params())(qs, h, h, la)


def gla_bwd(name, qs, h, la, states, do, QK, VD):
    S = qs.shape[0]
    dk, dv = QK // GLA_HEADS, VD // GLA_HEADS
    n_chunks = S // CHUNK
    cg = min(GLA_GROUP, n_chunks)
    ng = n_chunks // cg
    rows = cg * CHUNK
    rev = lambda g: ng - 1 - g
    sp = _gla_specs(S, QK, VD, rows, rev)

    def body(q_ref, k_ref, v_ref, la_ref, st_ref, prev_ref, do_ref, dq_ref, dk_ref, dv_ref, dla_ref, dst):
        g = pl.program_id(1)

        @pl.when(g == 0)
        def _():
            dst[...] = jnp.zeros(dst.shape, F32)

        first_group = (g == ng - 1).astype(F32)
        for c in reversed(range(cg)):
            sl = slice(c * CHUNK, (c + 1) * CHUNK)
            cum = _prefix_rows(la_ref[sl, :])
            tot = cum[CHUNK - 1:CHUNK, :]
            e = jnp.exp(tot - cum)
            kdec = k_ref[sl, :] * e
            decay = jnp.exp(tot)
            st = st_ref[c]
            st_prev = st_ref[c - 1] if c > 0 else prev_ref[0] * (1.0 - first_group)
            dov = do_ref[sl, :]
            qv = q_ref[sl, :]
            dq_ref[sl, :] = (_dot(dov, st, NN) * (dk ** -0.5)).astype(dq_ref.dtype)
            d = dst[...] + _dot(dov, qv, TN)
            ddecay = _colsum(d * st_prev)
            dkdec = _dot(v_ref[sl, :], d, NN)
            dv_ref[sl, :] = _dot(kdec, d, NT).astype(dv_ref.dtype)
            dk_ref[sl, :] = (dkdec * e).astype(dk_ref.dtype)
            darg = dkdec * kdec
            dtot = _colsum(darg) + ddecay * decay
            dla_ref[sl, :] = dtot - _suffix_rows(darg)
            dst[...] = d * decay

    prev_spec = pl.BlockSpec((None, 1, dv, dk), lambda h, g: (h, jnp.maximum(rev(g) * cg - 1, 0), 0, 0))
    return pl.pallas_call(
        body, name=name, grid=(GLA_HEADS, ng),
        out_shape=(jax.ShapeDtypeStruct((S, QK), MXU_DTYPE), jax.ShapeDtypeStruct((S, QK), MXU_DTYPE),
                   jax.ShapeDtypeStruct((S, VD), MXU_DTYPE), jax.ShapeDtypeStruct((S, QK), F32)),
        in_specs=[sp['qs'], sp['k'], sp['v'], sp['la'],
                  pl.BlockSpec((None, cg, dv, dk), lambda h, g: (h, rev(g), 0, 0)), prev_spec, sp['o']],
        out_specs=(sp['qs'], sp['qs'], sp['o'], sp['la']),
        scratch_shapes=[pltpu.VMEM((dv, dk), F32)],
        compiler_params=_params())(qs, h, h, la, states, states, do)


def _head_norm(o):
    mu = jnp.mean(o, -1, keepdims=True)
    oc = o - mu
    r = lax.rsqrt(jnp.mean(oc * oc, -1, keepdims=True) + EPS)
    return oc * r, r


def gla_post_fwd(name, o, h, o_norm, VD, r_blk):
    S = o.shape[0]
    dv = VD // GLA_HEADS

    def body(i, ins, outs):
        o_ref, r_ref, w_ref = ins
        for hh in range(GLA_HEADS):
            sl = slice(hh * dv, (hh + 1) * dv)
            xh, _ = _head_norm(o_ref[:, sl])
            r = r_ref[:, sl]
            og = xh * w_ref[:, sl] * (r * _sigmoid(r))
            outs[0][:, sl] = og.astype(MXU_DTYPE)
            outs[1][sl, :] = og.T.astype(MXU_DTYPE)

    return rowcall(name, body, S, _row_tile(S), [_row(o), _row(h, VD, r_blk)], [o_norm],
                   [(VD, MXU_DTYPE), (VD, MXU_DTYPE, 'T')])


def gla_post_bwd(name, o, h, dog, o_norm, VD, r_blk):
    S = o.shape[0]
    dv = VD // GLA_HEADS

    def body(i, ins, outs):
        o_ref, r_ref, dog_ref, w_ref = ins
        do_ref, dr_ref, dw_ref = outs
        _init_acc(i, (dw_ref,))
        for hh in range(GLA_HEADS):
            sl = slice(hh * dv, (hh + 1) * dv)
            xh, rs = _head_norm(o_ref[:, sl])
            r = r_ref[:, sl]
            w = w_ref[:, sl]
            dog = dog_ref[:, sl]
            sg = _sigmoid(r)
            dn = dog * (r * sg)
            dr_ref[:, sl] = (dog * (xh * w) * (sg * (1.0 + r * (1.0 - sg)))).astype(MXU_DTYPE)
            dw_ref[:, sl] += _colsum(dn * xh)
            dxh = dn * w
            do_ref[:, sl] = rs * (dxh - jnp.mean(dxh, -1, keepdims=True) - xh * jnp.mean(dxh * xh, -1, keepdims=True))

    return rowcall(name, body, S, _row_tile(S), [_row(o), _row(h, VD, r_blk), _row(dog)], [o_norm],
                   [(VD, F32), (VD, MXU_DTYPE)], [(1, VD)])


def gla_dh(name, dq, dk, dv, dr, dla, h, w2p, b_a, QK, VD, a_blk, HW):
    S = dq.shape[0]

    def body(i, ins, outs):
        dq_ref, dk_ref, dv_ref, dr_ref, dla_ref, a_ref, w_ref, b_ref = ins
        dh_ref, dw_ref, db_ref = outs
        _init_acc(i, (dw_ref, db_ref))
        a = a_ref[...]
        z = _dot(a, w_ref[...], NN) + b_ref[...]
        dz = dla_ref[...] * (1.0 / GLA_TAU) * _sigmoid(-z)
        dw_ref[...] += _dot(a, dz, TN)
        db_ref[...] += _colsum(dz)
        dh_ref[:, 0:QK] = dq_ref[...]
        dh_ref[:, QK:2 * QK] = dk_ref[...]
        dh_ref[:, 2 * QK:2 * QK + VD] = dv_ref[...]
        dh_ref[:, 2 * QK + VD:2 * QK + 2 * VD] = dr_ref[...]
        dh_ref[:, 2 * QK + 2 * VD:HW] = _dot(dz, w_ref[...], NT).astype(MXU_DTYPE)

    rows = [_row(dq), _row(dk), _row(dv), _row(dr), _row(dla), _row(h, LANE, a_blk)]
    return rowcall(name, body, S, _row_tile(S), rows, [w2p, b_a], [(HW, MXU_DTYPE)], [(LANE, QK), (1, QK)])


CONV_ROWS = 512
HALO = 8


def _gelu(x):
    return 0.5 * x * (1.0 + jnp.tanh(GELU_C * (x + GELU_A * x * x * x)))


def _gelu_grad(x):
    t = jnp.tanh(GELU_C * (x + GELU_A * x * x * x))
    return 0.5 * (1.0 + t) + 0.5 * x * (1.0 - t * t) * GELU_C * (1.0 + 3.0 * GELU_A * x * x)


def _shift_down(x, prev, d):
    row = lax.broadcasted_iota(jnp.int32, x.shape, 0)
    out = pltpu.roll(x, d, 0)
    for t in range(d):
        out = jnp.where(row == t, prev[HALO - d + t:HALO - d + t + 1, :], out)
    return out


def _shift_up(x, nxt, d):
    n = x.shape[0]
    row = lax.broadcasted_iota(jnp.int32, x.shape, 0)
    out = pltpu.roll(x, n - d, 0)
    for t in range(d):
        out = jnp.where(row == n - d + t, nxt[t:t + 1, :], out)
    return out


def _conv_taps(ref, r, rc):
    x = ref[r * rc:(r + 1) * rc, :]
    prev = ref[r * rc - HALO:r * rc, :] if r > 0 else jnp.zeros((HALO, x.shape[1]), F32)
    return x, _shift_down(x, prev, 1), _shift_down(x, prev, 2)


def _conv_apply(taps, w_ref, b_ref):
    x0, x1, x2 = taps
    return x2 * w_ref[0:1, :] + x1 * w_ref[1:2, :] + x0 * w_ref[2:3, :] + b_ref[...]


def conv_fwd(name, hu, hg, cw_u, cw_g, cb_u, cb_g):
    S, F = hu.shape
    tc = LANE
    rc = _tile(S, CONV_ROWS, unit=16)

    def body(u_ref, g_ref, wu_ref, wg_ref, bu_ref, bg_ref, a_ref, at_ref):
        for r in range(S // rc):
            uc = _conv_apply(_conv_taps(u_ref, r, rc), wu_ref, bu_ref)
            gc = _conv_apply(_conv_taps(g_ref, r, rc), wg_ref, bg_ref)
            a = uc * _gelu(gc)
            a_ref[r * rc:(r + 1) * rc, :] = a.astype(a_ref.dtype)
            at_ref[:, r * rc:(r + 1) * rc] = a.T.astype(at_ref.dtype)

    col = lambda rows: pl.BlockSpec((rows, tc), lambda j: (0, j))
    return pl.pallas_call(
        body, name=name, grid=(F // tc,),
        out_shape=(jax.ShapeDtypeStruct((S, F), MXU_DTYPE), jax.ShapeDtypeStruct((F, S), MXU_DTYPE)),
        in_specs=[col(S), col(S), col(3), col(3), col(1), col(1)],
        out_specs=(col(S), pl.BlockSpec((tc, S), lambda j: (j, 0))),
        compiler_params=_params())(hu, hg, cw_u, cw_g, cb_u, cb_g)


def conv_bwd(name, hu, hg, da, cw_u, cw_g, cb_u, cb_g):
    S, F = hu.shape
    tc = LANE
    rc = _tile(S, CONV_ROWS, unit=16)
    nr = S // rc

    def body(u_ref, g_ref, da_ref, wu_ref, wg_ref, bu_ref, bg_ref,
             dhu_ref, dhg_ref, dwu_ref, dwg_ref, dbu_ref, dbg_ref, du_scr, dg_scr):
        dw = [[jnp.zeros((1, tc), F32) for _ in range(3)] for _ in range(2)]
        db = [jnp.zeros((1, tc), F32) for _ in range(2)]
        for r in range(nr):
            sl = slice(r * rc, (r + 1) * rc)
            ut = _conv_taps(u_ref, r, rc)
            gt = _conv_taps(g_ref, r, rc)
            uc = _conv_apply(ut, wu_ref, bu_ref)
            gc = _conv_apply(gt, wg_ref, bg_ref)
            dav = da_ref[sl, :]
            duc = dav * _gelu(gc)
            dgc = dav * uc * _gelu_grad(gc)
            du_scr[sl, :] = duc
            dg_scr[sl, :] = dgc
            for part, (taps, d) in enumerate(((ut, duc), (gt, dgc))):
                db[part] = db[part] + _colsum(d)
                for tap in range(3):
                    dw[part][tap] = dw[part][tap] + _colsum(taps[2 - tap] * d)
        for part, (w_out, b_out) in enumerate(((dwu_ref, dbu_ref), (dwg_ref, dbg_ref))):
            b_out[...] = db[part]
            for tap in range(3):
                w_out[tap:tap + 1, :] = dw[part][tap]
        for scr, w_ref, out in ((du_scr, wu_ref, dhu_ref), (dg_scr, wg_ref, dhg_ref)):
            for r in range(nr):
                d = scr[r * rc:(r + 1) * rc, :]
                nxt = scr[(r + 1) * rc:(r + 1) * rc + HALO, :] if r + 1 < nr else jnp.zeros((HALO, tc), F32)
                dh = d * w_ref[2:3, :] + _shift_up(d, nxt, 1) * w_ref[1:2, :] + _shift_up(d, nxt, 2) * w_ref[0:1, :]
                out[r * rc:(r + 1) * rc, :] = dh.astype(out.dtype)

    col = lambda rows: pl.BlockSpec((rows, tc), lambda j: (0, j))
    sds = jax.ShapeDtypeStruct
    return pl.pallas_call(
        body, name=name, grid=(F // tc,),
        out_shape=(sds((S, F), MXU_DTYPE), sds((S, F), MXU_DTYPE), sds((3, F), F32), sds((3, F), F32),
                   sds((1, F), F32), sds((1, F), F32)),
        in_specs=[col(S), col(S), col(S), col(3), col(3), col(1), col(1)],
        out_specs=(col(S), col(S), col(3), col(3), col(1), col(1)),
        scratch_shapes=[pltpu.VMEM((S, tc), F32), pltpu.VMEM((S, tc), F32)],
        compiler_params=_params())(hu, hg, da, cw_u, cw_g, cb_u, cb_g)


def adamw(name, w, g, m, v):
    *lead, R, C = w.shape
    tr = _tile(R, max(8, (1 << 19) // max(C, 1) // 8 * 8), unit=8)

    def body(w_ref, g_ref, m_ref, v_ref, d_ref, nm_ref, nv_ref):
        gv = g_ref[...]
        mn = ADAM_B1 * m_ref[...] + (1.0 - ADAM_B1) * gv
        vn = ADAM_B2 * v_ref[...] + (1.0 - ADAM_B2) * (gv * gv)
        m_hat = mn / (1.0 - ADAM_B1 ** ADAM_STEP)
        v_hat = vn / (1.0 - ADAM_B2 ** ADAM_STEP)
        d_ref[...] = -ADAM_LR * (m_hat / (jnp.sqrt(v_hat) + ADAM_EPS) + ADAM_WD * w_ref[...])
        nm_ref[...] = mn
        nv_ref[...] = vn

    if lead:
        spec, grid = pl.BlockSpec((None, tr, C), lambda l, i: (l, i, 0)), (lead[0], R // tr)
    else:
        spec, grid = pl.BlockSpec((tr, C), lambda i: (i, 0)), (R // tr,)
    shp = jax.ShapeDtypeStruct(w.shape, F32)
    return pl.pallas_call(body, name=name, grid=grid, out_shape=(shp, shp, shp), in_specs=[spec] * 4,
                          out_specs=(spec, spec, spec), compiler_params=_params())(w, g, m, v)


def _position():
    return lax.axis_index('x'), lax.axis_index('y'), lax.axis_index('c')


def _other_chips(x, y):
    return ((1 - x, y), (x, 1 - y), (1 - x, 1 - y))


def _window(ref, dim, lo, n_lead, jj, rs, cs):
    if dim == 1:
        return ref.at[pl.ds(lo, n_lead), pl.ds(pl.multiple_of(jj * rs, 16), rs), :]
    return ref.at[pl.ds(lo, n_lead), :, pl.ds(pl.multiple_of(jj * cs, LANE), cs)]


def _hbm_call(name, body, out_shapes, n_sems, args, aliases=None):
    return pl.pallas_call(
        body, name=name, out_shape=tuple(out_shapes), in_specs=[ANY_SPEC] * len(args),
        out_specs=tuple(ANY_SPEC for _ in out_shapes),
        scratch_shapes=[pltpu.SemaphoreType.DMA((n,)) for n in n_sems],
        input_output_aliases=aliases or {},
        compiler_params=pltpu.CompilerParams(has_side_effects=True))(*args)


def place_shard(name, shard, dim, j_arr):
    L, rs, cs = shard.shape
    full_shape = (L, rs * N_CHIPS, cs) if dim == 1 else (L, rs, cs * N_CHIPS)
    tr = _tile(rs, max(16, (1 << 19) // cs // 16 * 16), unit=16)

    def body(j_ref, s_ref, o_ref):
        o_ref[...] = s_ref[...].astype(o_ref.dtype)

    if dim == 1:
        out_spec = pl.BlockSpec((None, tr, cs), lambda l, i, j: (l, j[0] * (rs // tr) + i, 0))
    else:
        out_spec = pl.BlockSpec((None, tr, cs), lambda l, i, j: (l, i, j[0]))
    grid_spec = pltpu.PrefetchScalarGridSpec(
        num_scalar_prefetch=1, grid=(L, rs // tr),
        in_specs=[pl.BlockSpec((None, tr, cs), lambda l, i, j: (l, i, 0))], out_specs=out_spec)
    return pl.pallas_call(body, name=name, grid_spec=grid_spec,
                          out_shape=jax.ShapeDtypeStruct(full_shape, MXU_DTYPE),
                          compiler_params=_params())(j_arr, shard)


class Exchange(NamedTuple):
    ins: tuple
    out_shapes: tuple
    in_place: bool
    sem_counts: tuple
    start: Callable
    finish: Callable


def run_exchange(name, ex):
    n_in, n_out = len(ex.ins), len(ex.out_shapes)

    def body(*refs):
        ins, outs, sems = refs[:n_in], refs[n_in:n_in + n_out], refs[n_in + n_out:]
        ex.start(ins, outs, sems)
        ex.finish(ins, outs, sems)

    return _hbm_call(name, body, ex.out_shapes, ex.sem_counts, ex.ins,
                     {k: k for k in range(n_in)} if ex.in_place else None)


def _rcopy(src, dst, ssem, rsem, device):
    return pltpu.make_async_remote_copy(src_ref=src, dst_ref=dst, send_sem=ssem, recv_sem=rsem,
                                        device_id=device, device_id_type=MESH)


def gather_exchange(fulls, dims, shard_shapes, items):
    n = len(items)

    def win(full, w, lo, nl, jj):
        return _window(full[w], dims[w], lo, nl, jj, shard_shapes[w][1], shard_shapes[w][2])

    def start(_, full, sems):
        ssem, rsem = sems[0], sems[1]
        x, y, c = _position()
        for it, (w, lo, nl, owner) in enumerate(items):
            @pl.when(c == owner)
            def _():
                own = win(full, w, lo, nl, 2 * x + y)
                for k, (cx, cy) in enumerate(_other_chips(x, y)):
                    _rcopy(own, own, ssem.at[3 * it + k], rsem.at[3 * it + k], (cx, cy, c)).start()

    def finish(_, full, sems):
        ssem, rsem, s2sem, r2sem = sems
        x, y, c = _position()
        sibling = (x, y, 1 - c)
        for it, (w, lo, nl, owner) in enumerate(items):
            @pl.when(c == owner)
            def _():
                for k, (cx, cy) in enumerate(_other_chips(x, y)):
                    theirs = win(full, w, lo, nl, 2 * cx + cy)
                    _rcopy(theirs, theirs, ssem.at[3 * it + k], rsem.at[3 * it + k], (cx, cy, c)).wait_recv()
                    _rcopy(theirs, theirs, s2sem.at[3 * it + k], r2sem.at[3 * it + k], sibling).start()
        for it, (w, lo, nl, owner) in enumerate(items):
            @pl.when(c == owner)
            def _():
                own = win(full, w, lo, nl, 2 * x + y)
                for k, (cx, cy) in enumerate(_other_chips(x, y)):
                    theirs = win(full, w, lo, nl, 2 * cx + cy)
                    _rcopy(own, own, ssem.at[3 * it + k], rsem.at[3 * it + k], (cx, cy, c)).wait_send()
                    _rcopy(theirs, theirs, s2sem.at[3 * it + k], r2sem.at[3 * it + k], sibling).wait_send()

            @pl.when(c != owner)
            def _():
                for k, (cx, cy) in enumerate(_other_chips(x, y)):
                    theirs = win(full, w, lo, nl, 2 * cx + cy)
                    _rcopy(theirs, theirs, s2sem.at[3 * it + k], r2sem.at[3 * it + k], sibling).wait_recv()

    shapes = tuple(jax.ShapeDtypeStruct(f.shape, f.dtype) for f in fulls)
    return Exchange(tuple(fulls), shapes, True, (3 * n,) * 4, start, finish)


def pair_exchange(grads, owners):
    n = len(grads)

    def start(gr, land, sems):
        x, y, c = _position()
        for it, owner in enumerate(owners):
            @pl.when(c != owner)
            def _():
                _rcopy(gr[it], land[it], sems[0].at[it], sems[1].at[it], (x, y, 1 - c)).start()

    def finish(gr, land, sems):
        x, y, c = _position()
        for it, owner in enumerate(owners):
            cp = _rcopy(gr[it], land[it], sems[0].at[it], sems[1].at[it], (x, y, 1 - c))
            pl.when(c != owner)(cp.wait_send)
            pl.when(c == owner)(cp.wait_recv)

    shapes = tuple(jax.ShapeDtypeStruct(g.shape, g.dtype) for g in grads)
    return Exchange(tuple(grads), shapes, False, (n, n), start, finish)


def pair_add(name, g, landed, owner, c_arr):
    R, C = g.shape
    tr = _tile(R, max(16, (1 << 20) // C // 16 * 16), unit=16)

    def body(c_ref, g_ref, l_ref, o_ref):
        @pl.when(c_ref[0] == owner)
        def _():
            o_ref[...] = (g_ref[...].astype(F32) + l_ref[...].astype(F32)).astype(o_ref.dtype)

    spec = pl.BlockSpec((tr, C), lambda i, c: (jnp.where(c[0] == owner, i, 0), 0))
    grid_spec = pltpu.PrefetchScalarGridSpec(num_scalar_prefetch=1, grid=(R // tr,), in_specs=[spec, spec],
                                             out_specs=spec)
    return pl.pallas_call(body, name=name, grid_spec=grid_spec, out_shape=jax.ShapeDtypeStruct(g.shape, g.dtype),
                          compiler_params=_params())(c_arr, g, landed)


def _window2(ref, dim, jj, rs, cs):
    if dim == 1:
        return ref.at[pl.ds(pl.multiple_of(jj * rs, 16), rs), :]
    return ref.at[:, pl.ds(pl.multiple_of(jj * cs, LANE), cs)]


def chip_exchange(partials, dims, shard_shapes, owners):
    n = len(partials)

    def copies(ps, land, sems, it):
        x, y, c = _position()
        rs, cs = shard_shapes[it]
        return [_rcopy(_window2(ps[it], dims[it], 2 * cx + cy, rs, cs), land[it].at[k],
                       sems[0].at[3 * it + k], sems[1].at[3 * it + k], (cx, cy, c))
                for k, (cx, cy) in enumerate(_other_chips(x, y))]

    def start(ps, land, sems):
        c = lax.axis_index('c')
        for it, owner in enumerate(owners):
            @pl.when(c == owner)
            def _():
                for cp in copies(ps, land, sems, it):
                    cp.start()

    def finish(ps, land, sems):
        c = lax.axis_index('c')
        for it, owner in enumerate(owners):
            @pl.when(c == owner)
            def _():
                for cp in copies(ps, land, sems, it):
                    cp.wait()

    shapes = tuple(jax.ShapeDtypeStruct((3,) + tuple(s), p.dtype) for p, s in zip(partials, shard_shapes))
    return Exchange(tuple(partials), shapes, False, (3 * n, 3 * n), start, finish)


def chip_sum(name, partial, landed, dim, buf, depth, layer, owner, j_arr, c_arr):
    _, rs, cs = landed.shape
    tr = _tile(rs, max(16, (1 << 19) // cs // 16 * 16), unit=16)
    has_buf = buf is not None

    def body(*refs):
        core_ref, p_ref, a_ref, b_ref, c_ref = refs[1:6]
        o_ref = refs[6 + has_buf]

        @pl.when(core_ref[0] == owner)
        def _():
            o_ref[...] = ((p_ref[...].astype(F32) + a_ref[...].astype(F32)) + b_ref[...].astype(F32)) + c_ref[...].astype(F32)

    def on(c, index):
        return jnp.where(c[0] == owner, index, 0)

    if dim == 1:
        own = pl.BlockSpec((tr, cs), lambda i, j, c: (on(c, j[0] * (rs // tr) + i), 0))
    else:
        own = pl.BlockSpec((tr, cs), lambda i, j, c: (on(c, i), on(c, j[0])))
    arrived = [pl.BlockSpec((None, tr, cs), lambda i, j, c, k=k: (k, on(c, i), 0)) for k in range(3)]
    grid_spec = pltpu.PrefetchScalarGridSpec(
        num_scalar_prefetch=2, grid=(rs // tr,), in_specs=[own] + arrived + ([ANY_SPEC] if has_buf else []),
        out_specs=pl.BlockSpec((None, tr, cs), lambda i, j, c: (layer, on(c, i), 0)))
    args = (j_arr, c_arr, partial, landed, landed, landed) + ((buf,) if has_buf else ())
    return pl.pallas_call(body, name=name, grid_spec=grid_spec,
                          out_shape=jax.ShapeDtypeStruct((depth, rs, cs), F32),
                          input_output_aliases={6: 0} if has_buf else {},
                          compiler_params=_params())(*args)


def share_exchange(bufs, items):
    n = len(items)

    def copy(full, sems, it):
        w, lo, nl, _ = items[it]
        x, y, c = _position()
        own = full[w].at[pl.ds(lo, nl)]
        return _rcopy(own, own, sems[0].at[it], sems[1].at[it], (x, y, 1 - c))

    def start(_, full, sems):
        c = lax.axis_index('c')
        for it in range(n):
            pl.when(c == items[it][3])(copy(full, sems, it).start)

    def finish(_, full, sems):
        c = lax.axis_index('c')
        for it in range(n):
            cp = copy(full, sems, it)
            pl.when(c == items[it][3])(cp.wait_send)
            pl.when(c != items[it][3])(cp.wait_recv)

    shapes = tuple(jax.ShapeDtypeStruct(f.shape, f.dtype) for f in bufs)
    return Exchange(tuple(bufs), shapes, True, (n, n), start, finish)


def exchange_small(name, pack, reduce_all):
    R = pack.shape[0]
    n_slots = N_DEVICES if reduce_all else N_CHIPS
    n_peers = n_slots - 1

    def body(p_ref, o_ref, buf, ssem, rsem):
        x, y, c = _position()
        if reduce_all:
            me = 4 * x + 2 * y + c
            peers = [(x ^ (k >> 2 & 1), y ^ (k >> 1 & 1), c ^ (k & 1)) for k in range(1, N_DEVICES)]
        else:
            me = 2 * x + y
            peers = [(cx, cy, c) for cx, cy in _other_chips(x, y)]
        buf[me] = p_ref[...]
        copies = []
        for k, peer in enumerate(peers):
            cp = pltpu.make_async_remote_copy(src_ref=p_ref, dst_ref=buf.at[me], send_sem=ssem.at[k],
                                              recv_sem=rsem.at[k], device_id=peer, device_id_type=MESH)
            cp.start()
            copies.append(cp)
        for k, (px, py, pc) in enumerate(peers):
            slot = 4 * px + 2 * py + pc if reduce_all else 2 * px + py
            pltpu.make_async_remote_copy(src_ref=p_ref, dst_ref=buf.at[slot], send_sem=ssem.at[k],
                                         recv_sem=rsem.at[k], device_id=(px, py, pc), device_id_type=MESH).wait_recv()
        for cp in copies:
            cp.wait_send()
        if reduce_all:
            acc = buf[0]
            for d in range(1, N_DEVICES):
                acc = acc + buf[d]
            o_ref[...] = acc
        else:
            o_ref[...] = buf[...]

    vmem = pl.BlockSpec(memory_space=pltpu.VMEM)
    out_shape = jax.ShapeDtypeStruct((R, LANE) if reduce_all else (N_CHIPS, R, LANE), F32)
    return pl.pallas_call(
        body, name=name, out_shape=out_shape, in_specs=[vmem], out_specs=vmem,
        scratch_shapes=[pltpu.VMEM((n_slots, R, LANE), F32), pltpu.SemaphoreType.DMA((n_peers,)),
                        pltpu.SemaphoreType.DMA((n_peers,))],
        compiler_params=_params())(pack)


def _pack(arrays):
    flat = jnp.concatenate([a.reshape(-1).astype(F32) for a in arrays])
    n = flat.shape[0]
    rows = -(-n // LANE)
    rows = -(-rows // 8) * 8
    return jnp.pad(flat, (0, rows * LANE - n)).reshape(rows, LANE)


def _unpack(pack, shapes, lead=()):
    flat = pack.reshape(lead + (-1,))
    out, off = [], 0
    for s in shapes:
        n = math.prod(s)
        out.append(flat[..., off:off + n].reshape(lead + tuple(s)))
        off += n
    return out


def kernel(x, p, positions, mla_w_in, mla_q_norm, mla_kv_norm, mla_w_uq, mla_w_uk, mla_w_uv, mla_w_o, gla_w_in, gla_w_a2, gla_b_a, gla_o_norm, gla_w_o, ln1_g, ln1_b, ln2_g, ln2_b, ffn_w_up, ffn_conv_w, ffn_conv_b, ffn_w_down, ple_w_proj, ple_w_gate, ple_b_gate, loss_target, m_mla_w_in, m_mla_q_norm, m_mla_kv_norm, m_mla_w_uq, m_mla_w_uk, m_mla_w_uv, m_mla_w_o, m_gla_w_in, m_gla_w_a2, m_gla_b_a, m_gla_o_norm, m_gla_w_o, m_ln1_g, m_ln1_b, m_ln2_g, m_ln2_b, m_ffn_w_up, m_ffn_conv_w, m_ffn_conv_b, m_ffn_w_down, m_ple_w_proj, m_ple_w_gate, m_ple_b_gate, v_mla_w_in, v_mla_q_norm, v_mla_kv_norm, v_mla_w_uq, v_mla_w_uk, v_mla_w_uv, v_mla_w_o, v_gla_w_in, v_gla_w_a2, v_gla_b_a, v_gla_o_norm, v_gla_w_o, v_ln1_g, v_ln1_b, v_ln2_g, v_ln2_b, v_ffn_w_up, v_ffn_conv_w, v_ffn_conv_b, v_ffn_w_down, v_ple_w_proj, v_ple_w_gate, v_ple_b_gate):
    given = dict(locals())
    S, D = x.shape[1], x.shape[2]
    QL, KL = mla_q_norm.shape[1], mla_kv_norm.shape[1]
    H = mla_w_uq.shape[2] * N_CHIPS // (MLA_NOPE + MLA_ROPE)
    QK, VD = gla_b_a.shape[1] * N_CHIPS, gla_o_norm.shape[1] * N_CHIPS
    FF = ffn_w_down.shape[1] * N_CHIPS
    GIN = 2 * QK + 2 * VD + GLA_GATE_RANK
    GIN_PAD = 2 * QK + 2 * VD + LANE
    MIN = QL + KL + MLA_ROPE
    MIN_PAD = QL + KL + LANE
    a_blk = (2 * QK + 2 * VD) // LANE
    r_blk = (2 * QK + VD) // VD
    xi, yi, ci = _position()
    chip = 2 * xi + yi
    c_arr = jnp.reshape(ci, (1,)).astype(jnp.int32)
    j_arr = jnp.reshape(chip, (1,)).astype(jnp.int32)

    names = [n for n, _ in BIG]
    big_dims = [1 if n == 'gla_w_in' else d for n, d in BIG]
    shapes = [given[n].shape for n in names]

    def layer_of(name, idx):
        return 2 * idx if name.startswith('mla') else 2 * idx + 1 if name.startswith('gla') else idx

    def owner_of(layer):
        return layer % 2

    def layer_runs(wanted):
        out = []
        for w, n in enumerate(names):
            for owner in (0, 1):
                idxs = [k for k in range(shapes[w][0])
                        if owner_of(layer_of(n, k)) == owner and wanted(n, layer_of(n, k))]
                for k in idxs:
                    if out and out[-1][0] == w and out[-1][3] == owner and out[-1][1] + out[-1][2] == k:
                        out[-1] = (w, out[-1][1], out[-1][2] + 1, owner)
                    else:
                        out.append((w, k, 1, owner))
        return out

    before_attention = ('mla_w_in', 'mla_w_uq', 'mla_w_uk', 'mla_w_uv')
    gather_phase = {0: lambda n, l: l == 0 and n not in before_attention or l in (1, 2),
                    2: lambda n, l: l == 3}
    placed = [place_shard(f'place_{n}', given[n], d, j_arr) for n, d in zip(names, big_dims)]
    first = run_exchange('gather_first', gather_exchange(
        placed, big_dims, shapes, layer_runs(lambda n, l: l == 0 and n in before_attention)))
    full = dict(zip(names, first))
    small_sharded = [n for n, d in SMALL if d is not None]
    spack = exchange_small('gather_small', _pack([given[n] for n in small_sharded]), False)
    parts = _unpack(spack, [given[n].shape for n in small_sharded], lead=(N_CHIPS,))
    for n, part in zip(small_sharded, parts):
        d = dict(SMALL)[n]
        full[n] = jnp.concatenate([part[k] for k in range(N_CHIPS)], axis=d)

    def mla_in_weights(j):
        w_in = jnp.pad(full['mla_w_in'][j], ((0, 0), (0, MIN_PAD - MIN)))
        w_uq = full['mla_w_uq'][j].reshape(QL, H, MLA_NOPE + MLA_ROPE)
        w_uq = jnp.pad(w_uq, ((0, 0), (0, 0), (0, HEAD_PAD - MLA_NOPE - MLA_ROPE))).reshape(QL, H * HEAD_PAD)
        return w_in, w_uq

    def gla_in_weight(j):
        w = full['gla_w_in'][j].reshape(N_CHIPS, D, GIN // N_CHIPS).transpose(1, 0, 2).reshape(D, GIN)
        return jnp.pad(w, ((0, 0), (0, GIN_PAD - GIN)))

    w_a2 = jnp.pad(full['gla_w_a2'], ((0, 0), (0, LANE - GLA_GATE_RANK), (0, 0))).astype(MXU_DTYPE)
    cw_u, cw_g = full['ffn_conv_w'][:, :, :FF], full['ffn_conv_w'][:, :, FF:]
    cb_u, cb_g = ffn_conv_b[:, None, :FF], ffn_conv_b[:, None, FF:]

    pos_row = positions[0]
    cid = pos_row // CHUNK
    tab_q = rope_tables(pos_row, MLA_NOPE)
    tab_k = rope_tables(pos_row, 0)

    def row1(a, i):
        return a[i:i + 1]

    saved = []
    xa = x[0]
    xb, xbt = xa, None
    for i in range(DEPTH):
        j = i // 2
        sv = dict(x=xa, xb=xb, xbt=xbt)
        if i % 2 == 0:
            w_in, w_uq = mla_in_weights(j)
            h = mm(f'mla_in_{i}', xb, w_in, 'nn', F32)
            cq, ckv, kr = mla_pre_fwd(f'mla_pre_{i}', h, row1(mla_q_norm, j), row1(mla_kv_norm, j), tab_k, QL, KL)
            qpre = mm(f'mla_uq_{i}', cq, w_uq, 'nn', F32)
            q = rope_q(f'mla_rope_{i}', qpre, tab_q, H, MXU_DTYPE)
            kn = mm(f'mla_uk_{i}', ckv, full['mla_w_uk'], 'nn', MXU_DTYPE, b_pre=(j,))
            vv = mm(f'mla_uv_{i}', ckv, full['mla_w_uv'], 'nn', MXU_DTYPE, b_pre=(j,))
            rider = gather_exchange([full[n] for n in names], big_dims, shapes, layer_runs(gather_phase[i]))
            (o, lse, ot), gathered = attn_fwd(f'mla_attn_{i}', q, kn, kr, vv, cid, H, rider=rider)
            full.update(zip(names, gathered))
            mix = mm(f'mla_o_{i}', o, full['mla_w_o'], 'nn', F32, b_pre=(j,))
            sv.update(h=h, cq=cq, ckv=ckv, kr=kr, q=q, kn=kn, v=vv, o=o, ot=ot, lse=lse, w_in=w_in, w_uq=w_uq)
        else:
            w_gin = gla_in_weight(j)
            sv.update(w_gin=w_gin)
            h = mm(f'gla_in_{i}', xb, w_gin, 'nn', F32)
            qs, la = gla_pre_fwd(f'gla_pre_{i}', h, w_a2[j], row1(full['gla_b_a'], j), QK, a_blk)
            o, states = gla_fwd(f'gla_scan_{i}', qs, h, la, QK, VD)
            og, ogt = gla_post_fwd(f'gla_post_{i}', o, h, row1(full['gla_o_norm'], j), VD, r_blk)
            mix = mm(f'gla_o_{i}', og, full['gla_w_o'], 'nn', F32, b_pre=(j,))
            sv.update(h=h, qs=qs, la=la, o=o, states=states, ot=ogt)
        x1, x1b, x1bt = ln_fwd(f'ln1_{i}', xa, mix, row1(ln1_g, i), row1(ln1_b, i))
        hu = mm(f'ffn_up_u_{i}', x1b, full['ffn_w_up'], 'nn', F32, b_pre=(i,), b_win=(0, D, 0, FF))
        hg = mm(f'ffn_up_g_{i}', x1b, full['ffn_w_up'], 'nn', F32, b_pre=(i,), b_win=(0, D, FF, FF))
        act, actt = conv_fwd(f'ffn_conv_{i}', hu, hg, cw_u[i], cw_g[i], cb_u[i], cb_g[i])
        f = mm(f'ffn_down_{i}', act, full['ffn_w_down'], 'nn', F32, b_pre=(i,))
        x2, x2b, x2bt = ln_fwd(f'ln2_{i}', x1, f, row1(ln2_g, i), row1(ln2_b, i))
        gp = mm(f'ple_gate_{i}', x2b, full['ple_w_gate'], 'nn', F32, b_pre=(i,))
        pp = mm(f'ple_proj_{i}', p, full['ple_w_proj'], 'nn', F32, a_pre=(i, 0), b_pre=(i,))
        x3, x3b, x3bt = ple_fwd(f'ple_{i}', x2, gp, pp, row1(ple_b_gate, i))
        sv.update(mix=mix, x1=x1, x1bt=x1bt, hu=hu, hg=hg, actt=actt, f=f, x2=x2, x2bt=x2bt, gp=gp, pp=pp)
        saved.append(sv)
        xa, xb, xbt = x3, x3b, x3bt

    dy, sq = loss_head('loss_head', xa, loss_target[0])
    loss_part = (0.5 / D) * jnp.sum(sq)

    small_g = {n: [None] * given[n].shape[0] for n, _ in SMALL}
    tab_q_t = tuple(t.T for t in tab_q)
    shard_buf = {n: None for n in names}

    def reduce_begin(tag, items):
        owners = [owner_of(layer) for _, _, layer, _ in items]
        landed = run_exchange(f'pair_exchange_{tag}', pair_exchange([g for *_, g in items], owners))
        partial = [pair_add(f'pair_add_{n}_{layer}', g, l, o, c_arr)
                   for (n, _, layer, g), l, o in zip(items, landed, owners)]
        dims = [big_dims[names.index(n)] for n, *_ in items]
        exchange = chip_exchange(partial, dims, [given[n].shape[1:] for n, *_ in items], owners)
        return exchange, (items, partial, dims, owners)

    def reduce_end(context, arrived):
        items, partial, dims, owners = context
        for (n, idx, layer, _), pt, ar, d, o in zip(items, partial, arrived, dims, owners):
            shard_buf[n] = chip_sum(f'chip_sum_{n}_{layer}', pt, ar, d, shard_buf[n], given[n].shape[0], idx, o,
                                    j_arr, c_arr)

    def wgrad(name, a, b, a_pre=()):
        return mm(f'd_{name}_{i}', a, b, 'tn', MXU_DTYPE, a_pre=a_pre)

    def wgrad_t(name, a_t, b):
        return mm(f'd_{name}_{i}', a_t, b, 'nn', MXU_DTYPE)

    def layer_items(layer_grads):
        return [(n, i if n.startswith(('ffn', 'ple')) else i // 2, i, g) for n, g in layer_grads.items()]

    dA, ca, dB = dy, 1.0, None
    waiting = []
    for i in reversed(range(DEPTH)):
        j = i // 2
        sv = saved[i]
        lg = {}
        dx3, dpp, dgp, dbg = ple_bwd(f'ple_b_{i}', dA, ca, dB, sv['gp'], sv['pp'], row1(ple_b_gate, i))
        small_g['ple_b_gate'][i] = dbg
        lg['ple_w_proj'] = wgrad('ple_w_proj', p, dpp, a_pre=(i, 0))
        lg['ple_w_gate'] = wgrad_t('ple_w_gate', sv['x2bt'], dgp)
        dx2 = mm(f'ple_gate_b_{i}', dgp, full['ple_w_gate'], 'nt', F32, b_pre=(i,))
        dz2, dz2b, dg2, db2 = ln_bwd(f'ln2_b_{i}', sv['x1'], sv['f'], dx3, 1.0, dx2, row1(ln2_g, i))
        small_g['ln2_g'][i], small_g['ln2_b'][i] = dg2, db2
        lg['ffn_w_down'] = wgrad_t('ffn_w_down', sv['actt'], dz2b)
        dact = mm(f'ffn_down_b_{i}', dz2b, full['ffn_w_down'], 'nt', F32, b_pre=(i,))
        dhu, dhg, dcwu, dcwg, dcbu, dcbg = conv_bwd(f'ffn_conv_b_{i}', sv['hu'], sv['hg'], dact,
                                                   cw_u[i], cw_g[i], cb_u[i], cb_g[i])
        small_g['ffn_conv_w'][i] = jnp.concatenate([dcwu, dcwg], -1)
        small_g['ffn_conv_b'][i] = jnp.concatenate([dcbu, dcbg], -1)
        up = mm(f'd_ffn_w_up_u_{i}', sv['x1bt'], dhu, 'nn', MXU_DTYPE, out_stack=(None, (1, D, 2 * FF), 0, 0))
        up = mm(f'd_ffn_w_up_g_{i}', sv['x1bt'], dhg, 'nn', MXU_DTYPE, out_stack=(up, (1, D, 2 * FF), 0, FF))
        lg['ffn_w_up'] = up[0]
        dx1 = mm(f'ffn_up_bu_{i}', dhu, full['ffn_w_up'], 'nt', F32, b_pre=(i,), b_win=(0, D, 0, FF))
        dx1 = mm(f'ffn_up_bg_{i}', dhg, full['ffn_w_up'], 'nt', F32, b_pre=(i,), b_win=(0, D, FF, FF), acc_in=dx1)
        dz1, dz1b, dg1, db1 = ln_bwd(f'ln1_b_{i}', sv['x'], sv['mix'], dz2, DN_ALPHA, dx1, row1(ln1_g, i))
        small_g['ln1_g'][i], small_g['ln1_b'][i] = dg1, db1
        if i % 2 == 0:
            lg['mla_w_o'] = wgrad_t('mla_w_o', sv['ot'], dz1b)
            do = mm(f'mla_o_b_{i}', dz1b, full['mla_w_o'], 'nt', MXU_DTYPE, b_pre=(j,))
            waiting += layer_items(lg)
            lg = {}
            riding, context = reduce_begin(f'before_{i}', waiting)
            waiting = []
            (dq, dkn, dv, dkr), arrived = attn_bwd(f'mla_attn_b_{i}', sv['q'], sv['kn'], sv['kr'], sv['v'], sv['o'],
                                                   sv['lse'], do, cid, H, rider=riding)
            reduce_end(context, arrived)
            dqpre_t = rope_q_bwd(f'mla_rope_b_{i}', dq, tab_q_t, H)
            uq_t = mm(f'd_mla_w_uq_{i}', dqpre_t, sv['cq'], 'nn', MXU_DTYPE)
            lg['mla_w_uq'] = uq_t.T.reshape(QL, H, HEAD_PAD)[..., :MLA_NOPE + MLA_ROPE].reshape(QL, -1)
            lg['mla_w_uk'] = wgrad('mla_w_uk', sv['ckv'], dkn)
            lg['mla_w_uv'] = wgrad('mla_w_uv', sv['ckv'], dv)
            dcq = mm(f'mla_uq_b_{i}', dqpre_t, sv['w_uq'].T, 'tn', F32)
            dckv_a = mm(f'mla_uk_b_{i}', dkn, full['mla_w_uk'], 'nt', F32, b_pre=(j,))
            dckv_b = mm(f'mla_uv_b_{i}', dv, full['mla_w_uv'], 'nt', F32, b_pre=(j,))
            dh, dqn, dkvn = mla_pre_bwd(f'mla_pre_b_{i}', sv['h'], dcq, dckv_a, dckv_b, dkr,
                                        row1(mla_q_norm, j), row1(mla_kv_norm, j), tab_k, QL, KL, H)
            small_g['mla_q_norm'][j], small_g['mla_kv_norm'][j] = dqn, dkvn
            g_in = wgrad('mla_w_in', sv['xb'], dh) if sv['xbt'] is None else wgrad_t('mla_w_in', sv['xbt'], dh)
            lg['mla_w_in'] = g_in[:, :MIN]
            dmix = mm(f'mla_in_b_{i}', dh, sv['w_in'], 'nt', F32)
        else:
            lg['gla_w_o'] = wgrad_t('gla_w_o', sv['ot'], dz1b)
            dog = mm(f'gla_o_b_{i}', dz1b, full['gla_w_o'], 'nt', F32, b_pre=(j,))
            do, dr, don = gla_post_bwd(f'gla_post_b_{i}', sv['o'], sv['h'], dog, row1(full['gla_o_norm'], j), VD, r_blk)
            dq, dk, dv, dla = gla_bwd(f'gla_scan_b_{i}', sv['qs'], sv['h'], sv['la'], sv['states'], do, QK, VD)
            dh, dw2, dba = gla_dh(f'gla_dh_{i}', dq, dk, dv, dr, dla, sv['h'], w_a2[j], row1(full['gla_b_a'], j),
                                  QK, VD, a_blk, GIN_PAD)
            small_g['gla_o_norm'][j], small_g['gla_b_a'][j] = don, dba
            small_g['gla_w_a2'][j] = dw2[:GLA_GATE_RANK]
            g_in = wgrad_t('gla_w_in', sv['xbt'], dh)[:, :GIN]
            lg['gla_w_in'] = g_in.reshape(D, N_CHIPS, GIN // N_CHIPS).transpose(1, 0, 2).reshape(N_CHIPS * D, -1)
            dmix = mm(f'gla_in_b_{i}', dh, sv['w_gin'], 'nt', F32)
        dA, ca, dB = dz1, DN_ALPHA, dmix
        waiting += layer_items(lg)
    grad_x = axpy('grad_x', dA, ca, dB)[None]

    exchange, context = reduce_begin('last', waiting)
    reduce_end(context, run_exchange('chip_exchange_last', exchange))
    shared = run_exchange('share_with_sibling',
                          share_exchange([shard_buf[n] for n in names], layer_runs(lambda n, l: True)))
    grads = dict(zip(names, shared))

    small_names = [n for n, _ in SMALL]
    small_full = [jnp.concatenate([g.reshape((1,) + g.shape[-(given[n].ndim - 1):]) for g in small_g[n]], 0)
                  for n in small_names]
    pack = _pack([jnp.reshape(loss_part, (1,))] + [jnp.zeros((LANE - 1,), F32)] + small_full)
    red = exchange_small('reduce_small', pack, True)
    red_parts = _unpack(red, [(LANE,)] + [g.shape for g in small_full])
    loss = red_parts[0][0]
    for (n, d), g in zip(SMALL, red_parts[1:]):
        if d is not None:
            width = given[n].shape[d]
            g = lax.dynamic_slice_in_dim(g, chip * width, width, axis=d)
        grads[n] = g

    delta, new_m, new_v = {}, {}, {}
    for n in names:
        delta[n], new_m[n], new_v[n] = adamw(f'adamw_{n}', given[n], grads[n], given['m_' + n], given['v_' + n])
    packs = [_pack([given[pre + n] for n in small_names]) for pre in ('', 'm_', 'v_')]
    outs = adamw('adamw_small', packs[0], _pack([grads[n] for n in small_names]), packs[1], packs[2])
    small_shapes = [given[n].shape for n in small_names]
    for dst, out in zip((delta, new_m, new_v), outs):
        for n, a in zip(small_names, _unpack(out, small_shapes)):
            dst[n] = a

    return (loss, grad_x, *[grads[n] for n in WEIGHT_NAMES], *[delta[n] for n in WEIGHT_NAMES],
            *[new_m[n] for n in WEIGHT_NAMES], *[new_v[n] for n in WEIGHT_NAMES])
```

```python
import functools
import math
from typing import Callable, NamedTuple

import jax
import jax.numpy as jnp
from jax import lax
from jax.experimental import pallas as pl
from jax.experimental.pallas import tpu as pltpu

F32 = jnp.float32
MXU_DTYPE = jnp.bfloat16

DEPTH = 4
CHUNK = 64
Q_BLOCK = 128
MLA_NOPE = 128
MLA_ROPE = 64
MLA_V = 128
ROPE_THETA = 10000.0
GLA_HEADS = 4
GLA_GATE_RANK = 16
GLA_TAU = 16.0
DN_ALPHA = (2 * DEPTH) ** 0.25
EPS = 1e-5
NEG_INF = -1e30
ADAM_LR = 0.001
ADAM_B1 = 0.9
ADAM_B2 = 0.999
ADAM_EPS = 1e-08
ADAM_WD = 0.01
ADAM_STEP = 10
GELU_C = math.sqrt(2.0 / math.pi)
GELU_A = 0.044715

LANE = 128
HEAD_PAD = 2 * LANE
VMEM_LIMIT_BYTES = 48 * 1024 * 1024

MESH = pl.DeviceIdType.MESH
ANY_SPEC = pl.BlockSpec(memory_space=pl.ANY)

WEIGHT_NAMES = ('mla_w_in', 'mla_q_norm', 'mla_kv_norm', 'mla_w_uq', 'mla_w_uk', 'mla_w_uv', 'mla_w_o',
                'gla_w_in', 'gla_w_a2', 'gla_b_a', 'gla_o_norm', 'gla_w_o', 'ln1_g', 'ln1_b', 'ln2_g', 'ln2_b',
                'ffn_w_up', 'ffn_conv_w', 'ffn_conv_b', 'ffn_w_down', 'ple_w_proj', 'ple_w_gate', 'ple_b_gate')
BIG = (('mla_w_in', 1), ('mla_w_uq', 2), ('mla_w_uk', 2), ('mla_w_uv', 2), ('mla_w_o', 1), ('gla_w_in', 2),
       ('gla_w_o', 1), ('ffn_w_up', 2), ('ffn_w_down', 1), ('ple_w_proj', 2), ('ple_w_gate', 1))
SMALL = (('mla_q_norm', None), ('mla_kv_norm', None), ('gla_w_a2', 2), ('gla_b_a', 1), ('gla_o_norm', 1),
         ('ln1_g', None), ('ln1_b', None), ('ln2_g', None), ('ln2_b', None), ('ffn_conv_w', 2),
         ('ffn_conv_b', None), ('ple_b_gate', None))
N_CHIPS = 4
N_DEVICES = 8


def _params():
    return pltpu.CompilerParams(vmem_limit_bytes=VMEM_LIMIT_BYTES)


def _tile(n, cap, *offsets, unit=LANE):
    g = n
    for o in offsets:
        if o:
            g = math.gcd(g, o)
    best = 0
    for d in range(unit, min(g, cap) + 1, unit):
        if g % d == 0:
            best = d
    if best:
        return best
    assert not any(offsets), (n, offsets)
    return n


def _dot(a, b, dims):
    return lax.dot_general(a.astype(MXU_DTYPE), b.astype(MXU_DTYPE), (dims, ((), ())),
                           preferred_element_type=F32)


NN = ((1,), (0,))
NT = ((1,), (1,))
TN = ((0,), (0,))


def mm(name, a, b, mode, out_dtype, *, a_pre=(), b_pre=(), b_win=None, acc_in=None, out_stack=None,
       epilogue=None):
    a2 = a.shape[len(a_pre):]
    b2 = b.shape[len(b_pre):]
    br0, brn, bc0, bcn = b_win or (0, b2[0], 0, b2[1])
    if mode == 'nn':
        (M, K), N, dims = a2, bcn, NN
        assert brn == K
    elif mode == 'nt':
        (M, K), N, dims = a2, brn, NT
        assert bcn == K
    else:
        (K, M), N, dims = a2, bcn, TN
        assert brn == K
    oc0 = out_stack[3] if out_stack else 0
    tm = _tile(M, 1408)
    if mode == 'nn':
        tn, tk = _tile(N, 1408, bc0, oc0), _tile(K, 1408, br0)
    elif mode == 'nt':
        tn, tk = _tile(N, 1408, br0, oc0), _tile(K, 1408, bc0)
    else:
        tn, tk = _tile(N, 1408, bc0, oc0), _tile(K, 1024, br0)
    nk = K // tk
    grid = (M // tm, N // tn, nk)

    na, nb = len(a_pre), len(b_pre)
    if mode == 'tn':
        a_spec = pl.BlockSpec((None,) * na + (tk, tm), lambda i, j, k: a_pre + (k, i))
    else:
        a_spec = pl.BlockSpec((None,) * na + (tm, tk), lambda i, j, k: a_pre + (i, k))
    if mode == 'nt':
        b_spec = pl.BlockSpec((None,) * nb + (tn, tk), lambda i, j, k: b_pre + (j + br0 // tn, k + bc0 // tk))
    else:
        b_spec = pl.BlockSpec((None,) * nb + (tk, tn), lambda i, j, k: b_pre + (k + br0 // tk, j + bc0 // tn))
    in_specs, args = [a_spec, b_spec], [a, b]
    if acc_in is not None:
        in_specs.append(pl.BlockSpec((tm, tn), lambda i, j, k: (i, j)))
        args.append(acc_in)
    aliases = {}
    if out_stack is None:
        out_shape = jax.ShapeDtypeStruct((M, N), out_dtype)
        out_spec = pl.BlockSpec((tm, tn), lambda i, j, k: (i, j))
    else:
        buf, full_shape, lead, _ = out_stack
        out_shape = jax.ShapeDtypeStruct(full_shape, out_dtype)
        out_spec = pl.BlockSpec((None, tm, tn), lambda i, j, k: (lead, i, j + oc0 // tn))
        if buf is not None:
            aliases = {len(args): 0}
            in_specs.append(ANY_SPEC)
            args.append(buf)
    has_c, has_alias = acc_in is not None, bool(aliases)
    finish, row_arrays = epilogue or (None, ())
    n_rows_in = len(row_arrays)
    for r in row_arrays:
        in_specs.append(pl.BlockSpec((tm, r.shape[1]), lambda i, j, k: (i, 0)))
        args.append(r)

    def body(*refs):
        a_ref, b_ref = refs[0], refs[1]
        c_ref = refs[2] if has_c else None
        first_row = 2 + has_c + has_alias
        o_ref = refs[first_row + n_rows_in]

        def result(tile):
            if finish is not None:
                tile = finish(tile, *[r[...] for r in refs[first_row:first_row + n_rows_in]])
            return tile.astype(out_dtype)

        prod = _dot(a_ref[...], b_ref[...], dims)
        if nk == 1:
            if has_c:
                prod = prod + c_ref[...]
            o_ref[...] = result(prod)
            return
        acc_ref = refs[first_row + n_rows_in + 1]
        k = pl.program_id(2)

        @pl.when(k == 0)
        def _():
            acc_ref[...] = prod + c_ref[...] if has_c else prod

        @pl.when(k > 0)
        def _():
            acc_ref[...] += prod

        @pl.when(k == nk - 1)
        def _():
            o_ref[...] = result(acc_ref[...])

    scratch = [pltpu.VMEM((tm, tn), F32)] if nk > 1 else []
    return pl.pallas_call(body, out_shape=out_shape, grid=grid, in_specs=in_specs, out_specs=out_spec,
                          scratch_shapes=scratch, input_output_aliases=aliases, name=name,
                          compiler_params=_params())(*args)


def rowcall(name, body, n_rows, ts, row_ins, full_ins, row_outs, acc_outs=()):
    in_specs, args = [], []
    for arr, width, colblk, pre in row_ins:
        in_specs.append(pl.BlockSpec((None,) * len(pre) + (ts, width),
                                     lambda i, pre=pre, cb=colblk: pre + (i, cb)))
        args.append(arr)
    for arr in full_ins:
        in_specs.append(pl.BlockSpec(arr.shape, lambda i, nd=arr.ndim: (0,) * nd))
        args.append(arr)
    out_shape, out_specs = [], []
    for width, dtype in row_outs:
        out_shape.append(jax.ShapeDtypeStruct((n_rows, width), dtype))
        out_specs.append(pl.BlockSpec((ts, width), lambda i: (i, 0)))
    for shape in acc_outs:
        out_shape.append(jax.ShapeDtypeStruct(shape, F32))
        out_specs.append(pl.BlockSpec(shape, lambda i, nd=len(shape): (0,) * nd))
    n_in = len(args)

    def kern(*refs):
        body(pl.program_id(0), refs[:n_in], refs[n_in:])

    return pl.pallas_call(kern, out_shape=tuple(out_shape), grid=(n_rows // ts,), in_specs=in_specs,
                          out_specs=tuple(out_specs), name=name, compiler_params=_params())(*args)


def _row(arr, width=None, colblk=0, pre=()):
    return (arr, arr.shape[-1] if width is None else width, colblk, pre)


def _init_acc(i, refs):
    @pl.when(i == 0)
    def _():
        for r in refs:
            r[...] = jnp.zeros(r.shape, r.dtype)


def _colsum(v):
    return jnp.sum(v, axis=0, keepdims=True)


def _sigmoid(z):
    return 1.0 / (1.0 + jnp.exp(-z))


def _row_tile(S):
    return _tile(S, 256, unit=16)


def _ln_stats(x_ref, m_ref):
    z = DN_ALPHA * x_ref[...] + m_ref[...]
    mu = jnp.mean(z, -1, keepdims=True)
    zc = z - mu
    var = jnp.mean(zc * zc, -1, keepdims=True)
    r = lax.rsqrt(var + EPS)
    return zc * r, r


def ln_fwd(name, x, m, g, b):
    S, D = x.shape

    def body(i, ins, outs):
        x_ref, m_ref, g_ref, b_ref = ins
        xh, _ = _ln_stats(x_ref, m_ref)
        y = xh * g_ref[...] + b_ref[...]
        outs[0][...] = y
        outs[1][...] = y.astype(MXU_DTYPE)

    return rowcall(name, body, S, _row_tile(S), [_row(x), _row(m)], [g, b], [(D, F32), (D, MXU_DTYPE)])


def ln_bwd(name, x, m, dA, ca, dB, g):
    S, D = x.shape
    has_b = dB is not None

    def body(i, ins, outs):
        x_ref, m_ref, a_ref = ins[:3]
        g_ref = ins[-1]
        dz_ref, dzb_ref, dg_ref, db_ref = outs
        _init_acc(i, (dg_ref, db_ref))
        dy = ca * a_ref[...]
        if has_b:
            dy = dy + ins[3][...]
        xh, r = _ln_stats(x_ref, m_ref)
        dg_ref[...] += _colsum(dy * xh)
        db_ref[...] += _colsum(dy)
        dxh = dy * g_ref[...]
        dz = r * (dxh - jnp.mean(dxh, -1, keepdims=True) - xh * jnp.mean(dxh * xh, -1, keepdims=True))
        dz_ref[...] = dz
        dzb_ref[...] = dz.astype(MXU_DTYPE)

    rows = [_row(x), _row(m), _row(dA)] + ([_row(dB)] if has_b else [])
    return rowcall(name, body, S, _row_tile(S), rows, [g], [(D, F32), (D, MXU_DTYPE)], [(1, D), (1, D)])


def ple_fwd(name, x2, gp, pp, bias):
    S, D = x2.shape

    def body(i, ins, outs):
        x_ref, gp_ref, pp_ref, b_ref = ins
        y = x_ref[...] + _sigmoid(gp_ref[...] + b_ref[...]) * pp_ref[...]
        outs[0][...] = y
        outs[1][...] = y.astype(MXU_DTYPE)

    return rowcall(name, body, S, _row_tile(S), [_row(x2), _row(gp), _row(pp)], [bias],
                   [(D, F32), (D, MXU_DTYPE)])


def ple_bwd(name, dA, ca, dB, gp, pp, bias):
    S, D = gp.shape
    has_b = dB is not None

    def body(i, ins, outs):
        a_ref = ins[0]
        gp_ref, pp_ref, b_ref = ins[-3:]
        dx_ref, dpp_ref, dgp_ref, db_ref = outs
        _init_acc(i, (db_ref,))
        dx = ca * a_ref[...]
        if has_b:
            dx = dx + ins[1][...]
        gate = _sigmoid(gp_ref[...] + b_ref[...])
        dgp = dx * pp_ref[...] * gate * (1.0 - gate)
        dx_ref[...] = dx
        dpp_ref[...] = (dx * gate).astype(MXU_DTYPE)
        dgp_ref[...] = dgp.astype(MXU_DTYPE)
        db_ref[...] += _colsum(dgp)

    rows = [_row(dA)] + ([_row(dB)] if has_b else []) + [_row(gp), _row(pp)]
    return rowcall(name, body, S, _row_tile(S), rows, [bias],
                   [(D, F32), (D, MXU_DTYPE), (D, MXU_DTYPE)], [(1, D)])


def loss_head(name, y, target):
    S, D = y.shape

    def body(i, ins, outs):
        _init_acc(i, (outs[1],))
        e = ins[0][...] - ins[1][...]
        outs[0][...] = e * (1.0 / D)
        outs[1][...] += _colsum(e * e)

    return rowcall(name, body, S, _row_tile(S), [_row(y), _row(target)], [], [(D, F32)], [(1, D)])


def axpy(name, a, ca, b):
    S, D = a.shape

    def body(i, ins, outs):
        outs[0][...] = ca * ins[0][...] + ins[1][...]

    return rowcall(name, body, S, _row_tile(S), [_row(a), _row(b)], [], [(D, F32)])[0]


HALF_ROPE = MLA_ROPE // 2


def _rope(x, c, sa, sb):
    n = x.shape[-1]
    return x * c + pltpu.roll(x, n - HALF_ROPE, 1) * sa + pltpu.roll(x, HALF_ROPE, 1) * sb


def _rope_t(d, c, sa, sb):
    n = d.shape[-1]
    return d * c + pltpu.roll(d * sa, HALF_ROPE, 1) + pltpu.roll(d * sb, n - HALF_ROPE, 1)


def rope_tables(positions_row, n_lead):
    inv = 1.0 / (ROPE_THETA ** (jnp.arange(0, MLA_ROPE, 2, dtype=F32) / MLA_ROPE))
    ang = positions_row.astype(F32)[:, None] * inv
    cos, sin = jnp.cos(ang), jnp.sin(ang)
    S = cos.shape[0]
    z = jnp.zeros((S, HALF_ROPE), F32)
    tail = jnp.zeros((S, LANE - MLA_ROPE), F32)
    c = jnp.concatenate([jnp.ones((S, n_lead), F32), cos, cos, tail], -1)
    sa = jnp.concatenate([jnp.zeros((S, n_lead), F32), -sin, z, tail], -1)
    sb = jnp.concatenate([jnp.zeros((S, n_lead), F32), z, sin, tail], -1)
    return c, sa, sb


def mla_pre_fwd(name, h, qn, kvn, tab_k, QL, KL):
    S = h.shape[0]

    def body(i, ins, outs):
        h_ref, c_ref, sa_ref, sb_ref, qn_ref, kvn_ref = ins
        cq = h_ref[:, 0:QL]
        ckv = h_ref[:, QL:QL + KL]
        kr = h_ref[:, QL + KL:QL + KL + LANE]
        outs[0][...] = (cq * lax.rsqrt(jnp.mean(cq * cq, -1, keepdims=True) + EPS) * qn_ref[...]).astype(MXU_DTYPE)
        outs[1][...] = (ckv * lax.rsqrt(jnp.mean(ckv * ckv, -1, keepdims=True) + EPS) * kvn_ref[...]).astype(MXU_DTYPE)
        outs[2][...] = _rope(kr, c_ref[...], sa_ref[...], sb_ref[...]).astype(MXU_DTYPE)

    rows = [_row(h)] + [_row(t) for t in tab_k]
    return rowcall(name, body, S, _row_tile(S), rows, [qn, kvn],
                   [(QL, MXU_DTYPE), (KL, MXU_DTYPE), (LANE, MXU_DTYPE)])


def _rms_bwd(x, g, dy):
    r = lax.rsqrt(jnp.mean(x * x, -1, keepdims=True) + EPS)
    dg = _colsum(dy * x * r)
    dxg = dy * g
    dx = r * dxg - x * (r * r * r) * jnp.mean(dxg * x, -1, keepdims=True)
    return dx, dg


def mla_pre_bwd(name, h, dcq, dckv_a, dckv_b, dkr_heads, qn, kvn, tab_k, QL, KL, H):
    S, HW = h.shape

    def body(i, ins, outs):
        h_ref, dcq_ref, da_ref, db_ref, dkr_ref, c_ref, sa_ref, sb_ref, qn_ref, kvn_ref = ins
        dh_ref, dqn_ref, dkvn_ref = outs
        _init_acc(i, (dqn_ref, dkvn_ref))
        dx, dg = _rms_bwd(h_ref[:, 0:QL], qn_ref[...], dcq_ref[...])
        dh_ref[:, 0:QL] = dx.astype(MXU_DTYPE)
        dqn_ref[...] += dg
        dx, dg = _rms_bwd(h_ref[:, QL:QL + KL], kvn_ref[...], da_ref[...] + db_ref[...])
        dh_ref[:, QL:QL + KL] = dx.astype(MXU_DTYPE)
        dkvn_ref[...] += dg
        d = dkr_ref[:, 0:LANE]
        for hh in range(1, H):
            d = d + dkr_ref[:, hh * LANE:(hh + 1) * LANE]
        dh_ref[:, QL + KL:QL + KL + LANE] = _rope_t(d, c_ref[...], sa_ref[...], sb_ref[...]).astype(MXU_DTYPE)

    rows = [_row(h), _row(dcq), _row(dckv_a), _row(dckv_b), _row(dkr_heads)] + [_row(t) for t in tab_k]
    return rowcall(name, body, S, _row_tile(S), rows, [qn, kvn], [(HW, MXU_DTYPE)], [(1, QL), (1, KL)])


def rope_heads(tile, c, sa, sb):
    return jnp.concatenate([_rope(tile[:, h0:h0 + HEAD_PAD], c, sa, sb)
                            for h0 in range(0, tile.shape[1], HEAD_PAD)], axis=-1)


ATT_T_FWD = 1024
ATT_T_BWD = 512


def _att_mask_t(ck_col, cq_row, k0, q0, t):
    kpos = k0 + lax.broadcasted_iota(jnp.int32, (t, 1), 0)
    qpos = q0 + lax.broadcasted_iota(jnp.int32, (1, t), 1)
    return (ck_col <= cq_row) & (kpos <= (qpos | (Q_BLOCK - 1)))


def _transpose(a):
    return a.astype(F32).T.astype(a.dtype)


def hosted_call(body, *, name, grid, in_specs, out_specs, out_shape, scratch_shapes, args, rider):
    if rider is None:
        return pl.pallas_call(body, name=name, grid=grid, in_specs=in_specs, out_specs=out_specs,
                              out_shape=out_shape, scratch_shapes=scratch_shapes,
                              compiler_params=_params())(*args), ()
    n_in, n_out, n_scr = len(args), len(out_shape), len(scratch_shapes)
    r_in, r_out = len(rider.ins), len(rider.out_shapes)

    def hosted(*refs):
        ins, refs = refs[:n_in], refs[n_in:]
        rin, refs = refs[:r_in], refs[r_in:]
        outs, refs = refs[:n_out], refs[n_out:]
        rout, refs = refs[:r_out], refs[r_out:]
        scr, sems = refs[:n_scr], refs[n_scr:]
        ids = [pl.program_id(a) for a in range(len(grid))]
        first = functools.reduce(jnp.logical_and, [i == 0 for i in ids])
        last = functools.reduce(jnp.logical_and, [i == g - 1 for i, g in zip(ids, grid)])

        @pl.when(first)
        def _():
            rider.start(rin, rout, sems)

        body(*ins, *outs, *scr)

        @pl.when(last)
        def _():
            rider.finish(rin, rout, sems)

    results = pl.pallas_call(
        hosted, name=name, grid=grid, in_specs=list(in_specs) + [ANY_SPEC] * r_in,
        out_specs=tuple(out_specs) + (ANY_SPEC,) * r_out, out_shape=tuple(out_shape) + tuple(rider.out_shapes),
        scratch_shapes=list(scratch_shapes) + [pltpu.SemaphoreType.DMA((k,)) for k in rider.sem_counts],
        input_output_aliases={n_in + k: n_out + k for k in range(r_in)} if rider.in_place else {},
        compiler_params=pltpu.CompilerParams(vmem_limit_bytes=VMEM_LIMIT_BYTES, has_side_effects=True),
    )(*args, *rider.ins)
    return results[:n_out], results[n_out:]


def attn_fwd(name, q, kn, kr, v, cid, H, rider=None):
    S = q.shape[0]
    t = _tile(S, ATT_T_FWD)
    n = S // t
    scale = (MLA_NOPE + MLA_ROPE) ** -0.5
    cid_col, cid_blk = cid.reshape(S, 1), cid.reshape(n, 1, t)

    def body(q_ref, kn_ref, kr_ref, v_ref, cc_ref, cr_ref, o_ref, lse_ref, k_scr, vt_scr):
        i = pl.program_id(1)

        @pl.when(i == 0)
        def _():
            k_scr[:, 0:LANE] = kn_ref[...]
            k_scr[:, LANE:HEAD_PAD] = kr_ref[...]
            for jj in range(n):
                vt_scr[jj] = _transpose(v_ref[jj * t:(jj + 1) * t, :])

        qv = q_ref[...]
        cq = cr_ref[i]

        def update(j, carry, masked):
            m, l, acc = carry
            k0 = pl.multiple_of(j * t, t)
            s = _dot(k_scr[pl.ds(k0, t), :], qv, NT) * scale
            if masked:
                s = jnp.where(_att_mask_t(cc_ref[pl.ds(k0, t), :], cq, k0, i * t, t), s, NEG_INF)
            m_new = jnp.maximum(m, jnp.max(s, 0, keepdims=True))
            p = jnp.exp(s - m_new)
            alpha = jnp.exp(m - m_new)
            l = alpha * l + jnp.sum(p, 0, keepdims=True)
            acc = alpha * acc + _dot(vt_scr[j], p, NN)
            return m_new, l, acc

        init = (jnp.full((1, t), NEG_INF, F32), jnp.zeros((1, t), F32), jnp.zeros((MLA_V, t), F32))
        carry = lax.fori_loop(0, i, lambda j, c: update(j, c, False), init)
        m, l, acc = update(i, carry, True)
        o_ref[...] = (acc / l).T.astype(o_ref.dtype)
        lse_ref[...] = m + jnp.log(l)

    return hosted_call(
        body, name=name, grid=(H, n),
        in_specs=[pl.BlockSpec((t, HEAD_PAD), lambda h, i: (i, h)),
                  pl.BlockSpec((S, MLA_NOPE), lambda h, i: (0, h)),
                  pl.BlockSpec((S, LANE), lambda h, i: (0, 0)),
                  pl.BlockSpec((S, MLA_V), lambda h, i: (0, h)),
                  pl.BlockSpec((S, 1), lambda h, i: (0, 0)),
                  pl.BlockSpec(cid_blk.shape, lambda h, i: (0, 0, 0))],
        out_specs=(pl.BlockSpec((t, MLA_V), lambda h, i: (i, h)),
                   pl.BlockSpec((None, None, 1, t), lambda h, i: (h, i, 0, 0))),
        scratch_shapes=[pltpu.VMEM((S, HEAD_PAD), MXU_DTYPE), pltpu.VMEM((n, MLA_V, t), MXU_DTYPE)],
        out_shape=(jax.ShapeDtypeStruct((S, H * MLA_V), MXU_DTYPE), jax.ShapeDtypeStruct((H, n, 1, t), F32)),
        args=(q, kn, kr, v, cid_col, cid_blk), rider=rider)


def attn_bwd(name, q, kn, kr, v, o, lse, do, cid, H, rider=None):
    S = q.shape[0]
    t = _tile(S, ATT_T_BWD)
    n = S // t
    scale = (MLA_NOPE + MLA_ROPE) ** -0.5
    cid_col, cid_blk = cid.reshape(S, 1), cid.reshape(n, 1, t)
    lse = lse.reshape(H, n, 1, t)

    def body(q_ref, kn_ref, kr_ref, v_ref, o_ref, lse_ref, do_ref, cc_ref, cr_ref,
             dq_ref, dkn_ref, dv_ref, dkr_ref, delta_scr, dk_scr, dv_scr):
        j = pl.program_id(1)

        @pl.when(j == 0)
        def _():
            dq_ref[...] = jnp.zeros(dq_ref.shape, F32)
            for ii in range(n):
                sl = slice(ii * t, (ii + 1) * t)
                prod = do_ref[sl, :].astype(F32) * o_ref[sl, :].astype(F32)
                delta_scr[ii] = jnp.sum(prod.T, 0, keepdims=True)

        kv = jnp.concatenate([kn_ref[...], kr_ref[...]], axis=-1)
        kt = _transpose(kv)
        vv = v_ref[...]
        ck = cc_ref[...]
        k0 = j * t
        dk_scr[...] = jnp.zeros(dk_scr.shape, F32)
        dv_scr[...] = jnp.zeros(dv_scr.shape, F32)

        def update(i, masked):
            q0 = pl.multiple_of(i * t, t)
            qv = q_ref[pl.ds(q0, t), :]
            dov = do_ref[pl.ds(q0, t), :]
            s = _dot(kv, qv, NT) * scale
            if masked:
                s = jnp.where(_att_mask_t(ck, cr_ref[i], k0, q0, t), s, NEG_INF)
            p = jnp.exp(s - lse_ref[i])
            dv_scr[...] += _dot(p, dov, NN)
            dp = _dot(vv, dov, NT)
            ds = p * (dp - delta_scr[i]) * scale
            dk_scr[...] += _dot(ds, qv, NN)
            dq_ref[i] += _dot(kt, ds, NN)

        def step(i, carry):
            update(i, False)
            return carry

        update(j, True)
        lax.fori_loop(j + 1, n, step, 0)
        dkn_ref[...] = dk_scr[:, 0:LANE].astype(dkn_ref.dtype)
        dkr_ref[...] = dk_scr[:, LANE:HEAD_PAD]
        dv_ref[...] = dv_scr[...].astype(dv_ref.dtype)

    head_rows = lambda w: pl.BlockSpec((S, w), lambda h, j: (0, h))
    tile_rows = lambda w: pl.BlockSpec((t, w), lambda h, j: (j, h))
    return hosted_call(
        body, name=name, grid=(H, n),
        in_specs=[head_rows(HEAD_PAD), tile_rows(MLA_NOPE), pl.BlockSpec((t, LANE), lambda h, j: (j, 0)),
                  tile_rows(MLA_V), head_rows(MLA_V),
                  pl.BlockSpec((None, n, 1, t), lambda h, j: (h, 0, 0, 0)), head_rows(MLA_V),
                  pl.BlockSpec((t, 1), lambda h, j: (j, 0)),
                  pl.BlockSpec(cid_blk.shape, lambda h, j: (0, 0, 0))],
        out_specs=(pl.BlockSpec((None, n, HEAD_PAD, t), lambda h, j: (h, 0, 0, 0)),
                   tile_rows(MLA_NOPE), tile_rows(MLA_V), tile_rows(LANE)),
        scratch_shapes=[pltpu.VMEM((n, 1, t), F32), pltpu.VMEM((t, HEAD_PAD), F32), pltpu.VMEM((t, MLA_V), F32)],
        out_shape=(jax.ShapeDtypeStruct((H, n, HEAD_PAD, t), F32), jax.ShapeDtypeStruct((S, H * MLA_NOPE), MXU_DTYPE),
                   jax.ShapeDtypeStruct((S, H * MLA_V), MXU_DTYPE), jax.ShapeDtypeStruct((S, H * LANE), F32)),
        args=(q, kn, kr, v, o, lse, do, cid_col, cid_blk), rider=rider)


def rope_q_bwd(name, dq_t, tab_q_t, H):
    _, n, _, t = dq_t.shape

    def body(d_ref, c_ref, sa_ref, sb_ref, o_ref):
        c, sa, sb = c_ref[...], sa_ref[...], sb_ref[...]
        for hh in range(H):
            d = d_ref[hh]
            out = d * c + pltpu.roll(d * sa, HALF_ROPE, 0) + pltpu.roll(d * sb, HEAD_PAD - HALF_ROPE, 0)
            o_ref[hh * HEAD_PAD:(hh + 1) * HEAD_PAD, :] = out.astype(o_ref.dtype)

    tab = pl.BlockSpec((HEAD_PAD, t), lambda i: (0, i))
    return pl.pallas_call(
        body, name=name, grid=(n,), out_shape=jax.ShapeDtypeStruct((H * HEAD_PAD, n * t), MXU_DTYPE),
        in_specs=[pl.BlockSpec((H, None, HEAD_PAD, t), lambda i: (0, i, 0, 0)), tab, tab, tab],
        out_specs=pl.BlockSpec((H * HEAD_PAD, t), lambda i: (0, i)),
        compiler_params=_params())(dq_t, *tab_q_t)


GLA_GROUP = 8


def _prefix_rows(x):
    n = x.shape[0]
    row = lax.broadcasted_iota(jnp.int32, x.shape, 0)
    d = 1
    while d < n:
        x = x + jnp.where(row >= d, pltpu.roll(x, d, 0), 0.0)
        d *= 2
    return x


def _suffix_rows(x):
    n = x.shape[0]
    row = lax.broadcasted_iota(jnp.int32, x.shape, 0)
    d = 1
    while d < n:
        x = x + jnp.where(row < n - d, pltpu.roll(x, n - d, 0), 0.0)
        d *= 2
    return x


def _log_sigmoid(z):
    return jnp.minimum(z, 0.0) - jnp.log(1.0 + jnp.exp(-jnp.abs(z)))


def gla_pre_fwd(name, h, w2p, b_a, QK, a_blk):
    S = h.shape[0]
    dk = QK // GLA_HEADS

    def body(i, ins, outs):
        q_ref, a_ref, w_ref, b_ref = ins
        outs[0][...] = (q_ref[...] * (dk ** -0.5)).astype(MXU_DTYPE)
        z = _dot(a_ref[...], w_ref[...], NN) + b_ref[...]
        outs[1][...] = _log_sigmoid(z) / GLA_TAU

    return rowcall(name, body, S, _row_tile(S), [_row(h, QK, 0), _row(h, LANE, a_blk)], [w2p, b_a],
                   [(QK, MXU_DTYPE), (QK, F32)])


def _gla_specs(S, QK, VD, rows, gmap):
    dk, dv = QK // GLA_HEADS, VD // GLA_HEADS
    return dict(
        qs=pl.BlockSpec((rows, dk), lambda h, g: (gmap(g), h)),
        k=pl.BlockSpec((rows, dk), lambda h, g: (gmap(g), QK // dk + h)),
        v=pl.BlockSpec((rows, dv), lambda h, g: (gmap(g), 2 * QK // dv + h)),
        la=pl.BlockSpec((rows, dk), lambda h, g: (gmap(g), h)),
        o=pl.BlockSpec((rows, dv), lambda h, g: (gmap(g), h)))


def gla_fwd(name, qs, h, la, QK, VD):
    S = qs.shape[0]
    dk, dv = QK // GLA_HEADS, VD // GLA_HEADS
    n_chunks = S // CHUNK
    cg = min(GLA_GROUP, n_chunks)
    rows = cg * CHUNK
    sp = _gla_specs(S, QK, VD, rows, lambda g: g)

    def body(q_ref, k_ref, v_ref, la_ref, o_ref, st_ref, state):
        @pl.when(pl.program_id(1) == 0)
        def _():
            state[...] = jnp.zeros(state.shape, F32)

        for c in range(cg):
            sl = slice(c * CHUNK, (c + 1) * CHUNK)
            cum = _prefix_rows(la_ref[sl, :])
            tot = cum[CHUNK - 1:CHUNK, :]
            kdec = k_ref[sl, :] * jnp.exp(tot - cum)
            st = state[...] * jnp.exp(tot) + _dot(v_ref[sl, :], kdec, TN)
            state[...] = st
            st_ref[c] = st
            o_ref[sl, :] = _dot(q_ref[sl, :], st, NT)

    return pl.pallas_call(
        body, name=name, grid=(GLA_HEADS, n_chunks // cg),
        out_shape=(jax.ShapeDtypeStruct((S, VD), F32), jax.ShapeDtypeStruct((GLA_HEADS, n_chunks, dv, dk), F32)),
        in_specs=[sp['qs'], sp['k'], sp['v'], sp['la']],
        out_specs=(sp['o'], pl.BlockSpec((None, cg, dv, dk), lambda h, g: (h, g, 0, 0))),
        scratch_shapes=[pltpu.VMEM((dv, dk), F32)],
        compiler_params=_params())(qs, h, h, la)


def gla_bwd(name, qs, h, la, states, do, QK, VD):
    S = qs.shape[0]
    dk, dv = QK // GLA_HEADS, VD // GLA_HEADS
    n_chunks = S // CHUNK
    cg = min(GLA_GROUP, n_chunks)
    ng = n_chunks // cg
    rows = cg * CHUNK
    rev = lambda g: ng - 1 - g
    sp = _gla_specs(S, QK, VD, rows, rev)

    def body(q_ref, k_ref, v_ref, la_ref, st_ref, prev_ref, do_ref, dq_ref, dk_ref, dv_ref, dla_ref, dst):
        g = pl.program_id(1)

        @pl.when(g == 0)
        def _():
            dst[...] = jnp.zeros(dst.shape, F32)

        first_group = (g == ng - 1).astype(F32)
        for c in reversed(range(cg)):
            sl = slice(c * CHUNK, (c + 1) * CHUNK)
            cum = _prefix_rows(la_ref[sl, :])
            tot = cum[CHUNK - 1:CHUNK, :]
            e = jnp.exp(tot - cum)
            kdec = k_ref[sl, :] * e
            decay = jnp.exp(tot)
            st = st_ref[c]
            st_prev = st_ref[c - 1] if c > 0 else prev_ref[0] * (1.0 - first_group)
            dov = do_ref[sl, :]
            qv = q_ref[sl, :]
            dq_ref[sl, :] = (_dot(dov, st, NN) * (dk ** -0.5)).astype(dq_ref.dtype)
            d = dst[...] + _dot(dov, qv, TN)
            ddecay = _colsum(d * st_prev)
            dkdec = _dot(v_ref[sl, :], d, NN)
            dv_ref[sl, :] = _dot(kdec, d, NT).astype(dv_ref.dtype)
            dk_ref[sl, :] = (dkdec * e).astype(dk_ref.dtype)
            darg = dkdec * kdec
            dtot = _colsum(darg) + ddecay * decay
            dla_ref[sl, :] = dtot - _suffix_rows(darg)
            dst[...] = d * decay

    prev_spec = pl.BlockSpec((None, 1, dv, dk), lambda h, g: (h, jnp.maximum(rev(g) * cg - 1, 0), 0, 0))
    return pl.pallas_call(
        body, name=name, grid=(GLA_HEADS, ng),
        out_shape=(jax.ShapeDtypeStruct((S, QK), MXU_DTYPE), jax.ShapeDtypeStruct((S, QK), MXU_DTYPE),
                   jax.ShapeDtypeStruct((S, VD), MXU_DTYPE), jax.ShapeDtypeStruct((S, QK), F32)),
        in_specs=[sp['qs'], sp['k'], sp['v'], sp['la'],
                  pl.BlockSpec((None, cg, dv, dk), lambda h, g: (h, rev(g), 0, 0)), prev_spec, sp['o']],
        out_specs=(sp['qs'], sp['qs'], sp['o'], sp['la']),
        scratch_shapes=[pltpu.VMEM((dv, dk), F32)],
        compiler_params=_params())(qs, h, h, la, states, states, do)


def _head_norm(o):
    mu = jnp.mean(o, -1, keepdims=True)
    oc = o - mu
    r = lax.rsqrt(jnp.mean(oc * oc, -1, keepdims=True) + EPS)
    return oc * r, r


def gla_post_fwd(name, o, h, o_norm, VD, r_blk):
    S = o.shape[0]
    dv = VD // GLA_HEADS

    def body(i, ins, outs):
        o_ref, r_ref, w_ref = ins
        for hh in range(GLA_HEADS):
            sl = slice(hh * dv, (hh + 1) * dv)
            xh, _ = _head_norm(o_ref[:, sl])
            r = r_ref[:, sl]
            outs[0][:, sl] = (xh * w_ref[:, sl] * (r * _sigmoid(r))).astype(MXU_DTYPE)

    return rowcall(name, body, S, _row_tile(S), [_row(o), _row(h, VD, r_blk)], [o_norm], [(VD, MXU_DTYPE)])[0]


def gla_post_bwd(name, o, h, dog, o_norm, VD, r_blk):
    S = o.shape[0]
    dv = VD // GLA_HEADS

    def body(i, ins, outs):
        o_ref, r_ref, dog_ref, w_ref = ins
        do_ref, dr_ref, dw_ref = outs
        _init_acc(i, (dw_ref,))
        for hh in range(GLA_HEADS):
            sl = slice(hh * dv, (hh + 1) * dv)
            xh, rs = _head_norm(o_ref[:, sl])
            r = r_ref[:, sl]
            w = w_ref[:, sl]
            dog = dog_ref[:, sl]
            sg = _sigmoid(r)
            dn = dog * (r * sg)
            dr_ref[:, sl] = (dog * (xh * w) * (sg * (1.0 + r * (1.0 - sg)))).astype(MXU_DTYPE)
            dw_ref[:, sl] += _colsum(dn * xh)
            dxh = dn * w
            do_ref[:, sl] = rs * (dxh - jnp.mean(dxh, -1, keepdims=True) - xh * jnp.mean(dxh * xh, -1, keepdims=True))

    return rowcall(name, body, S, _row_tile(S), [_row(o), _row(h, VD, r_blk), _row(dog)], [o_norm],
                   [(VD, F32), (VD, MXU_DTYPE)], [(1, VD)])


def gla_dh(name, dq, dk, dv, dr, dla, h, w2p, b_a, QK, VD, a_blk, HW):
    S = dq.shape[0]

    def body(i, ins, outs):
        dq_ref, dk_ref, dv_ref, dr_ref, dla_ref, a_ref, w_ref, b_ref = ins
        dh_ref, dw_ref, db_ref = outs
        _init_acc(i, (dw_ref, db_ref))
        a = a_ref[...]
        z = _dot(a, w_ref[...], NN) + b_ref[...]
        dz = dla_ref[...] * (1.0 / GLA_TAU) * _sigmoid(-z)
        dw_ref[...] += _dot(a, dz, TN)
        db_ref[...] += _colsum(dz)
        dh_ref[:, 0:QK] = dq_ref[...]
        dh_ref[:, QK:2 * QK] = dk_ref[...]
        dh_ref[:, 2 * QK:2 * QK + VD] = dv_ref[...]
        dh_ref[:, 2 * QK + VD:2 * QK + 2 * VD] = dr_ref[...]
        dh_ref[:, 2 * QK + 2 * VD:HW] = _dot(dz, w_ref[...], NT).astype(MXU_DTYPE)

    rows = [_row(dq), _row(dk), _row(dv), _row(dr), _row(dla), _row(h, LANE, a_blk)]
    return rowcall(name, body, S, _row_tile(S), rows, [w2p, b_a], [(HW, MXU_DTYPE)], [(LANE, QK), (1, QK)])


CONV_ROWS = 512
HALO = 8


def _gelu(x):
    return 0.5 * x * (1.0 + jnp.tanh(GELU_C * (x + GELU_A * x * x * x)))


def _gelu_grad(x):
    t = jnp.tanh(GELU_C * (x + GELU_A * x * x * x))
    return 0.5 * (1.0 + t) + 0.5 * x * (1.0 - t * t) * GELU_C * (1.0 + 3.0 * GELU_A * x * x)


def _shift_down(x, prev, d):
    row = lax.broadcasted_iota(jnp.int32, x.shape, 0)
    out = pltpu.roll(x, d, 0)
    for t in range(d):
        out = jnp.where(row == t, prev[HALO - d + t:HALO - d + t + 1, :], out)
    return out


def _shift_up(x, nxt, d):
    n = x.shape[0]
    row = lax.broadcasted_iota(jnp.int32, x.shape, 0)
    out = pltpu.roll(x, n - d, 0)
    for t in range(d):
        out = jnp.where(row == n - d + t, nxt[t:t + 1, :], out)
    return out


def _conv_taps(ref, r, rc):
    x = ref[r * rc:(r + 1) * rc, :]
    prev = ref[r * rc - HALO:r * rc, :] if r > 0 else jnp.zeros((HALO, x.shape[1]), F32)
    return x, _shift_down(x, prev, 1), _shift_down(x, prev, 2)


def _conv_apply(taps, w_ref, b_ref):
    x0, x1, x2 = taps
    return x2 * w_ref[0:1, :] + x1 * w_ref[1:2, :] + x0 * w_ref[2:3, :] + b_ref[...]


def conv_fwd(name, hu, hg, cw_u, cw_g, cb_u, cb_g):
    S, F = hu.shape
    tc = LANE
    rc = _tile(S, CONV_ROWS, unit=16)

    def body(u_ref, g_ref, wu_ref, wg_ref, bu_ref, bg_ref, a_ref):
        for r in range(S // rc):
            uc = _conv_apply(_conv_taps(u_ref, r, rc), wu_ref, bu_ref)
            gc = _conv_apply(_conv_taps(g_ref, r, rc), wg_ref, bg_ref)
            a_ref[r * rc:(r + 1) * rc, :] = (uc * _gelu(gc)).astype(a_ref.dtype)

    col = lambda rows: pl.BlockSpec((rows, tc), lambda j: (0, j))
    return pl.pallas_call(
        body, name=name, grid=(F // tc,), out_shape=jax.ShapeDtypeStruct((S, F), MXU_DTYPE),
        in_specs=[col(S), col(S), col(3), col(3), col(1), col(1)], out_specs=col(S),
        compiler_params=_params())(hu, hg, cw_u, cw_g, cb_u, cb_g)


def conv_bwd(name, hu, hg, da, cw_u, cw_g, cb_u, cb_g):
    S, F = hu.shape
    tc = LANE
    rc = _tile(S, CONV_ROWS, unit=16)
    nr = S // rc

    def body(u_ref, g_ref, da_ref, wu_ref, wg_ref, bu_ref, bg_ref,
             dhu_ref, dhg_ref, dwu_ref, dwg_ref, dbu_ref, dbg_ref, du_scr, dg_scr):
        dw = [[jnp.zeros((1, tc), F32) for _ in range(3)] for _ in range(2)]
        db = [jnp.zeros((1, tc), F32) for _ in range(2)]
        for r in range(nr):
            sl = slice(r * rc, (r + 1) * rc)
            ut = _conv_taps(u_ref, r, rc)
            gt = _conv_taps(g_ref, r, rc)
            uc = _conv_apply(ut, wu_ref, bu_ref)
            gc = _conv_apply(gt, wg_ref, bg_ref)
            dav = da_ref[sl, :]
            duc = dav * _gelu(gc)
            dgc = dav * uc * _gelu_grad(gc)
            du_scr[sl, :] = duc
            dg_scr[sl, :] = dgc
            for part, (taps, d) in enumerate(((ut, duc), (gt, dgc))):
                db[part] = db[part] + _colsum(d)
                for tap in range(3):
                    dw[part][tap] = dw[part][tap] + _colsum(taps[2 - tap] * d)
        for part, (w_out, b_out) in enumerate(((dwu_ref, dbu_ref), (dwg_ref, dbg_ref))):
            b_out[...] = db[part]
            for tap in range(3):
                w_out[tap:tap + 1, :] = dw[part][tap]
        for scr, w_ref, out in ((du_scr, wu_ref, dhu_ref), (dg_scr, wg_ref, dhg_ref)):
            for r in range(nr):
                d = scr[r * rc:(r + 1) * rc, :]
                nxt = scr[(r + 1) * rc:(r + 1) * rc + HALO, :] if r + 1 < nr else jnp.zeros((HALO, tc), F32)
                dh = d * w_ref[2:3, :] + _shift_up(d, nxt, 1) * w_ref[1:2, :] + _shift_up(d, nxt, 2) * w_ref[0:1, :]
                out[r * rc:(r + 1) * rc, :] = dh.astype(out.dtype)

    col = lambda rows: pl.BlockSpec((rows, tc), lambda j: (0, j))
    sds = jax.ShapeDtypeStruct
    return pl.pallas_call(
        body, name=name, grid=(F // tc,),
        out_shape=(sds((S, F), MXU_DTYPE), sds((S, F), MXU_DTYPE), sds((3, F), F32), sds((3, F), F32),
                   sds((1, F), F32), sds((1, F), F32)),
        in_specs=[col(S), col(S), col(S), col(3), col(3), col(1), col(1)],
        out_specs=(col(S), col(S), col(3), col(3), col(1), col(1)),
        scratch_shapes=[pltpu.VMEM((S, tc), F32), pltpu.VMEM((S, tc), F32)],
        compiler_params=_params())(hu, hg, da, cw_u, cw_g, cb_u, cb_g)


def adamw(name, w, g, m, v):
    *lead, R, C = w.shape
    tr = _tile(R, max(8, (1 << 19) // max(C, 1) // 8 * 8), unit=8)

    def body(w_ref, g_ref, m_ref, v_ref, d_ref, nm_ref, nv_ref):
        gv = g_ref[...]
        mn = ADAM_B1 * m_ref[...] + (1.0 - ADAM_B1) * gv
        vn = ADAM_B2 * v_ref[...] + (1.0 - ADAM_B2) * (gv * gv)
        m_hat = mn / (1.0 - ADAM_B1 ** ADAM_STEP)
        v_hat = vn / (1.0 - ADAM_B2 ** ADAM_STEP)
        d_ref[...] = -ADAM_LR * (m_hat / (jnp.sqrt(v_hat) + ADAM_EPS) + ADAM_WD * w_ref[...])
        nm_ref[...] = mn
        nv_ref[...] = vn

    if lead:
        spec, grid = pl.BlockSpec((None, tr, C), lambda l, i: (l, i, 0)), (lead[0], R // tr)
    else:
        spec, grid = pl.BlockSpec((tr, C), lambda i: (i, 0)), (R // tr,)
    shp = jax.ShapeDtypeStruct(w.shape, F32)
    return pl.pallas_call(body, name=name, grid=grid, out_shape=(shp, shp, shp), in_specs=[spec] * 4,
                          out_specs=(spec, spec, spec), compiler_params=_params())(w, g, m, v)


def _position():
    return lax.axis_index('x'), lax.axis_index('y'), lax.axis_index('c')


def _other_chips(x, y):
    return ((1 - x, y), (x, 1 - y), (1 - x, 1 - y))


def _window(ref, dim, lo, n_lead, jj, rs, cs):
    if dim == 1:
        return ref.at[pl.ds(lo, n_lead), pl.ds(pl.multiple_of(jj * rs, 16), rs), :]
    return ref.at[pl.ds(lo, n_lead), :, pl.ds(pl.multiple_of(jj * cs, LANE), cs)]


def _hbm_call(name, body, out_shapes, n_sems, args, aliases=None):
    return pl.pallas_call(
        body, name=name, out_shape=tuple(out_shapes), in_specs=[ANY_SPEC] * len(args),
        out_specs=tuple(ANY_SPEC for _ in out_shapes),
        scratch_shapes=[pltpu.SemaphoreType.DMA((n,)) for n in n_sems],
        input_output_aliases=aliases or {},
        compiler_params=pltpu.CompilerParams(has_side_effects=True))(*args)


def place_shard(name, shard, dim, j_arr):
    L, rs, cs = shard.shape
    full_shape = (L, rs * N_CHIPS, cs) if dim == 1 else (L, rs, cs * N_CHIPS)
    tr = _tile(rs, max(16, (1 << 19) // cs // 16 * 16), unit=16)

    def body(j_ref, s_ref, o_ref):
        o_ref[...] = s_ref[...].astype(o_ref.dtype)

    if dim == 1:
        out_spec = pl.BlockSpec((None, tr, cs), lambda l, i, j: (l, j[0] * (rs // tr) + i, 0))
    else:
        out_spec = pl.BlockSpec((None, tr, cs), lambda l, i, j: (l, i, j[0]))
    grid_spec = pltpu.PrefetchScalarGridSpec(
        num_scalar_prefetch=1, grid=(L, rs // tr),
        in_specs=[pl.BlockSpec((None, tr, cs), lambda l, i, j: (l, i, 0))], out_specs=out_spec)
    return pl.pallas_call(body, name=name, grid_spec=grid_spec,
                          out_shape=jax.ShapeDtypeStruct(full_shape, MXU_DTYPE),
                          compiler_params=_params())(j_arr, shard)


class Exchange(NamedTuple):
    ins: tuple
    out_shapes: tuple
    in_place: bool
    sem_counts: tuple
    start: Callable
    finish: Callable


def run_exchange(name, ex):
    n_in, n_out = len(ex.ins), len(ex.out_shapes)

    def body(*refs):
        ins, outs, sems = refs[:n_in], refs[n_in:n_in + n_out], refs[n_in + n_out:]
        ex.start(ins, outs, sems)
        ex.finish(ins, outs, sems)

    return _hbm_call(name, body, ex.out_shapes, ex.sem_counts, ex.ins,
                     {k: k for k in range(n_in)} if ex.in_place else None)


def _rcopy(src, dst, ssem, rsem, device):
    return pltpu.make_async_remote_copy(src_ref=src, dst_ref=dst, send_sem=ssem, recv_sem=rsem,
                                        device_id=device, device_id_type=MESH)


def gather_exchange(fulls, dims, shard_shapes, items):
    n = len(items)

    def win(full, w, lo, nl, jj):
        return _window(full[w], dims[w], lo, nl, jj, shard_shapes[w][1], shard_shapes[w][2])

    def start(_, full, sems):
        ssem, rsem = sems[0], sems[1]
        x, y, c = _position()
        for it, (w, lo, nl, owner) in enumerate(items):
            @pl.when(c == owner)
            def _():
                own = win(full, w, lo, nl, 2 * x + y)
                for k, (cx, cy) in enumerate(_other_chips(x, y)):
                    _rcopy(own, own, ssem.at[3 * it + k], rsem.at[3 * it + k], (cx, cy, c)).start()

    def finish(_, full, sems):
        ssem, rsem, s2sem, r2sem = sems
        x, y, c = _position()
        sibling = (x, y, 1 - c)
        for it, (w, lo, nl, owner) in enumerate(items):
            @pl.when(c == owner)
            def _():
                for k, (cx, cy) in enumerate(_other_chips(x, y)):
                    theirs = win(full, w, lo, nl, 2 * cx + cy)
                    _rcopy(theirs, theirs, ssem.at[3 * it + k], rsem.at[3 * it + k], (cx, cy, c)).wait_recv()
                    _rcopy(theirs, theirs, s2sem.at[3 * it + k], r2sem.at[3 * it + k], sibling).start()
        for it, (w, lo, nl, owner) in enumerate(items):
            @pl.when(c == owner)
            def _():
                own = win(full, w, lo, nl, 2 * x + y)
                for k, (cx, cy) in enumerate(_other_chips(x, y)):
                    theirs = win(full, w, lo, nl, 2 * cx + cy)
                    _rcopy(own, own, ssem.at[3 * it + k], rsem.at[3 * it + k], (cx, cy, c)).wait_send()
                    _rcopy(theirs, theirs, s2sem.at[3 * it + k], r2sem.at[3 * it + k], sibling).wait_send()

            @pl.when(c != owner)
            def _():
                for k, (cx, cy) in enumerate(_other_chips(x, y)):
                    theirs = win(full, w, lo, nl, 2 * cx + cy)
                    _rcopy(theirs, theirs, s2sem.at[3 * it + k], r2sem.at[3 * it + k], sibling).wait_recv()

    shapes = tuple(jax.ShapeDtypeStruct(f.shape, f.dtype) for f in fulls)
    return Exchange(tuple(fulls), shapes, True, (3 * n,) * 4, start, finish)


def pair_exchange(grads, owners):
    n = len(grads)

    def start(gr, land, sems):
        x, y, c = _position()
        for it, owner in enumerate(owners):
            @pl.when(c != owner)
            def _():
                _rcopy(gr[it], land[it], sems[0].at[it], sems[1].at[it], (x, y, 1 - c)).start()

    def finish(gr, land, sems):
        x, y, c = _position()
        for it, owner in enumerate(owners):
            cp = _rcopy(gr[it], land[it], sems[0].at[it], sems[1].at[it], (x, y, 1 - c))
            pl.when(c != owner)(cp.wait_send)
            pl.when(c == owner)(cp.wait_recv)

    shapes = tuple(jax.ShapeDtypeStruct(g.shape, g.dtype) for g in grads)
    return Exchange(tuple(grads), shapes, False, (n, n), start, finish)


def pair_add(name, g, landed, owner, c_arr):
    R, C = g.shape
    tr = _tile(R, max(16, (1 << 20) // C // 16 * 16), unit=16)

    def body(c_ref, g_ref, l_ref, o_ref):
        @pl.when(c_ref[0] == owner)
        def _():
            o_ref[...] = (g_ref[...].astype(F32) + l_ref[...].astype(F32)).astype(o_ref.dtype)

    spec = pl.BlockSpec((tr, C), lambda i, c: (jnp.where(c[0] == owner, i, 0), 0))
    grid_spec = pltpu.PrefetchScalarGridSpec(num_scalar_prefetch=1, grid=(R // tr,), in_specs=[spec, spec],
                                             out_specs=spec)
    return pl.pallas_call(body, name=name, grid_spec=grid_spec, out_shape=jax.ShapeDtypeStruct(g.shape, g.dtype),
                          compiler_params=_params())(c_arr, g, landed)


def _window2(ref, dim, jj, rs, cs):
    if dim == 1:
        return ref.at[pl.ds(pl.multiple_of(jj * rs, 16), rs), :]
    return ref.at[:, pl.ds(pl.multiple_of(jj * cs, LANE), cs)]


def chip_exchange(partials, dims, shard_shapes, owners):
    n = len(partials)

    def copies(ps, land, sems, it):
        x, y, c = _position()
        rs, cs = shard_shapes[it]
        return [_rcopy(_window2(ps[it], dims[it], 2 * cx + cy, rs, cs), land[it].at[k],
                       sems[0].at[3 * it + k], sems[1].at[3 * it + k], (cx, cy, c))
                for k, (cx, cy) in enumerate(_other_chips(x, y))]

    def start(ps, land, sems):
        c = lax.axis_index('c')
        for it, owner in enumerate(owners):
            @pl.when(c == owner)
            def _():
                for cp in copies(ps, land, sems, it):
                    cp.start()

    def finish(ps, land, sems):
        c = lax.axis_index('c')
        for it, owner in enumerate(owners):
            @pl.when(c == owner)
            def _():
                for cp in copies(ps, land, sems, it):
                    cp.wait()

    shapes = tuple(jax.ShapeDtypeStruct((3,) + tuple(s), p.dtype) for p, s in zip(partials, shard_shapes))
    return Exchange(tuple(partials), shapes, False, (3 * n, 3 * n), start, finish)


def chip_sum(name, partial, landed, dim, buf, depth, layer, owner, j_arr, c_arr):
    _, rs, cs = landed.shape
    tr = _tile(rs, max(16, (1 << 19) // cs // 16 * 16), unit=16)
    has_buf = buf is not None

    def body(*refs):
        core_ref, p_ref, a_ref, b_ref, c_ref = refs[1:6]
        o_ref = refs[6 + has_buf]

        @pl.when(core_ref[0] == owner)
        def _():
            o_ref[...] = ((p_ref[...].astype(F32) + a_ref[...].astype(F32)) + b_ref[...].astype(F32)) + c_ref[...].astype(F32)

    def on(c, index):
        return jnp.where(c[0] == owner, index, 0)

    if dim == 1:
        own = pl.BlockSpec((tr, cs), lambda i, j, c: (on(c, j[0] * (rs // tr) + i), 0))
    else:
        own = pl.BlockSpec((tr, cs), lambda i, j, c: (on(c, i), on(c, j[0])))
    arrived = [pl.BlockSpec((None, tr, cs), lambda i, j, c, k=k: (k, on(c, i), 0)) for k in range(3)]
    grid_spec = pltpu.PrefetchScalarGridSpec(
        num_scalar_prefetch=2, grid=(rs // tr,), in_specs=[own] + arrived + ([ANY_SPEC] if has_buf else []),
        out_specs=pl.BlockSpec((None, tr, cs), lambda i, j, c: (layer, on(c, i), 0)))
    args = (j_arr, c_arr, partial, landed, landed, landed) + ((buf,) if has_buf else ())
    return pl.pallas_call(body, name=name, grid_spec=grid_spec,
                          out_shape=jax.ShapeDtypeStruct((depth, rs, cs), F32),
                          input_output_aliases={6: 0} if has_buf else {},
                          compiler_params=_params())(*args)


def share_exchange(bufs, items):
    n = len(items)

    def copy(full, sems, it):
        w, lo, nl, _ = items[it]
        x, y, c = _position()
        own = full[w].at[pl.ds(lo, nl)]
        return _rcopy(own, own, sems[0].at[it], sems[1].at[it], (x, y, 1 - c))

    def start(_, full, sems):
        c = lax.axis_index('c')
        for it in range(n):
            pl.when(c == items[it][3])(copy(full, sems, it).start)

    def finish(_, full, sems):
        c = lax.axis_index('c')
        for it in range(n):
            cp = copy(full, sems, it)
            pl.when(c == items[it][3])(cp.wait_send)
            pl.when(c != items[it][3])(cp.wait_recv)

    shapes = tuple(jax.ShapeDtypeStruct(f.shape, f.dtype) for f in bufs)
    return Exchange(tuple(bufs), shapes, True, (n, n), start, finish)


def exchange_small(name, pack, reduce_all):
    R = pack.shape[0]
    n_slots = N_DEVICES if reduce_all else N_CHIPS
    n_peers = n_slots - 1

    def body(p_ref, o_ref, buf, ssem, rsem):
        x, y, c = _position()
        if reduce_all:
            me = 4 * x + 2 * y + c
            peers = [(x ^ (k >> 2 & 1), y ^ (k >> 1 & 1), c ^ (k & 1)) for k in range(1, N_DEVICES)]
        else:
            me = 2 * x + y
            peers = [(cx, cy, c) for cx, cy in _other_chips(x, y)]
        buf[me] = p_ref[...]
        copies = []
        for k, peer in enumerate(peers):
            cp = pltpu.make_async_remote_copy(src_ref=p_ref, dst_ref=buf.at[me], send_sem=ssem.at[k],
                                              recv_sem=rsem.at[k], device_id=peer, device_id_type=MESH)
            cp.start()
            copies.append(cp)
        for k, (px, py, pc) in enumerate(peers):
            slot = 4 * px + 2 * py + pc if reduce_all else 2 * px + py
            pltpu.make_async_remote_copy(src_ref=p_ref, dst_ref=buf.at[slot], send_sem=ssem.at[k],
                                         recv_sem=rsem.at[k], device_id=(px, py, pc), device_id_type=MESH).wait_recv()
        for cp in copies:
            cp.wait_send()
        if reduce_all:
            acc = buf[0]
            for d in range(1, N_DEVICES):
                acc = acc + buf[d]
            o_ref[...] = acc
        else:
            o_ref[...] = buf[...]

    vmem = pl.BlockSpec(memory_space=pltpu.VMEM)
    out_shape = jax.ShapeDtypeStruct((R, LANE) if reduce_all else (N_CHIPS, R, LANE), F32)
    return pl.pallas_call(
        body, name=name, out_shape=out_shape, in_specs=[vmem], out_specs=vmem,
        scratch_shapes=[pltpu.VMEM((n_slots, R, LANE), F32), pltpu.SemaphoreType.DMA((n_peers,)),
                        pltpu.SemaphoreType.DMA((n_peers,))],
        compiler_params=_params())(pack)


def _pack(arrays):
    flat = jnp.concatenate([a.reshape(-1).astype(F32) for a in arrays])
    n = flat.shape[0]
    rows = -(-n // LANE)
    rows = -(-rows // 8) * 8
    return jnp.pad(flat, (0, rows * LANE - n)).reshape(rows, LANE)


def _unpack(pack, shapes, lead=()):
    flat = pack.reshape(lead + (-1,))
    out, off = [], 0
    for s in shapes:
        n = math.prod(s)
        out.append(flat[..., off:off + n].reshape(lead + tuple(s)))
        off += n
    return out


def kernel(x, p, positions, mla_w_in, mla_q_norm, mla_kv_norm, mla_w_uq, mla_w_uk, mla_w_uv, mla_w_o, gla_w_in, gla_w_a2, gla_b_a, gla_o_norm, gla_w_o, ln1_g, ln1_b, ln2_g, ln2_b, ffn_w_up, ffn_conv_w, ffn_conv_b, ffn_w_down, ple_w_proj, ple_w_gate, ple_b_gate, loss_target, m_mla_w_in, m_mla_q_norm, m_mla_kv_norm, m_mla_w_uq, m_mla_w_uk, m_mla_w_uv, m_mla_w_o, m_gla_w_in, m_gla_w_a2, m_gla_b_a, m_gla_o_norm, m_gla_w_o, m_ln1_g, m_ln1_b, m_ln2_g, m_ln2_b, m_ffn_w_up, m_ffn_conv_w, m_ffn_conv_b, m_ffn_w_down, m_ple_w_proj, m_ple_w_gate, m_ple_b_gate, v_mla_w_in, v_mla_q_norm, v_mla_kv_norm, v_mla_w_uq, v_mla_w_uk, v_mla_w_uv, v_mla_w_o, v_gla_w_in, v_gla_w_a2, v_gla_b_a, v_gla_o_norm, v_gla_w_o, v_ln1_g, v_ln1_b, v_ln2_g, v_ln2_b, v_ffn_w_up, v_ffn_conv_w, v_ffn_conv_b, v_ffn_w_down, v_ple_w_proj, v_ple_w_gate, v_ple_b_gate):
    given = dict(locals())
    S, D = x.shape[1], x.shape[2]
    QL, KL = mla_q_norm.shape[1], mla_kv_norm.shape[1]
    H = mla_w_uq.shape[2] * N_CHIPS // (MLA_NOPE + MLA_ROPE)
    QK, VD = gla_b_a.shape[1] * N_CHIPS, gla_o_norm.shape[1] * N_CHIPS
    FF = ffn_w_down.shape[1] * N_CHIPS
    GIN = 2 * QK + 2 * VD + GLA_GATE_RANK
    GIN_PAD = 2 * QK + 2 * VD + LANE
    MIN = QL + KL + MLA_ROPE
    MIN_PAD = QL + KL + LANE
    a_blk = (2 * QK + 2 * VD) // LANE
    r_blk = (2 * QK + VD) // VD
    xi, yi, ci = _position()
    chip = 2 * xi + yi
    c_arr = jnp.reshape(ci, (1,)).astype(jnp.int32)
    j_arr = jnp.reshape(chip, (1,)).astype(jnp.int32)

    names = [n for n, _ in BIG]
    big_dims = [1 if n == 'gla_w_in' else d for n, d in BIG]
    shapes = [given[n].shape for n in names]

    def layer_of(name, idx):
        return 2 * idx if name.startswith('mla') else 2 * idx + 1 if name.startswith('gla') else idx

    def owner_of(layer):
        return layer % 2

    def layer_runs(wanted):
        out = []
        for w, n in enumerate(names):
            for owner in (0, 1):
                idxs = [k for k in range(shapes[w][0])
                        if owner_of(layer_of(n, k)) == owner and wanted(n, layer_of(n, k))]
                for k in idxs:
                    if out and out[-1][0] == w and out[-1][3] == owner and out[-1][1] + out[-1][2] == k:
                        out[-1] = (w, out[-1][1], out[-1][2] + 1, owner)
                    else:
                        out.append((w, k, 1, owner))
        return out

    before_attention = ('mla_w_in', 'mla_w_uq', 'mla_w_uk', 'mla_w_uv')
    gather_phase = {0: lambda n, l: l == 0 and n not in before_attention or l in (1, 2),
                    2: lambda n, l: l == 3}
    placed = [place_shard(f'place_{n}', given[n], d, j_arr) for n, d in zip(names, big_dims)]
    first = run_exchange('gather_first', gather_exchange(
        placed, big_dims, shapes, layer_runs(lambda n, l: l == 0 and n in before_attention)))
    full = dict(zip(names, first))
    small_sharded = [n for n, d in SMALL if d is not None]
    spack = exchange_small('gather_small', _pack([given[n] for n in small_sharded]), False)
    parts = _unpack(spack, [given[n].shape for n in small_sharded], lead=(N_CHIPS,))
    for n, part in zip(small_sharded, parts):
        d = dict(SMALL)[n]
        full[n] = jnp.concatenate([part[k] for k in range(N_CHIPS)], axis=d)

    def mla_in_weights(j):
        w_in = jnp.pad(full['mla_w_in'][j], ((0, 0), (0, MIN_PAD - MIN)))
        w_uq = full['mla_w_uq'][j].reshape(QL, H, MLA_NOPE + MLA_ROPE)
        w_uq = jnp.pad(w_uq, ((0, 0), (0, 0), (0, HEAD_PAD - MLA_NOPE - MLA_ROPE))).reshape(QL, H * HEAD_PAD)
        return w_in, w_uq

    def gla_in_weight(j):
        w = full['gla_w_in'][j].reshape(N_CHIPS, D, GIN // N_CHIPS).transpose(1, 0, 2).reshape(D, GIN)
        return jnp.pad(w, ((0, 0), (0, GIN_PAD - GIN)))

    w_a2 = jnp.pad(full['gla_w_a2'], ((0, 0), (0, LANE - GLA_GATE_RANK), (0, 0))).astype(MXU_DTYPE)
    cw_u, cw_g = full['ffn_conv_w'][:, :, :FF], full['ffn_conv_w'][:, :, FF:]
    cb_u, cb_g = ffn_conv_b[:, None, :FF], ffn_conv_b[:, None, FF:]

    pos_row = positions[0]
    cid = pos_row // CHUNK
    tab_q = rope_tables(pos_row, MLA_NOPE)
    tab_k = rope_tables(pos_row, 0)

    def row1(a, i):
        return a[i:i + 1]

    saved = []
    xa = x[0]
    xb = xa
    for i in range(DEPTH):
        j = i // 2
        sv = dict(x=xa, xb=xb)
        if i % 2 == 0:
            w_in, w_uq = mla_in_weights(j)
            h = mm(f'mla_in_{i}', xb, w_in, 'nn', F32)
            cq, ckv, kr = mla_pre_fwd(f'mla_pre_{i}', h, row1(mla_q_norm, j), row1(mla_kv_norm, j), tab_k, QL, KL)
            q = mm(f'mla_uq_{i}', cq, w_uq, 'nn', MXU_DTYPE, epilogue=(rope_heads, tab_q))
            kn = mm(f'mla_uk_{i}', ckv, full['mla_w_uk'], 'nn', MXU_DTYPE, b_pre=(j,))
            vv = mm(f'mla_uv_{i}', ckv, full['mla_w_uv'], 'nn', MXU_DTYPE, b_pre=(j,))
            rider = gather_exchange([full[n] for n in names], big_dims, shapes, layer_runs(gather_phase[i]))
            (o, lse), gathered = attn_fwd(f'mla_attn_{i}', q, kn, kr, vv, cid, H, rider=rider)
            full.update(zip(names, gathered))
            mix = mm(f'mla_o_{i}', o, full['mla_w_o'], 'nn', F32, b_pre=(j,))
            sv.update(h=h, cq=cq, ckv=ckv, kr=kr, q=q, kn=kn, v=vv, o=o, lse=lse, w_in=w_in, w_uq=w_uq)
        else:
            w_gin = gla_in_weight(j)
            sv.update(w_gin=w_gin)
            h = mm(f'gla_in_{i}', xb, w_gin, 'nn', F32)
            qs, la = gla_pre_fwd(f'gla_pre_{i}', h, w_a2[j], row1(full['gla_b_a'], j), QK, a_blk)
            o, states = gla_fwd(f'gla_scan_{i}', qs, h, la, QK, VD)
            og = gla_post_fwd(f'gla_post_{i}', o, h, row1(full['gla_o_norm'], j), VD, r_blk)
            mix = mm(f'gla_o_{i}', og, full['gla_w_o'], 'nn', F32, b_pre=(j,))
            sv.update(h=h, qs=qs, la=la, o=o, states=states, og=og)
        x1, x1b = ln_fwd(f'ln1_{i}', xa, mix, row1(ln1_g, i), row1(ln1_b, i))
        hu = mm(f'ffn_up_u_{i}', x1b, full['ffn_w_up'], 'nn', F32, b_pre=(i,), b_win=(0, D, 0, FF))
        hg = mm(f'ffn_up_g_{i}', x1b, full['ffn_w_up'], 'nn', F32, b_pre=(i,), b_win=(0, D, FF, FF))
        act = conv_fwd(f'ffn_conv_{i}', hu, hg, cw_u[i], cw_g[i], cb_u[i], cb_g[i])
        f = mm(f'ffn_down_{i}', act, full['ffn_w_down'], 'nn', F32, b_pre=(i,))
        x2, x2b = ln_fwd(f'ln2_{i}', x1, f, row1(ln2_g, i), row1(ln2_b, i))
        gp = mm(f'ple_gate_{i}', x2b, full['ple_w_gate'], 'nn', F32, b_pre=(i,))
        pp = mm(f'ple_proj_{i}', p, full['ple_w_proj'], 'nn', F32, a_pre=(i, 0), b_pre=(i,))
        x3, x3b = ple_fwd(f'ple_{i}', x2, gp, pp, row1(ple_b_gate, i))
        sv.update(mix=mix, x1=x1, x1b=x1b, hu=hu, hg=hg, act=act, f=f, x2=x2, x2b=x2b, gp=gp, pp=pp)
        saved.append(sv)
        xa, xb = x3, x3b

    dy, sq = loss_head('loss_head', xa, loss_target[0])
    loss_part = (0.5 / D) * jnp.sum(sq)

    small_g = {n: [None] * given[n].shape[0] for n, _ in SMALL}
    tab_q_t = tuple(t.T for t in tab_q)
    shard_buf = {n: None for n in names}

    def reduce_begin(tag, items):
        owners = [owner_of(layer) for _, _, layer, _ in items]
        landed = run_exchange(f'pair_exchange_{tag}', pair_exchange([g for *_, g in items], owners))
        partial = [pair_add(f'pair_add_{n}_{layer}', g, l, o, c_arr)
                   for (n, _, layer, g), l, o in zip(items, landed, owners)]
        dims = [big_dims[names.index(n)] for n, *_ in items]
        exchange = chip_exchange(partial, dims, [given[n].shape[1:] for n, *_ in items], owners)
        return exchange, (items, partial, dims, owners)

    def reduce_end(context, arrived):
        items, partial, dims, owners = context
        for (n, idx, layer, _), pt, ar, d, o in zip(items, partial, arrived, dims, owners):
            shard_buf[n] = chip_sum(f'chip_sum_{n}_{layer}', pt, ar, d, shard_buf[n], given[n].shape[0], idx, o,
                                    j_arr, c_arr)

    def wgrad(name, a, b, a_pre=()):
        return mm(f'd_{name}_{i}', a, b, 'tn', MXU_DTYPE, a_pre=a_pre)

    def layer_items(layer_grads):
        return [(n, i if n.startswith(('ffn', 'ple')) else i // 2, i, g) for n, g in layer_grads.items()]

    dA, ca, dB = dy, 1.0, None
    waiting = []
    for i in reversed(range(DEPTH)):
        j = i // 2
        sv = saved[i]
        lg = {}
        dx3, dpp, dgp, dbg = ple_bwd(f'ple_b_{i}', dA, ca, dB, sv['gp'], sv['pp'], row1(ple_b_gate, i))
        small_g['ple_b_gate'][i] = dbg
        lg['ple_w_proj'] = wgrad('ple_w_proj', p, dpp, a_pre=(i, 0))
        lg['ple_w_gate'] = wgrad('ple_w_gate', sv['x2b'], dgp)
        dx2 = mm(f'ple_gate_b_{i}', dgp, full['ple_w_gate'], 'nt', F32, b_pre=(i,))
        dz2, dz2b, dg2, db2 = ln_bwd(f'ln2_b_{i}', sv['x1'], sv['f'], dx3, 1.0, dx2, row1(ln2_g, i))
        small_g['ln2_g'][i], small_g['ln2_b'][i] = dg2, db2
        lg['ffn_w_down'] = wgrad('ffn_w_down', sv['act'], dz2b)
        dact = mm(f'ffn_down_b_{i}', dz2b, full['ffn_w_down'], 'nt', F32, b_pre=(i,))
        dhu, dhg, dcwu, dcwg, dcbu, dcbg = conv_bwd(f'ffn_conv_b_{i}', sv['hu'], sv['hg'], dact,
                                                   cw_u[i], cw_g[i], cb_u[i], cb_g[i])
        small_g['ffn_conv_w'][i] = jnp.concatenate([dcwu, dcwg], -1)
        small_g['ffn_conv_b'][i] = jnp.concatenate([dcbu, dcbg], -1)
        up = mm(f'd_ffn_w_up_u_{i}', sv['x1b'], dhu, 'tn', MXU_DTYPE, out_stack=(None, (1, D, 2 * FF), 0, 0))
        up = mm(f'd_ffn_w_up_g_{i}', sv['x1b'], dhg, 'tn', MXU_DTYPE, out_stack=(up, (1, D, 2 * FF), 0, FF))
        lg['ffn_w_up'] = up[0]
        dx1 = mm(f'ffn_up_bu_{i}', dhu, full['ffn_w_up'], 'nt', F32, b_pre=(i,), b_win=(0, D, 0, FF))
        dx1 = mm(f'ffn_up_bg_{i}', dhg, full['ffn_w_up'], 'nt', F32, b_pre=(i,), b_win=(0, D, FF, FF), acc_in=dx1)
        dz1, dz1b, dg1, db1 = ln_bwd(f'ln1_b_{i}', sv['x'], sv['mix'], dz2, DN_ALPHA, dx1, row1(ln1_g, i))
        small_g['ln1_g'][i], small_g['ln1_b'][i] = dg1, db1
        if i % 2 == 0:
            lg['mla_w_o'] = wgrad('mla_w_o', sv['o'], dz1b)
            do = mm(f'mla_o_b_{i}', dz1b, full['mla_w_o'], 'nt', MXU_DTYPE, b_pre=(j,))
            waiting += layer_items(lg)
            lg = {}
            riding, context = reduce_begin(f'before_{i}', waiting)
            waiting = []
            (dq, dkn, dv, dkr), arrived = attn_bwd(f'mla_attn_b_{i}', sv['q'], sv['kn'], sv['kr'], sv['v'], sv['o'],
                                                   sv['lse'], do, cid, H, rider=riding)
            reduce_end(context, arrived)
            dqpre_t = rope_q_bwd(f'mla_rope_b_{i}', dq, tab_q_t, H)
            uq_t = mm(f'd_mla_w_uq_{i}', dqpre_t, sv['cq'], 'nn', MXU_DTYPE)
            lg['mla_w_uq'] = uq_t.T.reshape(QL, H, HEAD_PAD)[..., :MLA_NOPE + MLA_ROPE].reshape(QL, -1)
            lg['mla_w_uk'] = wgrad('mla_w_uk', sv['ckv'], dkn)
            lg['mla_w_uv'] = wgrad('mla_w_uv', sv['ckv'], dv)
            dcq = mm(f'mla_uq_b_{i}', dqpre_t, sv['w_uq'].T, 'tn', F32)
            dckv_a = mm(f'mla_uk_b_{i}', dkn, full['mla_w_uk'], 'nt', F32, b_pre=(j,))
            dckv_b = mm(f'mla_uv_b_{i}', dv, full['mla_w_uv'], 'nt', F32, b_pre=(j,))
            dh, dqn, dkvn = mla_pre_bwd(f'mla_pre_b_{i}', sv['h'], dcq, dckv_a, dckv_b, dkr,
                                        row1(mla_q_norm, j), row1(mla_kv_norm, j), tab_k, QL, KL, H)
            small_g['mla_q_norm'][j], small_g['mla_kv_norm'][j] = dqn, dkvn
            lg['mla_w_in'] = wgrad('mla_w_in', sv['xb'], dh)[:, :MIN]
            dmix = mm(f'mla_in_b_{i}', dh, sv['w_in'], 'nt', F32)
        else:
            lg['gla_w_o'] = wgrad('gla_w_o', sv['og'], dz1b)
            dog = mm(f'gla_o_b_{i}', dz1b, full['gla_w_o'], 'nt', F32, b_pre=(j,))
            do, dr, don = gla_post_bwd(f'gla_post_b_{i}', sv['o'], sv['h'], dog, row1(full['gla_o_norm'], j), VD, r_blk)
            dq, dk, dv, dla = gla_bwd(f'gla_scan_b_{i}', sv['qs'], sv['h'], sv['la'], sv['states'], do, QK, VD)
            dh, dw2, dba = gla_dh(f'gla_dh_{i}', dq, dk, dv, dr, dla, sv['h'], w_a2[j], row1(full['gla_b_a'], j),
                                  QK, VD, a_blk, GIN_PAD)
            small_g['gla_o_norm'][j], small_g['gla_b_a'][j] = don, dba
            small_g['gla_w_a2'][j] = dw2[:GLA_GATE_RANK]
            g_in = wgrad('gla_w_in', sv['xb'], dh)[:, :GIN]
            lg['gla_w_in'] = g_in.reshape(D, N_CHIPS, GIN // N_CHIPS).transpose(1, 0, 2).reshape(N_CHIPS * D, -1)
            dmix = mm(f'gla_in_b_{i}', dh, sv['w_gin'], 'nt', F32)
        dA, ca, dB = dz1, DN_ALPHA, dmix
        waiting += layer_items(lg)
    grad_x = axpy('grad_x', dA, ca, dB)[None]

    exchange, context = reduce_begin('last', waiting)
    reduce_end(context, run_exchange('chip_exchange_last', exchange))
    shared = run_exchange('share_with_sibling',
                          share_exchange([shard_buf[n] for n in names], layer_runs(lambda n, l: True)))
    grads = dict(zip(names, shared))

    small_names = [n for n, _ in SMALL]
    small_full = [jnp.concatenate([g.reshape((1,) + g.shape[-(given[n].ndim - 1):]) for g in small_g[n]], 0)
                  for n in small_names]
    pack = _pack([jnp.reshape(loss_part, (1,))] + [jnp.zeros((LANE - 1,), F32)] + small_full)
    red = exchange_small('reduce_small', pack, True)
    red_parts = _unpack(red, [(LANE,)] + [g.shape for g in small_full])
    loss = red_parts[0][0]
    for (n, d), g in zip(SMALL, red_parts[1:]):
        if d is not None:
            width = given[n].shape[d]
            g = lax.dynamic_slice_in_dim(g, chip * width, width, axis=d)
        grads[n] = g

    delta, new_m, new_v = {}, {}, {}
    for n in names:
        delta[n], new_m[n], new_v[n] = adamw(f'adamw_{n}', given[n], grads[n], given['m_' + n], given['v_' + n])
    packs = [_pack([given[pre + n] for n in small_names]) for pre in ('', 'm_', 'v_')]
    outs = adamw('adamw_small', packs[0], _pack([grads[n] for n in small_names]), packs[1], packs[2])
    small_shapes = [given[n].shape for n in small_names]
    for dst, out in zip((delta, new_m, new_v), outs):
        for n, a in zip(small_names, _unpack(out, small_shapes)):
            dst[n] = a

    return (loss, grad_x, *[grads[n] for n in WEIGHT_NAMES], *[delta[n] for n in WEIGHT_NAMES],
            *[new_m[n] for n in WEIGHT_NAMES], *[new_v[n] for n in WEIGHT_NAMES])
```

```python
import functools
import math
from typing import Callable, NamedTuple

import jax
import jax.numpy as jnp
from jax import lax
from jax.experimental import pallas as pl
from jax.experimental.pallas import tpu as pltpu

F32 = jnp.float32
MXU_DTYPE = jnp.bfloat16

DEPTH = 4
CHUNK = 64
Q_BLOCK = 128
MLA_NOPE = 128
MLA_ROPE = 64
MLA_V = 128
ROPE_THETA = 10000.0
GLA_HEADS = 4
GLA_GATE_RANK = 16
GLA_TAU = 16.0
DN_ALPHA = (2 * DEPTH) ** 0.25
EPS = 1e-5
NEG_INF = -1e30
ADAM_LR = 0.001
ADAM_B1 = 0.9
ADAM_B2 = 0.999
ADAM_EPS = 1e-08
ADAM_WD = 0.01
ADAM_STEP = 10
GELU_C = math.sqrt(2.0 / math.pi)
GELU_A = 0.044715
LOG2_E = math.log2(math.e)

LANE = 128
HEAD_PAD = 2 * LANE
VMEM_LIMIT_BYTES = 48 * 1024 * 1024

MESH = pl.DeviceIdType.MESH
ANY_SPEC = pl.BlockSpec(memory_space=pl.ANY)

WEIGHT_NAMES = ('mla_w_in', 'mla_q_norm', 'mla_kv_norm', 'mla_w_uq', 'mla_w_uk', 'mla_w_uv', 'mla_w_o',
                'gla_w_in', 'gla_w_a2', 'gla_b_a', 'gla_o_norm', 'gla_w_o', 'ln1_g', 'ln1_b', 'ln2_g', 'ln2_b',
                'ffn_w_up', 'ffn_conv_w', 'ffn_conv_b', 'ffn_w_down', 'ple_w_proj', 'ple_w_gate', 'ple_b_gate')
BIG = (('mla_w_in', 1), ('mla_w_uq', 2), ('mla_w_uk', 2), ('mla_w_uv', 2), ('mla_w_o', 1), ('gla_w_in', 2),
       ('gla_w_o', 1), ('ffn_w_up', 2), ('ffn_w_down', 1), ('ple_w_proj', 2), ('ple_w_gate', 1))
SMALL = (('mla_q_norm', None), ('mla_kv_norm', None), ('gla_w_a2', 2), ('gla_b_a', 1), ('gla_o_norm', 1),
         ('ln1_g', None), ('ln1_b', None), ('ln2_g', None), ('ln2_b', None), ('ffn_conv_w', 2),
         ('ffn_conv_b', None), ('ple_b_gate', None))
N_CHIPS = 4
N_DEVICES = 8


def _params():
    return pltpu.CompilerParams(vmem_limit_bytes=VMEM_LIMIT_BYTES)


def _tile(n, cap, *offsets, unit=LANE):
    g = n
    for o in offsets:
        if o:
            g = math.gcd(g, o)
    best = 0
    for d in range(unit, min(g, cap) + 1, unit):
        if g % d == 0:
            best = d
    if best:
        return best
    assert not any(offsets), (n, offsets)
    return n


def _dot(a, b, dims):
    return lax.dot_general(a.astype(MXU_DTYPE), b.astype(MXU_DTYPE), (dims, ((), ())),
                           preferred_element_type=F32)


NN = ((1,), (0,))
NT = ((1,), (1,))
TN = ((0,), (0,))


def mm(name, a, b, mode, out_dtype, *, a_pre=(), b_pre=(), b_win=None, acc_in=None, out_stack=None,
       epilogue=None):
    a2 = a.shape[len(a_pre):]
    b2 = b.shape[len(b_pre):]
    br0, brn, bc0, bcn = b_win or (0, b2[0], 0, b2[1])
    if mode == 'nn':
        (M, K), N, dims = a2, bcn, NN
        assert brn == K
    elif mode == 'nt':
        (M, K), N, dims = a2, brn, NT
        assert bcn == K
    else:
        (K, M), N, dims = a2, bcn, TN
        assert brn == K
    oc0 = out_stack[3] if out_stack else 0
    tm = _tile(M, 1408)
    if mode == 'nn':
        tn, tk = _tile(N, 1408, bc0, oc0), _tile(K, 1408, br0)
    elif mode == 'nt':
        tn, tk = _tile(N, 1408, br0, oc0), _tile(K, 1408, bc0)
    else:
        tn, tk = _tile(N, 1408, bc0, oc0), _tile(K, 1024, br0)
    nk = K // tk
    grid = (M // tm, N // tn, nk)

    na, nb = len(a_pre), len(b_pre)
    if mode == 'tn':
        a_spec = pl.BlockSpec((None,) * na + (tk, tm), lambda i, j, k: a_pre + (k, i))
    else:
        a_spec = pl.BlockSpec((None,) * na + (tm, tk), lambda i, j, k: a_pre + (i, k))
    if mode == 'nt':
        b_spec = pl.BlockSpec((None,) * nb + (tn, tk), lambda i, j, k: b_pre + (j + br0 // tn, k + bc0 // tk))
    else:
        b_spec = pl.BlockSpec((None,) * nb + (tk, tn), lambda i, j, k: b_pre + (k + br0 // tk, j + bc0 // tn))
    in_specs, args = [a_spec, b_spec], [a, b]
    if acc_in is not None:
        in_specs.append(pl.BlockSpec((tm, tn), lambda i, j, k: (i, j)))
        args.append(acc_in)
    aliases = {}
    if out_stack is None:
        out_shape = jax.ShapeDtypeStruct((M, N), out_dtype)
        out_spec = pl.BlockSpec((tm, tn), lambda i, j, k: (i, j))
    else:
        buf, full_shape, lead, _ = out_stack
        out_shape = jax.ShapeDtypeStruct(full_shape, out_dtype)
        out_spec = pl.BlockSpec((None, tm, tn), lambda i, j, k: (lead, i, j + oc0 // tn))
        if buf is not None:
            aliases = {len(args): 0}
            in_specs.append(ANY_SPEC)
            args.append(buf)
    has_c, has_alias = acc_in is not None, bool(aliases)
    finish, row_arrays = epilogue or (None, ())
    n_rows_in = len(row_arrays)
    for r in row_arrays:
        in_specs.append(pl.BlockSpec((tm, r.shape[1]), lambda i, j, k: (i, 0)))
        args.append(r)

    def body(*refs):
        a_ref, b_ref = refs[0], refs[1]
        c_ref = refs[2] if has_c else None
        first_row = 2 + has_c + has_alias
        o_ref = refs[first_row + n_rows_in]

        def result(tile):
            if finish is not None:
                tile = finish(tile, *[r[...] for r in refs[first_row:first_row + n_rows_in]])
            return tile.astype(out_dtype)

        prod = _dot(a_ref[...], b_ref[...], dims)
        if nk == 1:
            if has_c:
                prod = prod + c_ref[...]
            o_ref[...] = result(prod)
            return
        acc_ref = refs[first_row + n_rows_in + 1]
        k = pl.program_id(2)

        @pl.when(k == 0)
        def _():
            acc_ref[...] = prod + c_ref[...] if has_c else prod

        @pl.when(k > 0)
        def _():
            acc_ref[...] += prod

        @pl.when(k == nk - 1)
        def _():
            o_ref[...] = result(acc_ref[...])

    scratch = [pltpu.VMEM((tm, tn), F32)] if nk > 1 else []
    return pl.pallas_call(body, out_shape=out_shape, grid=grid, in_specs=in_specs, out_specs=out_spec,
                          scratch_shapes=scratch, input_output_aliases=aliases, name=name,
                          compiler_params=_params())(*args)


def rowcall(name, body, n_rows, ts, row_ins, full_ins, row_outs, acc_outs=()):
    in_specs, args = [], []
    for arr, width, colblk, pre in row_ins:
        in_specs.append(pl.BlockSpec((None,) * len(pre) + (ts, width),
                                     lambda i, pre=pre, cb=colblk: pre + (i, cb)))
        args.append(arr)
    for arr in full_ins:
        in_specs.append(pl.BlockSpec(arr.shape, lambda i, nd=arr.ndim: (0,) * nd))
        args.append(arr)
    out_shape, out_specs = [], []
    for width, dtype in row_outs:
        out_shape.append(jax.ShapeDtypeStruct((n_rows, width), dtype))
        out_specs.append(pl.BlockSpec((ts, width), lambda i: (i, 0)))
    for shape in acc_outs:
        out_shape.append(jax.ShapeDtypeStruct(shape, F32))
        out_specs.append(pl.BlockSpec(shape, lambda i, nd=len(shape): (0,) * nd))
    n_in = len(args)

    def kern(*refs):
        body(pl.program_id(0), refs[:n_in], refs[n_in:])

    return pl.pallas_call(kern, out_shape=tuple(out_shape), grid=(n_rows // ts,), in_specs=in_specs,
                          out_specs=tuple(out_specs), name=name, compiler_params=_params())(*args)


def _row(arr, width=None, colblk=0, pre=()):
    return (arr, arr.shape[-1] if width is None else width, colblk, pre)


def _init_acc(i, refs):
    @pl.when(i == 0)
    def _():
        for r in refs:
            r[...] = jnp.zeros(r.shape, r.dtype)


def _colsum(v):
    return jnp.sum(v, axis=0, keepdims=True)


def _sigmoid(z):
    return 1.0 / (1.0 + jnp.exp(-z))


def _row_tile(S):
    return _tile(S, 256, unit=16)


def _ln_stats(x_ref, m_ref):
    z = DN_ALPHA * x_ref[...] + m_ref[...]
    mu = jnp.mean(z, -1, keepdims=True)
    zc = z - mu
    var = jnp.mean(zc * zc, -1, keepdims=True)
    r = lax.rsqrt(var + EPS)
    return zc * r, r


def ln_fwd(name, x, m, g, b):
    S, D = x.shape

    def body(i, ins, outs):
        x_ref, m_ref, g_ref, b_ref = ins
        xh, _ = _ln_stats(x_ref, m_ref)
        y = xh * g_ref[...] + b_ref[...]
        outs[0][...] = y
        outs[1][...] = y.astype(MXU_DTYPE)

    return rowcall(name, body, S, _row_tile(S), [_row(x), _row(m)], [g, b], [(D, F32), (D, MXU_DTYPE)])


def ln_bwd(name, x, m, dA, ca, dB, g):
    S, D = x.shape
    has_b = dB is not None

    def body(i, ins, outs):
        x_ref, m_ref, a_ref = ins[:3]
        g_ref = ins[-1]
        dz_ref, dzb_ref, dg_ref, db_ref = outs
        _init_acc(i, (dg_ref, db_ref))
        dy = ca * a_ref[...]
        if has_b:
            dy = dy + ins[3][...]
        xh, r = _ln_stats(x_ref, m_ref)
        dg_ref[...] += _colsum(dy * xh)
        db_ref[...] += _colsum(dy)
        dxh = dy * g_ref[...]
        dz = r * (dxh - jnp.mean(dxh, -1, keepdims=True) - xh * jnp.mean(dxh * xh, -1, keepdims=True))
        dz_ref[...] = dz
        dzb_ref[...] = dz.astype(MXU_DTYPE)

    rows = [_row(x), _row(m), _row(dA)] + ([_row(dB)] if has_b else [])
    return rowcall(name, body, S, _row_tile(S), rows, [g], [(D, F32), (D, MXU_DTYPE)], [(1, D), (1, D)])


def ple_fwd(name, x2, gp, pp, bias):
    S, D = x2.shape

    def body(i, ins, outs):
        x_ref, gp_ref, pp_ref, b_ref = ins
        y = x_ref[...] + _sigmoid(gp_ref[...] + b_ref[...]) * pp_ref[...]
        outs[0][...] = y
        outs[1][...] = y.astype(MXU_DTYPE)

    return rowcall(name, body, S, _row_tile(S), [_row(x2), _row(gp), _row(pp)], [bias],
                   [(D, F32), (D, MXU_DTYPE)])


def ple_bwd(name, dA, ca, dB, gp, pp, bias):
    S, D = gp.shape
    has_b = dB is not None

    def body(i, ins, outs):
        a_ref = ins[0]
        gp_ref, pp_ref, b_ref = ins[-3:]
        dx_ref, dpp_ref, dgp_ref, db_ref = outs
        _init_acc(i, (db_ref,))
        dx = ca * a_ref[...]
        if has_b:
            dx = dx + ins[1][...]
        gate = _sigmoid(gp_ref[...] + b_ref[...])
        dgp = dx * pp_ref[...] * gate * (1.0 - gate)
        dx_ref[...] = dx
        dpp_ref[...] = (dx * gate).astype(MXU_DTYPE)
        dgp_ref[...] = dgp.astype(MXU_DTYPE)
        db_ref[...] += _colsum(dgp)

    rows = [_row(dA)] + ([_row(dB)] if has_b else []) + [_row(gp), _row(pp)]
    return rowcall(name, body, S, _row_tile(S), rows, [bias],
                   [(D, F32), (D, MXU_DTYPE), (D, MXU_DTYPE)], [(1, D)])


def loss_head(name, y, target):
    S, D = y.shape

    def body(i, ins, outs):
        _init_acc(i, (outs[1],))
        e = ins[0][...] - ins[1][...]
        outs[0][...] = e * (1.0 / D)
        outs[1][...] += _colsum(e * e)

    return rowcall(name, body, S, _row_tile(S), [_row(y), _row(target)], [], [(D, F32)], [(1, D)])


def axpy(name, a, ca, b):
    S, D = a.shape

    def body(i, ins, outs):
        outs[0][...] = ca * ins[0][...] + ins[1][...]

    return rowcall(name, body, S, _row_tile(S), [_row(a), _row(b)], [], [(D, F32)])[0]


HALF_ROPE = MLA_ROPE // 2


def _rope(x, c, sa, sb):
    n = x.shape[-1]
    return x * c + pltpu.roll(x, n - HALF_ROPE, 1) * sa + pltpu.roll(x, HALF_ROPE, 1) * sb


def _rope_t(d, c, sa, sb):
    n = d.shape[-1]
    return d * c + pltpu.roll(d * sa, HALF_ROPE, 1) + pltpu.roll(d * sb, n - HALF_ROPE, 1)


def rope_tables(positions_row, n_lead):
    inv = 1.0 / (ROPE_THETA ** (jnp.arange(0, MLA_ROPE, 2, dtype=F32) / MLA_ROPE))
    ang = positions_row.astype(F32)[:, None] * inv
    cos, sin = jnp.cos(ang), jnp.sin(ang)
    S = cos.shape[0]
    z = jnp.zeros((S, HALF_ROPE), F32)
    tail = jnp.zeros((S, LANE - MLA_ROPE), F32)
    c = jnp.concatenate([jnp.ones((S, n_lead), F32), cos, cos, tail], -1)
    sa = jnp.concatenate([jnp.zeros((S, n_lead), F32), -sin, z, tail], -1)
    sb = jnp.concatenate([jnp.zeros((S, n_lead), F32), z, sin, tail], -1)
    return c, sa, sb


def mla_pre_fwd(name, h, qn, kvn, tab_k, QL, KL):
    S = h.shape[0]

    def body(i, ins, outs):
        h_ref, c_ref, sa_ref, sb_ref, qn_ref, kvn_ref = ins
        cq = h_ref[:, 0:QL]
        ckv = h_ref[:, QL:QL + KL]
        kr = h_ref[:, QL + KL:QL + KL + LANE]
        outs[0][...] = (cq * lax.rsqrt(jnp.mean(cq * cq, -1, keepdims=True) + EPS) * qn_ref[...]).astype(MXU_DTYPE)
        outs[1][...] = (ckv * lax.rsqrt(jnp.mean(ckv * ckv, -1, keepdims=True) + EPS) * kvn_ref[...]).astype(MXU_DTYPE)
        outs[2][...] = _rope(kr, c_ref[...], sa_ref[...], sb_ref[...]).astype(MXU_DTYPE)

    rows = [_row(h)] + [_row(t) for t in tab_k]
    return rowcall(name, body, S, _row_tile(S), rows, [qn, kvn],
                   [(QL, MXU_DTYPE), (KL, MXU_DTYPE), (LANE, MXU_DTYPE)])


def _rms_bwd(x, g, dy):
    r = lax.rsqrt(jnp.mean(x * x, -1, keepdims=True) + EPS)
    dg = _colsum(dy * x * r)
    dxg = dy * g
    dx = r * dxg - x * (r * r * r) * jnp.mean(dxg * x, -1, keepdims=True)
    return dx, dg


def mla_pre_bwd(name, h, dcq, dckv_a, dckv_b, dkr_heads, qn, kvn, tab_k, QL, KL, H):
    S, HW = h.shape

    def body(i, ins, outs):
        h_ref, dcq_ref, da_ref, db_ref, dkr_ref, c_ref, sa_ref, sb_ref, qn_ref, kvn_ref = ins
        dh_ref, dqn_ref, dkvn_ref = outs
        _init_acc(i, (dqn_ref, dkvn_ref))
        dx, dg = _rms_bwd(h_ref[:, 0:QL], qn_ref[...], dcq_ref[...])
        dh_ref[:, 0:QL] = dx.astype(MXU_DTYPE)
        dqn_ref[...] += dg
        dx, dg = _rms_bwd(h_ref[:, QL:QL + KL], kvn_ref[...], da_ref[...] + db_ref[...])
        dh_ref[:, QL:QL + KL] = dx.astype(MXU_DTYPE)
        dkvn_ref[...] += dg
        d = dkr_ref[:, 0:LANE]
        for hh in range(1, H):
            d = d + dkr_ref[:, hh * LANE:(hh + 1) * LANE]
        dh_ref[:, QL + KL:QL + KL + LANE] = _rope_t(d, c_ref[...], sa_ref[...], sb_ref[...]).astype(MXU_DTYPE)

    rows = [_row(h), _row(dcq), _row(dckv_a), _row(dckv_b), _row(dkr_heads)] + [_row(t) for t in tab_k]
    return rowcall(name, body, S, _row_tile(S), rows, [qn, kvn], [(HW, MXU_DTYPE)], [(1, QL), (1, KL)])


def rope_heads(tile, c, sa, sb):
    return jnp.concatenate([_rope(tile[:, h0:h0 + HEAD_PAD], c, sa, sb)
                            for h0 in range(0, tile.shape[1], HEAD_PAD)], axis=-1)


ATT_T_FWD = 1024
ATT_T_BWD = 512


def _att_mask_t(ck_col, cq_row, k0, q0, t):
    kpos = k0 + lax.broadcasted_iota(jnp.int32, (t, 1), 0)
    qpos = q0 + lax.broadcasted_iota(jnp.int32, (1, t), 1)
    return (ck_col <= cq_row) & (kpos <= (qpos | (Q_BLOCK - 1)))


def _transpose(a):
    return a.astype(F32).T.astype(a.dtype)


def hosted_call(body, *, name, grid, in_specs, out_specs, out_shape, scratch_shapes, args, rider):
    if rider is None:
        return pl.pallas_call(body, name=name, grid=grid, in_specs=in_specs, out_specs=out_specs,
                              out_shape=out_shape, scratch_shapes=scratch_shapes,
                              compiler_params=_params())(*args), ()
    n_in, n_out, n_scr = len(args), len(out_shape), len(scratch_shapes)
    r_in, r_out = len(rider.ins), len(rider.out_shapes)

    def hosted(*refs):
        ins, refs = refs[:n_in], refs[n_in:]
        rin, refs = refs[:r_in], refs[r_in:]
        outs, refs = refs[:n_out], refs[n_out:]
        rout, refs = refs[:r_out], refs[r_out:]
        scr, sems = refs[:n_scr], refs[n_scr:]
        ids = [pl.program_id(a) for a in range(len(grid))]
        first = functools.reduce(jnp.logical_and, [i == 0 for i in ids])
        last = functools.reduce(jnp.logical_and, [i == g - 1 for i, g in zip(ids, grid)])

        @pl.when(first)
        def _():
            rider.start(rin, rout, sems)

        body(*ins, *outs, *scr)

        @pl.when(last)
        def _():
            rider.finish(rin, rout, sems)

    results = pl.pallas_call(
        hosted, name=name, grid=grid, in_specs=list(in_specs) + [ANY_SPEC] * r_in,
        out_specs=tuple(out_specs) + (ANY_SPEC,) * r_out, out_shape=tuple(out_shape) + tuple(rider.out_shapes),
        scratch_shapes=list(scratch_shapes) + [pltpu.SemaphoreType.DMA((k,)) for k in rider.sem_counts],
        input_output_aliases={n_in + k: n_out + k for k in range(r_in)} if rider.in_place else {},
        compiler_params=pltpu.CompilerParams(vmem_limit_bytes=VMEM_LIMIT_BYTES, has_side_effects=True),
    )(*args, *rider.ins)
    return results[:n_out], results[n_out:]


def attn_fwd(name, q, kn, kr, v, cid, H, rider=None):
    S = q.shape[0]
    t = _tile(S, ATT_T_FWD)
    n = S // t
    scale = (MLA_NOPE + MLA_ROPE) ** -0.5
    cid_col, cid_blk = cid.reshape(S, 1), cid.reshape(n, 1, t)

    def body(q_ref, kn_ref, kr_ref, v_ref, cc_ref, cr_ref, o_ref, lse_ref, k_scr, vt_scr):
        i = pl.program_id(1)

        @pl.when(i == 0)
        def _():
            k_scr[:, 0:LANE] = kn_ref[...]
            k_scr[:, LANE:HEAD_PAD] = kr_ref[...]
            for jj in range(n):
                vt_scr[jj] = _transpose(v_ref[jj * t:(jj + 1) * t, :])

        qv = q_ref[...]
        cq = cr_ref[i]

        def update(j, carry, masked):
            m, l, acc = carry
            k0 = pl.multiple_of(j * t, t)
            s = _dot(k_scr[pl.ds(k0, t), :], qv, NT) * (scale * LOG2_E)
            if masked:
                s = jnp.where(_att_mask_t(cc_ref[pl.ds(k0, t), :], cq, k0, i * t, t), s, NEG_INF)
            m_new = jnp.maximum(m, jnp.max(s, 0, keepdims=True))
            p = jnp.exp2(s - m_new)
            alpha = jnp.exp2(m - m_new)
            l = alpha * l + jnp.sum(p, 0, keepdims=True)
            acc = alpha * acc + _dot(vt_scr[j], p, NN)
            return m_new, l, acc

        init = (jnp.full((1, t), NEG_INF, F32), jnp.zeros((1, t), F32), jnp.zeros((MLA_V, t), F32))
        carry = lax.fori_loop(0, i, lambda j, c: update(j, c, False), init)
        m, l, acc = update(i, carry, True)
        o_ref[...] = (acc / l).T.astype(o_ref.dtype)
        lse_ref[...] = m * (1.0 / LOG2_E) + jnp.log(l)

    return hosted_call(
        body, name=name, grid=(H, n),
        in_specs=[pl.BlockSpec((t, HEAD_PAD), lambda h, i: (i, h)),
                  pl.BlockSpec((S, MLA_NOPE), lambda h, i: (0, h)),
                  pl.BlockSpec((S, LANE), lambda h, i: (0, 0)),
                  pl.BlockSpec((S, MLA_V), lambda h, i: (0, h)),
                  pl.BlockSpec((S, 1), lambda h, i: (0, 0)),
                  pl.BlockSpec(cid_blk.shape, lambda h, i: (0, 0, 0))],
        out_specs=(pl.BlockSpec((t, MLA_V), lambda h, i: (i, h)),
                   pl.BlockSpec((None, None, 1, t), lambda h, i: (h, i, 0, 0))),
        scratch_shapes=[pltpu.VMEM((S, HEAD_PAD), MXU_DTYPE), pltpu.VMEM((n, MLA_V, t), MXU_DTYPE)],
        out_shape=(jax.ShapeDtypeStruct((S, H * MLA_V), MXU_DTYPE), jax.ShapeDtypeStruct((H, n, 1, t), F32)),
        args=(q, kn, kr, v, cid_col, cid_blk), rider=rider)


def attn_bwd(name, q, kn, kr, v, o, lse, do, cid, H, rider=None):
    S = q.shape[0]
    t = _tile(S, ATT_T_BWD)
    n = S // t
    scale = (MLA_NOPE + MLA_ROPE) ** -0.5
    cid_col, cid_blk = cid.reshape(S, 1), cid.reshape(n, 1, t)
    lse = lse.reshape(H, n, 1, t)

    def body(q_ref, kn_ref, kr_ref, v_ref, o_ref, lse_ref, do_ref, cc_ref, cr_ref,
             dq_ref, dkn_ref, dv_ref, dkr_ref, delta_scr, dk_scr, dv_scr):
        j = pl.program_id(1)

        @pl.when(j == 0)
        def _():
            dq_ref[...] = jnp.zeros(dq_ref.shape, F32)
            for ii in range(n):
                sl = slice(ii * t, (ii + 1) * t)
                prod = do_ref[sl, :].astype(F32) * o_ref[sl, :].astype(F32)
                delta_scr[ii] = jnp.sum(prod.T, 0, keepdims=True)

        kv = jnp.concatenate([kn_ref[...], kr_ref[...]], axis=-1)
        kt = _transpose(kv)
        vv = v_ref[...]
        ck = cc_ref[...]
        k0 = j * t
        dk_scr[...] = jnp.zeros(dk_scr.shape, F32)
        dv_scr[...] = jnp.zeros(dv_scr.shape, F32)

        def update(i, masked):
            q0 = pl.multiple_of(i * t, t)
            qv = q_ref[pl.ds(q0, t), :]
            dov = do_ref[pl.ds(q0, t), :]
            s = _dot(kv, qv, NT) * scale
            if masked:
                s = jnp.where(_att_mask_t(ck, cr_ref[i], k0, q0, t), s, NEG_INF)
            p = jnp.exp(s - lse_ref[i])
            dv_scr[...] += _dot(p, dov, NN)
            dp = _dot(vv, dov, NT)
            ds = p * (dp - delta_scr[i]) * scale
            dk_scr[...] += _dot(ds, qv, NN)
            dq_ref[i] += _dot(kt, ds, NN)

        def step(i, carry):
            update(i, False)
            return carry

        update(j, True)
        lax.fori_loop(j + 1, n, step, 0)
        dkn_ref[...] = dk_scr[:, 0:LANE].astype(dkn_ref.dtype)
        dkr_ref[...] = dk_scr[:, LANE:HEAD_PAD]
        dv_ref[...] = dv_scr[...].astype(dv_ref.dtype)

    head_rows = lambda w: pl.BlockSpec((S, w), lambda h, j: (0, h))
    tile_rows = lambda w: pl.BlockSpec((t, w), lambda h, j: (j, h))
    return hosted_call(
        body, name=name, grid=(H, n),
        in_specs=[head_rows(HEAD_PAD), tile_rows(MLA_NOPE), pl.BlockSpec((t, LANE), lambda h, j: (j, 0)),
                  tile_rows(MLA_V), head_rows(MLA_V),
                  pl.BlockSpec((None, n, 1, t), lambda h, j: (h, 0, 0, 0)), head_rows(MLA_V),
                  pl.BlockSpec((t, 1), lambda h, j: (j, 0)),
                  pl.BlockSpec(cid_blk.shape, lambda h, j: (0, 0, 0))],
        out_specs=(pl.BlockSpec((None, n, HEAD_PAD, t), lambda h, j: (h, 0, 0, 0)),
                   tile_rows(MLA_NOPE), tile_rows(MLA_V), tile_rows(LANE)),
        scratch_shapes=[pltpu.VMEM((n, 1, t), F32), pltpu.VMEM((t, HEAD_PAD), F32), pltpu.VMEM((t, MLA_V), F32)],
        out_shape=(jax.ShapeDtypeStruct((H, n, HEAD_PAD, t), F32), jax.ShapeDtypeStruct((S, H * MLA_NOPE), MXU_DTYPE),
                   jax.ShapeDtypeStruct((S, H * MLA_V), MXU_DTYPE), jax.ShapeDtypeStruct((S, H * LANE), F32)),
        args=(q, kn, kr, v, o, lse, do, cid_col, cid_blk), rider=rider)


def rope_q_bwd(name, dq_t, tab_q_t, H):
    _, n, _, t = dq_t.shape

    def body(d_ref, c_ref, sa_ref, sb_ref, o_ref):
        c, sa, sb = c_ref[...], sa_ref[...], sb_ref[...]
        for hh in range(H):
            d = d_ref[hh]
            out = d * c + pltpu.roll(d * sa, HALF_ROPE, 0) + pltpu.roll(d * sb, HEAD_PAD - HALF_ROPE, 0)
            o_ref[hh * HEAD_PAD:(hh + 1) * HEAD_PAD, :] = out.astype(o_ref.dtype)

    tab = pl.BlockSpec((HEAD_PAD, t), lambda i: (0, i))
    return pl.pallas_call(
        body, name=name, grid=(n,), out_shape=jax.ShapeDtypeStruct((H * HEAD_PAD, n * t), MXU_DTYPE),
        in_specs=[pl.BlockSpec((H, None, HEAD_PAD, t), lambda i: (0, i, 0, 0)), tab, tab, tab],
        out_specs=pl.BlockSpec((H * HEAD_PAD, t), lambda i: (0, i)),
        compiler_params=_params())(dq_t, *tab_q_t)


GLA_GROUP = 8


def _prefix_rows(x):
    n = x.shape[0]
    row = lax.broadcasted_iota(jnp.int32, x.shape, 0)
    d = 1
    while d < n:
        x = x + jnp.where(row >= d, pltpu.roll(x, d, 0), 0.0)
        d *= 2
    return x


def _suffix_rows(x):
    n = x.shape[0]
    row = lax.broadcasted_iota(jnp.int32, x.shape, 0)
    d = 1
    while d < n:
        x = x + jnp.where(row < n - d, pltpu.roll(x, n - d, 0), 0.0)
        d *= 2
    return x


def _log_sigmoid(z):
    return jnp.minimum(z, 0.0) - jnp.log(1.0 + jnp.exp(-jnp.abs(z)))


def gla_pre_fwd(name, h, w2p, b_a, QK, a_blk):
    S = h.shape[0]
    dk = QK // GLA_HEADS

    def body(i, ins, outs):
        q_ref, a_ref, w_ref, b_ref = ins
        outs[0][...] = (q_ref[...] * (dk ** -0.5)).astype(MXU_DTYPE)
        z = _dot(a_ref[...], w_ref[...], NN) + b_ref[...]
        outs[1][...] = _log_sigmoid(z) / GLA_TAU

    return rowcall(name, body, S, _row_tile(S), [_row(h, QK, 0), _row(h, LANE, a_blk)], [w2p, b_a],
                   [(QK, MXU_DTYPE), (QK, F32)])


def _gla_specs(S, QK, VD, rows, gmap):
    dk, dv = QK // GLA_HEADS, VD // GLA_HEADS
    return dict(
        qs=pl.BlockSpec((rows, dk), lambda h, g: (gmap(g), h)),
        k=pl.BlockSpec((rows, dk), lambda h, g: (gmap(g), QK // dk + h)),
        v=pl.BlockSpec((rows, dv), lambda h, g: (gmap(g), 2 * QK // dv + h)),
        la=pl.BlockSpec((rows, dk), lambda h, g: (gmap(g), h)),
        o=pl.BlockSpec((rows, dv), lambda h, g: (gmap(g), h)))


def gla_fwd(name, qs, h, la, QK, VD):
    S = qs.shape[0]
    dk, dv = QK // GLA_HEADS, VD // GLA_HEADS
    n_chunks = S // CHUNK
    cg = min(GLA_GROUP, n_chunks)
    rows = cg * CHUNK
    sp = _gla_specs(S, QK, VD, rows, lambda g: g)

    def body(q_ref, k_ref, v_ref, la_ref, o_ref, st_ref, state):
        @pl.when(pl.program_id(1) == 0)
        def _():
            state[...] = jnp.zeros(state.shape, F32)

        for c in range(cg):
            sl = slice(c * CHUNK, (c + 1) * CHUNK)
            cum = _prefix_rows(la_ref[sl, :])
            tot = cum[CHUNK - 1:CHUNK, :]
            kdec = k_ref[sl, :] * jnp.exp(tot - cum)
            st = state[...] * jnp.exp(tot) + _dot(v_ref[sl, :], kdec, TN)
            state[...] = st
            st_ref[c] = st
            o_ref[sl, :] = _dot(q_ref[sl, :], st, NT)

    return pl.pallas_call(
        body, name=name, grid=(GLA_HEADS, n_chunks // cg),
        out_shape=(jax.ShapeDtypeStruct((S, VD), F32), jax.ShapeDtypeStruct((GLA_HEADS, n_chunks, dv, dk), F32)),
        in_specs=[sp['qs'], sp['k'], sp['v'], sp['la']],
        out_specs=(sp['o'], pl.BlockSpec((None, cg, dv, dk), lambda h, g: (h, g, 0, 0))),
        scratch_shapes=[pltpu.VMEM((dv, dk), F32)],
        compiler_params=_params())(qs, h, h, la)


def gla_bwd(name, qs, h, la, states, do, QK, VD):
    S = qs.shape[0]
    dk, dv = QK // GLA_HEADS, VD // GLA_HEADS
    n_chunks = S // CHUNK
    cg = min(GLA_GROUP, n_chunks)
    ng = n_chunks // cg
    rows = cg * CHUNK
    rev = lambda g: ng - 1 - g
    sp = _gla_specs(S, QK, VD, rows, rev)

    def body(q_ref, k_ref, v_ref, la_ref, st_ref, prev_ref, do_ref, dq_ref, dk_ref, dv_ref, dla_ref, dst):
        g = pl.program_id(1)

        @pl.when(g == 0)
        def _():
            dst[...] = jnp.zeros(dst.shape, F32)

        first_group = (g == ng - 1).astype(F32)
        for c in reversed(range(cg)):
            sl = slice(c * CHUNK, (c + 1) * CHUNK)
            cum = _prefix_rows(la_ref[sl, :])
            tot = cum[CHUNK - 1:CHUNK, :]
            e = jnp.exp(tot - cum)
            kdec = k_ref[sl, :] * e
            decay = jnp.exp(tot)
            st = st_ref[c]
            st_prev = st_ref[c - 1] if c > 0 else prev_ref[0] * (1.0 - first_group)
            dov = do_ref[sl, :]
            qv = q_ref[sl, :]
            dq_ref[sl, :] = (_dot(dov, st, NN) * (dk ** -0.5)).astype(dq_ref.dtype)
            d = dst[...] + _dot(dov, qv, TN)
            ddecay = _colsum(d * st_prev)
            dkdec = _dot(v_ref[sl, :], d, NN)
            dv_ref[sl, :] = _dot(kdec, d, NT).astype(dv_ref.dtype)
            dk_ref[sl, :] = (dkdec * e).astype(dk_ref.dtype)
            darg = dkdec * kdec
            dtot = _colsum(darg) + ddecay * decay
            dla_ref[sl, :] = dtot - _suffix_rows(darg)
            dst[...] = d * decay

    prev_spec = pl.BlockSpec((None, 1, dv, dk), lambda h, g: (h, jnp.maximum(rev(g) * cg - 1, 0), 0, 0))
    return pl.pallas_call(
        body, name=name, grid=(GLA_HEADS, ng),
        out_shape=(jax.ShapeDtypeStruct((S, QK), MXU_DTYPE), jax.ShapeDtypeStruct((S, QK), MXU_DTYPE),
                   jax.ShapeDtypeStruct((S, VD), MXU_DTYPE), jax.ShapeDtypeStruct((S, QK), F32)),
        in_specs=[sp['qs'], sp['k'], sp['v'], sp['la'],
                  pl.BlockSpec((None, cg, dv, dk), lambda h, g: (h, rev(g), 0, 0)), prev_spec, sp['o']],
        out_specs=(sp['qs'], sp['qs'], sp['o'], sp['la']),
        scratch_shapes=[pltpu.VMEM((dv, dk), F32)],
        compiler_params=_params())(qs, h, h, la, states, states, do)


def _head_norm(o):
    mu = jnp.mean(o, -1, keepdims=True)
    oc = o - mu
    r = lax.rsqrt(jnp.mean(oc * oc, -1, keepdims=True) + EPS)
    return oc * r, r


def gla_post_fwd(name, o, h, o_norm, VD, r_blk):
    S = o.shape[0]
    dv = VD // GLA_HEADS

    def body(i, ins, outs):
        o_ref, r_ref, w_ref = ins
        for hh in range(GLA_HEADS):
            sl = slice(hh * dv, (hh + 1) * dv)
            xh, _ = _head_norm(o_ref[:, sl])
            r = r_ref[:, sl]
            outs[0][:, sl] = (xh * w_ref[:, sl] * (r * _sigmoid(r))).astype(MXU_DTYPE)

    return rowcall(name, body, S, _row_tile(S), [_row(o), _row(h, VD, r_blk)], [o_norm], [(VD, MXU_DTYPE)])[0]


def gla_post_bwd(name, o, h, dog, o_norm, VD, r_blk):
    S = o.shape[0]
    dv = VD // GLA_HEADS

    def body(i, ins, outs):
        o_ref, r_ref, dog_ref, w_ref = ins
        do_ref, dr_ref, dw_ref = outs
        _init_acc(i, (dw_ref,))
        for hh in range(GLA_HEADS):
            sl = slice(hh * dv, (hh + 1) * dv)
            xh, rs = _head_norm(o_ref[:, sl])
            r = r_ref[:, sl]
            w = w_ref[:, sl]
            dog = dog_ref[:, sl]
            sg = _sigmoid(r)
            dn = dog * (r * sg)
            dr_ref[:, sl] = (dog * (xh * w) * (sg * (1.0 + r * (1.0 - sg)))).astype(MXU_DTYPE)
            dw_ref[:, sl] += _colsum(dn * xh)
            dxh = dn * w
            do_ref[:, sl] = rs * (dxh - jnp.mean(dxh, -1, keepdims=True) - xh * jnp.mean(dxh * xh, -1, keepdims=True))

    return rowcall(name, body, S, _row_tile(S), [_row(o), _row(h, VD, r_blk), _row(dog)], [o_norm],
                   [(VD, F32), (VD, MXU_DTYPE)], [(1, VD)])


def gla_dh(name, dq, dk, dv, dr, dla, h, w2p, b_a, QK, VD, a_blk, HW):
    S = dq.shape[0]

    def body(i, ins, outs):
        dq_ref, dk_ref, dv_ref, dr_ref, dla_ref, a_ref, w_ref, b_ref = ins
        dh_ref, dw_ref, db_ref = outs
        _init_acc(i, (dw_ref, db_ref))
        a = a_ref[...]
        z = _dot(a, w_ref[...], NN) + b_ref[...]
        dz = dla_ref[...] * (1.0 / GLA_TAU) * _sigmoid(-z)
        dw_ref[...] += _dot(a, dz, TN)
        db_ref[...] += _colsum(dz)
        dh_ref[:, 0:QK] = dq_ref[...]
        dh_ref[:, QK:2 * QK] = dk_ref[...]
        dh_ref[:, 2 * QK:2 * QK + VD] = dv_ref[...]
        dh_ref[:, 2 * QK + VD:2 * QK + 2 * VD] = dr_ref[...]
        dh_ref[:, 2 * QK + 2 * VD:HW] = _dot(dz, w_ref[...], NT).astype(MXU_DTYPE)

    rows = [_row(dq), _row(dk), _row(dv), _row(dr), _row(dla), _row(h, LANE, a_blk)]
    return rowcall(name, body, S, _row_tile(S), rows, [w2p, b_a], [(HW, MXU_DTYPE)], [(LANE, QK), (1, QK)])


CONV_ROWS = 512
HALO = 8


def _gelu(x):
    return 0.5 * x * (1.0 + jnp.tanh(GELU_C * (x + GELU_A * x * x * x)))


def _gelu_grad(x):
    t = jnp.tanh(GELU_C * (x + GELU_A * x * x * x))
    return 0.5 * (1.0 + t) + 0.5 * x * (1.0 - t * t) * GELU_C * (1.0 + 3.0 * GELU_A * x * x)


def _shift_down(x, prev, d):
    row = lax.broadcasted_iota(jnp.int32, x.shape, 0)
    out = pltpu.roll(x, d, 0)
    for t in range(d):
        out = jnp.where(row == t, prev[HALO - d + t:HALO - d + t + 1, :], out)
    return out


def _shift_up(x, nxt, d):
    n = x.shape[0]
    row = lax.broadcasted_iota(jnp.int32, x.shape, 0)
    out = pltpu.roll(x, n - d, 0)
    for t in range(d):
        out = jnp.where(row == n - d + t, nxt[t:t + 1, :], out)
    return out


def _conv_taps(ref, r, rc):
    x = ref[r * rc:(r + 1) * rc, :]
    prev = ref[r * rc - HALO:r * rc, :] if r > 0 else jnp.zeros((HALO, x.shape[1]), F32)
    return x, _shift_down(x, prev, 1), _shift_down(x, prev, 2)


def _conv_apply(taps, w_ref, b_ref):
    x0, x1, x2 = taps
    return x2 * w_ref[0:1, :] + x1 * w_ref[1:2, :] + x0 * w_ref[2:3, :] + b_ref[...]


def conv_fwd(name, hu, hg, cw_u, cw_g, cb_u, cb_g):
    S, F = hu.shape
    tc = LANE
    rc = _tile(S, CONV_ROWS, unit=16)

    def body(u_ref, g_ref, wu_ref, wg_ref, bu_ref, bg_ref, a_ref):
        for r in range(S // rc):
            uc = _conv_apply(_conv_taps(u_ref, r, rc), wu_ref, bu_ref)
            gc = _conv_apply(_conv_taps(g_ref, r, rc), wg_ref, bg_ref)
            a_ref[r * rc:(r + 1) * rc, :] = (uc * _gelu(gc)).astype(a_ref.dtype)

    col = lambda rows: pl.BlockSpec((rows, tc), lambda j: (0, j))
    return pl.pallas_call(
        body, name=name, grid=(F // tc,), out_shape=jax.ShapeDtypeStruct((S, F), MXU_DTYPE),
        in_specs=[col(S), col(S), col(3), col(3), col(1), col(1)], out_specs=col(S),
        compiler_params=_params())(hu, hg, cw_u, cw_g, cb_u, cb_g)


def conv_bwd(name, hu, hg, da, cw_u, cw_g, cb_u, cb_g):
    S, F = hu.shape
    tc = LANE
    rc = _tile(S, CONV_ROWS, unit=16)
    nr = S // rc

    def body(u_ref, g_ref, da_ref, wu_ref, wg_ref, bu_ref, bg_ref,
             dhu_ref, dhg_ref, dwu_ref, dwg_ref, dbu_ref, dbg_ref, du_scr, dg_scr):
        dw = [[jnp.zeros((1, tc), F32) for _ in range(3)] for _ in range(2)]
        db = [jnp.zeros((1, tc), F32) for _ in range(2)]
        for r in range(nr):
            sl = slice(r * rc, (r + 1) * rc)
            ut = _conv_taps(u_ref, r, rc)
            gt = _conv_taps(g_ref, r, rc)
            uc = _conv_apply(ut, wu_ref, bu_ref)
            gc = _conv_apply(gt, wg_ref, bg_ref)
            dav = da_ref[sl, :]
            duc = dav * _gelu(gc)
            dgc = dav * uc * _gelu_grad(gc)
            du_scr[sl, :] = duc
            dg_scr[sl, :] = dgc
            for part, (taps, d) in enumerate(((ut, duc), (gt, dgc))):
                db[part] = db[part] + _colsum(d)
                for tap in range(3):
                    dw[part][tap] = dw[part][tap] + _colsum(taps[2 - tap] * d)
        for part, (w_out, b_out) in enumerate(((dwu_ref, dbu_ref), (dwg_ref, dbg_ref))):
            b_out[...] = db[part]
            for tap in range(3):
                w_out[tap:tap + 1, :] = dw[part][tap]
        for scr, w_ref, out in ((du_scr, wu_ref, dhu_ref), (dg_scr, wg_ref, dhg_ref)):
            for r in range(nr):
                d = scr[r * rc:(r + 1) * rc, :]
                nxt = scr[(r + 1) * rc:(r + 1) * rc + HALO, :] if r + 1 < nr else jnp.zeros((HALO, tc), F32)
                dh = d * w_ref[2:3, :] + _shift_up(d, nxt, 1) * w_ref[1:2, :] + _shift_up(d, nxt, 2) * w_ref[0:1, :]
                out[r * rc:(r + 1) * rc, :] = dh.astype(out.dtype)

    col = lambda rows: pl.BlockSpec((rows, tc), lambda j: (0, j))
    sds = jax.ShapeDtypeStruct
    return pl.pallas_call(
        body, name=name, grid=(F // tc,),
        out_shape=(sds((S, F), MXU_DTYPE), sds((S, F), MXU_DTYPE), sds((3, F), F32), sds((3, F), F32),
                   sds((1, F), F32), sds((1, F), F32)),
        in_specs=[col(S), col(S), col(S), col(3), col(3), col(1), col(1)],
        out_specs=(col(S), col(S), col(3), col(3), col(1), col(1)),
        scratch_shapes=[pltpu.VMEM((S, tc), F32), pltpu.VMEM((S, tc), F32)],
        compiler_params=_params())(hu, hg, da, cw_u, cw_g, cb_u, cb_g)


def adamw(name, w, g, m, v):
    *lead, R, C = w.shape
    tr = _tile(R, max(8, (1 << 19) // max(C, 1) // 8 * 8), unit=8)

    def body(w_ref, g_ref, m_ref, v_ref, d_ref, nm_ref, nv_ref):
        gv = g_ref[...]
        mn = ADAM_B1 * m_ref[...] + (1.0 - ADAM_B1) * gv
        vn = ADAM_B2 * v_ref[...] + (1.0 - ADAM_B2) * (gv * gv)
        m_hat = mn / (1.0 - ADAM_B1 ** ADAM_STEP)
        v_hat = vn / (1.0 - ADAM_B2 ** ADAM_STEP)
        d_ref[...] = -ADAM_LR * (m_hat / (jnp.sqrt(v_hat) + ADAM_EPS) + ADAM_WD * w_ref[...])
        nm_ref[...] = mn
        nv_ref[...] = vn

    if lead:
        spec, grid = pl.BlockSpec((None, tr, C), lambda l, i: (l, i, 0)), (lead[0], R // tr)
    else:
        spec, grid = pl.BlockSpec((tr, C), lambda i: (i, 0)), (R // tr,)
    shp = jax.ShapeDtypeStruct(w.shape, F32)
    return pl.pallas_call(body, name=name, grid=grid, out_shape=(shp, shp, shp), in_specs=[spec] * 4,
                          out_specs=(spec, spec, spec), compiler_params=_params())(w, g, m, v)


def _position():
    return lax.axis_index('x'), lax.axis_index('y'), lax.axis_index('c')


def _other_chips(x, y):
    return ((1 - x, y), (x, 1 - y), (1 - x, 1 - y))


def _window(ref, dim, lo, n_lead, jj, rs, cs):
    if dim == 1:
        return ref.at[pl.ds(lo, n_lead), pl.ds(pl.multiple_of(jj * rs, 16), rs), :]
    return ref.at[pl.ds(lo, n_lead), :, pl.ds(pl.multiple_of(jj * cs, LANE), cs)]


def _hbm_call(name, body, out_shapes, n_sems, args, aliases=None):
    return pl.pallas_call(
        body, name=name, out_shape=tuple(out_shapes), in_specs=[ANY_SPEC] * len(args),
        out_specs=tuple(ANY_SPEC for _ in out_shapes),
        scratch_shapes=[pltpu.SemaphoreType.DMA((n,)) for n in n_sems],
        input_output_aliases=aliases or {},
        compiler_params=pltpu.CompilerParams(has_side_effects=True))(*args)


def place_shard(name, shard, dim, j_arr):
    L, rs, cs = shard.shape
    full_shape = (L, rs * N_CHIPS, cs) if dim == 1 else (L, rs, cs * N_CHIPS)
    tr = _tile(rs, max(16, (1 << 19) // cs // 16 * 16), unit=16)

    def body(j_ref, s_ref, o_ref):
        o_ref[...] = s_ref[...].astype(o_ref.dtype)

    if dim == 1:
        out_spec = pl.BlockSpec((None, tr, cs), lambda l, i, j: (l, j[0] * (rs // tr) + i, 0))
    else:
        out_spec = pl.BlockSpec((None, tr, cs), lambda l, i, j: (l, i, j[0]))
    grid_spec = pltpu.PrefetchScalarGridSpec(
        num_scalar_prefetch=1, grid=(L, rs // tr),
        in_specs=[pl.BlockSpec((None, tr, cs), lambda l, i, j: (l, i, 0))], out_specs=out_spec)
    return pl.pallas_call(body, name=name, grid_spec=grid_spec,
                          out_shape=jax.ShapeDtypeStruct(full_shape, MXU_DTYPE),
                          compiler_params=_params())(j_arr, shard)


class Exchange(NamedTuple):
    ins: tuple
    out_shapes: tuple
    in_place: bool
    sem_counts: tuple
    start: Callable
    finish: Callable


def run_exchange(name, ex):
    n_in, n_out = len(ex.ins), len(ex.out_shapes)

    def body(*refs):
        ins, outs, sems = refs[:n_in], refs[n_in:n_in + n_out], refs[n_in + n_out:]
        ex.start(ins, outs, sems)
        ex.finish(ins, outs, sems)

    return _hbm_call(name, body, ex.out_shapes, ex.sem_counts, ex.ins,
                     {k: k for k in range(n_in)} if ex.in_place else None)


def _rcopy(src, dst, ssem, rsem, device):
    return pltpu.make_async_remote_copy(src_ref=src, dst_ref=dst, send_sem=ssem, recv_sem=rsem,
                                        device_id=device, device_id_type=MESH)


def gather_exchange(fulls, dims, shard_shapes, items):
    n = len(items)

    def win(full, w, lo, nl, jj):
        return _window(full[w], dims[w], lo, nl, jj, shard_shapes[w][1], shard_shapes[w][2])

    def start(_, full, sems):
        ssem, rsem = sems[0], sems[1]
        x, y, c = _position()
        for it, (w, lo, nl, owner) in enumerate(items):
            @pl.when(c == owner)
            def _():
                own = win(full, w, lo, nl, 2 * x + y)
                for k, (cx, cy) in enumerate(_other_chips(x, y)):
                    _rcopy(own, own, ssem.at[3 * it + k], rsem.at[3 * it + k], (cx, cy, c)).start()

    def finish(_, full, sems):
        ssem, rsem, s2sem, r2sem = sems
        x, y, c = _position()
        sibling = (x, y, 1 - c)
        for it, (w, lo, nl, owner) in enumerate(items):
            @pl.when(c == owner)
            def _():
                for k, (cx, cy) in enumerate(_other_chips(x, y)):
                    theirs = win(full, w, lo, nl, 2 * cx + cy)
                    _rcopy(theirs, theirs, ssem.at[3 * it + k], rsem.at[3 * it + k], (cx, cy, c)).wait_recv()
                    _rcopy(theirs, theirs, s2sem.at[3 * it + k], r2sem.at[3 * it + k], sibling).start()
        for it, (w, lo, nl, owner) in enumerate(items):
            @pl.when(c == owner)
            def _():
                own = win(full, w, lo, nl, 2 * x + y)
                for k, (cx, cy) in enumerate(_other_chips(x, y)):
                    theirs = win(full, w, lo, nl, 2 * cx + cy)
                    _rcopy(own, own, ssem.at[3 * it + k], rsem.at[3 * it + k], (cx, cy, c)).wait_send()
                    _rcopy(theirs, theirs, s2sem.at[3 * it + k], r2sem.at[3 * it + k], sibling).wait_send()

            @pl.when(c != owner)
            def _():
                for k, (cx, cy) in enumerate(_other_chips(x, y)):
                    theirs = win(full, w, lo, nl, 2 * cx + cy)
                    _rcopy(theirs, theirs, s2sem.at[3 * it + k], r2sem.at[3 * it + k], sibling).wait_recv()

    shapes = tuple(jax.ShapeDtypeStruct(f.shape, f.dtype) for f in fulls)
    return Exchange(tuple(fulls), shapes, True, (3 * n,) * 4, start, finish)


def pair_exchange(grads, owners):
    n = len(grads)

    def start(gr, land, sems):
        x, y, c = _position()
        for it, owner in enumerate(owners):
            @pl.when(c != owner)
            def _():
                _rcopy(gr[it], land[it], sems[0].at[it], sems[1].at[it], (x, y, 1 - c)).start()

    def finish(gr, land, sems):
        x, y, c = _position()
        for it, owner in enumerate(owners):
            cp = _rcopy(gr[it], land[it], sems[0].at[it], sems[1].at[it], (x, y, 1 - c))
            pl.when(c != owner)(cp.wait_send)
            pl.when(c == owner)(cp.wait_recv)

    shapes = tuple(jax.ShapeDtypeStruct(g.shape, g.dtype) for g in grads)
    return Exchange(tuple(grads), shapes, False, (n, n), start, finish)


def pair_add(name, g, landed, owner, c_arr):
    R, C = g.shape
    tr = _tile(R, max(16, (1 << 20) // C // 16 * 16), unit=16)

    def body(c_ref, g_ref, l_ref, o_ref):
        @pl.when(c_ref[0] == owner)
        def _():
            o_ref[...] = (g_ref[...].astype(F32) + l_ref[...].astype(F32)).astype(o_ref.dtype)

    spec = pl.BlockSpec((tr, C), lambda i, c: (jnp.where(c[0] == owner, i, 0), 0))
    grid_spec = pltpu.PrefetchScalarGridSpec(num_scalar_prefetch=1, grid=(R // tr,), in_specs=[spec, spec],
                                             out_specs=spec)
    return pl.pallas_call(body, name=name, grid_spec=grid_spec, out_shape=jax.ShapeDtypeStruct(g.shape, g.dtype),
                          compiler_params=_params())(c_arr, g, landed)


def _window2(ref, dim, jj, rs, cs):
    if dim == 1:
        return ref.at[pl.ds(pl.multiple_of(jj * rs, 16), rs), :]
    return ref.at[:, pl.ds(pl.multiple_of(jj * cs, LANE), cs)]


def chip_exchange(partials, dims, shard_shapes, owners):
    n = len(partials)

    def copies(ps, land, sems, it):
        x, y, c = _position()
        rs, cs = shard_shapes[it]
        return [_rcopy(_window2(ps[it], dims[it], 2 * cx + cy, rs, cs), land[it].at[k],
                       sems[0].at[3 * it + k], sems[1].at[3 * it + k], (cx, cy, c))
                for k, (cx, cy) in enumerate(_other_chips(x, y))]

    def start(ps, land, sems):
        c = lax.axis_index('c')
        for it, owner in enumerate(owners):
            @pl.when(c == owner)
            def _():
                for cp in copies(ps, land, sems, it):
                    cp.start()

    def finish(ps, land, sems):
        c = lax.axis_index('c')
        for it, owner in enumerate(owners):
            @pl.when(c == owner)
            def _():
                for cp in copies(ps, land, sems, it):
                    cp.wait()

    shapes = tuple(jax.ShapeDtypeStruct((3,) + tuple(s), p.dtype) for p, s in zip(partials, shard_shapes))
    return Exchange(tuple(partials), shapes, False, (3 * n, 3 * n), start, finish)


def chip_sum(name, partial, landed, dim, buf, depth, layer, owner, j_arr, c_arr):
    _, rs, cs = landed.shape
    tr = _tile(rs, max(16, (1 << 19) // cs // 16 * 16), unit=16)
    has_buf = buf is not None

    def body(*refs):
        core_ref, p_ref, a_ref, b_ref, c_ref = refs[1:6]
        o_ref = refs[6 + has_buf]

        @pl.when(core_ref[0] == owner)
        def _():
            o_ref[...] = ((p_ref[...].astype(F32) + a_ref[...].astype(F32)) + b_ref[...].astype(F32)) + c_ref[...].astype(F32)

    def on(c, index):
        return jnp.where(c[0] == owner, index, 0)

    if dim == 1:
        own = pl.BlockSpec((tr, cs), lambda i, j, c: (on(c, j[0] * (rs // tr) + i), 0))
    else:
        own = pl.BlockSpec((tr, cs), lambda i, j, c: (on(c, i), on(c, j[0])))
    arrived = [pl.BlockSpec((None, tr, cs), lambda i, j, c, k=k: (k, on(c, i), 0)) for k in range(3)]
    grid_spec = pltpu.PrefetchScalarGridSpec(
        num_scalar_prefetch=2, grid=(rs // tr,), in_specs=[own] + arrived + ([ANY_SPEC] if has_buf else []),
        out_specs=pl.BlockSpec((None, tr, cs), lambda i, j, c: (layer, on(c, i), 0)))
    args = (j_arr, c_arr, partial, landed, landed, landed) + ((buf,) if has_buf else ())
    return pl.pallas_call(body, name=name, grid_spec=grid_spec,
                          out_shape=jax.ShapeDtypeStruct((depth, rs, cs), F32),
                          input_output_aliases={6: 0} if has_buf else {},
                          compiler_params=_params())(*args)


def share_exchange(bufs, items):
    n = len(items)

    def copy(full, sems, it):
        w, lo, nl, _ = items[it]
        x, y, c = _position()
        own = full[w].at[pl.ds(lo, nl)]
        return _rcopy(own, own, sems[0].at[it], sems[1].at[it], (x, y, 1 - c))

    def start(_, full, sems):
        c = lax.axis_index('c')
        for it in range(n):
            pl.when(c == items[it][3])(copy(full, sems, it).start)

    def finish(_, full, sems):
        c = lax.axis_index('c')
        for it in range(n):
            cp = copy(full, sems, it)
            pl.when(c == items[it][3])(cp.wait_send)
            pl.when(c != items[it][3])(cp.wait_recv)

    shapes = tuple(jax.ShapeDtypeStruct(f.shape, f.dtype) for f in bufs)
    return Exchange(tuple(bufs), shapes, True, (n, n), start, finish)


def exchange_small(name, pack, reduce_all):
    R = pack.shape[0]
    n_slots = N_DEVICES if reduce_all else N_CHIPS
    n_peers = n_slots - 1

    def body(p_ref, o_ref, buf, ssem, rsem):
        x, y, c = _position()
        if reduce_all:
            me = 4 * x + 2 * y + c
            peers = [(x ^ (k >> 2 & 1), y ^ (k >> 1 & 1), c ^ (k & 1)) for k in range(1, N_DEVICES)]
        else:
            me = 2 * x + y
            peers = [(cx, cy, c) for cx, cy in _other_chips(x, y)]
        buf[me] = p_ref[...]
        copies = []
        for k, peer in enumerate(peers):
            cp = pltpu.make_async_remote_copy(src_ref=p_ref, dst_ref=buf.at[me], send_sem=ssem.at[k],
                                              recv_sem=rsem.at[k], device_id=peer, device_id_type=MESH)
            cp.start()
            copies.append(cp)
        for k, (px, py, pc) in enumerate(peers):
            slot = 4 * px + 2 * py + pc if reduce_all else 2 * px + py
            pltpu.make_async_remote_copy(src_ref=p_ref, dst_ref=buf.at[slot], send_sem=ssem.at[k],
                                         recv_sem=rsem.at[k], device_id=(px, py, pc), device_id_type=MESH).wait_recv()
        for cp in copies:
            cp.wait_send()
        if reduce_all:
            acc = buf[0]
            for d in range(1, N_DEVICES):
                acc = acc + buf[d]
            o_ref[...] = acc
        else:
            o_ref[...] = buf[...]

    vmem = pl.BlockSpec(memory_space=pltpu.VMEM)
    out_shape = jax.ShapeDtypeStruct((R, LANE) if reduce_all else (N_CHIPS, R, LANE), F32)
    return pl.pallas_call(
        body, name=name, out_shape=out_shape, in_specs=[vmem], out_specs=vmem,
        scratch_shapes=[pltpu.VMEM((n_slots, R, LANE), F32), pltpu.SemaphoreType.DMA((n_peers,)),
                        pltpu.SemaphoreType.DMA((n_peers,))],
        compiler_params=_params())(pack)


def _pack(arrays):
    flat = jnp.concatenate([a.reshape(-1).astype(F32) for a in arrays])
    n = flat.shape[0]
    rows = -(-n // LANE)
    rows = -(-rows // 8) * 8
    return jnp.pad(flat, (0, rows * LANE - n)).reshape(rows, LANE)


def _unpack(pack, shapes, lead=()):
    flat = pack.reshape(lead + (-1,))
    out, off = [], 0
    for s in shapes:
        n = math.prod(s)
        out.append(flat[..., off:off + n].reshape(lead + tuple(s)))
        off += n
    return out


def kernel(x, p, positions, mla_w_in, mla_q_norm, mla_kv_norm, mla_w_uq, mla_w_uk, mla_w_uv, mla_w_o, gla_w_in, gla_w_a2, gla_b_a, gla_o_norm, gla_w_o, ln1_g, ln1_b, ln2_g, ln2_b, ffn_w_up, ffn_conv_w, ffn_conv_b, ffn_w_down, ple_w_proj, ple_w_gate, ple_b_gate, loss_target, m_mla_w_in, m_mla_q_norm, m_mla_kv_norm, m_mla_w_uq, m_mla_w_uk, m_mla_w_uv, m_mla_w_o, m_gla_w_in, m_gla_w_a2, m_gla_b_a, m_gla_o_norm, m_gla_w_o, m_ln1_g, m_ln1_b, m_ln2_g, m_ln2_b, m_ffn_w_up, m_ffn_conv_w, m_ffn_conv_b, m_ffn_w_down, m_ple_w_proj, m_ple_w_gate, m_ple_b_gate, v_mla_w_in, v_mla_q_norm, v_mla_kv_norm, v_mla_w_uq, v_mla_w_uk, v_mla_w_uv, v_mla_w_o, v_gla_w_in, v_gla_w_a2, v_gla_b_a, v_gla_o_norm, v_gla_w_o, v_ln1_g, v_ln1_b, v_ln2_g, v_ln2_b, v_ffn_w_up, v_ffn_conv_w, v_ffn_conv_b, v_ffn_w_down, v_ple_w_proj, v_ple_w_gate, v_ple_b_gate):
    given = dict(locals())
    S, D = x.shape[1], x.shape[2]
    QL, KL = mla_q_norm.shape[1], mla_kv_norm.shape[1]
    H = mla_w_uq.shape[2] * N_CHIPS // (MLA_NOPE + MLA_ROPE)
    QK, VD = gla_b_a.shape[1] * N_CHIPS, gla_o_norm.shape[1] * N_CHIPS
    FF = ffn_w_down.shape[1] * N_CHIPS
    GIN = 2 * QK + 2 * VD + GLA_GATE_RANK
    GIN_PAD = 2 * QK + 2 * VD + LANE
    MIN = QL + KL + MLA_ROPE
    MIN_PAD = QL + KL + LANE
    a_blk = (2 * QK + 2 * VD) // LANE
    r_blk = (2 * QK + VD) // VD
    xi, yi, ci = _position()
    chip = 2 * xi + yi
    c_arr = jnp.reshape(ci, (1,)).astype(jnp.int32)
    j_arr = jnp.reshape(chip, (1,)).astype(jnp.int32)

    names = [n for n, _ in BIG]
    big_dims = [1 if n == 'gla_w_in' else d for n, d in BIG]
    shapes = [given[n].shape for n in names]

    def layer_of(name, idx):
        return 2 * idx if name.startswith('mla') else 2 * idx + 1 if name.startswith('gla') else idx

    def owner_of(layer):
        return layer % 2

    def layer_runs(wanted):
        out = []
        for w, n in enumerate(names):
            for owner in (0, 1):
                idxs = [k for k in range(shapes[w][0])
                        if owner_of(layer_of(n, k)) == owner and wanted(n, layer_of(n, k))]
                for k in idxs:
                    if out and out[-1][0] == w and out[-1][3] == owner and out[-1][1] + out[-1][2] == k:
                        out[-1] = (w, out[-1][1], out[-1][2] + 1, owner)
                    else:
                        out.append((w, k, 1, owner))
        return out

    before_attention = ('mla_w_in', 'mla_w_uq', 'mla_w_uk', 'mla_w_uv')
    gather_phase = {0: lambda n, l: l == 0 and n not in before_attention or l in (1, 2),
                    2: lambda n, l: l == 3}
    placed = [place_shard(f'place_{n}', given[n], d, j_arr) for n, d in zip(names, big_dims)]
    first = run_exchange('gather_first', gather_exchange(
        placed, big_dims, shapes, layer_runs(lambda n, l: l == 0 and n in before_attention)))
    full = dict(zip(names, first))
    small_sharded = [n for n, d in SMALL if d is not None]
    spack = exchange_small('gather_small', _pack([given[n] for n in small_sharded]), False)
    parts = _unpack(spack, [given[n].shape for n in small_sharded], lead=(N_CHIPS,))
    for n, part in zip(small_sharded, parts):
        d = dict(SMALL)[n]
        full[n] = jnp.concatenate([part[k] for k in range(N_CHIPS)], axis=d)

    def mla_in_weights(j):
        w_in = jnp.pad(full['mla_w_in'][j], ((0, 0), (0, MIN_PAD - MIN)))
        w_uq = full['mla_w_uq'][j].reshape(QL, H, MLA_NOPE + MLA_ROPE)
        w_uq = jnp.pad(w_uq, ((0, 0), (0, 0), (0, HEAD_PAD - MLA_NOPE - MLA_ROPE))).reshape(QL, H * HEAD_PAD)
        return w_in, w_uq

    def gla_in_weight(j):
        w = full['gla_w_in'][j].reshape(N_CHIPS, D, GIN // N_CHIPS).transpose(1, 0, 2).reshape(D, GIN)
        return jnp.pad(w, ((0, 0), (0, GIN_PAD - GIN)))

    w_a2 = jnp.pad(full['gla_w_a2'], ((0, 0), (0, LANE - GLA_GATE_RANK), (0, 0))).astype(MXU_DTYPE)
    cw_u, cw_g = full['ffn_conv_w'][:, :, :FF], full['ffn_conv_w'][:, :, FF:]
    cb_u, cb_g = ffn_conv_b[:, None, :FF], ffn_conv_b[:, None, FF:]

    pos_row = positions[0]
    cid = pos_row // CHUNK
    tab_q = rope_tables(pos_row, MLA_NOPE)
    tab_k = rope_tables(pos_row, 0)

    def row1(a, i):
        return a[i:i + 1]

    saved = []
    xa = x[0]
    xb = xa
    for i in range(DEPTH):
        j = i // 2
        sv = dict(x=xa, xb=xb)
        if i % 2 == 0:
            w_in, w_uq = mla_in_weights(j)
            h = mm(f'mla_in_{i}', xb, w_in, 'nn', F32)
            cq, ckv, kr = mla_pre_fwd(f'mla_pre_{i}', h, row1(mla_q_norm, j), row1(mla_kv_norm, j), tab_k, QL, KL)
            q = mm(f'mla_uq_{i}', cq, w_uq, 'nn', MXU_DTYPE, epilogue=(rope_heads, tab_q))
            kn = mm(f'mla_uk_{i}', ckv, full['mla_w_uk'], 'nn', MXU_DTYPE, b_pre=(j,))
            vv = mm(f'mla_uv_{i}', ckv, full['mla_w_uv'], 'nn', MXU_DTYPE, b_pre=(j,))
            rider = gather_exchange([full[n] for n in names], big_dims, shapes, layer_runs(gather_phase[i]))
            (o, lse), gathered = attn_fwd(f'mla_attn_{i}', q, kn, kr, vv, cid, H, rider=rider)
            full.update(zip(names, gathered))
            mix = mm(f'mla_o_{i}', o, full['mla_w_o'], 'nn', F32, b_pre=(j,))
            sv.update(h=h, cq=cq, ckv=ckv, kr=kr, q=q, kn=kn, v=vv, o=o, lse=lse, w_in=w_in, w_uq=w_uq)
        else:
            w_gin = gla_in_weight(j)
            sv.update(w_gin=w_gin)
            h = mm(f'gla_in_{i}', xb, w_gin, 'nn', F32)
            qs, la = gla_pre_fwd(f'gla_pre_{i}', h, w_a2[j], row1(full['gla_b_a'], j), QK, a_blk)
            o, states = gla_fwd(f'gla_scan_{i}', qs, h, la, QK, VD)
            og = gla_post_fwd(f'gla_post_{i}', o, h, row1(full['gla_o_norm'], j), VD, r_blk)
            mix = mm(f'gla_o_{i}', og, full['gla_w_o'], 'nn', F32, b_pre=(j,))
            sv.update(h=h, qs=qs, la=la, o=o, states=states, og=og)
        x1, x1b = ln_fwd(f'ln1_{i}', xa, mix, row1(ln1_g, i), row1(ln1_b, i))
        hu = mm(f'ffn_up_u_{i}', x1b, full['ffn_w_up'], 'nn', F32, b_pre=(i,), b_win=(0, D, 0, FF))
        hg = mm(f'ffn_up_g_{i}', x1b, full['ffn_w_up'], 'nn', F32, b_pre=(i,), b_win=(0, D, FF, FF))
        act = conv_fwd(f'ffn_conv_{i}', hu, hg, cw_u[i], cw_g[i], cb_u[i], cb_g[i])
        f = mm(f'ffn_down_{i}', act, full['ffn_w_down'], 'nn', F32, b_pre=(i,))
        x2, x2b = ln_fwd(f'ln2_{i}', x1, f, row1(ln2_g, i), row1(ln2_b, i))
        gp = mm(f'ple_gate_{i}', x2b, full['ple_w_gate'], 'nn', F32, b_pre=(i,))
        pp = mm(f'ple_proj_{i}', p, full['ple_w_proj'], 'nn', F32, a_pre=(i, 0), b_pre=(i,))
        x3, x3b = ple_fwd(f'ple_{i}', x2, gp, pp, row1(ple_b_gate, i))
        sv.update(mix=mix, x1=x1, x1b=x1b, hu=hu, hg=hg, act=act, f=f, x2=x2, x2b=x2b, gp=gp, pp=pp)
        saved.append(sv)
        xa, xb = x3, x3b

    dy, sq = loss_head('loss_head', xa, loss_target[0])
    loss_part = (0.5 / D) * jnp.sum(sq)

    small_g = {n: [None] * given[n].shape[0] for n, _ in SMALL}
    tab_q_t = tuple(t.T for t in tab_q)
    shard_buf = {n: None for n in names}

    def reduce_begin(tag, items):
        owners = [owner_of(layer) for _, _, layer, _ in items]
        landed = run_exchange(f'pair_exchange_{tag}', pair_exchange([g for *_, g in items], owners))
        partial = [pair_add(f'pair_add_{n}_{layer}', g, l, o, c_arr)
                   for (n, _, layer, g), l, o in zip(items, landed, owners)]
        dims = [big_dims[names.index(n)] for n, *_ in items]
        exchange = chip_exchange(partial, dims, [given[n].shape[1:] for n, *_ in items], owners)
        return exchange, (items, partial, dims, owners)

    def reduce_end(context, arrived):
        items, partial, dims, owners = context
        for (n, idx, layer, _), pt, ar, d, o in zip(items, partial, arrived, dims, owners):
            shard_buf[n] = chip_sum(f'chip_sum_{n}_{layer}', pt, ar, d, shard_buf[n], given[n].shape[0], idx, o,
                                    j_arr, c_arr)

    def wgrad(name, a, b, a_pre=()):
        return mm(f'd_{name}_{i}', a, b, 'tn', MXU_DTYPE, a_pre=a_pre)

    def layer_items(layer_grads):
        return [(n, i if n.startswith(('ffn', 'ple')) else i // 2, i, g) for n, g in layer_grads.items()]

    dA, ca, dB = dy, 1.0, None
    waiting = []
    for i in reversed(range(DEPTH)):
        j = i // 2
        sv = saved[i]
        lg = {}
        dx3, dpp, dgp, dbg = ple_bwd(f'ple_b_{i}', dA, ca, dB, sv['gp'], sv['pp'], row1(ple_b_gate, i))
        small_g['ple_b_gate'][i] = dbg
        lg['ple_w_proj'] = wgrad('ple_w_proj', p, dpp, a_pre=(i, 0))
        lg['ple_w_gate'] = wgrad('ple_w_gate', sv['x2b'], dgp)
        dx2 = mm(f'ple_gate_b_{i}', dgp, full['ple_w_gate'], 'nt', F32, b_pre=(i,))
        dz2, dz2b, dg2, db2 = ln_bwd(f'ln2_b_{i}', sv['x1'], sv['f'], dx3, 1.0, dx2, row1(ln2_g, i))
        small_g['ln2_g'][i], small_g['ln2_b'][i] = dg2, db2
        lg['ffn_w_down'] = wgrad('ffn_w_down', sv['act'], dz2b)
        dact = mm(f'ffn_down_b_{i}', dz2b, full['ffn_w_down'], 'nt', F32, b_pre=(i,))
        dhu, dhg, dcwu, dcwg, dcbu, dcbg = conv_bwd(f'ffn_conv_b_{i}', sv['hu'], sv['hg'], dact,
                                                   cw_u[i], cw_g[i], cb_u[i], cb_g[i])
        small_g['ffn_conv_w'][i] = jnp.concatenate([dcwu, dcwg], -1)
        small_g['ffn_conv_b'][i] = jnp.concatenate([dcbu, dcbg], -1)
        up = mm(f'd_ffn_w_up_u_{i}', sv['x1b'], dhu, 'tn', MXU_DTYPE, out_stack=(None, (1, D, 2 * FF), 0, 0))
        up = mm(f'd_ffn_w_up_g_{i}', sv['x1b'], dhg, 'tn', MXU_DTYPE, out_stack=(up, (1, D, 2 * FF), 0, FF))
        lg['ffn_w_up'] = up[0]
        dx1 = mm(f'ffn_up_bu_{i}', dhu, full['ffn_w_up'], 'nt', F32, b_pre=(i,), b_win=(0, D, 0, FF))
        dx1 = mm(f'ffn_up_bg_{i}', dhg, full['ffn_w_up'], 'nt', F32, b_pre=(i,), b_win=(0, D, FF, FF), acc_in=dx1)
        dz1, dz1b, dg1, db1 = ln_bwd(f'ln1_b_{i}', sv['x'], sv['mix'], dz2, DN_ALPHA, dx1, row1(ln1_g, i))
        small_g['ln1_g'][i], small_g['ln1_b'][i] = dg1, db1
        if i % 2 == 0:
            lg['mla_w_o'] = wgrad('mla_w_o', sv['o'], dz1b)
            do = mm(f'mla_o_b_{i}', dz1b, full['mla_w_o'], 'nt', MXU_DTYPE, b_pre=(j,))
            waiting += layer_items(lg)
            lg = {}
            riding, context = reduce_begin(f'before_{i}', waiting)
            waiting = []
            (dq, dkn, dv, dkr), arrived = attn_bwd(f'mla_attn_b_{i}', sv['q'], sv['kn'], sv['kr'], sv['v'], sv['o'],
                                                   sv['lse'], do, cid, H, rider=riding)
            reduce_end(context, arrived)
            dqpre_t = rope_q_bwd(f'mla_rope_b_{i}', dq, tab_q_t, H)
            uq_t = mm(f'd_mla_w_uq_{i}', dqpre_t, sv['cq'], 'nn', MXU_DTYPE)
            lg['mla_w_uq'] = uq_t.T.reshape(QL, H, HEAD_PAD)[..., :MLA_NOPE + MLA_ROPE].reshape(QL, -1)
            lg['mla_w_uk'] = wgrad('mla_w_uk', sv['ckv'], dkn)
            lg['mla_w_uv'] = wgrad('mla_w_uv', sv['ckv'], dv)
            dcq = mm(f'mla_uq_b_{i}', dqpre_t, sv['w_uq'].T, 'tn', F32)
            dckv_a = mm(f'mla_uk_b_{i}', dkn, full['mla_w_uk'], 'nt', F32, b_pre=(j,))
            dckv_b = mm(f'mla_uv_b_{i}', dv, full['mla_w_uv'], 'nt', F32, b_pre=(j,))
            dh, dqn, dkvn = mla_pre_bwd(f'mla_pre_b_{i}', sv['h'], dcq, dckv_a, dckv_b, dkr,
                                        row1(mla_q_norm, j), row1(mla_kv_norm, j), tab_k, QL, KL, H)
            small_g['mla_q_norm'][j], small_g['mla_kv_norm'][j] = dqn, dkvn
            lg['mla_w_in'] = wgrad('mla_w_in', sv['xb'], dh)[:, :MIN]
            dmix = mm(f'mla_in_b_{i}', dh, sv['w_in'], 'nt', F32)
        else:
            lg['gla_w_o'] = wgrad('gla_w_o', sv['og'], dz1b)
            dog = mm(f'gla_o_b_{i}', dz1b, full['gla_w_o'], 'nt', F32, b_pre=(j,))
            do, dr, don = gla_post_bwd(f'gla_post_b_{i}', sv['o'], sv['h'], dog, row1(full['gla_o_norm'], j), VD, r_blk)
            dq, dk, dv, dla = gla_bwd(f'gla_scan_b_{i}', sv['qs'], sv['h'], sv['la'], sv['states'], do, QK, VD)
            dh, dw2, dba = gla_dh(f'gla_dh_{i}', dq, dk, dv, dr, dla, sv['h'], w_a2[j], row1(full['gla_b_a'], j),
                                  QK, VD, a_blk, GIN_PAD)
            small_g['gla_o_norm'][j], small_g['gla_b_a'][j] = don, dba
            small_g['gla_w_a2'][j] = dw2[:GLA_GATE_RANK]
            g_in = wgrad('gla_w_in', sv['xb'], dh)[:, :GIN]
            lg['gla_w_in'] = g_in.reshape(D, N_CHIPS, GIN // N_CHIPS).transpose(1, 0, 2).reshape(N_CHIPS * D, -1)
            dmix = mm(f'gla_in_b_{i}', dh, sv['w_gin'], 'nt', F32)
        dA, ca, dB = dz1, DN_ALPHA, dmix
        waiting += layer_items(lg)
    grad_x = axpy('grad_x', dA, ca, dB)[None]

    exchange, context = reduce_begin('last', waiting)
    reduce_end(context, run_exchange('chip_exchange_last', exchange))
    shared = run_exchange('share_with_sibling',
                          share_exchange([shard_buf[n] for n in names], layer_runs(lambda n, l: True)))
    grads = dict(zip(names, shared))

    small_names = [n for n, _ in SMALL]
    small_full = [jnp.concatenate([g.reshape((1,) + g.shape[-(given[n].ndim - 1):]) for g in small_g[n]], 0)
                  for n in small_names]
    pack = _pack([jnp.reshape(loss_part, (1,))] + [jnp.zeros((LANE - 1,), F32)] + small_full)
    red = exchange_small('reduce_small', pack, True)
    red_parts = _unpack(red, [(LANE,)] + [g.shape for g in small_full])
    loss = red_parts[0][0]
    for (n, d), g in zip(SMALL, red_parts[1:]):
        if d is not None:
            width = given[n].shape[d]
            g = lax.dynamic_slice_in_dim(g, chip * width, width, axis=d)
        grads[n] = g

    delta, new_m, new_v = {}, {}, {}
    for n in names:
        delta[n], new_m[n], new_v[n] = adamw(f'adamw_{n}', given[n], grads[n], given['m_' + n], given['v_' + n])
    packs = [_pack([given[pre + n] for n in small_names]) for pre in ('', 'm_', 'v_')]
    outs = adamw('adamw_small', packs[0], _pack([grads[n] for n in small_names]), packs[1], packs[2])
    small_shapes = [given[n].shape for n in small_names]
    for dst, out in zip((delta, new_m, new_v), outs):
        for n, a in zip(small_names, _unpack(out, small_shapes)):
            dst[n] = a

    return (loss, grad_x, *[grads[n] for n in WEIGHT_NAMES], *[delta[n] for n in WEIGHT_NAMES],
            *[new_m[n] for n in WEIGHT_NAMES], *[new_v[n] for n in WEIGHT_NAMES])
```

```python
import functools
import math
from typing import Callable, NamedTuple

import jax
import jax.numpy as jnp
from jax import lax
from jax.experimental import pallas as pl
from jax.experimental.pallas import tpu as pltpu

F32 = jnp.float32
MXU_DTYPE = jnp.bfloat16

DEPTH = 4
CHUNK = 64
Q_BLOCK = 128
MLA_NOPE = 128
MLA_ROPE = 64
MLA_V = 128
ROPE_THETA = 10000.0
GLA_HEADS = 4
GLA_GATE_RANK = 16
GLA_TAU = 16.0
DN_ALPHA = (2 * DEPTH) ** 0.25
EPS = 1e-5
NEG_INF = -1e30
ADAM_LR = 0.001
ADAM_B1 = 0.9
ADAM_B2 = 0.999
ADAM_EPS = 1e-08
ADAM_WD = 0.01
ADAM_STEP = 10
GELU_C = math.sqrt(2.0 / math.pi)
GELU_A = 0.044715
LOG2_E = math.log2(math.e)

LANE = 128
HEAD_PAD = 2 * LANE
VMEM_LIMIT_BYTES = 48 * 1024 * 1024

MESH = pl.DeviceIdType.MESH
ANY_SPEC = pl.BlockSpec(memory_space=pl.ANY)

WEIGHT_NAMES = ('mla_w_in', 'mla_q_norm', 'mla_kv_norm', 'mla_w_uq', 'mla_w_uk', 'mla_w_uv', 'mla_w_o',
                'gla_w_in', 'gla_w_a2', 'gla_b_a', 'gla_o_norm', 'gla_w_o', 'ln1_g', 'ln1_b', 'ln2_g', 'ln2_b',
                'ffn_w_up', 'ffn_conv_w', 'ffn_conv_b', 'ffn_w_down', 'ple_w_proj', 'ple_w_gate', 'ple_b_gate')
BIG = (('mla_w_in', 1), ('mla_w_uq', 2), ('mla_w_uk', 2), ('mla_w_uv', 2), ('mla_w_o', 1), ('gla_w_in', 2),
       ('gla_w_o', 1), ('ffn_w_up', 2), ('ffn_w_down', 1), ('ple_w_proj', 2), ('ple_w_gate', 1))
SMALL = (('mla_q_norm', None), ('mla_kv_norm', None), ('gla_w_a2', 2), ('gla_b_a', 1), ('gla_o_norm', 1),
         ('ln1_g', None), ('ln1_b', None), ('ln2_g', None), ('ln2_b', None), ('ffn_conv_w', 2),
         ('ffn_conv_b', None), ('ple_b_gate', None))
N_CHIPS = 4
N_DEVICES = 8


def _params():
    return pltpu.CompilerParams(vmem_limit_bytes=VMEM_LIMIT_BYTES)


def _tile(n, cap, *offsets, unit=LANE):
    g = n
    for o in offsets:
        if o:
            g = math.gcd(g, o)
    best = 0
    for d in range(unit, min(g, cap) + 1, unit):
        if g % d == 0:
            best = d
    if best:
        return best
    assert not any(offsets), (n, offsets)
    return n


def _dot(a, b, dims):
    return lax.dot_general(a.astype(MXU_DTYPE), b.astype(MXU_DTYPE), (dims, ((), ())),
                           preferred_element_type=F32)


NN = ((1,), (0,))
NT = ((1,), (1,))
TN = ((0,), (0,))


def mm(name, a, b, mode, out_dtype, *, a_pre=(), b_pre=(), b_win=None, acc_in=None, out_stack=None,
       epilogue=None):
    a2 = a.shape[len(a_pre):]
    b2 = b.shape[len(b_pre):]
    br0, brn, bc0, bcn = b_win or (0, b2[0], 0, b2[1])
    if mode == 'nn':
        (M, K), N, dims = a2, bcn, NN
        assert brn == K
    elif mode == 'nt':
        (M, K), N, dims = a2, brn, NT
        assert bcn == K
    else:
        (K, M), N, dims = a2, bcn, TN
        assert brn == K
    oc0 = out_stack[3] if out_stack else 0
    tm = _tile(M, 1408)
    if mode == 'nn':
        tn, tk = _tile(N, 1408, bc0, oc0), _tile(K, 1408, br0)
    elif mode == 'nt':
        tn, tk = _tile(N, 1408, br0, oc0), _tile(K, 1408, bc0)
    else:
        tn, tk = _tile(N, 1408, bc0, oc0), _tile(K, 1024, br0)
    nk = K // tk
    grid = (M // tm, N // tn, nk)

    na, nb = len(a_pre), len(b_pre)
    if mode == 'tn':
        a_spec = pl.BlockSpec((None,) * na + (tk, tm), lambda i, j, k: a_pre + (k, i))
    else:
        a_spec = pl.BlockSpec((None,) * na + (tm, tk), lambda i, j, k: a_pre + (i, k))
    if mode == 'nt':
        b_spec = pl.BlockSpec((None,) * nb + (tn, tk), lambda i, j, k: b_pre + (j + br0 // tn, k + bc0 // tk))
    else:
        b_spec = pl.BlockSpec((None,) * nb + (tk, tn), lambda i, j, k: b_pre + (k + br0 // tk, j + bc0 // tn))
    in_specs, args = [a_spec, b_spec], [a, b]
    if acc_in is not None:
        in_specs.append(pl.BlockSpec((tm, tn), lambda i, j, k: (i, j)))
        args.append(acc_in)
    aliases = {}
    if out_stack is None:
        out_shape = jax.ShapeDtypeStruct((M, N), out_dtype)
        out_spec = pl.BlockSpec((tm, tn), lambda i, j, k: (i, j))
    else:
        buf, full_shape, lead, _ = out_stack
        out_shape = jax.ShapeDtypeStruct(full_shape, out_dtype)
        out_spec = pl.BlockSpec((None, tm, tn), lambda i, j, k: (lead, i, j + oc0 // tn))
        if buf is not None:
            aliases = {len(args): 0}
            in_specs.append(ANY_SPEC)
            args.append(buf)
    has_c, has_alias = acc_in is not None, bool(aliases)
    finish, row_arrays = epilogue or (None, ())
    n_rows_in = len(row_arrays)
    for r in row_arrays:
        in_specs.append(pl.BlockSpec((tm, r.shape[1]), lambda i, j, k: (i, 0)))
        args.append(r)

    def body(*refs):
        a_ref, b_ref = refs[0], refs[1]
        c_ref = refs[2] if has_c else None
        first_row = 2 + has_c + has_alias
        o_ref = refs[first_row + n_rows_in]

        def result(tile):
            if finish is not None:
                tile = finish(tile, *[r[...] for r in refs[first_row:first_row + n_rows_in]])
            return tile.astype(out_dtype)

        prod = _dot(a_ref[...], b_ref[...], dims)
        if nk == 1:
            if has_c:
                prod = prod + c_ref[...]
            o_ref[...] = result(prod)
            return
        acc_ref = refs[first_row + n_rows_in + 1]
        k = pl.program_id(2)

        @pl.when(k == 0)
        def _():
            acc_ref[...] = prod + c_ref[...] if has_c else prod

        @pl.when(k > 0)
        def _():
            acc_ref[...] += prod

        @pl.when(k == nk - 1)
        def _():
            o_ref[...] = result(acc_ref[...])

    scratch = [pltpu.VMEM((tm, tn), F32)] if nk > 1 else []
    return pl.pallas_call(body, out_shape=out_shape, grid=grid, in_specs=in_specs, out_specs=out_spec,
                          scratch_shapes=scratch, input_output_aliases=aliases, name=name,
                          compiler_params=_params())(*args)


def rowcall(name, body, n_rows, ts, row_ins, full_ins, row_outs, acc_outs=()):
    in_specs, args = [], []
    for arr, width, colblk, pre in row_ins:
        in_specs.append(pl.BlockSpec((None,) * len(pre) + (ts, width),
                                     lambda i, pre=pre, cb=colblk: pre + (i, cb)))
        args.append(arr)
    for arr in full_ins:
        in_specs.append(pl.BlockSpec(arr.shape, lambda i, nd=arr.ndim: (0,) * nd))
        args.append(arr)
    out_shape, out_specs = [], []
    for width, dtype in row_outs:
        out_shape.append(jax.ShapeDtypeStruct((n_rows, width), dtype))
        out_specs.append(pl.BlockSpec((ts, width), lambda i: (i, 0)))
    for shape in acc_outs:
        out_shape.append(jax.ShapeDtypeStruct(shape, F32))
        out_specs.append(pl.BlockSpec(shape, lambda i, nd=len(shape): (0,) * nd))
    n_in = len(args)

    def kern(*refs):
        body(pl.program_id(0), refs[:n_in], refs[n_in:])

    return pl.pallas_call(kern, out_shape=tuple(out_shape), grid=(n_rows // ts,), in_specs=in_specs,
                          out_specs=tuple(out_specs), name=name, compiler_params=_params())(*args)


def _row(arr, width=None, colblk=0, pre=()):
    return (arr, arr.shape[-1] if width is None else width, colblk, pre)


def _init_acc(i, refs):
    @pl.when(i == 0)
    def _():
        for r in refs:
            r[...] = jnp.zeros(r.shape, r.dtype)


def _colsum(v):
    return jnp.sum(v, axis=0, keepdims=True)


def _sigmoid(z):
    return 1.0 / (1.0 + jnp.exp(-z))


def _row_tile(S):
    return _tile(S, 256, unit=16)


def _ln_stats(x_ref, m_ref):
    z = DN_ALPHA * x_ref[...] + m_ref[...]
    mu = jnp.mean(z, -1, keepdims=True)
    zc = z - mu
    var = jnp.mean(zc * zc, -1, keepdims=True)
    r = lax.rsqrt(var + EPS)
    return zc * r, r


def ln_fwd(name, x, m, g, b):
    S, D = x.shape

    def body(i, ins, outs):
        x_ref, m_ref, g_ref, b_ref = ins
        xh, _ = _ln_stats(x_ref, m_ref)
        y = xh * g_ref[...] + b_ref[...]
        outs[0][...] = y
        outs[1][...] = y.astype(MXU_DTYPE)

    return rowcall(name, body, S, _row_tile(S), [_row(x), _row(m)], [g, b], [(D, F32), (D, MXU_DTYPE)])


def ln_bwd(name, x, m, dA, ca, dB, g):
    S, D = x.shape
    has_b = dB is not None

    def body(i, ins, outs):
        x_ref, m_ref, a_ref = ins[:3]
        g_ref = ins[-1]
        dz_ref, dzb_ref, dg_ref, db_ref = outs
        _init_acc(i, (dg_ref, db_ref))
        dy = ca * a_ref[...]
        if has_b:
            dy = dy + ins[3][...]
        xh, r = _ln_stats(x_ref, m_ref)
        dg_ref[...] += _colsum(dy * xh)
        db_ref[...] += _colsum(dy)
        dxh = dy * g_ref[...]
        dz = r * (dxh - jnp.mean(dxh, -1, keepdims=True) - xh * jnp.mean(dxh * xh, -1, keepdims=True))
        dz_ref[...] = dz
        dzb_ref[...] = dz.astype(MXU_DTYPE)

    rows = [_row(x), _row(m), _row(dA)] + ([_row(dB)] if has_b else [])
    return rowcall(name, body, S, _row_tile(S), rows, [g], [(D, F32), (D, MXU_DTYPE)], [(1, D), (1, D)])


def ple_fwd(name, x2, gp, pp, bias):
    S, D = x2.shape

    def body(i, ins, outs):
        x_ref, gp_ref, pp_ref, b_ref = ins
        y = x_ref[...] + _sigmoid(gp_ref[...] + b_ref[...]) * pp_ref[...]
        outs[0][...] = y
        outs[1][...] = y.astype(MXU_DTYPE)

    return rowcall(name, body, S, _row_tile(S), [_row(x2), _row(gp), _row(pp)], [bias],
                   [(D, F32), (D, MXU_DTYPE)])


def ple_bwd(name, dA, ca, dB, gp, pp, bias):
    S, D = gp.shape
    has_b = dB is not None

    def body(i, ins, outs):
        a_ref = ins[0]
        gp_ref, pp_ref, b_ref = ins[-3:]
        dx_ref, dpp_ref, dgp_ref, db_ref = outs
        _init_acc(i, (db_ref,))
        dx = ca * a_ref[...]
        if has_b:
            dx = dx + ins[1][...]
        gate = _sigmoid(gp_ref[...] + b_ref[...])
        dgp = dx * pp_ref[...] * gate * (1.0 - gate)
        dx_ref[...] = dx
        dpp_ref[...] = (dx * gate).astype(MXU_DTYPE)
        dgp_ref[...] = dgp.astype(MXU_DTYPE)
        db_ref[...] += _colsum(dgp)

    rows = [_row(dA)] + ([_row(dB)] if has_b else []) + [_row(gp), _row(pp)]
    return rowcall(name, body, S, _row_tile(S), rows, [bias],
                   [(D, F32), (D, MXU_DTYPE), (D, MXU_DTYPE)], [(1, D)])


def loss_head(name, y, target):
    S, D = y.shape

    def body(i, ins, outs):
        _init_acc(i, (outs[1],))
        e = ins[0][...] - ins[1][...]
        outs[0][...] = e * (1.0 / D)
        outs[1][...] += _colsum(e * e)

    return rowcall(name, body, S, _row_tile(S), [_row(y), _row(target)], [], [(D, F32)], [(1, D)])


def axpy(name, a, ca, b):
    S, D = a.shape

    def body(i, ins, outs):
        outs[0][...] = ca * ins[0][...] + ins[1][...]

    return rowcall(name, body, S, _row_tile(S), [_row(a), _row(b)], [], [(D, F32)])[0]


HALF_ROPE = MLA_ROPE // 2


def _rope(x, c, sa, sb):
    n = x.shape[-1]
    return x * c + pltpu.roll(x, n - HALF_ROPE, 1) * sa + pltpu.roll(x, HALF_ROPE, 1) * sb


def _rope_t(d, c, sa, sb):
    n = d.shape[-1]
    return d * c + pltpu.roll(d * sa, HALF_ROPE, 1) + pltpu.roll(d * sb, n - HALF_ROPE, 1)


def rope_tables(positions_row, n_lead):
    inv = 1.0 / (ROPE_THETA ** (jnp.arange(0, MLA_ROPE, 2, dtype=F32) / MLA_ROPE))
    ang = positions_row.astype(F32)[:, None] * inv
    cos, sin = jnp.cos(ang), jnp.sin(ang)
    S = cos.shape[0]
    z = jnp.zeros((S, HALF_ROPE), F32)
    tail = jnp.zeros((S, LANE - MLA_ROPE), F32)
    c = jnp.concatenate([jnp.ones((S, n_lead), F32), cos, cos, tail], -1)
    sa = jnp.concatenate([jnp.zeros((S, n_lead), F32), -sin, z, tail], -1)
    sb = jnp.concatenate([jnp.zeros((S, n_lead), F32), z, sin, tail], -1)
    return c, sa, sb


def mla_pre_fwd(name, h, qn, kvn, tab_k, QL, KL):
    S = h.shape[0]

    def body(i, ins, outs):
        h_ref, c_ref, sa_ref, sb_ref, qn_ref, kvn_ref = ins
        cq = h_ref[:, 0:QL]
        ckv = h_ref[:, QL:QL + KL]
        kr = h_ref[:, QL + KL:QL + KL + LANE]
        outs[0][...] = (cq * lax.rsqrt(jnp.mean(cq * cq, -1, keepdims=True) + EPS) * qn_ref[...]).astype(MXU_DTYPE)
        outs[1][...] = (ckv * lax.rsqrt(jnp.mean(ckv * ckv, -1, keepdims=True) + EPS) * kvn_ref[...]).astype(MXU_DTYPE)
        outs[2][...] = _rope(kr, c_ref[...], sa_ref[...], sb_ref[...]).astype(MXU_DTYPE)

    rows = [_row(h)] + [_row(t) for t in tab_k]
    return rowcall(name, body, S, _row_tile(S), rows, [qn, kvn],
                   [(QL, MXU_DTYPE), (KL, MXU_DTYPE), (LANE, MXU_DTYPE)])


def _rms_bwd(x, g, dy):
    r = lax.rsqrt(jnp.mean(x * x, -1, keepdims=True) + EPS)
    dg = _colsum(dy * x * r)
    dxg = dy * g
    dx = r * dxg - x * (r * r * r) * jnp.mean(dxg * x, -1, keepdims=True)
    return dx, dg


def mla_pre_bwd(name, h, dcq, dckv_a, dckv_b, dkr_heads, qn, kvn, tab_k, QL, KL, H):
    S, HW = h.shape

    def body(i, ins, outs):
        h_ref, dcq_ref, da_ref, db_ref, dkr_ref, c_ref, sa_ref, sb_ref, qn_ref, kvn_ref = ins
        dh_ref, dqn_ref, dkvn_ref = outs
        _init_acc(i, (dqn_ref, dkvn_ref))
        dx, dg = _rms_bwd(h_ref[:, 0:QL], qn_ref[...], dcq_ref[...])
        dh_ref[:, 0:QL] = dx.astype(MXU_DTYPE)
        dqn_ref[...] += dg
        dx, dg = _rms_bwd(h_ref[:, QL:QL + KL], kvn_ref[...], da_ref[...] + db_ref[...])
        dh_ref[:, QL:QL + KL] = dx.astype(MXU_DTYPE)
        dkvn_ref[...] += dg
        d = dkr_ref[:, 0:LANE]
        for hh in range(1, H):
            d = d + dkr_ref[:, hh * LANE:(hh + 1) * LANE]
        dh_ref[:, QL + KL:QL + KL + LANE] = _rope_t(d, c_ref[...], sa_ref[...], sb_ref[...]).astype(MXU_DTYPE)

    rows = [_row(h), _row(dcq), _row(dckv_a), _row(dckv_b), _row(dkr_heads)] + [_row(t) for t in tab_k]
    return rowcall(name, body, S, _row_tile(S), rows, [qn, kvn], [(HW, MXU_DTYPE)], [(1, QL), (1, KL)])


def rope_heads(tile, c, sa, sb):
    return jnp.concatenate([_rope(tile[:, h0:h0 + HEAD_PAD], c, sa, sb)
                            for h0 in range(0, tile.shape[1], HEAD_PAD)], axis=-1)


ATT_T_FWD = 1024
ATT_T_BWD = 512


def _att_mask_t(ck_col, cq_row, k0, q0, t):
    kpos = k0 + lax.broadcasted_iota(jnp.int32, (t, 1), 0)
    qpos = q0 + lax.broadcasted_iota(jnp.int32, (1, t), 1)
    return (ck_col <= cq_row) & (kpos <= (qpos | (Q_BLOCK - 1)))


def _transpose(a):
    return a.astype(F32).T.astype(a.dtype)


def hosted_call(body, *, name, grid, in_specs, out_specs, out_shape, scratch_shapes, args, rider):
    if rider is None:
        return pl.pallas_call(body, name=name, grid=grid, in_specs=in_specs, out_specs=out_specs,
                              out_shape=out_shape, scratch_shapes=scratch_shapes,
                              compiler_params=_params())(*args), ()
    n_in, n_out, n_scr = len(args), len(out_shape), len(scratch_shapes)
    r_in, r_out = len(rider.ins), len(rider.out_shapes)

    def hosted(*refs):
        ins, refs = refs[:n_in], refs[n_in:]
        rin, refs = refs[:r_in], refs[r_in:]
        outs, refs = refs[:n_out], refs[n_out:]
        rout, refs = refs[:r_out], refs[r_out:]
        scr, sems = refs[:n_scr], refs[n_scr:]
        ids = [pl.program_id(a) for a in range(len(grid))]
        first = functools.reduce(jnp.logical_and, [i == 0 for i in ids])
        last = functools.reduce(jnp.logical_and, [i == g - 1 for i, g in zip(ids, grid)])

        @pl.when(first)
        def _():
            rider.start(rin, rout, sems)

        body(*ins, *outs, *scr)

        @pl.when(last)
        def _():
            rider.finish(rin, rout, sems)

    results = pl.pallas_call(
        hosted, name=name, grid=grid, in_specs=list(in_specs) + [ANY_SPEC] * r_in,
        out_specs=tuple(out_specs) + (ANY_SPEC,) * r_out, out_shape=tuple(out_shape) + tuple(rider.out_shapes),
        scratch_shapes=list(scratch_shapes) + [pltpu.SemaphoreType.DMA((k,)) for k in rider.sem_counts],
        input_output_aliases={n_in + k: n_out + k for k in range(r_in)} if rider.in_place else {},
        compiler_params=pltpu.CompilerParams(vmem_limit_bytes=VMEM_LIMIT_BYTES, has_side_effects=True),
    )(*args, *rider.ins)
    return results[:n_out], results[n_out:]


def attn_fwd(name, q, kn, kr, v, cid, H, rider=None):
    S = q.shape[0]
    t = _tile(S, ATT_T_FWD)
    n = S // t
    scale = (MLA_NOPE + MLA_ROPE) ** -0.5
    cid_col, cid_blk = cid.reshape(S, 1), cid.reshape(n, 1, t)

    def body(q_ref, kn_ref, kr_ref, v_ref, cc_ref, cr_ref, o_ref, lse_ref, k_scr, vt_scr):
        i = pl.program_id(1)

        @pl.when(i == 0)
        def _():
            k_scr[:, 0:LANE] = kn_ref[...]
            k_scr[:, LANE:HEAD_PAD] = kr_ref[...]
            for jj in range(n):
                vt_scr[jj] = _transpose(v_ref[jj * t:(jj + 1) * t, :])

        qv = q_ref[...]
        cq = cr_ref[i]

        def update(j, carry, masked):
            m, l, acc = carry
            k0 = pl.multiple_of(j * t, t)
            s = _dot(k_scr[pl.ds(k0, t), :], qv, NT) * (scale * LOG2_E)
            if masked:
                s = jnp.where(_att_mask_t(cc_ref[pl.ds(k0, t), :], cq, k0, i * t, t), s, NEG_INF)
            m_new = jnp.maximum(m, jnp.max(s, 0, keepdims=True))
            p = jnp.exp2(s - m_new)
            alpha = jnp.exp2(m - m_new)
            l = alpha * l + jnp.sum(p, 0, keepdims=True)
            acc = alpha * acc + _dot(vt_scr[j], p, NN)
            return m_new, l, acc

        init = (jnp.full((1, t), NEG_INF, F32), jnp.zeros((1, t), F32), jnp.zeros((MLA_V, t), F32))
        carry = lax.fori_loop(0, i, lambda j, c: update(j, c, False), init)
        m, l, acc = update(i, carry, True)
        o_ref[...] = (acc / l).T.astype(o_ref.dtype)
        lse_ref[...] = m * (1.0 / LOG2_E) + jnp.log(l)

    return hosted_call(
        body, name=name, grid=(H, n),
        in_specs=[pl.BlockSpec((t, HEAD_PAD), lambda h, i: (i, h)),
                  pl.BlockSpec((S, MLA_NOPE), lambda h, i: (0, h)),
                  pl.BlockSpec((S, LANE), lambda h, i: (0, 0)),
                  pl.BlockSpec((S, MLA_V), lambda h, i: (0, h)),
                  pl.BlockSpec((S, 1), lambda h, i: (0, 0)),
                  pl.BlockSpec(cid_blk.shape, lambda h, i: (0, 0, 0))],
        out_specs=(pl.BlockSpec((t, MLA_V), lambda h, i: (i, h)),
                   pl.BlockSpec((None, None, 1, t), lambda h, i: (h, i, 0, 0))),
        scratch_shapes=[pltpu.VMEM((S, HEAD_PAD), MXU_DTYPE), pltpu.VMEM((n, MLA_V, t), MXU_DTYPE)],
        out_shape=(jax.ShapeDtypeStruct((S, H * MLA_V), MXU_DTYPE), jax.ShapeDtypeStruct((H, n, 1, t), F32)),
        args=(q, kn, kr, v, cid_col, cid_blk), rider=rider)


def attn_bwd(name, q, kn, kr, v, o, lse, do, cid, H, rider=None):
    S = q.shape[0]
    t = _tile(S, ATT_T_BWD)
    n = S // t
    scale = (MLA_NOPE + MLA_ROPE) ** -0.5
    cid_col, cid_blk = cid.reshape(S, 1), cid.reshape(n, 1, t)
    lse = lse.reshape(H, n, 1, t)

    def body(q_ref, kn_ref, kr_ref, v_ref, o_ref, lse_ref, do_ref, cc_ref, cr_ref,
             dq_ref, dkn_ref, dv_ref, dkr_ref, delta_scr, dk_scr, dv_scr):
        j = pl.program_id(1)

        @pl.when(j == 0)
        def _():
            dq_ref[...] = jnp.zeros(dq_ref.shape, F32)
            for ii in range(n):
                sl = slice(ii * t, (ii + 1) * t)
                prod = do_ref[sl, :].astype(F32) * o_ref[sl, :].astype(F32)
                delta_scr[ii] = jnp.sum(prod.T, 0, keepdims=True)

        kv = jnp.concatenate([kn_ref[...], kr_ref[...]], axis=-1)
        kt = _transpose(kv)
        vv = v_ref[...]
        ck = cc_ref[...]
        k0 = j * t
        dk_scr[...] = jnp.zeros(dk_scr.shape, F32)
        dv_scr[...] = jnp.zeros(dv_scr.shape, F32)

        def update(i, masked):
            q0 = pl.multiple_of(i * t, t)
            qv = q_ref[pl.ds(q0, t), :]
            dov = do_ref[pl.ds(q0, t), :]
            s = _dot(kv, qv, NT) * scale
            if masked:
                s = jnp.where(_att_mask_t(ck, cr_ref[i], k0, q0, t), s, NEG_INF)
            p = jnp.exp(s - lse_ref[i])
            dv_scr[...] += _dot(p, dov, NN)
            dp = _dot(vv, dov, NT)
            ds = p * (dp - delta_scr[i]) * scale
            dk_scr[...] += _dot(ds, qv, NN)
            dq_ref[i] += _dot(kt, ds, NN)

        def step(i, carry):
            update(i, False)
            return carry

        update(j, True)
        lax.fori_loop(j + 1, n, step, 0)
        dkn_ref[...] = dk_scr[:, 0:LANE].astype(dkn_ref.dtype)
        dkr_ref[...] = dk_scr[:, LANE:HEAD_PAD]
        dv_ref[...] = dv_scr[...].astype(dv_ref.dtype)

    head_rows = lambda w: pl.BlockSpec((S, w), lambda h, j: (0, h))
    tile_rows = lambda w: pl.BlockSpec((t, w), lambda h, j: (j, h))
    return hosted_call(
        body, name=name, grid=(H, n),
        in_specs=[head_rows(HEAD_PAD), tile_rows(MLA_NOPE), pl.BlockSpec((t, LANE), lambda h, j: (j, 0)),
                  tile_rows(MLA_V), head_rows(MLA_V),
                  pl.BlockSpec((None, n, 1, t), lambda h, j: (h, 0, 0, 0)), head_rows(MLA_V),
                  pl.BlockSpec((t, 1), lambda h, j: (j, 0)),
                  pl.BlockSpec(cid_blk.shape, lambda h, j: (0, 0, 0))],
        out_specs=(pl.BlockSpec((None, n, HEAD_PAD, t), lambda h, j: (h, 0, 0, 0)),
                   tile_rows(MLA_NOPE), tile_rows(MLA_V), tile_rows(LANE)),
        scratch_shapes=[pltpu.VMEM((n, 1, t), F32), pltpu.VMEM((t, HEAD_PAD), F32), pltpu.VMEM((t, MLA_V), F32)],
        out_shape=(jax.ShapeDtypeStruct((H, n, HEAD_PAD, t), F32), jax.ShapeDtypeStruct((S, H * MLA_NOPE), MXU_DTYPE),
                   jax.ShapeDtypeStruct((S, H * MLA_V), MXU_DTYPE), jax.ShapeDtypeStruct((S, H * LANE), F32)),
        args=(q, kn, kr, v, o, lse, do, cid_col, cid_blk), rider=rider)


def rope_q_bwd(name, dq_t, tab_q_t, H):
    _, n, _, t = dq_t.shape

    def body(d_ref, c_ref, sa_ref, sb_ref, o_ref):
        c, sa, sb = c_ref[...], sa_ref[...], sb_ref[...]
        for hh in range(H):
            d = d_ref[hh]
            out = d * c + pltpu.roll(d * sa, HALF_ROPE, 0) + pltpu.roll(d * sb, HEAD_PAD - HALF_ROPE, 0)
            o_ref[hh * HEAD_PAD:(hh + 1) * HEAD_PAD, :] = out.astype(o_ref.dtype)

    tab = pl.BlockSpec((HEAD_PAD, t), lambda i: (0, i))
    return pl.pallas_call(
        body, name=name, grid=(n,), out_shape=jax.ShapeDtypeStruct((H * HEAD_PAD, n * t), MXU_DTYPE),
        in_specs=[pl.BlockSpec((H, None, HEAD_PAD, t), lambda i: (0, i, 0, 0)), tab, tab, tab],
        out_specs=pl.BlockSpec((H * HEAD_PAD, t), lambda i: (0, i)),
        compiler_params=_params())(dq_t, *tab_q_t)


GLA_GROUP = 8


def _prefix_rows(x):
    n = x.shape[0]
    row = lax.broadcasted_iota(jnp.int32, x.shape, 0)
    d = 1
    while d < n:
        x = x + jnp.where(row >= d, pltpu.roll(x, d, 0), 0.0)
        d *= 2
    return x


def _suffix_rows(x):
    n = x.shape[0]
    row = lax.broadcasted_iota(jnp.int32, x.shape, 0)
    d = 1
    while d < n:
        x = x + jnp.where(row < n - d, pltpu.roll(x, n - d, 0), 0.0)
        d *= 2
    return x


def _log_sigmoid(z):
    return jnp.minimum(z, 0.0) - jnp.log(1.0 + jnp.exp(-jnp.abs(z)))


def gla_pre_fwd(name, h, w2p, b_a, QK, a_blk):
    S = h.shape[0]
    dk = QK // GLA_HEADS

    def body(i, ins, outs):
        q_ref, a_ref, w_ref, b_ref = ins
        outs[0][...] = (q_ref[...] * (dk ** -0.5)).astype(MXU_DTYPE)
        z = _dot(a_ref[...], w_ref[...], NN) + b_ref[...]
        outs[1][...] = _log_sigmoid(z) / GLA_TAU

    return rowcall(name, body, S, _row_tile(S), [_row(h, QK, 0), _row(h, LANE, a_blk)], [w2p, b_a],
                   [(QK, MXU_DTYPE), (QK, F32)])


def _gla_specs(S, QK, VD, rows, gmap):
    dk, dv = QK // GLA_HEADS, VD // GLA_HEADS
    return dict(
        qs=pl.BlockSpec((rows, dk), lambda h, g: (gmap(g), h)),
        k=pl.BlockSpec((rows, dk), lambda h, g: (gmap(g), QK // dk + h)),
        v=pl.BlockSpec((rows, dv), lambda h, g: (gmap(g), 2 * QK // dv + h)),
        la=pl.BlockSpec((rows, dk), lambda h, g: (gmap(g), h)),
        o=pl.BlockSpec((rows, dv), lambda h, g: (gmap(g), h)))


def gla_fwd(name, qs, h, la, QK, VD):
    S = qs.shape[0]
    dk, dv = QK // GLA_HEADS, VD // GLA_HEADS
    n_chunks = S // CHUNK
    cg = min(GLA_GROUP, n_chunks)
    rows = cg * CHUNK
    sp = _gla_specs(S, QK, VD, rows, lambda g: g)

    def body(q_ref, k_ref, v_ref, la_ref, o_ref, st_ref, state):
        @pl.when(pl.program_id(1) == 0)
        def _():
            state[...] = jnp.zeros(state.shape, F32)

        for c in range(cg):
            sl = slice(c * CHUNK, (c + 1) * CHUNK)
            cum = _prefix_rows(la_ref[sl, :])
            tot = cum[CHUNK - 1:CHUNK, :]
            kdec = k_ref[sl, :] * jnp.exp(tot - cum)
            st = state[...] * jnp.exp(tot) + _dot(v_ref[sl, :], kdec, TN)
            state[...] = st
            st_ref[c] = st
            o_ref[sl, :] = _dot(q_ref[sl, :], st, NT)

    return pl.pallas_call(
        body, name=name, grid=(GLA_HEADS, n_chunks // cg),
        out_shape=(jax.ShapeDtypeStruct((S, VD), F32), jax.ShapeDtypeStruct((GLA_HEADS, n_chunks, dv, dk), F32)),
        in_specs=[sp['qs'], sp['k'], sp['v'], sp['la']],
        out_specs=(sp['o'], pl.BlockSpec((None, cg, dv, dk), lambda h, g: (h, g, 0, 0))),
        scratch_shapes=[pltpu.VMEM((dv, dk), F32)],
        compiler_params=_params())(qs, h, h, la)


def gla_bwd(name, qs, h, la, states, do, QK, VD):
    S = qs.shape[0]
    dk, dv = QK // GLA_HEADS, VD // GLA_HEADS
    n_chunks = S // CHUNK
    cg = min(GLA_GROUP, n_chunks)
    ng = n_chunks // cg
    rows = cg * CHUNK
    rev = lambda g: ng - 1 - g
    sp = _gla_specs(S, QK, VD, rows, rev)

    def body(q_ref, k_ref, v_ref, la_ref, st_ref, prev_ref, do_ref, dq_ref, dk_ref, dv_ref, dla_ref, dst):
        g = pl.program_id(1)

        @pl.when(g == 0)
        def _():
            dst[...] = jnp.zeros(dst.shape, F32)

        first_group = (g == ng - 1).astype(F32)
        for c in reversed(range(cg)):
            sl = slice(c * CHUNK, (c + 1) * CHUNK)
            cum = _prefix_rows(la_ref[sl, :])
            tot = cum[CHUNK - 1:CHUNK, :]
            e = jnp.exp(tot - cum)
            kdec = k_ref[sl, :] * e
            decay = jnp.exp(tot)
            st = st_ref[c]
            st_prev = st_ref[c - 1] if c > 0 else prev_ref[0] * (1.0 - first_group)
            dov = do_ref[sl, :]
            qv = q_ref[sl, :]
            dq_ref[sl, :] = (_dot(dov, st, NN) * (dk ** -0.5)).astype(dq_ref.dtype)
            d = dst[...] + _dot(dov, qv, TN)
            ddecay = _colsum(d * st_prev)
            dkdec = _dot(v_ref[sl, :], d, NN)
            dv_ref[sl, :] = _dot(kdec, d, NT).astype(dv_ref.dtype)
            dk_ref[sl, :] = (dkdec * e).astype(dk_ref.dtype)
            darg = dkdec * kdec
            dtot = _colsum(darg) + ddecay * decay
            dla_ref[sl, :] = dtot - _suffix_rows(darg)
            dst[...] = d * decay

    prev_spec = pl.BlockSpec((None, 1, dv, dk), lambda h, g: (h, jnp.maximum(rev(g) * cg - 1, 0), 0, 0))
    return pl.pallas_call(
        body, name=name, grid=(GLA_HEADS, ng),
        out_shape=(jax.ShapeDtypeStruct((S, QK), MXU_DTYPE), jax.ShapeDtypeStruct((S, QK), MXU_DTYPE),
                   jax.ShapeDtypeStruct((S, VD), MXU_DTYPE), jax.ShapeDtypeStruct((S, QK), F32)),
        in_specs=[sp['qs'], sp['k'], sp['v'], sp['la'],
                  pl.BlockSpec((None, cg, dv, dk), lambda h, g: (h, rev(g), 0, 0)), prev_spec, sp['o']],
        out_specs=(sp['qs'], sp['qs'], sp['o'], sp['la']),
        scratch_shapes=[pltpu.VMEM((dv, dk), F32)],
        compiler_params=_params())(qs, h, h, la, states, states, do)


def _head_norm(o):
    mu = jnp.mean(o, -1, keepdims=True)
    oc = o - mu
    r = lax.rsqrt(jnp.mean(oc * oc, -1, keepdims=True) + EPS)
    return oc * r, r


def gla_post_fwd(name, o, h, o_norm, VD, r_blk):
    S = o.shape[0]
    dv = VD // GLA_HEADS

    def body(i, ins, outs):
        o_ref, r_ref, w_ref = ins
        for hh in range(GLA_HEADS):
            sl = slice(hh * dv, (hh + 1) * dv)
            xh, _ = _head_norm(o_ref[:, sl])
            r = r_ref[:, sl]
            outs[0][:, sl] = (xh * w_ref[:, sl] * (r * _sigmoid(r))).astype(MXU_DTYPE)

    return rowcall(name, body, S, _row_tile(S), [_row(o), _row(h, VD, r_blk)], [o_norm], [(VD, MXU_DTYPE)])[0]


def gla_post_bwd(name, o, h, dog, o_norm, VD, r_blk):
    S = o.shape[0]
    dv = VD // GLA_HEADS

    def body(i, ins, outs):
        o_ref, r_ref, dog_ref, w_ref = ins
        do_ref, dr_ref, dw_ref = outs
        _init_acc(i, (dw_ref,))
        for hh in range(GLA_HEADS):
            sl = slice(hh * dv, (hh + 1) * dv)
            xh, rs = _head_norm(o_ref[:, sl])
            r = r_ref[:, sl]
            w = w_ref[:, sl]
            dog = dog_ref[:, sl]
            sg = _sigmoid(r)
            dn = dog * (r * sg)
            dr_ref[:, sl] = (dog * (xh * w) * (sg * (1.0 + r * (1.0 - sg)))).astype(MXU_DTYPE)
            dw_ref[:, sl] += _colsum(dn * xh)
            dxh = dn * w
            do_ref[:, sl] = rs * (dxh - jnp.mean(dxh, -1, keepdims=True) - xh * jnp.mean(dxh * xh, -1, keepdims=True))

    return rowcall(name, body, S, _row_tile(S), [_row(o), _row(h, VD, r_blk), _row(dog)], [o_norm],
                   [(VD, F32), (VD, MXU_DTYPE)], [(1, VD)])


def gla_dh(name, dq, dk, dv, dr, dla, h, w2p, b_a, QK, VD, a_blk, HW):
    S = dq.shape[0]

    def body(i, ins, outs):
        dq_ref, dk_ref, dv_ref, dr_ref, dla_ref, a_ref, w_ref, b_ref = ins
        dh_ref, dw_ref, db_ref = outs
        _init_acc(i, (dw_ref, db_ref))
        a = a_ref[...]
        z = _dot(a, w_ref[...], NN) + b_ref[...]
        dz = dla_ref[...] * (1.0 / GLA_TAU) * _sigmoid(-z)
        dw_ref[...] += _dot(a, dz, TN)
        db_ref[...] += _colsum(dz)
        dh_ref[:, 0:QK] = dq_ref[...]
        dh_ref[:, QK:2 * QK] = dk_ref[...]
        dh_ref[:, 2 * QK:2 * QK + VD] = dv_ref[...]
        dh_ref[:, 2 * QK + VD:2 * QK + 2 * VD] = dr_ref[...]
        dh_ref[:, 2 * QK + 2 * VD:HW] = _dot(dz, w_ref[...], NT).astype(MXU_DTYPE)

    rows = [_row(dq), _row(dk), _row(dv), _row(dr), _row(dla), _row(h, LANE, a_blk)]
    return rowcall(name, body, S, _row_tile(S), rows, [w2p, b_a], [(HW, MXU_DTYPE)], [(LANE, QK), (1, QK)])


CONV_ROWS = 512
HALO = 8


def _gelu(x):
    return 0.5 * x * (1.0 + jnp.tanh(GELU_C * (x + GELU_A * x * x * x)))


def _gelu_grad(x):
    t = jnp.tanh(GELU_C * (x + GELU_A * x * x * x))
    return 0.5 * (1.0 + t) + 0.5 * x * (1.0 - t * t) * GELU_C * (1.0 + 3.0 * GELU_A * x * x)


def _shift_down(x, prev, d):
    row = lax.broadcasted_iota(jnp.int32, x.shape, 0)
    out = pltpu.roll(x, d, 0)
    for t in range(d):
        out = jnp.where(row == t, prev[HALO - d + t:HALO - d + t + 1, :], out)
    return out


def _shift_up(x, nxt, d):
    n = x.shape[0]
    row = lax.broadcasted_iota(jnp.int32, x.shape, 0)
    out = pltpu.roll(x, n - d, 0)
    for t in range(d):
        out = jnp.where(row == n - d + t, nxt[t:t + 1, :], out)
    return out


def _conv_taps(ref, r, rc):
    x = ref[r * rc:(r + 1) * rc, :]
    prev = ref[r * rc - HALO:r * rc, :] if r > 0 else jnp.zeros((HALO, x.shape[1]), F32)
    return x, _shift_down(x, prev, 1), _shift_down(x, prev, 2)


def _conv_apply(taps, w_ref, b_ref):
    x0, x1, x2 = taps
    return x2 * w_ref[0:1, :] + x1 * w_ref[1:2, :] + x0 * w_ref[2:3, :] + b_ref[...]


def conv_fwd(name, hu, hg, cw_u, cw_g, cb_u, cb_g):
    S, F = hu.shape
    tc = LANE
    rc = _tile(S, CONV_ROWS, unit=16)

    def body(u_ref, g_ref, wu_ref, wg_ref, bu_ref, bg_ref, a_ref):
        for r in range(S // rc):
            uc = _conv_apply(_conv_taps(u_ref, r, rc), wu_ref, bu_ref)
            gc = _conv_apply(_conv_taps(g_ref, r, rc), wg_ref, bg_ref)
            a_ref[r * rc:(r + 1) * rc, :] = (uc * _gelu(gc)).astype(a_ref.dtype)

    col = lambda rows: pl.BlockSpec((rows, tc), lambda j: (0, j))
    return pl.pallas_call(
        body, name=name, grid=(F // tc,), out_shape=jax.ShapeDtypeStruct((S, F), MXU_DTYPE),
        in_specs=[col(S), col(S), col(3), col(3), col(1), col(1)], out_specs=col(S),
        compiler_params=_params())(hu, hg, cw_u, cw_g, cb_u, cb_g)


def conv_bwd(name, hu, hg, da, cw_u, cw_g, cb_u, cb_g):
    S, F = hu.shape
    tc = LANE
    rc = _tile(S, CONV_ROWS, unit=16)
    nr = S // rc

    def body(u_ref, g_ref, da_ref, wu_ref, wg_ref, bu_ref, bg_ref,
             dhu_ref, dhg_ref, dwu_ref, dwg_ref, dbu_ref, dbg_ref, du_scr, dg_scr):
        dw = [[jnp.zeros((1, tc), F32) for _ in range(3)] for _ in range(2)]
        db = [jnp.zeros((1, tc), F32) for _ in range(2)]
        for r in range(nr):
            sl = slice(r * rc, (r + 1) * rc)
            ut = _conv_taps(u_ref, r, rc)
            gt = _conv_taps(g_ref, r, rc)
            uc = _conv_apply(ut, wu_ref, bu_ref)
            gc = _conv_apply(gt, wg_ref, bg_ref)
            dav = da_ref[sl, :]
            duc = dav * _gelu(gc)
            dgc = dav * uc * _gelu_grad(gc)
            du_scr[sl, :] = duc
            dg_scr[sl, :] = dgc
            for part, (taps, d) in enumerate(((ut, duc), (gt, dgc))):
                db[part] = db[part] + _colsum(d)
                for tap in range(3):
                    dw[part][tap] = dw[part][tap] + _colsum(taps[2 - tap] * d)
        for part, (w_out, b_out) in enumerate(((dwu_ref, dbu_ref), (dwg_ref, dbg_ref))):
            b_out[...] = db[part]
            for tap in range(3):
                w_out[tap:tap + 1, :] = dw[part][tap]
        for scr, w_ref, out in ((du_scr, wu_ref, dhu_ref), (dg_scr, wg_ref, dhg_ref)):
            for r in range(nr):
                d = scr[r * rc:(r + 1) * rc, :]
                nxt = scr[(r + 1) * rc:(r + 1) * rc + HALO, :] if r + 1 < nr else jnp.zeros((HALO, tc), F32)
                dh = d * w_ref[2:3, :] + _shift_up(d, nxt, 1) * w_ref[1:2, :] + _shift_up(d, nxt, 2) * w_ref[0:1, :]
                out[r * rc:(r + 1) * rc, :] = dh.astype(out.dtype)

    col = lambda rows: pl.BlockSpec((rows, tc), lambda j: (0, j))
    sds = jax.ShapeDtypeStruct
    return pl.pallas_call(
        body, name=name, grid=(F // tc,),
        out_shape=(sds((S, F), MXU_DTYPE), sds((S, F), MXU_DTYPE), sds((3, F), F32), sds((3, F), F32),
                   sds((1, F), F32), sds((1, F), F32)),
        in_specs=[col(S), col(S), col(S), col(3), col(3), col(1), col(1)],
        out_specs=(col(S), col(S), col(3), col(3), col(1), col(1)),
        scratch_shapes=[pltpu.VMEM((S, tc), F32), pltpu.VMEM((S, tc), F32)],
        compiler_params=_params())(hu, hg, da, cw_u, cw_g, cb_u, cb_g)


def adamw(name, w, g, m, v):
    *lead, R, C = w.shape
    tr = _tile(R, max(8, (1 << 19) // max(C, 1) // 8 * 8), unit=8)

    def body(w_ref, g_ref, m_ref, v_ref, d_ref, nm_ref, nv_ref):
        gv = g_ref[...]
        mn = ADAM_B1 * m_ref[...] + (1.0 - ADAM_B1) * gv
        vn = ADAM_B2 * v_ref[...] + (1.0 - ADAM_B2) * (gv * gv)
        m_hat = mn / (1.0 - ADAM_B1 ** ADAM_STEP)
        v_hat = vn / (1.0 - ADAM_B2 ** ADAM_STEP)
        d_ref[...] = -ADAM_LR * (m_hat / (jnp.sqrt(v_hat) + ADAM_EPS) + ADAM_WD * w_ref[...])
        nm_ref[...] = mn
        nv_ref[...] = vn

    if lead:
        spec, grid = pl.BlockSpec((None, tr, C), lambda l, i: (l, i, 0)), (lead[0], R // tr)
    else:
        spec, grid = pl.BlockSpec((tr, C), lambda i: (i, 0)), (R // tr,)
    shp = jax.ShapeDtypeStruct(w.shape, F32)
    return pl.pallas_call(body, name=name, grid=grid, out_shape=(shp, shp, shp), in_specs=[spec] * 4,
                          out_specs=(spec, spec, spec), compiler_params=_params())(w, g, m, v)


def _position():
    return lax.axis_index('x'), lax.axis_index('y'), lax.axis_index('c')


def _other_chips(x, y):
    return ((1 - x, y), (x, 1 - y), (1 - x, 1 - y))


def _window(ref, dim, lo, n_lead, jj, rs, cs):
    if dim == 1:
        return ref.at[pl.ds(lo, n_lead), pl.ds(pl.multiple_of(jj * rs, 16), rs), :]
    return ref.at[pl.ds(lo, n_lead), :, pl.ds(pl.multiple_of(jj * cs, LANE), cs)]


def _hbm_call(name, body, out_shapes, n_sems, args, aliases=None):
    return pl.pallas_call(
        body, name=name, out_shape=tuple(out_shapes), in_specs=[ANY_SPEC] * len(args),
        out_specs=tuple(ANY_SPEC for _ in out_shapes),
        scratch_shapes=[pltpu.SemaphoreType.DMA((n,)) for n in n_sems],
        input_output_aliases=aliases or {},
        compiler_params=pltpu.CompilerParams(has_side_effects=True))(*args)


def place_shard(name, shard, dim, j_arr):
    L, rs, cs = shard.shape
    full_shape = (L, rs * N_CHIPS, cs) if dim == 1 else (L, rs, cs * N_CHIPS)
    tr = _tile(rs, max(16, (1 << 19) // cs // 16 * 16), unit=16)

    def body(j_ref, s_ref, o_ref):
        o_ref[...] = s_ref[...].astype(o_ref.dtype)

    if dim == 1:
        out_spec = pl.BlockSpec((None, tr, cs), lambda l, i, j: (l, j[0] * (rs // tr) + i, 0))
    else:
        out_spec = pl.BlockSpec((None, tr, cs), lambda l, i, j: (l, i, j[0]))
    grid_spec = pltpu.PrefetchScalarGridSpec(
        num_scalar_prefetch=1, grid=(L, rs // tr),
        in_specs=[pl.BlockSpec((None, tr, cs), lambda l, i, j: (l, i, 0))], out_specs=out_spec)
    return pl.pallas_call(body, name=name, grid_spec=grid_spec,
                          out_shape=jax.ShapeDtypeStruct(full_shape, MXU_DTYPE),
                          compiler_params=_params())(j_arr, shard)


class Exchange(NamedTuple):
    ins: tuple
    out_shapes: tuple
    in_place: bool
    sem_counts: tuple
    start: Callable
    finish: Callable


def run_exchange(name, ex):
    n_in, n_out = len(ex.ins), len(ex.out_shapes)

    def body(*refs):
        ins, outs, sems = refs[:n_in], refs[n_in:n_in + n_out], refs[n_in + n_out:]
        ex.start(ins, outs, sems)
        ex.finish(ins, outs, sems)

    return _hbm_call(name, body, ex.out_shapes, ex.sem_counts, ex.ins,
                     {k: k for k in range(n_in)} if ex.in_place else None)


def _rcopy(src, dst, ssem, rsem, device):
    return pltpu.make_async_remote_copy(src_ref=src, dst_ref=dst, send_sem=ssem, recv_sem=rsem,
                                        device_id=device, device_id_type=MESH)


def gather_exchange(fulls, dims, shard_shapes, items):
    n = len(items)

    def win(full, w, lo, nl, jj):
        return _window(full[w], dims[w], lo, nl, jj, shard_shapes[w][1], shard_shapes[w][2])

    def start(_, full, sems):
        ssem, rsem = sems[0], sems[1]
        x, y, c = _position()
        for it, (w, lo, nl, owner) in enumerate(items):
            @pl.when(c == owner)
            def _():
                own = win(full, w, lo, nl, 2 * x + y)
                for k, (cx, cy) in enumerate(_other_chips(x, y)):
                    _rcopy(own, own, ssem.at[3 * it + k], rsem.at[3 * it + k], (cx, cy, c)).start()

    def finish(_, full, sems):
        ssem, rsem, s2sem, r2sem = sems
        x, y, c = _position()
        sibling = (x, y, 1 - c)
        for it, (w, lo, nl, owner) in enumerate(items):
            @pl.when(c == owner)
            def _():
                for k, (cx, cy) in enumerate(_other_chips(x, y)):
                    theirs = win(full, w, lo, nl, 2 * cx + cy)
                    _rcopy(theirs, theirs, ssem.at[3 * it + k], rsem.at[3 * it + k], (cx, cy, c)).wait_recv()
                    _rcopy(theirs, theirs, s2sem.at[3 * it + k], r2sem.at[3 * it + k], sibling).start()
        for it, (w, lo, nl, owner) in enumerate(items):
            @pl.when(c == owner)
            def _():
                own = win(full, w, lo, nl, 2 * x + y)
                for k, (cx, cy) in enumerate(_other_chips(x, y)):
                    theirs = win(full, w, lo, nl, 2 * cx + cy)
                    _rcopy(own, own, ssem.at[3 * it + k], rsem.at[3 * it + k], (cx, cy, c)).wait_send()
                    _rcopy(theirs, theirs, s2sem.at[3 * it + k], r2sem.at[3 * it + k], sibling).wait_send()

            @pl.when(c != owner)
            def _():
                for k, (cx, cy) in enumerate(_other_chips(x, y)):
                    theirs = win(full, w, lo, nl, 2 * cx + cy)
                    _rcopy(theirs, theirs, s2sem.at[3 * it + k], r2sem.at[3 * it + k], sibling).wait_recv()

    shapes = tuple(jax.ShapeDtypeStruct(f.shape, f.dtype) for f in fulls)
    return Exchange(tuple(fulls), shapes, True, (3 * n,) * 4, start, finish)


def pair_exchange(grads, owners):
    n = len(grads)

    def start(gr, land, sems):
        x, y, c = _position()
        for it, owner in enumerate(owners):
            @pl.when(c != owner)
            def _():
                _rcopy(gr[it], land[it], sems[0].at[it], sems[1].at[it], (x, y, 1 - c)).start()

    def finish(gr, land, sems):
        x, y, c = _position()
        for it, owner in enumerate(owners):
            cp = _rcopy(gr[it], land[it], sems[0].at[it], sems[1].at[it], (x, y, 1 - c))
            pl.when(c != owner)(cp.wait_send)
            pl.when(c == owner)(cp.wait_recv)

    shapes = tuple(jax.ShapeDtypeStruct(g.shape, g.dtype) for g in grads)
    return Exchange(tuple(grads), shapes, False, (n, n), start, finish)


def pair_add(name, g, landed, owner, c_arr):
    R, C = g.shape
    tr = _tile(R, max(16, (1 << 20) // C // 16 * 16), unit=16)

    def body(c_ref, g_ref, l_ref, o_ref):
        @pl.when(c_ref[0] == owner)
        def _():
            o_ref[...] = (g_ref[...].astype(F32) + l_ref[...].astype(F32)).astype(o_ref.dtype)

    spec = pl.BlockSpec((tr, C), lambda i, c: (jnp.where(c[0] == owner, i, 0), 0))
    grid_spec = pltpu.PrefetchScalarGridSpec(num_scalar_prefetch=1, grid=(R // tr,), in_specs=[spec, spec],
                                             out_specs=spec)
    return pl.pallas_call(body, name=name, grid_spec=grid_spec, out_shape=jax.ShapeDtypeStruct(g.shape, g.dtype),
                          compiler_params=_params())(c_arr, g, landed)


def _window2(ref, dim, jj, rs, cs):
    if dim == 1:
        return ref.at[pl.ds(pl.multiple_of(jj * rs, 16), rs), :]
    return ref.at[:, pl.ds(pl.multiple_of(jj * cs, LANE), cs)]


def chip_exchange(partials, dims, shard_shapes, owners):
    n = len(partials)

    def copies(ps, land, sems, it):
        x, y, c = _position()
        rs, cs = shard_shapes[it]
        return [_rcopy(_window2(ps[it], dims[it], 2 * cx + cy, rs, cs), land[it].at[k],
                       sems[0].at[3 * it + k], sems[1].at[3 * it + k], (cx, cy, c))
                for k, (cx, cy) in enumerate(_other_chips(x, y))]

    def start(ps, land, sems):
        c = lax.axis_index('c')
        for it, owner in enumerate(owners):
            @pl.when(c == owner)
            def _():
                for cp in copies(ps, land, sems, it):
                    cp.start()

    def finish(ps, land, sems):
        c = lax.axis_index('c')
        for it, owner in enumerate(owners):
            @pl.when(c == owner)
            def _():
                for cp in copies(ps, land, sems, it):
                    cp.wait()

    shapes = tuple(jax.ShapeDtypeStruct((3,) + tuple(s), p.dtype) for p, s in zip(partials, shard_shapes))
    return Exchange(tuple(partials), shapes, False, (3 * n, 3 * n), start, finish)


def chip_sum(name, partial, landed, dim, buf, depth, layer, owner, j_arr, c_arr):
    _, rs, cs = landed.shape
    tr = _tile(rs, max(16, (1 << 19) // cs // 16 * 16), unit=16)
    has_buf = buf is not None

    def body(*refs):
        core_ref, p_ref, a_ref, b_ref, c_ref = refs[1:6]
        o_ref = refs[6 + has_buf]

        @pl.when(core_ref[0] == owner)
        def _():
            o_ref[...] = ((p_ref[...].astype(F32) + a_ref[...].astype(F32)) + b_ref[...].astype(F32)) + c_ref[...].astype(F32)

    def on(c, index):
        return jnp.where(c[0] == owner, index, 0)

    if dim == 1:
        own = pl.BlockSpec((tr, cs), lambda i, j, c: (on(c, j[0] * (rs // tr) + i), 0))
    else:
        own = pl.BlockSpec((tr, cs), lambda i, j, c: (on(c, i), on(c, j[0])))
    arrived = [pl.BlockSpec((None, tr, cs), lambda i, j, c, k=k: (k, on(c, i), 0)) for k in range(3)]
    grid_spec = pltpu.PrefetchScalarGridSpec(
        num_scalar_prefetch=2, grid=(rs // tr,), in_specs=[own] + arrived + ([ANY_SPEC] if has_buf else []),
        out_specs=pl.BlockSpec((None, tr, cs), lambda i, j, c: (layer, on(c, i), 0)))
    args = (j_arr, c_arr, partial, landed, landed, landed) + ((buf,) if has_buf else ())
    return pl.pallas_call(body, name=name, grid_spec=grid_spec,
                          out_shape=jax.ShapeDtypeStruct((depth, rs, cs), F32),
                          input_output_aliases={6: 0} if has_buf else {},
                          compiler_params=_params())(*args)


def share_exchange(bufs, items):
    n = len(items)

    def copy(full, sems, it):
        w, lo, nl, _ = items[it]
        x, y, c = _position()
        own = full[w].at[pl.ds(lo, nl)]
        return _rcopy(own, own, sems[0].at[it], sems[1].at[it], (x, y, 1 - c))

    def start(_, full, sems):
        c = lax.axis_index('c')
        for it in range(n):
            pl.when(c == items[it][3])(copy(full, sems, it).start)

    def finish(_, full, sems):
        c = lax.axis_index('c')
        for it in range(n):
            cp = copy(full, sems, it)
            pl.when(c == items[it][3])(cp.wait_send)
            pl.when(c != items[it][3])(cp.wait_recv)

    shapes = tuple(jax.ShapeDtypeStruct(f.shape, f.dtype) for f in bufs)
    return Exchange(tuple(bufs), shapes, True, (n, n), start, finish)


def exchange_small(name, pack, reduce_all):
    R = pack.shape[0]
    n_slots = N_DEVICES if reduce_all else N_CHIPS
    n_peers = n_slots - 1

    def body(p_ref, o_ref, buf, ssem, rsem):
        x, y, c = _position()
        if reduce_all:
            me = 4 * x + 2 * y + c
            peers = [(x ^ (k >> 2 & 1), y ^ (k >> 1 & 1), c ^ (k & 1)) for k in range(1, N_DEVICES)]
        else:
            me = 2 * x + y
            peers = [(cx, cy, c) for cx, cy in _other_chips(x, y)]
        buf[me] = p_ref[...]
        copies = []
        for k, peer in enumerate(peers):
            cp = pltpu.make_async_remote_copy(src_ref=p_ref, dst_ref=buf.at[me], send_sem=ssem.at[k],
                                              recv_sem=rsem.at[k], device_id=peer, device_id_type=MESH)
            cp.start()
            copies.append(cp)
        for k, (px, py, pc) in enumerate(peers):
            slot = 4 * px + 2 * py + pc if reduce_all else 2 * px + py
            pltpu.make_async_remote_copy(src_ref=p_ref, dst_ref=buf.at[slot], send_sem=ssem.at[k],
                                         recv_sem=rsem.at[k], device_id=(px, py, pc), device_id_type=MESH).wait_recv()
        for cp in copies:
            cp.wait_send()
        if reduce_all:
            acc = buf[0]
            for d in range(1, N_DEVICES):
                acc = acc + buf[d]
            o_ref[...] = acc
        else:
            o_ref[...] = buf[...]

    vmem = pl.BlockSpec(memory_space=pltpu.VMEM)
    out_shape = jax.ShapeDtypeStruct((R, LANE) if reduce_all else (N_CHIPS, R, LANE), F32)
    return pl.pallas_call(
        body, name=name, out_shape=out_shape, in_specs=[vmem], out_specs=vmem,
        scratch_shapes=[pltpu.VMEM((n_slots, R, LANE), F32), pltpu.SemaphoreType.DMA((n_peers,)),
                        pltpu.SemaphoreType.DMA((n_peers,))],
        compiler_params=_params())(pack)


def _pack(arrays):
    flat = jnp.concatenate([a.reshape(-1).astype(F32) for a in arrays])
    n = flat.shape[0]
    rows = -(-n // LANE)
    rows = -(-rows // 8) * 8
    return jnp.pad(flat, (0, rows * LANE - n)).reshape(rows, LANE)


def _unpack(pack, shapes, lead=()):
    flat = pack.reshape(lead + (-1,))
    out, off = [], 0
    for s in shapes:
        n = math.prod(s)
        out.append(flat[..., off:off + n].reshape(lead + tuple(s)))
        off += n
    return out


def kernel(x, p, positions, mla_w_in, mla_q_norm, mla_kv_norm, mla_w_uq, mla_w_uk, mla_w_uv, mla_w_o, gla_w_in, gla_w_a2, gla_b_a, gla_o_norm, gla_w_o, ln1_g, ln1_b, ln2_g, ln2_b, ffn_w_up, ffn_conv_w, ffn_conv_b, ffn_w_down, ple_w_proj, ple_w_gate, ple_b_gate, loss_target, m_mla_w_in, m_mla_q_norm, m_mla_kv_norm, m_mla_w_uq, m_mla_w_uk, m_mla_w_uv, m_mla_w_o, m_gla_w_in, m_gla_w_a2, m_gla_b_a, m_gla_o_norm, m_gla_w_o, m_ln1_g, m_ln1_b, m_ln2_g, m_ln2_b, m_ffn_w_up, m_ffn_conv_w, m_ffn_conv_b, m_ffn_w_down, m_ple_w_proj, m_ple_w_gate, m_ple_b_gate, v_mla_w_in, v_mla_q_norm, v_mla_kv_norm, v_mla_w_uq, v_mla_w_uk, v_mla_w_uv, v_mla_w_o, v_gla_w_in, v_gla_w_a2, v_gla_b_a, v_gla_o_norm, v_gla_w_o, v_ln1_g, v_ln1_b, v_ln2_g, v_ln2_b, v_ffn_w_up, v_ffn_conv_w, v_ffn_conv_b, v_ffn_w_down, v_ple_w_proj, v_ple_w_gate, v_ple_b_gate):
    given = dict(locals())
    S, D = x.shape[1], x.shape[2]
    QL, KL = mla_q_norm.shape[1], mla_kv_norm.shape[1]
    H = mla_w_uq.shape[2] * N_CHIPS // (MLA_NOPE + MLA_ROPE)
    QK, VD = gla_b_a.shape[1] * N_CHIPS, gla_o_norm.shape[1] * N_CHIPS
    FF = ffn_w_down.shape[1] * N_CHIPS
    GIN = 2 * QK + 2 * VD + GLA_GATE_RANK
    GIN_PAD = 2 * QK + 2 * VD + LANE
    MIN = QL + KL + MLA_ROPE
    MIN_PAD = QL + KL + LANE
    a_blk = (2 * QK + 2 * VD) // LANE
    r_blk = (2 * QK + VD) // VD
    xi, yi, ci = _position()
    chip = 2 * xi + yi
    c_arr = jnp.reshape(ci, (1,)).astype(jnp.int32)
    j_arr = jnp.reshape(chip, (1,)).astype(jnp.int32)

    names = [n for n, _ in BIG]
    big_dims = [1 if n == 'gla_w_in' else d for n, d in BIG]
    shapes = [given[n].shape for n in names]

    def layer_of(name, idx):
        return 2 * idx if name.startswith('mla') else 2 * idx + 1 if name.startswith('gla') else idx

    def owner_of(layer):
        return layer % 2

    def layer_runs(wanted):
        out = []
        for w, n in enumerate(names):
            for owner in (0, 1):
                idxs = [k for k in range(shapes[w][0])
                        if owner_of(layer_of(n, k)) == owner and wanted(n, layer_of(n, k))]
                for k in idxs:
                    if out and out[-1][0] == w and out[-1][3] == owner and out[-1][1] + out[-1][2] == k:
                        out[-1] = (w, out[-1][1], out[-1][2] + 1, owner)
                    else:
                        out.append((w, k, 1, owner))
        return out

    before_attention = ('mla_w_in', 'mla_w_uq', 'mla_w_uk', 'mla_w_uv')
    gather_phase = {0: lambda n, l: l == 1 or (l == 0) != (n in before_attention) and l in (0, 2),
                    2: lambda n, l: l == 3 or l == 2 and n not in before_attention}
    placed = [place_shard(f'place_{n}', given[n], d, j_arr) for n, d in zip(names, big_dims)]
    first = run_exchange('gather_first', gather_exchange(
        placed, big_dims, shapes, layer_runs(lambda n, l: l == 0 and n in before_attention)))
    full = dict(zip(names, first))
    small_sharded = [n for n, d in SMALL if d is not None]
    spack = exchange_small('gather_small', _pack([given[n] for n in small_sharded]), False)
    parts = _unpack(spack, [given[n].shape for n in small_sharded], lead=(N_CHIPS,))
    for n, part in zip(small_sharded, parts):
        d = dict(SMALL)[n]
        full[n] = jnp.concatenate([part[k] for k in range(N_CHIPS)], axis=d)

    def mla_in_weights(j):
        w_in = jnp.pad(full['mla_w_in'][j], ((0, 0), (0, MIN_PAD - MIN)))
        w_uq = full['mla_w_uq'][j].reshape(QL, H, MLA_NOPE + MLA_ROPE)
        w_uq = jnp.pad(w_uq, ((0, 0), (0, 0), (0, HEAD_PAD - MLA_NOPE - MLA_ROPE))).reshape(QL, H * HEAD_PAD)
        return w_in, w_uq

    def gla_in_weight(j):
        w = full['gla_w_in'][j].reshape(N_CHIPS, D, GIN // N_CHIPS).transpose(1, 0, 2).reshape(D, GIN)
        return jnp.pad(w, ((0, 0), (0, GIN_PAD - GIN)))

    w_a2 = jnp.pad(full['gla_w_a2'], ((0, 0), (0, LANE - GLA_GATE_RANK), (0, 0))).astype(MXU_DTYPE)
    cw_u, cw_g = full['ffn_conv_w'][:, :, :FF], full['ffn_conv_w'][:, :, FF:]
    cb_u, cb_g = ffn_conv_b[:, None, :FF], ffn_conv_b[:, None, FF:]

    pos_row = positions[0]
    cid = pos_row // CHUNK
    tab_q = rope_tables(pos_row, MLA_NOPE)
    tab_k = rope_tables(pos_row, 0)

    def row1(a, i):
        return a[i:i + 1]

    saved = []
    xa = x[0]
    xb = xa
    for i in range(DEPTH):
        j = i // 2
        sv = dict(x=xa, xb=xb)
        if i % 2 == 0:
            w_in, w_uq = mla_in_weights(j)
            h = mm(f'mla_in_{i}', xb, w_in, 'nn', F32)
            cq, ckv, kr = mla_pre_fwd(f'mla_pre_{i}', h, row1(mla_q_norm, j), row1(mla_kv_norm, j), tab_k, QL, KL)
            q = mm(f'mla_uq_{i}', cq, w_uq, 'nn', MXU_DTYPE, epilogue=(rope_heads, tab_q))
            kn = mm(f'mla_uk_{i}', ckv, full['mla_w_uk'], 'nn', MXU_DTYPE, b_pre=(j,))
            vv = mm(f'mla_uv_{i}', ckv, full['mla_w_uv'], 'nn', MXU_DTYPE, b_pre=(j,))
            rider = gather_exchange([full[n] for n in names], big_dims, shapes, layer_runs(gather_phase[i]))
            (o, lse), gathered = attn_fwd(f'mla_attn_{i}', q, kn, kr, vv, cid, H, rider=rider)
            full.update(zip(names, gathered))
            mix = mm(f'mla_o_{i}', o, full['mla_w_o'], 'nn', F32, b_pre=(j,))
            sv.update(h=h, cq=cq, ckv=ckv, kr=kr, q=q, kn=kn, v=vv, o=o, lse=lse, w_in=w_in, w_uq=w_uq)
        else:
            w_gin = gla_in_weight(j)
            sv.update(w_gin=w_gin)
            h = mm(f'gla_in_{i}', xb, w_gin, 'nn', F32)
            qs, la = gla_pre_fwd(f'gla_pre_{i}', h, w_a2[j], row1(full['gla_b_a'], j), QK, a_blk)
            o, states = gla_fwd(f'gla_scan_{i}', qs, h, la, QK, VD)
            og = gla_post_fwd(f'gla_post_{i}', o, h, row1(full['gla_o_norm'], j), VD, r_blk)
            mix = mm(f'gla_o_{i}', og, full['gla_w_o'], 'nn', F32, b_pre=(j,))
            sv.update(h=h, qs=qs, la=la, o=o, states=states, og=og)
        x1, x1b = ln_fwd(f'ln1_{i}', xa, mix, row1(ln1_g, i), row1(ln1_b, i))
        hu = mm(f'ffn_up_u_{i}', x1b, full['ffn_w_up'], 'nn', F32, b_pre=(i,), b_win=(0, D, 0, FF))
        hg = mm(f'ffn_up_g_{i}', x1b, full['ffn_w_up'], 'nn', F32, b_pre=(i,), b_win=(0, D, FF, FF))
        act = conv_fwd(f'ffn_conv_{i}', hu, hg, cw_u[i], cw_g[i], cb_u[i], cb_g[i])
        f = mm(f'ffn_down_{i}', act, full['ffn_w_down'], 'nn', F32, b_pre=(i,))
        x2, x2b = ln_fwd(f'ln2_{i}', x1, f, row1(ln2_g, i), row1(ln2_b, i))
        gp = mm(f'ple_gate_{i}', x2b, full['ple_w_gate'], 'nn', F32, b_pre=(i,))
        pp = mm(f'ple_proj_{i}', p, full['ple_w_proj'], 'nn', F32, a_pre=(i, 0), b_pre=(i,))
        x3, x3b = ple_fwd(f'ple_{i}', x2, gp, pp, row1(ple_b_gate, i))
        sv.update(mix=mix, x1=x1, x1b=x1b, hu=hu, hg=hg, act=act, f=f, x2=x2, x2b=x2b, gp=gp, pp=pp)
        saved.append(sv)
        xa, xb = x3, x3b

    dy, sq = loss_head('loss_head', xa, loss_target[0])
    loss_part = (0.5 / D) * jnp.sum(sq)

    small_g = {n: [None] * given[n].shape[0] for n, _ in SMALL}
    tab_q_t = tuple(t.T for t in tab_q)
    shard_buf = {n: None for n in names}

    def reduce_begin(tag, items):
        owners = [owner_of(layer) for _, _, layer, _ in items]
        landed = run_exchange(f'pair_exchange_{tag}', pair_exchange([g for *_, g in items], owners))
        partial = [pair_add(f'pair_add_{n}_{layer}', g, l, o, c_arr)
                   for (n, _, layer, g), l, o in zip(items, landed, owners)]
        dims = [big_dims[names.index(n)] for n, *_ in items]
        exchange = chip_exchange(partial, dims, [given[n].shape[1:] for n, *_ in items], owners)
        return exchange, (items, partial, dims, owners)

    def reduce_end(context, arrived):
        items, partial, dims, owners = context
        for (n, idx, layer, _), pt, ar, d, o in zip(items, partial, arrived, dims, owners):
            shard_buf[n] = chip_sum(f'chip_sum_{n}_{layer}', pt, ar, d, shard_buf[n], given[n].shape[0], idx, o,
                                    j_arr, c_arr)

    def wgrad(name, a, b, a_pre=()):
        return mm(f'd_{name}_{i}', a, b, 'tn', MXU_DTYPE, a_pre=a_pre)

    def layer_items(layer_grads):
        return [(n, i if n.startswith(('ffn', 'ple')) else i // 2, i, g) for n, g in layer_grads.items()]

    dA, ca, dB = dy, 1.0, None
    waiting = []
    for i in reversed(range(DEPTH)):
        j = i // 2
        sv = saved[i]
        lg = {}
        dx3, dpp, dgp, dbg = ple_bwd(f'ple_b_{i}', dA, ca, dB, sv['gp'], sv['pp'], row1(ple_b_gate, i))
        small_g['ple_b_gate'][i] = dbg
        lg['ple_w_proj'] = wgrad('ple_w_proj', p, dpp, a_pre=(i, 0))
        lg['ple_w_gate'] = wgrad('ple_w_gate', sv['x2b'], dgp)
        dx2 = mm(f'ple_gate_b_{i}', dgp, full['ple_w_gate'], 'nt', F32, b_pre=(i,))
        dz2, dz2b, dg2, db2 = ln_bwd(f'ln2_b_{i}', sv['x1'], sv['f'], dx3, 1.0, dx2, row1(ln2_g, i))
        small_g['ln2_g'][i], small_g['ln2_b'][i] = dg2, db2
        lg['ffn_w_down'] = wgrad('ffn_w_down', sv['act'], dz2b)
        dact = mm(f'ffn_down_b_{i}', dz2b, full['ffn_w_down'], 'nt', F32, b_pre=(i,))
        dhu, dhg, dcwu, dcwg, dcbu, dcbg = conv_bwd(f'ffn_conv_b_{i}', sv['hu'], sv['hg'], dact,
                                                   cw_u[i], cw_g[i], cb_u[i], cb_g[i])
        small_g['ffn_conv_w'][i] = jnp.concatenate([dcwu, dcwg], -1)
        small_g['ffn_conv_b'][i] = jnp.concatenate([dcbu, dcbg], -1)
        up = mm(f'd_ffn_w_up_u_{i}', sv['x1b'], dhu, 'tn', MXU_DTYPE, out_stack=(None, (1, D, 2 * FF), 0, 0))
        up = mm(f'd_ffn_w_up_g_{i}', sv['x1b'], dhg, 'tn', MXU_DTYPE, out_stack=(up, (1, D, 2 * FF), 0, FF))
        lg['ffn_w_up'] = up[0]
        dx1 = mm(f'ffn_up_bu_{i}', dhu, full['ffn_w_up'], 'nt', F32, b_pre=(i,), b_win=(0, D, 0, FF))
        dx1 = mm(f'ffn_up_bg_{i}', dhg, full['ffn_w_up'], 'nt', F32, b_pre=(i,), b_win=(0, D, FF, FF), acc_in=dx1)
        dz1, dz1b, dg1, db1 = ln_bwd(f'ln1_b_{i}', sv['x'], sv['mix'], dz2, DN_ALPHA, dx1, row1(ln1_g, i))
        small_g['ln1_g'][i], small_g['ln1_b'][i] = dg1, db1
        if i % 2 == 0:
            lg['mla_w_o'] = wgrad('mla_w_o', sv['o'], dz1b)
            do = mm(f'mla_o_b_{i}', dz1b, full['mla_w_o'], 'nt', MXU_DTYPE, b_pre=(j,))
            waiting += layer_items(lg)
            lg = {}
            riding, context = reduce_begin(f'before_{i}', waiting)
            waiting = []
            (dq, dkn, dv, dkr), arrived = attn_bwd(f'mla_attn_b_{i}', sv['q'], sv['kn'], sv['kr'], sv['v'], sv['o'],
                                                   sv['lse'], do, cid, H, rider=riding)
            reduce_end(context, arrived)
            dqpre_t = rope_q_bwd(f'mla_rope_b_{i}', dq, tab_q_t, H)
            uq_t = mm(f'd_mla_w_uq_{i}', dqpre_t, sv['cq'], 'nn', MXU_DTYPE)
            lg['mla_w_uq'] = uq_t.T.reshape(QL, H, HEAD_PAD)[..., :MLA_NOPE + MLA_ROPE].reshape(QL, -1)
            lg['mla_w_uk'] = wgrad('mla_w_uk', sv['ckv'], dkn)
            lg['mla_w_uv'] = wgrad('mla_w_uv', sv['ckv'], dv)
            dcq = mm(f'mla_uq_b_{i}', dqpre_t, sv['w_uq'].T, 'tn', F32)
            dckv_a = mm(f'mla_uk_b_{i}', dkn, full['mla_w_uk'], 'nt', F32, b_pre=(j,))
            dckv_b = mm(f'mla_uv_b_{i}', dv, full['mla_w_uv'], 'nt', F32, b_pre=(j,))
            dh, dqn, dkvn = mla_pre_bwd(f'mla_pre_b_{i}', sv['h'], dcq, dckv_a, dckv_b, dkr,
                                        row1(mla_q_norm, j), row1(mla_kv_norm, j), tab_k, QL, KL, H)
            small_g['mla_q_norm'][j], small_g['mla_kv_norm'][j] = dqn, dkvn
            lg['mla_w_in'] = wgrad('mla_w_in', sv['xb'], dh)[:, :MIN]
            dmix = mm(f'mla_in_b_{i}', dh, sv['w_in'], 'nt', F32)
        else:
            lg['gla_w_o'] = wgrad('gla_w_o', sv['og'], dz1b)
            dog = mm(f'gla_o_b_{i}', dz1b, full['gla_w_o'], 'nt', F32, b_pre=(j,))
            do, dr, don = gla_post_bwd(f'gla_post_b_{i}', sv['o'], sv['h'], dog, row1(full['gla_o_norm'], j), VD, r_blk)
            dq, dk, dv, dla = gla_bwd(f'gla_scan_b_{i}', sv['qs'], sv['h'], sv['la'], sv['states'], do, QK, VD)
            dh, dw2, dba = gla_dh(f'gla_dh_{i}', dq, dk, dv, dr, dla, sv['h'], w_a2[j], row1(full['gla_b_a'], j),
                                  QK, VD, a_blk, GIN_PAD)
            small_g['gla_o_norm'][j], small_g['gla_b_a'][j] = don, dba
            small_g['gla_w_a2'][j] = dw2[:GLA_GATE_RANK]
            g_in = wgrad('gla_w_in', sv['xb'], dh)[:, :GIN]
            lg['gla_w_in'] = g_in.reshape(D, N_CHIPS, GIN // N_CHIPS).transpose(1, 0, 2).reshape(N_CHIPS * D, -1)
            dmix = mm(f'gla_in_b_{i}', dh, sv['w_gin'], 'nt', F32)
        dA, ca, dB = dz1, DN_ALPHA, dmix
        waiting += layer_items(lg)
    grad_x = axpy('grad_x', dA, ca, dB)[None]

    exchange, context = reduce_begin('last', waiting)
    reduce_end(context, run_exchange('chip_exchange_last', exchange))
    shared = run_exchange('share_with_sibling',
                          share_exchange([shard_buf[n] for n in names], layer_runs(lambda n, l: True)))
    grads = dict(zip(names, shared))

    small_names = [n for n, _ in SMALL]
    small_full = [jnp.concatenate([g.reshape((1,) + g.shape[-(given[n].ndim - 1):]) for g in small_g[n]], 0)
                  for n in small_names]
    pack = _pack([jnp.reshape(loss_part, (1,))] + [jnp.zeros((LANE - 1,), F32)] + small_full)
    red = exchange_small('reduce_small', pack, True)
    red_parts = _unpack(red, [(LANE,)] + [g.shape for g in small_full])
    loss = red_parts[0][0]
    for (n, d), g in zip(SMALL, red_parts[1:]):
        if d is not None:
            width = given[n].shape[d]
            g = lax.dynamic_slice_in_dim(g, chip * width, width, axis=d)
        grads[n] = g

    delta, new_m, new_v = {}, {}, {}
    for n in names:
        delta[n], new_m[n], new_v[n] = adamw(f'adamw_{n}', given[n], grads[n], given['m_' + n], given['v_' + n])
    packs = [_pack([given[pre + n] for n in small_names]) for pre in ('', 'm_', 'v_')]
    outs = adamw('adamw_small', packs[0], _pack([grads[n] for n in small_names]), packs[1], packs[2])
    small_shapes = [given[n].shape for n in small_names]
    for dst, out in zip((delta, new_m, new_v), outs):
        for n, a in zip(small_names, _unpack(out, small_shapes)):
            dst[n] = a

    return (loss, grad_x, *[grads[n] for n in WEIGHT_NAMES], *[delta[n] for n in WEIGHT_NAMES],
            *[new_m[n] for n in WEIGHT_NAMES], *[new_v[n] for n in WEIGHT_NAMES])
```

```python
import functools
import math
from typing import Callable, NamedTuple

import jax
import jax.numpy as jnp
from jax import lax
from jax.experimental import pallas as pl
from jax.experimental.pallas import tpu as pltpu

F32 = jnp.float32
MXU_DTYPE = jnp.bfloat16

DEPTH = 4
CHUNK = 64
Q_BLOCK = 128
MLA_NOPE = 128
MLA_ROPE = 64
MLA_V = 128
ROPE_THETA = 10000.0
GLA_HEADS = 4
GLA_GATE_RANK = 16
GLA_TAU = 16.0
DN_ALPHA = (2 * DEPTH) ** 0.25
EPS = 1e-5
NEG_INF = -1e30
ADAM_LR = 0.001
ADAM_B1 = 0.9
ADAM_B2 = 0.999
ADAM_EPS = 1e-08
ADAM_WD = 0.01
ADAM_STEP = 10
GELU_C = math.sqrt(2.0 / math.pi)
GELU_A = 0.044715
LOG2_E = math.log2(math.e)

LANE = 128
HEAD_PAD = 2 * LANE
VMEM_LIMIT_BYTES = 48 * 1024 * 1024

MESH = pl.DeviceIdType.MESH
ANY_SPEC = pl.BlockSpec(memory_space=pl.ANY)

WEIGHT_NAMES = ('mla_w_in', 'mla_q_norm', 'mla_kv_norm', 'mla_w_uq', 'mla_w_uk', 'mla_w_uv', 'mla_w_o',
                'gla_w_in', 'gla_w_a2', 'gla_b_a', 'gla_o_norm', 'gla_w_o', 'ln1_g', 'ln1_b', 'ln2_g', 'ln2_b',
                'ffn_w_up', 'ffn_conv_w', 'ffn_conv_b', 'ffn_w_down', 'ple_w_proj', 'ple_w_gate', 'ple_b_gate')
BIG = (('mla_w_in', 1), ('mla_w_uq', 2), ('mla_w_uk', 2), ('mla_w_uv', 2), ('mla_w_o', 1), ('gla_w_in', 2),
       ('gla_w_o', 1), ('ffn_w_up', 2), ('ffn_w_down', 1), ('ple_w_proj', 2), ('ple_w_gate', 1))
SMALL = (('mla_q_norm', None), ('mla_kv_norm', None), ('gla_w_a2', 2), ('gla_b_a', 1), ('gla_o_norm', 1),
         ('ln1_g', None), ('ln1_b', None), ('ln2_g', None), ('ln2_b', None), ('ffn_conv_w', 2),
         ('ffn_conv_b', None), ('ple_b_gate', None))
N_CHIPS = 4
N_DEVICES = 8


def _params():
    return pltpu.CompilerParams(vmem_limit_bytes=VMEM_LIMIT_BYTES)


def _tile(n, cap, *offsets, unit=LANE):
    g = n
    for o in offsets:
        if o:
            g = math.gcd(g, o)
    best = 0
    for d in range(unit, min(g, cap) + 1, unit):
        if g % d == 0:
            best = d
    if best:
        return best
    assert not any(offsets), (n, offsets)
    return n


def _dot(a, b, dims):
    return lax.dot_general(a.astype(MXU_DTYPE), b.astype(MXU_DTYPE), (dims, ((), ())),
                           preferred_element_type=F32)


NN = ((1,), (0,))
NT = ((1,), (1,))
TN = ((0,), (0,))


def mm(name, a, b, mode, out_dtype, *, a_pre=(), b_pre=(), b_win=None, acc_in=None, out_stack=None,
       epilogue=None):
    a2 = a.shape[len(a_pre):]
    b2 = b.shape[len(b_pre):]
    br0, brn, bc0, bcn = b_win or (0, b2[0], 0, b2[1])
    if mode == 'nn':
        (M, K), N, dims = a2, bcn, NN
        assert brn == K
    elif mode == 'nt':
        (M, K), N, dims = a2, brn, NT
        assert bcn == K
    else:
        (K, M), N, dims = a2, bcn, TN
        assert brn == K
    oc0 = out_stack[3] if out_stack else 0
    tm = _tile(M, 1408)
    if mode == 'nn':
        tn, tk = _tile(N, 1408, bc0, oc0), _tile(K, 1408, br0)
    elif mode == 'nt':
        tn, tk = _tile(N, 1408, br0, oc0), _tile(K, 1408, bc0)
    else:
        tn, tk = _tile(N, 1408, bc0, oc0), _tile(K, 1024, br0)
    nk = K // tk
    grid = (M // tm, N // tn, nk)

    na, nb = len(a_pre), len(b_pre)
    if mode == 'tn':
        a_spec = pl.BlockSpec((None,) * na + (tk, tm), lambda i, j, k: a_pre + (k, i))
    else:
        a_spec = pl.BlockSpec((None,) * na + (tm, tk), lambda i, j, k: a_pre + (i, k))
    if mode == 'nt':
        b_spec = pl.BlockSpec((None,) * nb + (tn, tk), lambda i, j, k: b_pre + (j + br0 // tn, k + bc0 // tk))
    else:
        b_spec = pl.BlockSpec((None,) * nb + (tk, tn), lambda i, j, k: b_pre + (k + br0 // tk, j + bc0 // tn))
    in_specs, args = [a_spec, b_spec], [a, b]
    if acc_in is not None:
        in_specs.append(pl.BlockSpec((tm, tn), lambda i, j, k: (i, j)))
        args.append(acc_in)
    aliases = {}
    if out_stack is None:
        out_shape = jax.ShapeDtypeStruct((M, N), out_dtype)
        out_spec = pl.BlockSpec((tm, tn), lambda i, j, k: (i, j))
    else:
        buf, full_shape, lead, _ = out_stack
        out_shape = jax.ShapeDtypeStruct(full_shape, out_dtype)
        out_spec = pl.BlockSpec((None, tm, tn), lambda i, j, k: (lead, i, j + oc0 // tn))
        if buf is not None:
            aliases = {len(args): 0}
            in_specs.append(ANY_SPEC)
            args.append(buf)
    has_c, has_alias = acc_in is not None, bool(aliases)
    finish, row_arrays = epilogue or (None, ())
    n_rows_in = len(row_arrays)
    for r in row_arrays:
        in_specs.append(pl.BlockSpec((tm, r.shape[1]), lambda i, j, k: (i, 0)))
        args.append(r)

    def body(*refs):
        a_ref, b_ref = refs[0], refs[1]
        c_ref = refs[2] if has_c else None
        first_row = 2 + has_c + has_alias
        o_ref = refs[first_row + n_rows_in]

        def result(tile):
            if finish is not None:
                tile = finish(tile, *[r[...] for r in refs[first_row:first_row + n_rows_in]])
            return tile.astype(out_dtype)

        prod = _dot(a_ref[...], b_ref[...], dims)
        if nk == 1:
            if has_c:
                prod = prod + c_ref[...]
            o_ref[...] = result(prod)
            return
        acc_ref = refs[first_row + n_rows_in + 1]
        k = pl.program_id(2)

        @pl.when(k == 0)
        def _():
            acc_ref[...] = prod + c_ref[...] if has_c else prod

        @pl.when(k > 0)
        def _():
            acc_ref[...] += prod

        @pl.when(k == nk - 1)
        def _():
            o_ref[...] = result(acc_ref[...])

    scratch = [pltpu.VMEM((tm, tn), F32)] if nk > 1 else []
    return pl.pallas_call(body, out_shape=out_shape, grid=grid, in_specs=in_specs, out_specs=out_spec,
                          scratch_shapes=scratch, input_output_aliases=aliases, name=name,
                          compiler_params=_params())(*args)


def rowcall(name, body, n_rows, ts, row_ins, full_ins, row_outs, acc_outs=()):
    in_specs, args = [], []
    for arr, width, colblk, pre in row_ins:
        in_specs.append(pl.BlockSpec((None,) * len(pre) + (ts, width),
                                     lambda i, pre=pre, cb=colblk: pre + (i, cb)))
        args.append(arr)
    for arr in full_ins:
        in_specs.append(pl.BlockSpec(arr.shape, lambda i, nd=arr.ndim: (0,) * nd))
        args.append(arr)
    out_shape, out_specs = [], []
    for width, dtype in row_outs:
        out_shape.append(jax.ShapeDtypeStruct((n_rows, width), dtype))
        out_specs.append(pl.BlockSpec((ts, width), lambda i: (i, 0)))
    for shape in acc_outs:
        out_shape.append(jax.ShapeDtypeStruct(shape, F32))
        out_specs.append(pl.BlockSpec(shape, lambda i, nd=len(shape): (0,) * nd))
    n_in = len(args)

    def kern(*refs):
        body(pl.program_id(0), refs[:n_in], refs[n_in:])

    return pl.pallas_call(kern, out_shape=tuple(out_shape), grid=(n_rows // ts,), in_specs=in_specs,
                          out_specs=tuple(out_specs), name=name, compiler_params=_params())(*args)


def _row(arr, width=None, colblk=0, pre=()):
    return (arr, arr.shape[-1] if width is None else width, colblk, pre)


def _init_acc(i, refs):
    @pl.when(i == 0)
    def _():
        for r in refs:
            r[...] = jnp.zeros(r.shape, r.dtype)


def _colsum(v):
    return jnp.sum(v, axis=0, keepdims=True)


def _sigmoid(z):
    return 1.0 / (1.0 + jnp.exp(-z))


def _row_tile(S):
    return _tile(S, 256, unit=16)


def _ln_stats(x_ref, m_ref):
    z = DN_ALPHA * x_ref[...] + m_ref[...]
    mu = jnp.mean(z, -1, keepdims=True)
    zc = z - mu
    var = jnp.mean(zc * zc, -1, keepdims=True)
    r = lax.rsqrt(var + EPS)
    return zc * r, r


def ln_fwd(name, x, m, g, b):
    S, D = x.shape

    def body(i, ins, outs):
        x_ref, m_ref, g_ref, b_ref = ins
        xh, _ = _ln_stats(x_ref, m_ref)
        y = xh * g_ref[...] + b_ref[...]
        outs[0][...] = y
        outs[1][...] = y.astype(MXU_DTYPE)

    return rowcall(name, body, S, _row_tile(S), [_row(x), _row(m)], [g, b], [(D, F32), (D, MXU_DTYPE)])


def ln_bwd(name, x, m, dA, ca, dB, g):
    S, D = x.shape
    has_b = dB is not None

    def body(i, ins, outs):
        x_ref, m_ref, a_ref = ins[:3]
        g_ref = ins[-1]
        dz_ref, dzb_ref, dg_ref, db_ref = outs
        _init_acc(i, (dg_ref, db_ref))
        dy = ca * a_ref[...]
        if has_b:
            dy = dy + ins[3][...]
        xh, r = _ln_stats(x_ref, m_ref)
        dg_ref[...] += _colsum(dy * xh)
        db_ref[...] += _colsum(dy)
        dxh = dy * g_ref[...]
        dz = r * (dxh - jnp.mean(dxh, -1, keepdims=True) - xh * jnp.mean(dxh * xh, -1, keepdims=True))
        dz_ref[...] = dz
        dzb_ref[...] = dz.astype(MXU_DTYPE)

    rows = [_row(x), _row(m), _row(dA)] + ([_row(dB)] if has_b else [])
    return rowcall(name, body, S, _row_tile(S), rows, [g], [(D, F32), (D, MXU_DTYPE)], [(1, D), (1, D)])


def ple_fwd(name, x2, gp, pp, bias):
    S, D = x2.shape

    def body(i, ins, outs):
        x_ref, gp_ref, pp_ref, b_ref = ins
        y = x_ref[...] + _sigmoid(gp_ref[...] + b_ref[...]) * pp_ref[...]
        outs[0][...] = y
        outs[1][...] = y.astype(MXU_DTYPE)

    return rowcall(name, body, S, _row_tile(S), [_row(x2), _row(gp), _row(pp)], [bias],
                   [(D, F32), (D, MXU_DTYPE)])


def ple_bwd(name, dA, ca, dB, gp, pp, bias):
    S, D = gp.shape
    has_b = dB is not None

    def body(i, ins, outs):
        a_ref = ins[0]
        gp_ref, pp_ref, b_ref = ins[-3:]
        dx_ref, dpp_ref, dgp_ref, db_ref = outs
        _init_acc(i, (db_ref,))
        dx = ca * a_ref[...]
        if has_b:
            dx = dx + ins[1][...]
        gate = _sigmoid(gp_ref[...] + b_ref[...])
        dgp = dx * pp_ref[...] * gate * (1.0 - gate)
        dx_ref[...] = dx
        dpp_ref[...] = (dx * gate).astype(MXU_DTYPE)
        dgp_ref[...] = dgp.astype(MXU_DTYPE)
        db_ref[...] += _colsum(dgp)

    rows = [_row(dA)] + ([_row(dB)] if has_b else []) + [_row(gp), _row(pp)]
    return rowcall(name, body, S, _row_tile(S), rows, [bias],
                   [(D, F32), (D, MXU_DTYPE), (D, MXU_DTYPE)], [(1, D)])


def loss_head(name, y, target):
    S, D = y.shape

    def body(i, ins, outs):
        _init_acc(i, (outs[1],))
        e = ins[0][...] - ins[1][...]
        outs[0][...] = e * (1.0 / D)
        outs[1][...] += _colsum(e * e)

    return rowcall(name, body, S, _row_tile(S), [_row(y), _row(target)], [], [(D, F32)], [(1, D)])


def axpy(name, a, ca, b):
    S, D = a.shape

    def body(i, ins, outs):
        outs[0][...] = ca * ins[0][...] + ins[1][...]

    return rowcall(name, body, S, _row_tile(S), [_row(a), _row(b)], [], [(D, F32)])[0]


HALF_ROPE = MLA_ROPE // 2


def _rope(x, c, sa, sb):
    n = x.shape[-1]
    return x * c + pltpu.roll(x, n - HALF_ROPE, 1) * sa + pltpu.roll(x, HALF_ROPE, 1) * sb


def _rope_t(d, c, sa, sb):
    n = d.shape[-1]
    return d * c + pltpu.roll(d * sa, HALF_ROPE, 1) + pltpu.roll(d * sb, n - HALF_ROPE, 1)


def rope_tables(positions_row, n_lead):
    inv = 1.0 / (ROPE_THETA ** (jnp.arange(0, MLA_ROPE, 2, dtype=F32) / MLA_ROPE))
    ang = positions_row.astype(F32)[:, None] * inv
    cos, sin = jnp.cos(ang), jnp.sin(ang)
    S = cos.shape[0]
    z = jnp.zeros((S, HALF_ROPE), F32)
    tail = jnp.zeros((S, LANE - MLA_ROPE), F32)
    c = jnp.concatenate([jnp.ones((S, n_lead), F32), cos, cos, tail], -1)
    sa = jnp.concatenate([jnp.zeros((S, n_lead), F32), -sin, z, tail], -1)
    sb = jnp.concatenate([jnp.zeros((S, n_lead), F32), z, sin, tail], -1)
    return c, sa, sb


def mla_pre_fwd(name, h, qn, kvn, tab_k, QL, KL):
    S = h.shape[0]

    def body(i, ins, outs):
        h_ref, c_ref, sa_ref, sb_ref, qn_ref, kvn_ref = ins
        cq = h_ref[:, 0:QL]
        ckv = h_ref[:, QL:QL + KL]
        kr = h_ref[:, QL + KL:QL + KL + LANE]
        outs[0][...] = (cq * lax.rsqrt(jnp.mean(cq * cq, -1, keepdims=True) + EPS) * qn_ref[...]).astype(MXU_DTYPE)
        outs[1][...] = (ckv * lax.rsqrt(jnp.mean(ckv * ckv, -1, keepdims=True) + EPS) * kvn_ref[...]).astype(MXU_DTYPE)
        outs[2][...] = _rope(kr, c_ref[...], sa_ref[...], sb_ref[...]).astype(MXU_DTYPE)

    rows = [_row(h)] + [_row(t) for t in tab_k]
    return rowcall(name, body, S, _row_tile(S), rows, [qn, kvn],
                   [(QL, MXU_DTYPE), (KL, MXU_DTYPE), (LANE, MXU_DTYPE)])


def _rms_bwd(x, g, dy):
    r = lax.rsqrt(jnp.mean(x * x, -1, keepdims=True) + EPS)
    dg = _colsum(dy * x * r)
    dxg = dy * g
    dx = r * dxg - x * (r * r * r) * jnp.mean(dxg * x, -1, keepdims=True)
    return dx, dg


def mla_pre_bwd(name, h, dcq, dckv_a, dckv_b, dkr_heads, qn, kvn, tab_k, QL, KL, H):
    S, HW = h.shape

    def body(i, ins, outs):
        h_ref, dcq_ref, da_ref, db_ref, dkr_ref, c_ref, sa_ref, sb_ref, qn_ref, kvn_ref = ins
        dh_ref, dqn_ref, dkvn_ref = outs
        _init_acc(i, (dqn_ref, dkvn_ref))
        dx, dg = _rms_bwd(h_ref[:, 0:QL], qn_ref[...], dcq_ref[...])
        dh_ref[:, 0:QL] = dx.astype(MXU_DTYPE)
        dqn_ref[...] += dg
        dx, dg = _rms_bwd(h_ref[:, QL:QL + KL], kvn_ref[...], da_ref[...] + db_ref[...])
        dh_ref[:, QL:QL + KL] = dx.astype(MXU_DTYPE)
        dkvn_ref[...] += dg
        d = dkr_ref[:, 0:LANE]
        for hh in range(1, H):
            d = d + dkr_ref[:, hh * LANE:(hh + 1) * LANE]
        dh_ref[:, QL + KL:QL + KL + LANE] = _rope_t(d, c_ref[...], sa_ref[...], sb_ref[...]).astype(MXU_DTYPE)

    rows = [_row(h), _row(dcq), _row(dckv_a), _row(dckv_b), _row(dkr_heads)] + [_row(t) for t in tab_k]
    return rowcall(name, body, S, _row_tile(S), rows, [qn, kvn], [(HW, MXU_DTYPE)], [(1, QL), (1, KL)])


def rope_heads(tile, c, sa, sb):
    return jnp.concatenate([_rope(tile[:, h0:h0 + HEAD_PAD], c, sa, sb)
                            for h0 in range(0, tile.shape[1], HEAD_PAD)], axis=-1)


ATT_T_FWD = 1024
ATT_T_BWD = 512


def _att_mask_t(ck_col, cq_row, k0, q0, t):
    kpos = k0 + lax.broadcasted_iota(jnp.int32, (t, 1), 0)
    qpos = q0 + lax.broadcasted_iota(jnp.int32, (1, t), 1)
    return (ck_col <= cq_row) & (kpos <= (qpos | (Q_BLOCK - 1)))


def _transpose(a):
    return a.astype(F32).T.astype(a.dtype)


def hosted_call(body, *, name, grid, in_specs, out_specs, out_shape, scratch_shapes, args, rider):
    if rider is None:
        return pl.pallas_call(body, name=name, grid=grid, in_specs=in_specs, out_specs=out_specs,
                              out_shape=out_shape, scratch_shapes=scratch_shapes,
                              compiler_params=_params())(*args), ()
    n_in, n_out, n_scr = len(args), len(out_shape), len(scratch_shapes)
    r_in, r_out = len(rider.ins), len(rider.out_shapes)

    def hosted(*refs):
        ins, refs = refs[:n_in], refs[n_in:]
        rin, refs = refs[:r_in], refs[r_in:]
        outs, refs = refs[:n_out], refs[n_out:]
        rout, refs = refs[:r_out], refs[r_out:]
        scr, sems = refs[:n_scr], refs[n_scr:]
        ids = [pl.program_id(a) for a in range(len(grid))]
        first = functools.reduce(jnp.logical_and, [i == 0 for i in ids])
        last = functools.reduce(jnp.logical_and, [i == g - 1 for i, g in zip(ids, grid)])

        @pl.when(first)
        def _():
            rider.start(rin, rout, sems)

        body(*ins, *outs, *scr)

        @pl.when(last)
        def _():
            rider.finish(rin, rout, sems)

    results = pl.pallas_call(
        hosted, name=name, grid=grid, in_specs=list(in_specs) + [ANY_SPEC] * r_in,
        out_specs=tuple(out_specs) + (ANY_SPEC,) * r_out, out_shape=tuple(out_shape) + tuple(rider.out_shapes),
        scratch_shapes=list(scratch_shapes) + [pltpu.SemaphoreType.DMA((k,)) for k in rider.sem_counts],
        input_output_aliases={n_in + k: n_out + k for k in range(r_in)} if rider.in_place else {},
        compiler_params=pltpu.CompilerParams(vmem_limit_bytes=VMEM_LIMIT_BYTES, has_side_effects=True),
    )(*args, *rider.ins)
    return results[:n_out], results[n_out:]


def attn_fwd(name, q, kn, kr, v, cid, H, rider=None):
    S = q.shape[0]
    t = _tile(S, ATT_T_FWD)
    n = S // t
    scale = (MLA_NOPE + MLA_ROPE) ** -0.5
    cid_col, cid_blk = cid.reshape(S, 1), cid.reshape(n, 1, t)

    def body(q_ref, kn_ref, kr_ref, v_ref, cc_ref, cr_ref, o_ref, lse_ref, k_scr, vt_scr):
        i = pl.program_id(1)

        @pl.when(i == 0)
        def _():
            k_scr[:, 0:LANE] = kn_ref[...]
            k_scr[:, LANE:HEAD_PAD] = kr_ref[...]
            for jj in range(n):
                vt_scr[jj] = _transpose(v_ref[jj * t:(jj + 1) * t, :])

        qv = q_ref[...]
        cq = cr_ref[i]

        def update(j, carry, masked):
            m, l, acc = carry
            k0 = pl.multiple_of(j * t, t)
            s = _dot(k_scr[pl.ds(k0, t), :], qv, NT) * (scale * LOG2_E)
            if masked:
                s = jnp.where(_att_mask_t(cc_ref[pl.ds(k0, t), :], cq, k0, i * t, t), s, NEG_INF)
            m_new = jnp.maximum(m, jnp.max(s, 0, keepdims=True))
            p = jnp.exp2(s - m_new)
            alpha = jnp.exp2(m - m_new)
            l = alpha * l + jnp.sum(p, 0, keepdims=True)
            acc = alpha * acc + _dot(vt_scr[j], p, NN)
            return m_new, l, acc

        init = (jnp.full((1, t), NEG_INF, F32), jnp.zeros((1, t), F32), jnp.zeros((MLA_V, t), F32))
        carry = lax.fori_loop(0, i, lambda j, c: update(j, c, False), init)
        m, l, acc = update(i, carry, True)
        o_ref[...] = (acc / l).T.astype(o_ref.dtype)
        lse_ref[...] = m * (1.0 / LOG2_E) + jnp.log(l)

    return hosted_call(
        body, name=name, grid=(H, n),
        in_specs=[pl.BlockSpec((t, HEAD_PAD), lambda h, i: (i, h)),
                  pl.BlockSpec((S, MLA_NOPE), lambda h, i: (0, h)),
                  pl.BlockSpec((S, LANE), lambda h, i: (0, 0)),
                  pl.BlockSpec((S, MLA_V), lambda h, i: (0, h)),
                  pl.BlockSpec((S, 1), lambda h, i: (0, 0)),
                  pl.BlockSpec(cid_blk.shape, lambda h, i: (0, 0, 0))],
        out_specs=(pl.BlockSpec((t, MLA_V), lambda h, i: (i, h)),
                   pl.BlockSpec((None, None, 1, t), lambda h, i: (h, i, 0, 0))),
        scratch_shapes=[pltpu.VMEM((S, HEAD_PAD), MXU_DTYPE), pltpu.VMEM((n, MLA_V, t), MXU_DTYPE)],
        out_shape=(jax.ShapeDtypeStruct((S, H * MLA_V), MXU_DTYPE), jax.ShapeDtypeStruct((H, n, 1, t), F32)),
        args=(q, kn, kr, v, cid_col, cid_blk), rider=rider)


def attn_bwd(name, q, kn, kr, v, o, lse, do, cid, H, rider=None):
    S = q.shape[0]
    t = _tile(S, ATT_T_BWD)
    n = S // t
    scale = (MLA_NOPE + MLA_ROPE) ** -0.5
    cid_col, cid_blk = cid.reshape(S, 1), cid.reshape(n, 1, t)
    lse = lse.reshape(H, n, 1, t)

    def body(q_ref, kn_ref, kr_ref, v_ref, o_ref, lse_ref, do_ref, cc_ref, cr_ref,
             dq_ref, dkn_ref, dv_ref, dkr_ref, delta_scr, dk_scr, dv_scr):
        j = pl.program_id(1)

        @pl.when(j == 0)
        def _():
            dq_ref[...] = jnp.zeros(dq_ref.shape, F32)
            for ii in range(n):
                sl = slice(ii * t, (ii + 1) * t)
                prod = do_ref[sl, :].astype(F32) * o_ref[sl, :].astype(F32)
                delta_scr[ii] = jnp.sum(prod.T, 0, keepdims=True)

        kv = jnp.concatenate([kn_ref[...], kr_ref[...]], axis=-1)
        kt = _transpose(kv)
        vv = v_ref[...]
        ck = cc_ref[...]
        k0 = j * t
        dk_scr[...] = jnp.zeros(dk_scr.shape, F32)
        dv_scr[...] = jnp.zeros(dv_scr.shape, F32)

        def update(i, masked):
            q0 = pl.multiple_of(i * t, t)
            qv = q_ref[pl.ds(q0, t), :]
            dov = do_ref[pl.ds(q0, t), :]
            s = _dot(kv, qv, NT) * scale
            if masked:
                s = jnp.where(_att_mask_t(ck, cr_ref[i], k0, q0, t), s, NEG_INF)
            p = jnp.exp(s - lse_ref[i])
            dv_scr[...] += _dot(p, dov, NN)
            dp = _dot(vv, dov, NT)
            ds = p * (dp - delta_scr[i]) * scale
            dk_scr[...] += _dot(ds, qv, NN)
            dq_ref[i] += _dot(kt, ds, NN)

        def step(i, carry):
            update(i, False)
            return carry

        update(j, True)
        lax.fori_loop(j + 1, n, step, 0)
        dkn_ref[...] = dk_scr[:, 0:LANE].astype(dkn_ref.dtype)
        dkr_ref[...] = dk_scr[:, LANE:HEAD_PAD]
        dv_ref[...] = dv_scr[...].astype(dv_ref.dtype)

    head_rows = lambda w: pl.BlockSpec((S, w), lambda h, j: (0, h))
    tile_rows = lambda w: pl.BlockSpec((t, w), lambda h, j: (j, h))
    return hosted_call(
        body, name=name, grid=(H, n),
        in_specs=[head_rows(HEAD_PAD), tile_rows(MLA_NOPE), pl.BlockSpec((t, LANE), lambda h, j: (j, 0)),
                  tile_rows(MLA_V), head_rows(MLA_V),
                  pl.BlockSpec((None, n, 1, t), lambda h, j: (h, 0, 0, 0)), head_rows(MLA_V),
                  pl.BlockSpec((t, 1), lambda h, j: (j, 0)),
                  pl.BlockSpec(cid_blk.shape, lambda h, j: (0, 0, 0))],
        out_specs=(pl.BlockSpec((None, n, HEAD_PAD, t), lambda h, j: (h, 0, 0, 0)),
                   tile_rows(MLA_NOPE), tile_rows(MLA_V), tile_rows(LANE)),
        scratch_shapes=[pltpu.VMEM((n, 1, t), F32), pltpu.VMEM((t, HEAD_PAD), F32), pltpu.VMEM((t, MLA_V), F32)],
        out_shape=(jax.ShapeDtypeStruct((H, n, HEAD_PAD, t), F32), jax.ShapeDtypeStruct((S, H * MLA_NOPE), MXU_DTYPE),
                   jax.ShapeDtypeStruct((S, H * MLA_V), MXU_DTYPE), jax.ShapeDtypeStruct((S, H * LANE), F32)),
        args=(q, kn, kr, v, o, lse, do, cid_col, cid_blk), rider=rider)


def rope_q_bwd(name, dq_t, tab_q_t, H):
    _, n, _, t = dq_t.shape

    def body(d_ref, c_ref, sa_ref, sb_ref, o_ref):
        c, sa, sb = c_ref[...], sa_ref[...], sb_ref[...]
        for hh in range(H):
            d = d_ref[hh]
            out = d * c + pltpu.roll(d * sa, HALF_ROPE, 0) + pltpu.roll(d * sb, HEAD_PAD - HALF_ROPE, 0)
            o_ref[hh * HEAD_PAD:(hh + 1) * HEAD_PAD, :] = out.astype(o_ref.dtype)

    tab = pl.BlockSpec((HEAD_PAD, t), lambda i: (0, i))
    return pl.pallas_call(
        body, name=name, grid=(n,), out_shape=jax.ShapeDtypeStruct((H * HEAD_PAD, n * t), MXU_DTYPE),
        in_specs=[pl.BlockSpec((H, None, HEAD_PAD, t), lambda i: (0, i, 0, 0)), tab, tab, tab],
        out_specs=pl.BlockSpec((H * HEAD_PAD, t), lambda i: (0, i)),
        compiler_params=_params())(dq_t, *tab_q_t)


GLA_GROUP = 8


def _prefix_rows(x):
    n = x.shape[0]
    row = lax.broadcasted_iota(jnp.int32, x.shape, 0)
    d = 1
    while d < n:
        x = x + jnp.where(row >= d, pltpu.roll(x, d, 0), 0.0)
        d *= 2
    return x


def _suffix_rows(x):
    n = x.shape[0]
    row = lax.broadcasted_iota(jnp.int32, x.shape, 0)
    d = 1
    while d < n:
        x = x + jnp.where(row < n - d, pltpu.roll(x, n - d, 0), 0.0)
        d *= 2
    return x


def _log_sigmoid(z):
    return jnp.minimum(z, 0.0) - jnp.log(1.0 + jnp.exp(-jnp.abs(z)))


def gla_pre_fwd(name, h, w2p, b_a, QK, a_blk):
    S = h.shape[0]
    dk = QK // GLA_HEADS

    def body(i, ins, outs):
        q_ref, a_ref, w_ref, b_ref = ins
        outs[0][...] = (q_ref[...] * (dk ** -0.5)).astype(MXU_DTYPE)
        z = _dot(a_ref[...], w_ref[...], NN) + b_ref[...]
        outs[1][...] = _log_sigmoid(z) / GLA_TAU

    return rowcall(name, body, S, _row_tile(S), [_row(h, QK, 0), _row(h, LANE, a_blk)], [w2p, b_a],
                   [(QK, MXU_DTYPE), (QK, F32)])


def _gla_specs(S, QK, VD, rows, gmap):
    dk, dv = QK // GLA_HEADS, VD // GLA_HEADS
    return dict(
        qs=pl.BlockSpec((rows, dk), lambda h, g: (gmap(g), h)),
        k=pl.BlockSpec((rows, dk), lambda h, g: (gmap(g), QK // dk + h)),
        v=pl.BlockSpec((rows, dv), lambda h, g: (gmap(g), 2 * QK // dv + h)),
        la=pl.BlockSpec((rows, dk), lambda h, g: (gmap(g), h)),
        o=pl.BlockSpec((rows, dv), lambda h, g: (gmap(g), h)))


def gla_fwd(name, qs, h, la, QK, VD):
    S = qs.shape[0]
    dk, dv = QK // GLA_HEADS, VD // GLA_HEADS
    n_chunks = S // CHUNK
    cg = min(GLA_GROUP, n_chunks)
    rows = cg * CHUNK
    sp = _gla_specs(S, QK, VD, rows, lambda g: g)

    def body(q_ref, k_ref, v_ref, la_ref, o_ref, st_ref, state):
        @pl.when(pl.program_id(1) == 0)
        def _():
            state[...] = jnp.zeros(state.shape, F32)

        for c in range(cg):
            sl = slice(c * CHUNK, (c + 1) * CHUNK)
            cum = _prefix_rows(la_ref[sl, :])
            tot = cum[CHUNK - 1:CHUNK, :]
            kdec = k_ref[sl, :] * jnp.exp(tot - cum)
            st = state[...] * jnp.exp(tot) + _dot(v_ref[sl, :], kdec, TN)
            state[...] = st
            st_ref[c] = st
            o_ref[sl, :] = _dot(q_ref[sl, :], st, NT)

    return pl.pallas_call(
        body, name=name, grid=(GLA_HEADS, n_chunks // cg),
        out_shape=(jax.ShapeDtypeStruct((S, VD), F32), jax.ShapeDtypeStruct((GLA_HEADS, n_chunks, dv, dk), F32)),
        in_specs=[sp['qs'], sp['k'], sp['v'], sp['la']],
        out_specs=(sp['o'], pl.BlockSpec((None, cg, dv, dk), lambda h, g: (h, g, 0, 0))),
        scratch_shapes=[pltpu.VMEM((dv, dk), F32)],
        compiler_params=_params())(qs, h, h, la)


def gla_bwd(name, qs, h, la, states, do, QK, VD):
    S = qs.shape[0]
    dk, dv = QK // GLA_HEADS, VD // GLA_HEADS
    n_chunks = S // CHUNK
    cg = min(GLA_GROUP, n_chunks)
    ng = n_chunks // cg
    rows = cg * CHUNK
    rev = lambda g: ng - 1 - g
    sp = _gla_specs(S, QK, VD, rows, rev)

    def body(q_ref, k_ref, v_ref, la_ref, st_ref, prev_ref, do_ref, dq_ref, dk_ref, dv_ref, dla_ref, dst):
        g = pl.program_id(1)

        @pl.when(g == 0)
        def _():
            dst[...] = jnp.zeros(dst.shape, F32)

        first_group = (g == ng - 1).astype(F32)
        for c in reversed(range(cg)):
            sl = slice(c * CHUNK, (c + 1) * CHUNK)
            cum = _prefix_rows(la_ref[sl, :])
            tot = cum[CHUNK - 1:CHUNK, :]
            e = jnp.exp(tot - cum)
            kdec = k_ref[sl, :] * e
            decay = jnp.exp(tot)
            st = st_ref[c]
            st_prev = st_ref[c - 1] if c > 0 else prev_ref[0] * (1.0 - first_group)
            dov = do_ref[sl, :]
            qv = q_ref[sl, :]
            dq_ref[sl, :] = (_dot(dov, st, NN) * (dk ** -0.5)).astype(dq_ref.dtype)
            d = dst[...] + _dot(dov, qv, TN)
            ddecay = _colsum(d * st_prev)
            dkdec = _dot(v_ref[sl, :], d, NN)
            dv_ref[sl, :] = _dot(kdec, d, NT).astype(dv_ref.dtype)
            dk_ref[sl, :] = (dkdec * e).astype(dk_ref.dtype)
            darg = dkdec * kdec
            dtot = _colsum(darg) + ddecay * decay
            dla_ref[sl, :] = dtot - _suffix_rows(darg)
            dst[...] = d * decay

    prev_spec = pl.BlockSpec((None, 1, dv, dk), lambda h, g: (h, jnp.maximum(rev(g) * cg - 1, 0), 0, 0))
    return pl.pallas_call(
        body, name=name, grid=(GLA_HEADS, ng),
        out_shape=(jax.ShapeDtypeStruct((S, QK), MXU_DTYPE), jax.ShapeDtypeStruct((S, QK), MXU_DTYPE),
                   jax.ShapeDtypeStruct((S, VD), MXU_DTYPE), jax.ShapeDtypeStruct((S, QK), F32)),
        in_specs=[sp['qs'], sp['k'], sp['v'], sp['la'],
                  pl.BlockSpec((None, cg, dv, dk), lambda h, g: (h, rev(g), 0, 0)), prev_spec, sp['o']],
        out_specs=(sp['qs'], sp['qs'], sp['o'], sp['la']),
        scratch_shapes=[pltpu.VMEM((dv, dk), F32)],
        compiler_params=_params())(qs, h, h, la, states, states, do)


def _head_norm(o):
    mu = jnp.mean(o, -1, keepdims=True)
    oc = o - mu
    r = lax.rsqrt(jnp.mean(oc * oc, -1, keepdims=True) + EPS)
    return oc * r, r


def gla_post_fwd(name, o, h, o_norm, VD, r_blk):
    S = o.shape[0]
    dv = VD // GLA_HEADS

    def body(i, ins, outs):
        o_ref, r_ref, w_ref = ins
        for hh in range(GLA_HEADS):
            sl = slice(hh * dv, (hh + 1) * dv)
            xh, _ = _head_norm(o_ref[:, sl])
            r = r_ref[:, sl]
            outs[0][:, sl] = (xh * w_ref[:, sl] * (r * _sigmoid(r))).astype(MXU_DTYPE)

    return rowcall(name, body, S, _row_tile(S), [_row(o), _row(h, VD, r_blk)], [o_norm], [(VD, MXU_DTYPE)])[0]


def gla_post_bwd(name, o, h, dog, o_norm, VD, r_blk):
    S = o.shape[0]
    dv = VD // GLA_HEADS

    def body(i, ins, outs):
        o_ref, r_ref, dog_ref, w_ref = ins
        do_ref, dr_ref, dw_ref = outs
        _init_acc(i, (dw_ref,))
        for hh in range(GLA_HEADS):
            sl = slice(hh * dv, (hh + 1) * dv)
            xh, rs = _head_norm(o_ref[:, sl])
            r = r_ref[:, sl]
            w = w_ref[:, sl]
            dog = dog_ref[:, sl]
            sg = _sigmoid(r)
            dn = dog * (r * sg)
            dr_ref[:, sl] = (dog * (xh * w) * (sg * (1.0 + r * (1.0 - sg)))).astype(MXU_DTYPE)
            dw_ref[:, sl] += _colsum(dn * xh)
            dxh = dn * w
            do_ref[:, sl] = rs * (dxh - jnp.mean(dxh, -1, keepdims=True) - xh * jnp.mean(dxh * xh, -1, keepdims=True))

    return rowcall(name, body, S, _row_tile(S), [_row(o), _row(h, VD, r_blk), _row(dog)], [o_norm],
                   [(VD, F32), (VD, MXU_DTYPE)], [(1, VD)])


def gla_dh(name, dq, dk, dv, dr, dla, h, w2p, b_a, QK, VD, a_blk, HW):
    S = dq.shape[0]

    def body(i, ins, outs):
        dq_ref, dk_ref, dv_ref, dr_ref, dla_ref, a_ref, w_ref, b_ref = ins
        dh_ref, dw_ref, db_ref = outs
        _init_acc(i, (dw_ref, db_ref))
        a = a_ref[...]
        z = _dot(a, w_ref[...], NN) + b_ref[...]
        dz = dla_ref[...] * (1.0 / GLA_TAU) * _sigmoid(-z)
        dw_ref[...] += _dot(a, dz, TN)
        db_ref[...] += _colsum(dz)
        dh_ref[:, 0:QK] = dq_ref[...]
        dh_ref[:, QK:2 * QK] = dk_ref[...]
        dh_ref[:, 2 * QK:2 * QK + VD] = dv_ref[...]
        dh_ref[:, 2 * QK + VD:2 * QK + 2 * VD] = dr_ref[...]
        dh_ref[:, 2 * QK + 2 * VD:HW] = _dot(dz, w_ref[...], NT).astype(MXU_DTYPE)

    rows = [_row(dq), _row(dk), _row(dv), _row(dr), _row(dla), _row(h, LANE, a_blk)]
    return rowcall(name, body, S, _row_tile(S), rows, [w2p, b_a], [(HW, MXU_DTYPE)], [(LANE, QK), (1, QK)])


CONV_ROWS = 512
HALO = 8


def _gelu(x):
    return 0.5 * x * (1.0 + jnp.tanh(GELU_C * (x + GELU_A * x * x * x)))


def _gelu_grad(x):
    t = jnp.tanh(GELU_C * (x + GELU_A * x * x * x))
    return 0.5 * (1.0 + t) + 0.5 * x * (1.0 - t * t) * GELU_C * (1.0 + 3.0 * GELU_A * x * x)


def _shift_down(x, prev, d):
    rolled = pltpu.roll(x, d, 0)
    row = lax.broadcasted_iota(jnp.int32, (HALO, x.shape[1]), 0)
    head = rolled[0:HALO]
    for t in range(d):
        head = jnp.where(row == t, prev[HALO - d + t:HALO - d + t + 1, :], head)
    return jnp.concatenate([head, rolled[HALO:]], axis=0)


def _shift_up(x, nxt, d):
    n = x.shape[0]
    rolled = pltpu.roll(x, n - d, 0)
    row = lax.broadcasted_iota(jnp.int32, (HALO, x.shape[1]), 0)
    tail = rolled[n - HALO:]
    for t in range(d):
        tail = jnp.where(row == HALO - d + t, nxt[t:t + 1, :], tail)
    return jnp.concatenate([rolled[:n - HALO], tail], axis=0)


def _conv_taps(ref, r, rc):
    x = ref[r * rc:(r + 1) * rc, :]
    prev = ref[r * rc - HALO:r * rc, :] if r > 0 else jnp.zeros((HALO, x.shape[1]), F32)
    return x, _shift_down(x, prev, 1), _shift_down(x, prev, 2)


def _conv_apply(taps, w_ref, b_ref):
    x0, x1, x2 = taps
    return x2 * w_ref[0:1, :] + x1 * w_ref[1:2, :] + x0 * w_ref[2:3, :] + b_ref[...]


def conv_fwd(name, hu, hg, cw_u, cw_g, cb_u, cb_g):
    S, F = hu.shape
    tc = LANE
    rc = _tile(S, CONV_ROWS, unit=16)

    def body(u_ref, g_ref, wu_ref, wg_ref, bu_ref, bg_ref, a_ref):
        for r in range(S // rc):
            uc = _conv_apply(_conv_taps(u_ref, r, rc), wu_ref, bu_ref)
            gc = _conv_apply(_conv_taps(g_ref, r, rc), wg_ref, bg_ref)
            a_ref[r * rc:(r + 1) * rc, :] = (uc * _gelu(gc)).astype(a_ref.dtype)

    col = lambda rows: pl.BlockSpec((rows, tc), lambda j: (0, j))
    return pl.pallas_call(
        body, name=name, grid=(F // tc,), out_shape=jax.ShapeDtypeStruct((S, F), MXU_DTYPE),
        in_specs=[col(S), col(S), col(3), col(3), col(1), col(1)], out_specs=col(S),
        compiler_params=_params())(hu, hg, cw_u, cw_g, cb_u, cb_g)


def conv_bwd(name, hu, hg, da, cw_u, cw_g, cb_u, cb_g):
    S, F = hu.shape
    tc = LANE
    rc = _tile(S, CONV_ROWS, unit=16)
    nr = S // rc

    def body(u_ref, g_ref, da_ref, wu_ref, wg_ref, bu_ref, bg_ref,
             dhu_ref, dhg_ref, dwu_ref, dwg_ref, dbu_ref, dbg_ref, du_scr, dg_scr):
        dw = [[jnp.zeros((1, tc), F32) for _ in range(3)] for _ in range(2)]
        db = [jnp.zeros((1, tc), F32) for _ in range(2)]
        for r in range(nr):
            sl = slice(r * rc, (r + 1) * rc)
            ut = _conv_taps(u_ref, r, rc)
            gt = _conv_taps(g_ref, r, rc)
            uc = _conv_apply(ut, wu_ref, bu_ref)
            gc = _conv_apply(gt, wg_ref, bg_ref)
            dav = da_ref[sl, :]
            duc = dav * _gelu(gc)
            dgc = dav * uc * _gelu_grad(gc)
            du_scr[sl, :] = duc
            dg_scr[sl, :] = dgc
            for part, (taps, d) in enumerate(((ut, duc), (gt, dgc))):
                db[part] = db[part] + _colsum(d)
                for tap in range(3):
                    dw[part][tap] = dw[part][tap] + _colsum(taps[2 - tap] * d)
        for part, (w_out, b_out) in enumerate(((dwu_ref, dbu_ref), (dwg_ref, dbg_ref))):
            b_out[...] = db[part]
            for tap in range(3):
                w_out[tap:tap + 1, :] = dw[part][tap]
        for scr, w_ref, out in ((du_scr, wu_ref, dhu_ref), (dg_scr, wg_ref, dhg_ref)):
            for r in range(nr):
                d = scr[r * rc:(r + 1) * rc, :]
                nxt = scr[(r + 1) * rc:(r + 1) * rc + HALO, :] if r + 1 < nr else jnp.zeros((HALO, tc), F32)
                dh = d * w_ref[2:3, :] + _shift_up(d, nxt, 1) * w_ref[1:2, :] + _shift_up(d, nxt, 2) * w_ref[0:1, :]
                out[r * rc:(r + 1) * rc, :] = dh.astype(out.dtype)

    col = lambda rows: pl.BlockSpec((rows, tc), lambda j: (0, j))
    sds = jax.ShapeDtypeStruct
    return pl.pallas_call(
        body, name=name, grid=(F // tc,),
        out_shape=(sds((S, F), MXU_DTYPE), sds((S, F), MXU_DTYPE), sds((3, F), F32), sds((3, F), F32),
                   sds((1, F), F32), sds((1, F), F32)),
        in_specs=[col(S), col(S), col(S), col(3), col(3), col(1), col(1)],
        out_specs=(col(S), col(S), col(3), col(3), col(1), col(1)),
        scratch_shapes=[pltpu.VMEM((S, tc), F32), pltpu.VMEM((S, tc), F32)],
        compiler_params=_params())(hu, hg, da, cw_u, cw_g, cb_u, cb_g)


def adamw(name, w, g, m, v):
    *lead, R, C = w.shape
    tr = _tile(R, max(8, (1 << 19) // max(C, 1) // 8 * 8), unit=8)

    def body(w_ref, g_ref, m_ref, v_ref, d_ref, nm_ref, nv_ref):
        gv = g_ref[...]
        mn = ADAM_B1 * m_ref[...] + (1.0 - ADAM_B1) * gv
        vn = ADAM_B2 * v_ref[...] + (1.0 - ADAM_B2) * (gv * gv)
        m_hat = mn / (1.0 - ADAM_B1 ** ADAM_STEP)
        v_hat = vn / (1.0 - ADAM_B2 ** ADAM_STEP)
        d_ref[...] = -ADAM_LR * (m_hat / (jnp.sqrt(v_hat) + ADAM_EPS) + ADAM_WD * w_ref[...])
        nm_ref[...] = mn
        nv_ref[...] = vn

    if lead:
        spec, grid = pl.BlockSpec((None, tr, C), lambda l, i: (l, i, 0)), (lead[0], R // tr)
    else:
        spec, grid = pl.BlockSpec((tr, C), lambda i: (i, 0)), (R // tr,)
    shp = jax.ShapeDtypeStruct(w.shape, F32)
    return pl.pallas_call(body, name=name, grid=grid, out_shape=(shp, shp, shp), in_specs=[spec] * 4,
                          out_specs=(spec, spec, spec), compiler_params=_params())(w, g, m, v)


def _position():
    return lax.axis_index('x'), lax.axis_index('y'), lax.axis_index('c')


def _other_chips(x, y):
    return ((1 - x, y), (x, 1 - y), (1 - x, 1 - y))


def _window(ref, dim, lo, n_lead, jj, rs, cs):
    if dim == 1:
        return ref.at[pl.ds(lo, n_lead), pl.ds(pl.multiple_of(jj * rs, 16), rs), :]
    return ref.at[pl.ds(lo, n_lead), :, pl.ds(pl.multiple_of(jj * cs, LANE), cs)]


def _hbm_call(name, body, out_shapes, n_sems, args, aliases=None):
    return pl.pallas_call(
        body, name=name, out_shape=tuple(out_shapes), in_specs=[ANY_SPEC] * len(args),
        out_specs=tuple(ANY_SPEC for _ in out_shapes),
        scratch_shapes=[pltpu.SemaphoreType.DMA((n,)) for n in n_sems],
        input_output_aliases=aliases or {},
        compiler_params=pltpu.CompilerParams(has_side_effects=True))(*args)


def place_shard(name, shard, dim, j_arr):
    L, rs, cs = shard.shape
    full_shape = (L, rs * N_CHIPS, cs) if dim == 1 else (L, rs, cs * N_CHIPS)
    tr = _tile(rs, max(16, (1 << 19) // cs // 16 * 16), unit=16)

    def body(j_ref, s_ref, o_ref):
        o_ref[...] = s_ref[...].astype(o_ref.dtype)

    if dim == 1:
        out_spec = pl.BlockSpec((None, tr, cs), lambda l, i, j: (l, j[0] * (rs // tr) + i, 0))
    else:
        out_spec = pl.BlockSpec((None, tr, cs), lambda l, i, j: (l, i, j[0]))
    grid_spec = pltpu.PrefetchScalarGridSpec(
        num_scalar_prefetch=1, grid=(L, rs // tr),
        in_specs=[pl.BlockSpec((None, tr, cs), lambda l, i, j: (l, i, 0))], out_specs=out_spec)
    return pl.pallas_call(body, name=name, grid_spec=grid_spec,
                          out_shape=jax.ShapeDtypeStruct(full_shape, MXU_DTYPE),
                          compiler_params=_params())(j_arr, shard)


class Exchange(NamedTuple):
    ins: tuple
    out_shapes: tuple
    in_place: bool
    sem_counts: tuple
    start: Callable
    finish: Callable


def run_exchange(name, ex):
    n_in, n_out = len(ex.ins), len(ex.out_shapes)

    def body(*refs):
        ins, outs, sems = refs[:n_in], refs[n_in:n_in + n_out], refs[n_in + n_out:]
        ex.start(ins, outs, sems)
        ex.finish(ins, outs, sems)

    return _hbm_call(name, body, ex.out_shapes, ex.sem_counts, ex.ins,
                     {k: k for k in range(n_in)} if ex.in_place else None)


def _rcopy(src, dst, ssem, rsem, device):
    return pltpu.make_async_remote_copy(src_ref=src, dst_ref=dst, send_sem=ssem, recv_sem=rsem,
                                        device_id=device, device_id_type=MESH)


def gather_exchange(fulls, dims, shard_shapes, items):
    n = len(items)

    def win(full, w, lo, nl, jj):
        return _window(full[w], dims[w], lo, nl, jj, shard_shapes[w][1], shard_shapes[w][2])

    def start(_, full, sems):
        ssem, rsem = sems[0], sems[1]
        x, y, c = _position()
        for it, (w, lo, nl, owner) in enumerate(items):
            @pl.when(c == owner)
            def _():
                own = win(full, w, lo, nl, 2 * x + y)
                for k, (cx, cy) in enumerate(_other_chips(x, y)):
                    _rcopy(own, own, ssem.at[3 * it + k], rsem.at[3 * it + k], (cx, cy, c)).start()

    def finish(_, full, sems):
        ssem, rsem, s2sem, r2sem = sems
        x, y, c = _position()
        sibling = (x, y, 1 - c)
        for it, (w, lo, nl, owner) in enumerate(items):
            @pl.when(c == owner)
            def _():
                for k, (cx, cy) in enumerate(_other_chips(x, y)):
                    theirs = win(full, w, lo, nl, 2 * cx + cy)
                    _rcopy(theirs, theirs, ssem.at[3 * it + k], rsem.at[3 * it + k], (cx, cy, c)).wait_recv()
                    _rcopy(theirs, theirs, s2sem.at[3 * it + k], r2sem.at[3 * it + k], sibling).start()
        for it, (w, lo, nl, owner) in enumerate(items):
            @pl.when(c == owner)
            def _():
                own = win(full, w, lo, nl, 2 * x + y)
                for k, (cx, cy) in enumerate(_other_chips(x, y)):
                    theirs = win(full, w, lo, nl, 2 * cx + cy)
                    _rcopy(own, own, ssem.at[3 * it + k], rsem.at[3 * it + k], (cx, cy, c)).wait_send()
                    _rcopy(theirs, theirs, s2sem.at[3 * it + k], r2sem.at[3 * it + k], sibling).wait_send()

            @pl.when(c != owner)
            def _():
                for k, (cx, cy) in enumerate(_other_chips(x, y)):
                    theirs = win(full, w, lo, nl, 2 * cx + cy)
                    _rcopy(theirs, theirs, s2sem.at[3 * it + k], r2sem.at[3 * it + k], sibling).wait_recv()

    shapes = tuple(jax.ShapeDtypeStruct(f.shape, f.dtype) for f in fulls)
    return Exchange(tuple(fulls), shapes, True, (3 * n,) * 4, start, finish)


def pair_exchange(grads, owners):
    n = len(grads)

    def start(gr, land, sems):
        x, y, c = _position()
        for it, owner in enumerate(owners):
            @pl.when(c != owner)
            def _():
                _rcopy(gr[it], land[it], sems[0].at[it], sems[1].at[it], (x, y, 1 - c)).start()

    def finish(gr, land, sems):
        x, y, c = _position()
        for it, owner in enumerate(owners):
            cp = _rcopy(gr[it], land[it], sems[0].at[it], sems[1].at[it], (x, y, 1 - c))
            pl.when(c != owner)(cp.wait_send)
            pl.when(c == owner)(cp.wait_recv)

    shapes = tuple(jax.ShapeDtypeStruct(g.shape, g.dtype) for g in grads)
    return Exchange(tuple(grads), shapes, False, (n, n), start, finish)


def pair_add(name, g, landed, owner, c_arr):
    R, C = g.shape
    tr = _tile(R, max(16, (1 << 20) // C // 16 * 16), unit=16)

    def body(c_ref, g_ref, l_ref, o_ref):
        @pl.when(c_ref[0] == owner)
        def _():
            o_ref[...] = (g_ref[...].astype(F32) + l_ref[...].astype(F32)).astype(o_ref.dtype)

    spec = pl.BlockSpec((tr, C), lambda i, c: (jnp.where(c[0] == owner, i, 0), 0))
    grid_spec = pltpu.PrefetchScalarGridSpec(num_scalar_prefetch=1, grid=(R // tr,), in_specs=[spec, spec],
                                             out_specs=spec)
    return pl.pallas_call(body, name=name, grid_spec=grid_spec, out_shape=jax.ShapeDtypeStruct(g.shape, g.dtype),
                          compiler_params=_params())(c_arr, g, landed)


def _window2(ref, dim, jj, rs, cs):
    if dim == 1:
        return ref.at[pl.ds(pl.multiple_of(jj * rs, 16), rs), :]
    return ref.at[:, pl.ds(pl.multiple_of(jj * cs, LANE), cs)]


def chip_exchange(partials, dims, shard_shapes, owners):
    n = len(partials)

    def copies(ps, land, sems, it):
        x, y, c = _position()
        rs, cs = shard_shapes[it]
        return [_rcopy(_window2(ps[it], dims[it], 2 * cx + cy, rs, cs), land[it].at[k],
                       sems[0].at[3 * it + k], sems[1].at[3 * it + k], (cx, cy, c))
                for k, (cx, cy) in enumerate(_other_chips(x, y))]

    def start(ps, land, sems):
        c = lax.axis_index('c')
        for it, owner in enumerate(owners):
            @pl.when(c == owner)
            def _():
                for cp in copies(ps, land, sems, it):
                    cp.start()

    def finish(ps, land, sems):
        c = lax.axis_index('c')
        for it, owner in enumerate(owners):
            @pl.when(c == owner)
            def _():
                for cp in copies(ps, land, sems, it):
                    cp.wait()

    shapes = tuple(jax.ShapeDtypeStruct((3,) + tuple(s), p.dtype) for p, s in zip(partials, shard_shapes))
    return Exchange(tuple(partials), shapes, False, (3 * n, 3 * n), start, finish)


def chip_sum(name, partial, landed, dim, buf, depth, layer, owner, j_arr, c_arr):
    _, rs, cs = landed.shape
    tr = _tile(rs, max(16, (1 << 19) // cs // 16 * 16), unit=16)
    has_buf = buf is not None

    def body(*refs):
        core_ref, p_ref, a_ref, b_ref, c_ref = refs[1:6]
        o_ref = refs[6 + has_buf]

        @pl.when(core_ref[0] == owner)
        def _():
            o_ref[...] = ((p_ref[...].astype(F32) + a_ref[...].astype(F32)) + b_ref[...].astype(F32)) + c_ref[...].astype(F32)

    def on(c, index):
        return jnp.where(c[0] == owner, index, 0)

    if dim == 1:
        own = pl.BlockSpec((tr, cs), lambda i, j, c: (on(c, j[0] * (rs // tr) + i), 0))
    else:
        own = pl.BlockSpec((tr, cs), lambda i, j, c: (on(c, i), on(c, j[0])))
    arrived = [pl.BlockSpec((None, tr, cs), lambda i, j, c, k=k: (k, on(c, i), 0)) for k in range(3)]
    grid_spec = pltpu.PrefetchScalarGridSpec(
        num_scalar_prefetch=2, grid=(rs // tr,), in_specs=[own] + arrived + ([ANY_SPEC] if has_buf else []),
        out_specs=pl.BlockSpec((None, tr, cs), lambda i, j, c: (layer, on(c, i), 0)))
    args = (j_arr, c_arr, partial, landed, landed, landed) + ((buf,) if has_buf else ())
    return pl.pallas_call(body, name=name, grid_spec=grid_spec,
                          out_shape=jax.ShapeDtypeStruct((depth, rs, cs), F32),
                          input_output_aliases={6: 0} if has_buf else {},
                          compiler_params=_params())(*args)


def share_exchange(bufs, items):
    n = len(items)

    def copy(full, sems, it):
        w, lo, nl, _ = items[it]
        x, y, c = _position()
        own = full[w].at[pl.ds(lo, nl)]
        return _rcopy(own, own, sems[0].at[it], sems[1].at[it], (x, y, 1 - c))

    def start(_, full, sems):
        c = lax.axis_index('c')
        for it in range(n):
            pl.when(c == items[it][3])(copy(full, sems, it).start)

    def finish(_, full, sems):
        c = lax.axis_index('c')
        for it in range(n):
            cp = copy(full, sems, it)
            pl.when(c == items[it][3])(cp.wait_send)
            pl.when(c != items[it][3])(cp.wait_recv)

    shapes = tuple(jax.ShapeDtypeStruct(f.shape, f.dtype) for f in bufs)
    return Exchange(tuple(bufs), shapes, True, (n, n), start, finish)


def exchange_small(name, pack, reduce_all):
    R = pack.shape[0]
    n_slots = N_DEVICES if reduce_all else N_CHIPS
    n_peers = n_slots - 1

    def body(p_ref, o_ref, buf, ssem, rsem):
        x, y, c = _position()
        if reduce_all:
            me = 4 * x + 2 * y + c
            peers = [(x ^ (k >> 2 & 1), y ^ (k >> 1 & 1), c ^ (k & 1)) for k in range(1, N_DEVICES)]
        else:
            me = 2 * x + y
            peers = [(cx, cy, c) for cx, cy in _other_chips(x, y)]
        buf[me] = p_ref[...]
        copies = []
        for k, peer in enumerate(peers):
            cp = pltpu.make_async_remote_copy(src_ref=p_ref, dst_ref=buf.at[me], send_sem=ssem.at[k],
                                              recv_sem=rsem.at[k], device_id=peer, device_id_type=MESH)
            cp.start()
            copies.append(cp)
        for k, (px, py, pc) in enumerate(peers):
            slot = 4 * px + 2 * py + pc if reduce_all else 2 * px + py
            pltpu.make_async_remote_copy(src_ref=p_ref, dst_ref=buf.at[slot], send_sem=ssem.at[k],
                                         recv_sem=rsem.at[k], device_id=(px, py, pc), device_id_type=MESH).wait_recv()
        for cp in copies:
            cp.wait_send()
        if reduce_all:
            acc = buf[0]
            for d in range(1, N_DEVICES):
                acc = acc + buf[d]
            o_ref[...] = acc
        else:
            o_ref[...] = buf[...]

    vmem = pl.BlockSpec(memory_space=pltpu.VMEM)
    out_shape = jax.ShapeDtypeStruct((R, LANE) if reduce_all else (N_CHIPS, R, LANE), F32)
    return pl.pallas_call(
        body, name=name, out_shape=out_shape, in_specs=[vmem], out_specs=vmem,
        scratch_shapes=[pltpu.VMEM((n_slots, R, LANE), F32), pltpu.SemaphoreType.DMA((n_peers,)),
                        pltpu.SemaphoreType.DMA((n_peers,))],
        compiler_params=_params())(pack)


def _pack(arrays):
    flat = jnp.concatenate([a.reshape(-1).astype(F32) for a in arrays])
    n = flat.shape[0]
    rows = -(-n // LANE)
    rows = -(-rows // 8) * 8
    return jnp.pad(flat, (0, rows * LANE - n)).reshape(rows, LANE)


def _unpack(pack, shapes, lead=()):
    flat = pack.reshape(lead + (-1,))
    out, off = [], 0
    for s in shapes:
        n = math.prod(s)
        out.append(flat[..., off:off + n].reshape(lead + tuple(s)))
        off += n
    return out


def kernel(x, p, positions, mla_w_in, mla_q_norm, mla_kv_norm, mla_w_uq, mla_w_uk, mla_w_uv, mla_w_o, gla_w_in, gla_w_a2, gla_b_a, gla_o_norm, gla_w_o, ln1_g, ln1_b, ln2_g, ln2_b, ffn_w_up, ffn_conv_w, ffn_conv_b, ffn_w_down, ple_w_proj, ple_w_gate, ple_b_gate, loss_target, m_mla_w_in, m_mla_q_norm, m_mla_kv_norm, m_mla_w_uq, m_mla_w_uk, m_mla_w_uv, m_mla_w_o, m_gla_w_in, m_gla_w_a2, m_gla_b_a, m_gla_o_norm, m_gla_w_o, m_ln1_g, m_ln1_b, m_ln2_g, m_ln2_b, m_ffn_w_up, m_ffn_conv_w, m_ffn_conv_b, m_ffn_w_down, m_ple_w_proj, m_ple_w_gate, m_ple_b_gate, v_mla_w_in, v_mla_q_norm, v_mla_kv_norm, v_mla_w_uq, v_mla_w_uk, v_mla_w_uv, v_mla_w_o, v_gla_w_in, v_gla_w_a2, v_gla_b_a, v_gla_o_norm, v_gla_w_o, v_ln1_g, v_ln1_b, v_ln2_g, v_ln2_b, v_ffn_w_up, v_ffn_conv_w, v_ffn_conv_b, v_ffn_w_down, v_ple_w_proj, v_ple_w_gate, v_ple_b_gate):
    given = dict(locals())
    S, D = x.shape[1], x.shape[2]
    QL, KL = mla_q_norm.shape[1], mla_kv_norm.shape[1]
    H = mla_w_uq.shape[2] * N_CHIPS // (MLA_NOPE + MLA_ROPE)
    QK, VD = gla_b_a.shape[1] * N_CHIPS, gla_o_norm.shape[1] * N_CHIPS
    FF = ffn_w_down.shape[1] * N_CHIPS
    GIN = 2 * QK + 2 * VD + GLA_GATE_RANK
    GIN_PAD = 2 * QK + 2 * VD + LANE
    MIN = QL + KL + MLA_ROPE
    MIN_PAD = QL + KL + LANE
    a_blk = (2 * QK + 2 * VD) // LANE
    r_blk = (2 * QK + VD) // VD
    xi, yi, ci = _position()
    chip = 2 * xi + yi
    c_arr = jnp.reshape(ci, (1,)).astype(jnp.int32)
    j_arr = jnp.reshape(chip, (1,)).astype(jnp.int32)

    names = [n for n, _ in BIG]
    big_dims = [1 if n == 'gla_w_in' else d for n, d in BIG]
    shapes = [given[n].shape for n in names]

    def layer_of(name, idx):
        return 2 * idx if name.startswith('mla') else 2 * idx + 1 if name.startswith('gla') else idx

    def owner_of(layer):
        return layer % 2

    def layer_runs(wanted):
        out = []
        for w, n in enumerate(names):
            for owner in (0, 1):
                idxs = [k for k in range(shapes[w][0])
                        if owner_of(layer_of(n, k)) == owner and wanted(n, layer_of(n, k))]
                for k in idxs:
                    if out and out[-1][0] == w and out[-1][3] == owner and out[-1][1] + out[-1][2] == k:
                        out[-1] = (w, out[-1][1], out[-1][2] + 1, owner)
                    else:
                        out.append((w, k, 1, owner))
        return out

    before_attention = ('mla_w_in', 'mla_w_uq', 'mla_w_uk', 'mla_w_uv')
    gather_phase = {0: lambda n, l: l == 1 or (l == 0) != (n in before_attention) and l in (0, 2),
                    2: lambda n, l: l == 3 or l == 2 and n not in before_attention}
    placed = [place_shard(f'place_{n}', given[n], d, j_arr) for n, d in zip(names, big_dims)]
    first = run_exchange('gather_first', gather_exchange(
        placed, big_dims, shapes, layer_runs(lambda n, l: l == 0 and n in before_attention)))
    full = dict(zip(names, first))
    small_sharded = [n for n, d in SMALL if d is not None]
    spack = exchange_small('gather_small', _pack([given[n] for n in small_sharded]), False)
    parts = _unpack(spack, [given[n].shape for n in small_sharded], lead=(N_CHIPS,))
    for n, part in zip(small_sharded, parts):
        d = dict(SMALL)[n]
        full[n] = jnp.concatenate([part[k] for k in range(N_CHIPS)], axis=d)

    def mla_in_weights(j):
        w_in = jnp.pad(full['mla_w_in'][j], ((0, 0), (0, MIN_PAD - MIN)))
        w_uq = full['mla_w_uq'][j].reshape(QL, H, MLA_NOPE + MLA_ROPE)
        w_uq = jnp.pad(w_uq, ((0, 0), (0, 0), (0, HEAD_PAD - MLA_NOPE - MLA_ROPE))).reshape(QL, H * HEAD_PAD)
        return w_in, w_uq

    def gla_in_weight(j):
        w = full['gla_w_in'][j].reshape(N_CHIPS, D, GIN // N_CHIPS).transpose(1, 0, 2).reshape(D, GIN)
        return jnp.pad(w, ((0, 0), (0, GIN_PAD - GIN)))

    w_a2 = jnp.pad(full['gla_w_a2'], ((0, 0), (0, LANE - GLA_GATE_RANK), (0, 0))).astype(MXU_DTYPE)
    cw_u, cw_g = full['ffn_conv_w'][:, :, :FF], full['ffn_conv_w'][:, :, FF:]
    cb_u, cb_g = ffn_conv_b[:, None, :FF], ffn_conv_b[:, None, FF:]

    pos_row = positions[0]
    cid = pos_row // CHUNK
    tab_q = rope_tables(pos_row, MLA_NOPE)
    tab_k = rope_tables(pos_row, 0)

    def row1(a, i):
        return a[i:i + 1]

    saved = []
    xa = x[0]
    xb = xa
    for i in range(DEPTH):
        j = i // 2
        sv = dict(x=xa, xb=xb)
        if i % 2 == 0:
            w_in, w_uq = mla_in_weights(j)
            h = mm(f'mla_in_{i}', xb, w_in, 'nn', F32)
            cq, ckv, kr = mla_pre_fwd(f'mla_pre_{i}', h, row1(mla_q_norm, j), row1(mla_kv_norm, j), tab_k, QL, KL)
            q = mm(f'mla_uq_{i}', cq, w_uq, 'nn', MXU_DTYPE, epilogue=(rope_heads, tab_q))
            kn = mm(f'mla_uk_{i}', ckv, full['mla_w_uk'], 'nn', MXU_DTYPE, b_pre=(j,))
            vv = mm(f'mla_uv_{i}', ckv, full['mla_w_uv'], 'nn', MXU_DTYPE, b_pre=(j,))
            rider = gather_exchange([full[n] for n in names], big_dims, shapes, layer_runs(gather_phase[i]))
            (o, lse), gathered = attn_fwd(f'mla_attn_{i}', q, kn, kr, vv, cid, H, rider=rider)
            full.update(zip(names, gathered))
            mix = mm(f'mla_o_{i}', o, full['mla_w_o'], 'nn', F32, b_pre=(j,))
            sv.update(h=h, cq=cq, ckv=ckv, kr=kr, q=q, kn=kn, v=vv, o=o, lse=lse, w_in=w_in, w_uq=w_uq)
        else:
            w_gin = gla_in_weight(j)
            sv.update(w_gin=w_gin)
            h = mm(f'gla_in_{i}', xb, w_gin, 'nn', F32)
            qs, la = gla_pre_fwd(f'gla_pre_{i}', h, w_a2[j], row1(full['gla_b_a'], j), QK, a_blk)
            o, states = gla_fwd(f'gla_scan_{i}', qs, h, la, QK, VD)
            og = gla_post_fwd(f'gla_post_{i}', o, h, row1(full['gla_o_norm'], j), VD, r_blk)
            mix = mm(f'gla_o_{i}', og, full['gla_w_o'], 'nn', F32, b_pre=(j,))
            sv.update(h=h, qs=qs, la=la, o=o, states=states, og=og)
        x1, x1b = ln_fwd(f'ln1_{i}', xa, mix, row1(ln1_g, i), row1(ln1_b, i))
        hu = mm(f'ffn_up_u_{i}', x1b, full['ffn_w_up'], 'nn', F32, b_pre=(i,), b_win=(0, D, 0, FF))
        hg = mm(f'ffn_up_g_{i}', x1b, full['ffn_w_up'], 'nn', F32, b_pre=(i,), b_win=(0, D, FF, FF))
        act = conv_fwd(f'ffn_conv_{i}', hu, hg, cw_u[i], cw_g[i], cb_u[i], cb_g[i])
        f = mm(f'ffn_down_{i}', act, full['ffn_w_down'], 'nn', F32, b_pre=(i,))
        x2, x2b = ln_fwd(f'ln2_{i}', x1, f, row1(ln2_g, i), row1(ln2_b, i))
        gp = mm(f'ple_gate_{i}', x2b, full['ple_w_gate'], 'nn', F32, b_pre=(i,))
        pp = mm(f'ple_proj_{i}', p, full['ple_w_proj'], 'nn', F32, a_pre=(i, 0), b_pre=(i,))
        x3, x3b = ple_fwd(f'ple_{i}', x2, gp, pp, row1(ple_b_gate, i))
        sv.update(mix=mix, x1=x1, x1b=x1b, hu=hu, hg=hg, act=act, f=f, x2=x2, x2b=x2b, gp=gp, pp=pp)
        saved.append(sv)
        xa, xb = x3, x3b

    dy, sq = loss_head('loss_head', xa, loss_target[0])
    loss_part = (0.5 / D) * jnp.sum(sq)

    small_g = {n: [None] * given[n].shape[0] for n, _ in SMALL}
    tab_q_t = tuple(t.T for t in tab_q)
    shard_buf = {n: None for n in names}

    def reduce_begin(tag, items):
        owners = [owner_of(layer) for _, _, layer, _ in items]
        landed = run_exchange(f'pair_exchange_{tag}', pair_exchange([g for *_, g in items], owners))
        partial = [pair_add(f'pair_add_{n}_{layer}', g, l, o, c_arr)
                   for (n, _, layer, g), l, o in zip(items, landed, owners)]
        dims = [big_dims[names.index(n)] for n, *_ in items]
        exchange = chip_exchange(partial, dims, [given[n].shape[1:] for n, *_ in items], owners)
        return exchange, (items, partial, dims, owners)

    def reduce_end(context, arrived):
        items, partial, dims, owners = context
        for (n, idx, layer, _), pt, ar, d, o in zip(items, partial, arrived, dims, owners):
            shard_buf[n] = chip_sum(f'chip_sum_{n}_{layer}', pt, ar, d, shard_buf[n], given[n].shape[0], idx, o,
                                    j_arr, c_arr)

    def wgrad(name, a, b, a_pre=()):
        return mm(f'd_{name}_{i}', a, b, 'tn', MXU_DTYPE, a_pre=a_pre)

    def layer_items(layer_grads):
        return [(n, i if n.startswith(('ffn', 'ple')) else i // 2, i, g) for n, g in layer_grads.items()]

    dA, ca, dB = dy, 1.0, None
    waiting = []
    for i in reversed(range(DEPTH)):
        j = i // 2
        sv = saved[i]
        lg = {}
        dx3, dpp, dgp, dbg = ple_bwd(f'ple_b_{i}', dA, ca, dB, sv['gp'], sv['pp'], row1(ple_b_gate, i))
        small_g['ple_b_gate'][i] = dbg
        lg['ple_w_proj'] = wgrad('ple_w_proj', p, dpp, a_pre=(i, 0))
        lg['ple_w_gate'] = wgrad('ple_w_gate', sv['x2b'], dgp)
        dx2 = mm(f'ple_gate_b_{i}', dgp, full['ple_w_gate'], 'nt', F32, b_pre=(i,))
        dz2, dz2b, dg2, db2 = ln_bwd(f'ln2_b_{i}', sv['x1'], sv['f'], dx3, 1.0, dx2, row1(ln2_g, i))
        small_g['ln2_g'][i], small_g['ln2_b'][i] = dg2, db2
        lg['ffn_w_down'] = wgrad('ffn_w_down', sv['act'], dz2b)
        dact = mm(f'ffn_down_b_{i}', dz2b, full['ffn_w_down'], 'nt', F32, b_pre=(i,))
        dhu, dhg, dcwu, dcwg, dcbu, dcbg = conv_bwd(f'ffn_conv_b_{i}', sv['hu'], sv['hg'], dact,
                                                   cw_u[i], cw_g[i], cb_u[i], cb_g[i])
        small_g['ffn_conv_w'][i] = jnp.concatenate([dcwu, dcwg], -1)
        small_g['ffn_conv_b'][i] = jnp.concatenate([dcbu, dcbg], -1)
        up = mm(f'd_ffn_w_up_u_{i}', sv['x1b'], dhu, 'tn', MXU_DTYPE, out_stack=(None, (1, D, 2 * FF), 0, 0))
        up = mm(f'd_ffn_w_up_g_{i}', sv['x1b'], dhg, 'tn', MXU_DTYPE, out_stack=(up, (1, D, 2 * FF), 0, FF))
        lg['ffn_w_up'] = up[0]
        dx1 = mm(f'ffn_up_bu_{i}', dhu, full['ffn_w_up'], 'nt', F32, b_pre=(i,), b_win=(0, D, 0, FF))
        dx1 = mm(f'ffn_up_bg_{i}', dhg, full['ffn_w_up'], 'nt', F32, b_pre=(i,), b_win=(0, D, FF, FF), acc_in=dx1)
        dz1, dz1b, dg1, db1 = ln_bwd(f'ln1_b_{i}', sv['x'], sv['mix'], dz2, DN_ALPHA, dx1, row1(ln1_g, i))
        small_g['ln1_g'][i], small_g['ln1_b'][i] = dg1, db1
        if i % 2 == 0:
            lg['mla_w_o'] = wgrad('mla_w_o', sv['o'], dz1b)
            do = mm(f'mla_o_b_{i}', dz1b, full['mla_w_o'], 'nt', MXU_DTYPE, b_pre=(j,))
            waiting += layer_items(lg)
            lg = {}
            riding, context = reduce_begin(f'before_{i}', waiting)
            waiting = []
            (dq, dkn, dv, dkr), arrived = attn_bwd(f'mla_attn_b_{i}', sv['q'], sv['kn'], sv['kr'], sv['v'], sv['o'],
                                                   sv['lse'], do, cid, H, rider=riding)
            reduce_end(context, arrived)
            dqpre_t = rope_q_bwd(f'mla_rope_b_{i}', dq, tab_q_t, H)
            uq_t = mm(f'd_mla_w_uq_{i}', dqpre_t, sv['cq'], 'nn', MXU_DTYPE)
            lg['mla_w_uq'] = uq_t.T.reshape(QL, H, HEAD_PAD)[..., :MLA_NOPE + MLA_ROPE].reshape(QL, -1)
            lg['mla_w_uk'] = wgrad('mla_w_uk', sv['ckv'], dkn)
            lg['mla_w_uv'] = wgrad('mla_w_uv', sv['ckv'], dv)
            dcq = mm(f'mla_uq_b_{i}', dqpre_t, sv['w_uq'].T, 'tn', F32)
            dckv_a = mm(f'mla_uk_b_{i}', dkn, full['mla_w_uk'], 'nt', F32, b_pre=(j,))
            dckv_b = mm(f'mla_uv_b_{i}', dv, full['mla_w_uv'], 'nt', F32, b_pre=(j,))
            dh, dqn, dkvn = mla_pre_bwd(f'mla_pre_b_{i}', sv['h'], dcq, dckv_a, dckv_b, dkr,
                                        row1(mla_q_norm, j), row1(mla_kv_norm, j), tab_k, QL, KL, H)
            small_g['mla_q_norm'][j], small_g['mla_kv_norm'][j] = dqn, dkvn
            lg['mla_w_in'] = wgrad('mla_w_in', sv['xb'], dh)[:, :MIN]
            dmix = mm(f'mla_in_b_{i}', dh, sv['w_in'], 'nt', F32)
        else:
            lg['gla_w_o'] = wgrad('gla_w_o', sv['og'], dz1b)
            dog = mm(f'gla_o_b_{i}', dz1b, full['gla_w_o'], 'nt', F32, b_pre=(j,))
            do, dr, don = gla_post_bwd(f'gla_post_b_{i}', sv['o'], sv['h'], dog, row1(full['gla_o_norm'], j), VD, r_blk)
            dq, dk, dv, dla = gla_bwd(f'gla_scan_b_{i}', sv['qs'], sv['h'], sv['la'], sv['states'], do, QK, VD)
            dh, dw2, dba = gla_dh(f'gla_dh_{i}', dq, dk, dv, dr, dla, sv['h'], w_a2[j], row1(full['gla_b_a'], j),
                                  QK, VD, a_blk, GIN_PAD)
            small_g['gla_o_norm'][j], small_g['gla_b_a'][j] = don, dba
            small_g['gla_w_a2'][j] = dw2[:GLA_GATE_RANK]
            g_in = wgrad('gla_w_in', sv['xb'], dh)[:, :GIN]
            lg['gla_w_in'] = g_in.reshape(D, N_CHIPS, GIN // N_CHIPS).transpose(1, 0, 2).reshape(N_CHIPS * D, -1)
            dmix = mm(f'gla_in_b_{i}', dh, sv['w_gin'], 'nt', F32)
        dA, ca, dB = dz1, DN_ALPHA, dmix
        waiting += layer_items(lg)
    grad_x = axpy('grad_x', dA, ca, dB)[None]

    exchange, context = reduce_begin('last', waiting)
    reduce_end(context, run_exchange('chip_exchange_last', exchange))
    shared = run_exchange('share_with_sibling',
                          share_exchange([shard_buf[n] for n in names], layer_runs(lambda n, l: True)))
    grads = dict(zip(names, shared))

    small_names = [n for n, _ in SMALL]
    small_full = [jnp.concatenate([g.reshape((1,) + g.shape[-(given[n].ndim - 1):]) for g in small_g[n]], 0)
                  for n in small_names]
    pack = _pack([jnp.reshape(loss_part, (1,))] + [jnp.zeros((LANE - 1,), F32)] + small_full)
    red = exchange_small('reduce_small', pack, True)
    red_parts = _unpack(red, [(LANE,)] + [g.shape for g in small_full])
    loss = red_parts[0][0]
    for (n, d), g in zip(SMALL, red_parts[1:]):
        if d is not None:
            width = given[n].shape[d]
            g = lax.dynamic_slice_in_dim(g, chip * width, width, axis=d)
        grads[n] = g

    delta, new_m, new_v = {}, {}, {}
    for n in names:
        delta[n], new_m[n], new_v[n] = adamw(f'adamw_{n}', given[n], grads[n], given['m_' + n], given['v_' + n])
    packs = [_pack([given[pre + n] for n in small_names]) for pre in ('', 'm_', 'v_')]
    outs = adamw('adamw_small', packs[0], _pack([grads[n] for n in small_names]), packs[1], packs[2])
    small_shapes = [given[n].shape for n in small_names]
    for dst, out in zip((delta, new_m, new_v), outs):
        for n, a in zip(small_names, _unpack(out, small_shapes)):
            dst[n] = a

    return (loss, grad_x, *[grads[n] for n in WEIGHT_NAMES], *[delta[n] for n in WEIGHT_NAMES],
            *[new_m[n] for n in WEIGHT_NAMES], *[new_v[n] for n in WEIGHT_NAMES])
```
